```python
import jax, jax.numpy as jnp
from jax import lax
import numpy as np

D_MODEL = 1024
BATCH = 8
SEQ = 4096
DEPTH = 2

D_MIX = D_MODEL
CONV_CH = D_MIX // 2
CONV_WIDTH = 31
FOX_HEADS = 8
FOX_HEAD_DIM = 64
FOX_WIDTH = FOX_HEADS * FOX_HEAD_DIM
Q_BLOCK = 128
N_MEM = 256
MEM_HEADS = 4
MEM_HEAD_DIM = 128
MEM_INNER = MEM_HEADS * MEM_HEAD_DIM
D_FF = 4 * D_MODEL
EPS = 1e-6
NEG_INF = -1e30
IN_COLS = 2 * CONV_CH + 3 * FOX_WIDTH + FOX_HEADS

kernel_name = "hymba_conformer_fox_sandwich_memory"


def rms_norm(x, g):
    xf = x.astype(jnp.float32)
    y = xf * lax.rsqrt(jnp.mean(xf * xf, axis=-1, keepdims=True) + EPS)
    return (y * g.astype(jnp.float32)).astype(x.dtype)


def layer_norm(x, g, b):
    xf = x.astype(jnp.float32)
    mu = jnp.mean(xf, axis=-1, keepdims=True)
    xc = xf - mu
    y = xc * lax.rsqrt(jnp.mean(xc * xc, axis=-1, keepdims=True) + EPS)
    return (y * g.astype(jnp.float32) + b.astype(jnp.float32)).astype(x.dtype)


def causal_depthwise_conv(u, w, b):
    out = lax.conv_general_dilated(
        u, w[:, None, :], window_strides=(1,), padding=[(CONV_WIDTH - 1, 0)],
        dimension_numbers=("NWC", "WIO", "NWC"), feature_group_count=u.shape[-1])
    return out + b


def forgetting_attention(q, k, v, log_f):
    S = q.shape[1]
    dh = q.shape[-1]
    scale = dh ** -0.5
    cum = jnp.cumsum(log_f, axis=1).transpose(0, 2, 1)
    outs = []
    for i in range(S // Q_BLOCK):
        q0 = i * Q_BLOCK
        kl = q0 + Q_BLOCK
        qb = q[:, q0:kl]
        kb = k[:, :kl]
        vb = v[:, :kl]
        logits = jnp.einsum("bqhd,bkhd->bhqk", qb, kb,
                            preferred_element_type=jnp.float32) * scale
        logits = logits + cum[:, :, q0:kl, None] - cum[:, :, None, :kl]
        q_pos = q0 + jnp.arange(Q_BLOCK)
        k_pos = jnp.arange(kl)
        mask = k_pos[None, :] <= q_pos[:, None]
        p = jax.nn.softmax(jnp.where(mask, logits, NEG_INF), axis=-1)
        outs.append(jnp.einsum("bhqk,bkhd->bqhd", p.astype(vb.dtype), vb))
    return jnp.concatenate(outs, axis=1)


def hybrid_mixer(h, w_in, b_forget, conv_w, conv_b, conv_ln_g, conv_ln_b, w_out):
    B, S, _ = h.shape
    z = h @ w_in
    o = 0
    a = z[..., o:o + CONV_CH]; o += CONV_CH
    g = z[..., o:o + CONV_CH]; o += CONV_CH
    q = z[..., o:o + FOX_WIDTH]; o += FOX_WIDTH
    k = z[..., o:o + FOX_WIDTH]; o += FOX_WIDTH
    v = z[..., o:o + FOX_WIDTH]; o += FOX_WIDTH
    f_logit = z[..., o:o + FOX_HEADS]

    u = a * jax.nn.sigmoid(g)
    u = causal_depthwise_conv(u, conv_w, conv_b)
    u = jax.nn.silu(layer_norm(u, conv_ln_g, conv_ln_b))

    log_f = jax.nn.log_sigmoid((f_logit + b_forget).astype(jnp.float32))
    shp = (B, S, FOX_HEADS, FOX_HEAD_DIM)
    att = forgetting_attention(q.reshape(shp), k.reshape(shp), v.reshape(shp), log_f)
    att = att.reshape(B, S, FOX_WIDTH)

    return jnp.concatenate([u, att], axis=-1) @ w_out


def memory_cross_attention(h, mem_n, w_mq, w_mk, w_mv, w_mo):
    B, S, _ = h.shape
    q = (h @ w_mq).reshape(B, S, MEM_HEADS, MEM_HEAD_DIM)
    k = (mem_n @ w_mk).reshape(B, N_MEM, MEM_HEADS, MEM_HEAD_DIM)
    v = (mem_n @ w_mv).reshape(B, N_MEM, MEM_HEADS, MEM_HEAD_DIM)
    logits = jnp.einsum("bqhd,bmhd->bhqm", q, k,
                        preferred_element_type=jnp.float32) * (MEM_HEAD_DIM ** -0.5)
    p = jax.nn.softmax(logits, axis=-1)
    out = jnp.einsum("bhqm,bmhd->bqhd", p.astype(v.dtype), v).reshape(B, S, MEM_INNER)
    return out @ w_mo


def squared_relu_mlp(h, w_up, w_down):
    return jnp.square(jax.nn.relu(h @ w_up)) @ w_down


def _fwd_setup_inputs(seed: int = 0) -> dict:
    key = jax.random.key(seed)
    ks = jax.random.split(key, 24)
    nrm = lambda k, shape, fan_in: jax.random.normal(k, shape, jnp.float32) * (fan_in ** -0.5)
    gain = lambda k, shape: 1.0 + 0.05 * jax.random.normal(k, shape, jnp.float32)
    small = lambda k, shape: 0.02 * jax.random.normal(k, shape, jnp.float32)
    L = DEPTH
    return {
        "x": jax.random.normal(ks[0], (BATCH, SEQ, D_MODEL), jnp.float32),
        "mem": jax.random.normal(ks[1], (BATCH, N_MEM, D_MODEL), jnp.float32),
        "norm_mix_pre": gain(ks[2], (L, D_MODEL)),
        "norm_mix_post": gain(ks[3], (L, D_MODEL)),
        "w_in": nrm(ks[4], (L, D_MODEL, IN_COLS), D_MODEL),
        "b_forget": jax.random.uniform(ks[5], (L, FOX_HEADS), jnp.float32, 1.0, 5.0),
        "conv_w": nrm(ks[6], (L, CONV_WIDTH, CONV_CH), CONV_WIDTH),
        "conv_b": small(ks[7], (L, CONV_CH)),
        "conv_ln_g": gain(ks[8], (L, CONV_CH)),
        "conv_ln_b": small(ks[9], (L, CONV_CH)),
        "w_out": nrm(ks[10], (L, D_MIX, D_MODEL), D_MIX),
        "norm_mem_pre": gain(ks[11], (L, D_MODEL)),
        "norm_mem_post": gain(ks[12], (L, D_MODEL)),
        "norm_memkv": gain(ks[13], (L, D_MODEL)),
        "w_mq": nrm(ks[14], (L, D_MODEL, MEM_INNER), D_MODEL),
        "w_mk": nrm(ks[15], (L, D_MODEL, MEM_INNER), D_MODEL),
        "w_mv": nrm(ks[16], (L, D_MODEL, MEM_INNER), D_MODEL),
        "w_mo": nrm(ks[17], (L, MEM_INNER, D_MODEL), MEM_INNER),
        "norm_mlp_pre": gain(ks[18], (L, D_MODEL)),
        "norm_mlp_post": gain(ks[19], (L, D_MODEL)),
        "w_up": nrm(ks[20], (L, D_MODEL, D_FF), D_MODEL),
        "w_down": nrm(ks[21], (L, D_FF, D_MODEL), D_FF),
    }


def _fwd_reference(x, mem, norm_mix_pre, norm_mix_post, w_in, b_forget, conv_w, conv_b,
              conv_ln_g, conv_ln_b, w_out, norm_mem_pre, norm_mem_post, norm_memkv,
              w_mq, w_mk, w_mv, w_mo, norm_mlp_pre, norm_mlp_post, w_up, w_down):
    for l in range(DEPTH):
        h = rms_norm(x, norm_mix_pre[l])
        y = hybrid_mixer(h, w_in[l], b_forget[l], conv_w[l], conv_b[l],
                         conv_ln_g[l], conv_ln_b[l], w_out[l])
        x = x + rms_norm(y, norm_mix_post[l])
        h = rms_norm(x, norm_mem_pre[l])
        mem_n = rms_norm(mem, norm_memkv[l])
        y = memory_cross_attention(h, mem_n, w_mq[l], w_mk[l], w_mv[l], w_mo[l])
        x = x + rms_norm(y, norm_mem_post[l])
        h = rms_norm(x, norm_mlp_pre[l])
        y = squared_relu_mlp(h, w_up[l], w_down[l])
        x = x + rms_norm(y, norm_mlp_post[l])
    return x


import jax as _jax
import jax.numpy as _jnp

TWIN_FORMAT = 'train_step'
FWD_PARAMS = ['x', 'mem', 'norm_mix_pre', 'norm_mix_post', 'w_in', 'b_forget', 'conv_w', 'conv_b', 'conv_ln_g', 'conv_ln_b', 'w_out', 'norm_mem_pre', 'norm_mem_post', 'norm_memkv', 'w_mq', 'w_mk', 'w_mv', 'w_mo', 'norm_mlp_pre', 'norm_mlp_post', 'w_up', 'w_down']
TWIN_WEIGHTS = ['norm_mix_pre', 'norm_mix_post', 'w_in', 'b_forget', 'conv_w', 'conv_b', 'conv_ln_g', 'conv_ln_b', 'w_out', 'norm_mem_pre', 'norm_mem_post', 'norm_memkv', 'w_mq', 'w_mk', 'w_mv', 'w_mo', 'norm_mlp_pre', 'norm_mlp_post', 'w_up', 'w_down']
TWIN_DIFF_INPUT = 'x'
TWIN_INPUTS = ['x', 'mem', 'norm_mix_pre', 'norm_mix_post', 'w_in', 'b_forget', 'conv_w', 'conv_b', 'conv_ln_g', 'conv_ln_b', 'w_out', 'norm_mem_pre', 'norm_mem_post', 'norm_memkv', 'w_mq', 'w_mk', 'w_mv', 'w_mo', 'norm_mlp_pre', 'norm_mlp_post', 'w_up', 'w_down', 'loss_target', 'm_norm_mix_pre', 'm_norm_mix_post', 'm_w_in', 'm_b_forget', 'm_conv_w', 'm_conv_b', 'm_conv_ln_g', 'm_conv_ln_b', 'm_w_out', 'm_norm_mem_pre', 'm_norm_mem_post', 'm_norm_memkv', 'm_w_mq', 'm_w_mk', 'm_w_mv', 'm_w_mo', 'm_norm_mlp_pre', 'm_norm_mlp_post', 'm_w_up', 'm_w_down', 'v_norm_mix_pre', 'v_norm_mix_post', 'v_w_in', 'v_b_forget', 'v_conv_w', 'v_conv_b', 'v_conv_ln_g', 'v_conv_ln_b', 'v_w_out', 'v_norm_mem_pre', 'v_norm_mem_post', 'v_norm_memkv', 'v_w_mq', 'v_w_mk', 'v_w_mv', 'v_w_mo', 'v_norm_mlp_pre', 'v_norm_mlp_post', 'v_w_up', 'v_w_down']
TWIN_OUTPUTS = ['loss', 'grad_x', 'grad_norm_mix_pre', 'grad_norm_mix_post', 'grad_w_in', 'grad_b_forget', 'grad_conv_w', 'grad_conv_b', 'grad_conv_ln_g', 'grad_conv_ln_b', 'grad_w_out', 'grad_norm_mem_pre', 'grad_norm_mem_post', 'grad_norm_memkv', 'grad_w_mq', 'grad_w_mk', 'grad_w_mv', 'grad_w_mo', 'grad_norm_mlp_pre', 'grad_norm_mlp_post', 'grad_w_up', 'grad_w_down', 'delta_norm_mix_pre', 'delta_norm_mix_post', 'delta_w_in', 'delta_b_forget', 'delta_conv_w', 'delta_conv_b', 'delta_conv_ln_g', 'delta_conv_ln_b', 'delta_w_out', 'delta_norm_mem_pre', 'delta_norm_mem_post', 'delta_norm_memkv', 'delta_w_mq', 'delta_w_mk', 'delta_w_mv', 'delta_w_mo', 'delta_norm_mlp_pre', 'delta_norm_mlp_post', 'delta_w_up', 'delta_w_down', 'new_m_norm_mix_pre', 'new_m_norm_mix_post', 'new_m_w_in', 'new_m_b_forget', 'new_m_conv_w', 'new_m_conv_b', 'new_m_conv_ln_g', 'new_m_conv_ln_b', 'new_m_w_out', 'new_m_norm_mem_pre', 'new_m_norm_mem_post', 'new_m_norm_memkv', 'new_m_w_mq', 'new_m_w_mk', 'new_m_w_mv', 'new_m_w_mo', 'new_m_norm_mlp_pre', 'new_m_norm_mlp_post', 'new_m_w_up', 'new_m_w_down', 'new_v_norm_mix_pre', 'new_v_norm_mix_post', 'new_v_w_in', 'new_v_b_forget', 'new_v_conv_w', 'new_v_conv_b', 'new_v_conv_ln_g', 'new_v_conv_ln_b', 'new_v_w_out', 'new_v_norm_mem_pre', 'new_v_norm_mem_post', 'new_v_norm_memkv', 'new_v_w_mq', 'new_v_w_mk', 'new_v_w_mv', 'new_v_w_mo', 'new_v_norm_mlp_pre', 'new_v_norm_mlp_post', 'new_v_w_up', 'new_v_w_down']
TWIN_LEAF_KINDS = {'loss': 'loss', 'grad_x': 'grad_x', 'grad_norm_mix_pre': 'grad_w', 'grad_norm_mix_post': 'grad_w', 'grad_w_in': 'grad_w', 'grad_b_forget': 'grad_w', 'grad_conv_w': 'grad_w', 'grad_conv_b': 'grad_w', 'grad_conv_ln_g': 'grad_w', 'grad_conv_ln_b': 'grad_w', 'grad_w_out': 'grad_w', 'grad_norm_mem_pre': 'grad_w', 'grad_norm_mem_post': 'grad_w', 'grad_norm_memkv': 'grad_w', 'grad_w_mq': 'grad_w', 'grad_w_mk': 'grad_w', 'grad_w_mv': 'grad_w', 'grad_w_mo': 'grad_w', 'grad_norm_mlp_pre': 'grad_w', 'grad_norm_mlp_post': 'grad_w', 'grad_w_up': 'grad_w', 'grad_w_down': 'grad_w', 'delta_norm_mix_pre': 'delta_w', 'delta_norm_mix_post': 'delta_w', 'delta_w_in': 'delta_w', 'delta_b_forget': 'delta_w', 'delta_conv_w': 'delta_w', 'delta_conv_b': 'delta_w', 'delta_conv_ln_g': 'delta_w', 'delta_conv_ln_b': 'delta_w', 'delta_w_out': 'delta_w', 'delta_norm_mem_pre': 'delta_w', 'delta_norm_mem_post': 'delta_w', 'delta_norm_memkv': 'delta_w', 'delta_w_mq': 'delta_w', 'delta_w_mk': 'delta_w', 'delta_w_mv': 'delta_w', 'delta_w_mo': 'delta_w', 'delta_norm_mlp_pre': 'delta_w', 'delta_norm_mlp_post': 'delta_w', 'delta_w_up': 'delta_w', 'delta_w_down': 'delta_w', 'new_m_norm_mix_pre': 'new_m', 'new_m_norm_mix_post': 'new_m', 'new_m_w_in': 'new_m', 'new_m_b_forget': 'new_m', 'new_m_conv_w': 'new_m', 'new_m_conv_b': 'new_m', 'new_m_conv_ln_g': 'new_m', 'new_m_conv_ln_b': 'new_m', 'new_m_w_out': 'new_m', 'new_m_norm_mem_pre': 'new_m', 'new_m_norm_mem_post': 'new_m', 'new_m_norm_memkv': 'new_m', 'new_m_w_mq': 'new_m', 'new_m_w_mk': 'new_m', 'new_m_w_mv': 'new_m', 'new_m_w_mo': 'new_m', 'new_m_norm_mlp_pre': 'new_m', 'new_m_norm_mlp_post': 'new_m', 'new_m_w_up': 'new_m', 'new_m_w_down': 'new_m', 'new_v_norm_mix_pre': 'new_v', 'new_v_norm_mix_post': 'new_v', 'new_v_w_in': 'new_v', 'new_v_b_forget': 'new_v', 'new_v_conv_w': 'new_v', 'new_v_conv_b': 'new_v', 'new_v_conv_ln_g': 'new_v', 'new_v_conv_ln_b': 'new_v', 'new_v_w_out': 'new_v', 'new_v_norm_mem_pre': 'new_v', 'new_v_norm_mem_post': 'new_v', 'new_v_norm_memkv': 'new_v', 'new_v_w_mq': 'new_v', 'new_v_w_mk': 'new_v', 'new_v_w_mv': 'new_v', 'new_v_w_mo': 'new_v', 'new_v_norm_mlp_pre': 'new_v', 'new_v_norm_mlp_post': 'new_v', 'new_v_w_up': 'new_v', 'new_v_w_down': 'new_v'}


def _forward(args):
    return _fwd_reference(*[args[k] for k in FWD_PARAMS])


def _output_shape():
    out = _jax.eval_shape(lambda: _forward(_fwd_setup_inputs(0)))
    return out.shape, out.dtype

N_MICROBATCH = 1
ADAM_LR = 0.001
ADAM_B1 = 0.9
ADAM_B2 = 0.999
ADAM_EPS = 1e-08
ADAM_WD = 0.01
ADAM_STEP = 10
PER_EXAMPLE_BATCH_AXIS = {'x': 0, 'mem': 0, 'loss_target': 0}
SHARED_INPUTS = []
_WEIGHT_DTYPES = {'norm_mix_pre': _jnp.float32, 'norm_mix_post': _jnp.float32, 'w_in': _jnp.float32, 'b_forget': _jnp.float32, 'conv_w': _jnp.float32, 'conv_b': _jnp.float32, 'conv_ln_g': _jnp.float32, 'conv_ln_b': _jnp.float32, 'w_out': _jnp.float32, 'norm_mem_pre': _jnp.float32, 'norm_mem_post': _jnp.float32, 'norm_memkv': _jnp.float32, 'w_mq': _jnp.float32, 'w_mk': _jnp.float32, 'w_mv': _jnp.float32, 'w_mo': _jnp.float32, 'norm_mlp_pre': _jnp.float32, 'norm_mlp_post': _jnp.float32, 'w_up': _jnp.float32, 'w_down': _jnp.float32}
MOMENT_SCALE = {'norm_mix_pre': 1.025264e+01, 'norm_mix_post': 3.589686e+01, 'w_in': 6.198954e+00, 'b_forget': 2.076434e+00, 'conv_w': 6.476032e+00, 'conv_b': 6.922379e+01, 'conv_ln_g': 2.754771e+01, 'conv_ln_b': 3.879388e+01, 'w_out': 1.478794e+01, 'norm_mem_pre': 5.415782e+00, 'norm_mem_post': 3.651262e+01, 'norm_memkv': 1.747475e+01, 'w_mq': 7.560895e+00, 'w_mk': 7.876411e+00, 'w_mv': 2.174568e+01, 'w_mo': 1.537081e+01, 'norm_mlp_pre': 8.656570e+00, 'norm_mlp_post': 3.581155e+01, 'w_up': 4.280795e+00, 'w_down': 1.561861e+01}


def _to_microbatches(a, axis):
    t = _jnp.moveaxis(a, axis, 0)
    t = t.reshape((N_MICROBATCH, t.shape[0] // N_MICROBATCH) + t.shape[1:])
    return _jnp.moveaxis(t, 1, axis + 1)


def setup_inputs(seed: int = 0) -> dict:
    inp = _fwd_setup_inputs(seed)
    key = _jax.random.fold_in(_jax.random.key(seed), 7919)
    shape, _ = _output_shape()
    out = dict(inp)
    out["loss_target"] = _jax.random.normal(_jax.random.fold_in(key, 0), shape, _jnp.float32)
    for i, name in enumerate(TWIN_WEIGHTS):
        w = inp[name].astype(_jnp.float32)
        if MOMENT_SCALE is None:
            s = _jnp.sqrt(_jnp.mean(_jnp.square(w)) + 1e-30)
        else:
            s = MOMENT_SCALE[name]
        km, kv = _jax.random.split(_jax.random.fold_in(key, i + 1))
        out[name] = w
        out["m_" + name] = s * _jax.random.normal(km, w.shape, _jnp.float32)
        out["v_" + name] = (s * s) * _jax.random.uniform(kv, w.shape, _jnp.float32, 0.5, 1.5)
    if N_MICROBATCH > 1:
        for name, axis in PER_EXAMPLE_BATCH_AXIS.items():
            out[name] = _to_microbatches(out[name], axis)
    return {'x': out['x'], 'mem': out['mem'], 'norm_mix_pre': out['norm_mix_pre'], 'norm_mix_post': out['norm_mix_post'], 'w_in': out['w_in'], 'b_forget': out['b_forget'], 'conv_w': out['conv_w'], 'conv_b': out['conv_b'], 'conv_ln_g': out['conv_ln_g'], 'conv_ln_b': out['conv_ln_b'], 'w_out': out['w_out'], 'norm_mem_pre': out['norm_mem_pre'], 'norm_mem_post': out['norm_mem_post'], 'norm_memkv': out['norm_memkv'], 'w_mq': out['w_mq'], 'w_mk': out['w_mk'], 'w_mv': out['w_mv'], 'w_mo': out['w_mo'], 'norm_mlp_pre': out['norm_mlp_pre'], 'norm_mlp_post': out['norm_mlp_post'], 'w_up': out['w_up'], 'w_down': out['w_down'], 'loss_target': out['loss_target'], 'm_norm_mix_pre': out['m_norm_mix_pre'], 'm_norm_mix_post': out['m_norm_mix_post'], 'm_w_in': out['m_w_in'], 'm_b_forget': out['m_b_forget'], 'm_conv_w': out['m_conv_w'], 'm_conv_b': out['m_conv_b'], 'm_conv_ln_g': out['m_conv_ln_g'], 'm_conv_ln_b': out['m_conv_ln_b'], 'm_w_out': out['m_w_out'], 'm_norm_mem_pre': out['m_norm_mem_pre'], 'm_norm_mem_post': out['m_norm_mem_post'], 'm_norm_memkv': out['m_norm_memkv'], 'm_w_mq': out['m_w_mq'], 'm_w_mk': out['m_w_mk'], 'm_w_mv': out['m_w_mv'], 'm_w_mo': out['m_w_mo'], 'm_norm_mlp_pre': out['m_norm_mlp_pre'], 'm_norm_mlp_post': out['m_norm_mlp_post'], 'm_w_up': out['m_w_up'], 'm_w_down': out['m_w_down'], 'v_norm_mix_pre': out['v_norm_mix_pre'], 'v_norm_mix_post': out['v_norm_mix_post'], 'v_w_in': out['v_w_in'], 'v_b_forget': out['v_b_forget'], 'v_conv_w': out['v_conv_w'], 'v_conv_b': out['v_conv_b'], 'v_conv_ln_g': out['v_conv_ln_g'], 'v_conv_ln_b': out['v_conv_ln_b'], 'v_w_out': out['v_w_out'], 'v_norm_mem_pre': out['v_norm_mem_pre'], 'v_norm_mem_post': out['v_norm_mem_post'], 'v_norm_memkv': out['v_norm_memkv'], 'v_w_mq': out['v_w_mq'], 'v_w_mk': out['v_w_mk'], 'v_w_mv': out['v_w_mv'], 'v_w_mo': out['v_w_mo'], 'v_norm_mlp_pre': out['v_norm_mlp_pre'], 'v_norm_mlp_post': out['v_norm_mlp_post'], 'v_w_up': out['v_w_up'], 'v_w_down': out['v_w_down']}


def _loss(weights, diff, rest, loss_target):
    with _jax.named_scope("forward"):
        args = {**rest, TWIN_DIFF_INPUT: diff, **{k: w.astype(_WEIGHT_DTYPES[k]) for k, w in weights.items()}}
        y = _forward(args)
    with _jax.named_scope("loss_head"):
        err = _jnp.square(y.astype(_jnp.float32) - loss_target)
        return 0.5 * _jnp.sum(_jnp.mean(err, axis=-1)) if err.ndim else 0.5 * err


def _adamw(w, g, m, v):
    m = ADAM_B1 * m + (1.0 - ADAM_B1) * g
    v = ADAM_B2 * v + (1.0 - ADAM_B2) * _jnp.square(g)
    m_hat = m / (1.0 - ADAM_B1 ** ADAM_STEP)
    v_hat = v / (1.0 - ADAM_B2 ** ADAM_STEP)
    delta = -ADAM_LR * (m_hat / (_jnp.sqrt(v_hat) + ADAM_EPS) + ADAM_WD * w)
    return delta, m, v


def reference(x, mem, norm_mix_pre, norm_mix_post, w_in, b_forget, conv_w, conv_b, conv_ln_g, conv_ln_b, w_out, norm_mem_pre, norm_mem_post, norm_memkv, w_mq, w_mk, w_mv, w_mo, norm_mlp_pre, norm_mlp_post, w_up, w_down, loss_target, m_norm_mix_pre, m_norm_mix_post, m_w_in, m_b_forget, m_conv_w, m_conv_b, m_conv_ln_g, m_conv_ln_b, m_w_out, m_norm_mem_pre, m_norm_mem_post, m_norm_memkv, m_w_mq, m_w_mk, m_w_mv, m_w_mo, m_norm_mlp_pre, m_norm_mlp_post, m_w_up, m_w_down, v_norm_mix_pre, v_norm_mix_post, v_w_in, v_b_forget, v_conv_w, v_conv_b, v_conv_ln_g, v_conv_ln_b, v_w_out, v_norm_mem_pre, v_norm_mem_post, v_norm_memkv, v_w_mq, v_w_mk, v_w_mv, v_w_mo, v_norm_mlp_pre, v_norm_mlp_post, v_w_up, v_w_down):
    given = dict(x=x, mem=mem, norm_mix_pre=norm_mix_pre, norm_mix_post=norm_mix_post, w_in=w_in, b_forget=b_forget, conv_w=conv_w, conv_b=conv_b, conv_ln_g=conv_ln_g, conv_ln_b=conv_ln_b, w_out=w_out, norm_mem_pre=norm_mem_pre, norm_mem_post=norm_mem_post, norm_memkv=norm_memkv, w_mq=w_mq, w_mk=w_mk, w_mv=w_mv, w_mo=w_mo, norm_mlp_pre=norm_mlp_pre, norm_mlp_post=norm_mlp_post, w_up=w_up, w_down=w_down, loss_target=loss_target, m_norm_mix_pre=m_norm_mix_pre, m_norm_mix_post=m_norm_mix_post, m_w_in=m_w_in, m_b_forget=m_b_forget, m_conv_w=m_conv_w, m_conv_b=m_conv_b, m_conv_ln_g=m_conv_ln_g, m_conv_ln_b=m_conv_ln_b, m_w_out=m_w_out, m_norm_mem_pre=m_norm_mem_pre, m_norm_mem_post=m_norm_mem_post, m_norm_memkv=m_norm_memkv, m_w_mq=m_w_mq, m_w_mk=m_w_mk, m_w_mv=m_w_mv, m_w_mo=m_w_mo, m_norm_mlp_pre=m_norm_mlp_pre, m_norm_mlp_post=m_norm_mlp_post, m_w_up=m_w_up, m_w_down=m_w_down, v_norm_mix_pre=v_norm_mix_pre, v_norm_mix_post=v_norm_mix_post, v_w_in=v_w_in, v_b_forget=v_b_forget, v_conv_w=v_conv_w, v_conv_b=v_conv_b, v_conv_ln_g=v_conv_ln_g, v_conv_ln_b=v_conv_ln_b, v_w_out=v_w_out, v_norm_mem_pre=v_norm_mem_pre, v_norm_mem_post=v_norm_mem_post, v_norm_memkv=v_norm_memkv, v_w_mq=v_w_mq, v_w_mk=v_w_mk, v_w_mv=v_w_mv, v_w_mo=v_w_mo, v_norm_mlp_pre=v_norm_mlp_pre, v_norm_mlp_post=v_norm_mlp_post, v_w_up=v_w_up, v_w_down=v_w_down)
    weights = {n: given[n] for n in TWIN_WEIGHTS}
    shared = {n: given[n] for n in SHARED_INPUTS}
    per_example = {n: given[n] for n in ['x', 'mem']}
    grad_fn = _jax.value_and_grad(_loss, argnums=(0, 1))

    def one_microbatch(ex, loss_target):
        ex = dict(ex)
        diff = ex.pop(TWIN_DIFF_INPUT)
        return grad_fn(weights, diff, {**shared, **ex}, loss_target)

    if N_MICROBATCH == 1:
        loss, (grad_w, grad_x) = one_microbatch(per_example, given["loss_target"])
    else:
        def body(carry, xs):
            loss_sum, grad_sum = carry
            l_k, (gw_k, gx_k) = one_microbatch(xs[0], xs[1])
            with _jax.named_scope("update"):
                return (loss_sum + l_k, _jax.tree.map(_jnp.add, grad_sum, gw_k)), gx_k

        init = (_jnp.zeros((), _jnp.float32), _jax.tree.map(_jnp.zeros_like, weights))
        (loss, grad_w), grad_x = _jax.lax.scan(body, init, (per_example, given["loss_target"]))
    with _jax.named_scope("update"):
        delta_w, new_m, new_v = {}, {}, {}
        for n in TWIN_WEIGHTS:
            delta_w[n], new_m[n], new_v[n] = _adamw(weights[n], grad_w[n], given["m_" + n], given["v_" + n])
    return (loss, grad_x, *[grad_w[n] for n in TWIN_WEIGHTS], *[delta_w[n] for n in TWIN_WEIGHTS],
            *[new_m[n] for n in TWIN_WEIGHTS], *[new_v[n] for n in TWIN_WEIGHTS])
```

```python
import functools

import jax
import jax.numpy as jnp
from jax import lax
from jax.experimental import pallas as pl
from jax.experimental.pallas import tpu as pltpu

F32 = jnp.float32
BF16 = jnp.bfloat16

N_DEV = 8
DEPTH = 2
D_MODEL = 1024
CONV_CH = 512
CONV_WIDTH = 31
CONV_PAD = 32
FOX_HEADS = 8
FOX_HEAD_DIM = 64
FOX_WIDTH = 512
N_PAIRS = 4
MEM_HEADS = 4
MEM_HEAD_DIM = 128
MEM_INNER = 512
D_FF = 4096
IN_MAIN = 2560
IN_COLS = 2568
IN_CAT = IN_MAIN + 128
LANES = 128
EPS = 1e-6
NEG_INF = -1e30

ADAM_LR = 0.001
ADAM_B1 = 0.9
ADAM_B2 = 0.999
ADAM_EPS = 1e-08
ADAM_WD = 0.01
ADAM_STEP = 10

NT_DIMS = (((1,), (1,)), ((), ()))
TN_DIMS = (((0,), (0,)), ((), ()))

BIG = ("w_in", "w_out", "w_mq", "w_mk", "w_mv", "w_mo", "w_up", "w_down")
BIG_ROWS = {"w_in": 321, "w_out": 128, "w_mq": 64, "w_mk": 64, "w_mv": 64, "w_mo": 64, "w_up": 512, "w_down": 512}
ROWS_PER_LAYER = sum(BIG_ROWS.values())
PACK_TILE = 496
PACK_ROWS = 7 * PACK_TILE
assert PACK_ROWS >= DEPTH * ROWS_PER_LAYER and PACK_TILE % 16 == 0

VEC = ("norm_mix_pre", "norm_mix_post", "norm_mem_pre", "norm_mem_post", "norm_memkv", "norm_mlp_pre", "norm_mlp_post",
       "b_forget", "conv_b", "conv_ln_g", "conv_ln_b")
SMALL_ROWS = 32


def _sigmoid(x):
    return 1.0 / (1.0 + jnp.exp(-x))


def _rms(x, g):
    r = lax.rsqrt(jnp.mean(x * x, axis=-1, keepdims=True) + EPS)
    return x * r * g


def _rms_bwd(x, g, dh):
    r = lax.rsqrt(jnp.mean(x * x, axis=-1, keepdims=True) + EPS)
    gh = dh * g
    c = jnp.mean(gh * x, axis=-1, keepdims=True)
    dx = r * gh - x * (r * r * r * c)
    dg = jnp.sum(dh * (x * r), axis=0, keepdims=True)
    return dx, dg


def _full(shape):
    nd = len(shape)
    return pl.BlockSpec(shape, lambda *_: (0,) * nd)


def _params(*sem):
    return pltpu.CompilerParams(dimension_semantics=sem)


def _rms_matmul(x, g, w, segs, tile, name):
    s_len, d = x.shape
    n = w.shape[1]
    chunk = 512

    def body(x_ref, g_ref, w_ref, h_ref, *outs):
        h = _rms(x_ref[...], g_ref[...]).astype(BF16)
        h_ref[...] = h
        oi = 0
        for c0, c1, fns in segs:
            for a in range(c0, c1, chunk):
                b = min(a + chunk, c1)
                z = jnp.dot(h, w_ref[:, a:b], preferred_element_type=F32)
                for k, (dt, fn) in enumerate(fns):
                    outs[oi + k][:, a - c0:b - c0] = fn(z).astype(dt)
            oi += len(fns)

    out_shape = [jax.ShapeDtypeStruct((s_len, d), BF16)]
    out_specs = [pl.BlockSpec((tile, d), lambda i: (i, 0))]
    for c0, c1, fns in segs:
        for dt, _ in fns:
            out_shape.append(jax.ShapeDtypeStruct((s_len, c1 - c0), dt))
            out_specs.append(pl.BlockSpec((tile, c1 - c0), lambda i: (i, 0)))
    return pl.pallas_call(
        body, name=name, grid=(s_len // tile,),
        in_specs=[pl.BlockSpec((tile, d), lambda i: (i, 0)), _full((1, d)), _full((d, n))],
        out_specs=out_specs, out_shape=out_shape, compiler_params=_params("parallel"),
    )(x, g, w)


def _matmul_resnorm(a, w, x, g, tile, name):
    s_len, k = a.shape
    d = w.shape[1]

    def body(a_ref, w_ref, x_ref, g_ref, y_ref, xo_ref):
        y = jnp.dot(a_ref[...], w_ref[...], preferred_element_type=F32)
        y_ref[...] = y
        xo_ref[...] = x_ref[...] + _rms(y, g_ref[...])

    row = lambda i: (i, 0)
    return pl.pallas_call(
        body, name=name, grid=(s_len // tile,),
        in_specs=[pl.BlockSpec((tile, k), row), _full((k, d)), pl.BlockSpec((tile, d), row), _full((1, d))],
        out_specs=[pl.BlockSpec((tile, d), row), pl.BlockSpec((tile, d), row)],
        out_shape=[jax.ShapeDtypeStruct((s_len, d), F32), jax.ShapeDtypeStruct((s_len, d), F32)],
        compiler_params=_params("parallel"),
    )(a, w, x, g)


def _resnorm_bwd_mm(dx, y, g, w, tile, name, out_dtype, pre=None):
    s_len, d = dx.shape
    k = w.shape[0]
    chunk = 512

    def body(*refs):
        if pre is None:
            dx_ref, y_ref, g_ref, w_ref, dy_ref, da_ref, dg_ref = refs
        else:
            dx_ref, y_ref, g_ref, w_ref, pre_ref, dy_ref, da_ref, dg_ref = refs
        dy, dg = _rms_bwd(y_ref[...], g_ref[...], dx_ref[...])
        dyb = dy.astype(BF16)
        dy_ref[...] = dyb

        @pl.when(pl.program_id(0) == 0)
        def _():
            dg_ref[...] = jnp.zeros_like(dg_ref)

        dg_ref[...] += dg
        for a in range(0, k, chunk):
            b = min(a + chunk, k)
            da = lax.dot_general(dyb, w_ref[a:b, :], NT_DIMS, preferred_element_type=F32)
            if pre is not None:
                da = da * (2.0 * jnp.maximum(pre_ref[:, a:b].astype(F32), 0.0))
            da_ref[:, a:b] = da.astype(out_dtype)

    row = lambda i: (i, 0)
    in_specs = [pl.BlockSpec((tile, d), row), pl.BlockSpec((tile, d), row), _full((1, d)), _full((k, d))]
    args = [dx, y, g, w]
    if pre is not None:
        in_specs.append(pl.BlockSpec((tile, k), row))
        args.append(pre)
    return pl.pallas_call(
        body, name=name, grid=(s_len // tile,), in_specs=in_specs,
        out_specs=[pl.BlockSpec((tile, d), row), pl.BlockSpec((tile, k), row), _full((1, d))],
        out_shape=[jax.ShapeDtypeStruct((s_len, d), BF16), jax.ShapeDtypeStruct((s_len, k), out_dtype),
                   jax.ShapeDtypeStruct((1, d), F32)],
        compiler_params=_params("arbitrary"),
    )(*args)


def _mm_prenorm_bwd(dz, w, x, g, dres, tile, name):
    s_len, n = dz.shape
    d = w.shape[0]

    def body(*refs):
        if dres is None:
            dz_ref, w_ref, x_ref, g_ref, dx_ref, dg_ref = refs
        else:
            dz_ref, w_ref, x_ref, g_ref, dres_ref, dx_ref, dg_ref = refs
        dh = lax.dot_general(dz_ref[...], w_ref[...], NT_DIMS, preferred_element_type=F32)
        dx, dg = _rms_bwd(x_ref[...], g_ref[...], dh)
        if dres is not None:
            dx = dx + dres_ref[...]
        dx_ref[...] = dx

        @pl.when(pl.program_id(0) == 0)
        def _():
            dg_ref[...] = jnp.zeros_like(dg_ref)

        dg_ref[...] += dg

    row = lambda i: (i, 0)
    in_specs = [pl.BlockSpec((tile, n), row), _full((d, n)), pl.BlockSpec((tile, d), row), _full((1, d))]
    args = [dz, w, x, g]
    if dres is not None:
        in_specs.append(pl.BlockSpec((tile, d), row))
        args.append(dres)
    return pl.pallas_call(
        body, name=name, grid=(s_len // tile,), in_specs=in_specs,
        out_specs=[pl.BlockSpec((tile, d), row), _full((1, d))],
        out_shape=[jax.ShapeDtypeStruct((s_len, d), F32), jax.ShapeDtypeStruct((1, d), F32)],
        compiler_params=_params("arbitrary"),
    )(*args)


def _matmul_tn(a, b, tm, tk, name):
    s_len, m = a.shape
    n = b.shape[1]

    def body(a_ref, b_ref, o_ref):
        @pl.when(pl.program_id(1) == 0)
        def _():
            o_ref[...] = jnp.zeros_like(o_ref)

        o_ref[...] += lax.dot_general(a_ref[...], b_ref[...], TN_DIMS, preferred_element_type=F32)

    return pl.pallas_call(
        body, name=name, grid=(m // tm, s_len // tk),
        in_specs=[pl.BlockSpec((tk, tm), lambda i, k: (k, i)), pl.BlockSpec((tk, n), lambda i, k: (k, 0))],
        out_specs=pl.BlockSpec((tm, n), lambda i, k: (i, 0)),
        out_shape=jax.ShapeDtypeStruct((m, n), F32),
        compiler_params=_params("parallel", "arbitrary"),
    )(a, b)


def _cumsum_fwd(fl, b, tile, name):
    s_len = fl.shape[0]

    def body(fl_ref, b_ref, cum_ref, cumt_ref, carry):
        @pl.when(pl.program_id(0) == 0)
        def _():
            carry[...] = jnp.zeros_like(carry)

        xx = fl_ref[...] + b_ref[...]
        lf = jnp.minimum(xx, 0.0) - jnp.log1p(jnp.exp(-jnp.abs(xx)))
        r = lax.broadcasted_iota(jnp.int32, (tile, tile), 0)
        c = lax.broadcasted_iota(jnp.int32, (tile, tile), 1)
        tri = (c <= r).astype(F32)
        cs = jnp.dot(tri, lf, precision=lax.Precision.HIGHEST, preferred_element_type=F32) + carry[...]
        cum_ref[...] = cs
        cumt_ref[...] = cs.T[0:FOX_HEADS, :]
        carry[...] = cs[tile - 1:tile, :]

    return pl.pallas_call(
        body, name=name, grid=(s_len // tile,),
        in_specs=[pl.BlockSpec((tile, LANES), lambda i: (i, 0)), _full((1, LANES))],
        out_specs=[pl.BlockSpec((tile, LANES), lambda i: (i, 0)), pl.BlockSpec((FOX_HEADS, tile), lambda i: (0, i))],
        out_shape=[jax.ShapeDtypeStruct((s_len, LANES), F32), jax.ShapeDtypeStruct((FOX_HEADS, s_len), F32)],
        scratch_shapes=[pltpu.VMEM((1, LANES), F32)],
        compiler_params=_params("arbitrary"),
    )(fl, b)


def _cumsum_bwd(dcum, fl, b, tile, name):
    s_len = fl.shape[0]
    n_t = s_len // tile

    def body(dc_ref, fl_ref, b_ref, dfl_ref, db_ref, carry):
        @pl.when(pl.program_id(0) == 0)
        def _():
            carry[...] = jnp.zeros_like(carry)
            db_ref[...] = jnp.zeros_like(db_ref)

        r = lax.broadcasted_iota(jnp.int32, (tile, tile), 0)
        c = lax.broadcasted_iota(jnp.int32, (tile, tile), 1)
        tri = (c >= r).astype(F32)
        dl = jnp.dot(tri, dc_ref[...], precision=lax.Precision.HIGHEST, preferred_element_type=F32) + carry[...]
        carry[...] = dl[0:1, :]
        dfl = dl * _sigmoid(-(fl_ref[...] + b_ref[...]))
        dfl_ref[...] = dfl
        db_ref[...] += jnp.sum(dfl, axis=0, keepdims=True)

    rev = lambda i: (n_t - 1 - i, 0)
    return pl.pallas_call(
        body, name=name, grid=(n_t,),
        in_specs=[pl.BlockSpec((tile, LANES), rev), pl.BlockSpec((tile, LANES), rev), _full((1, LANES))],
        out_specs=[pl.BlockSpec((tile, LANES), rev), _full((1, LANES))],
        out_shape=[jax.ShapeDtypeStruct((s_len, LANES), F32), jax.ShapeDtypeStruct((1, LANES), F32)],
        scratch_shapes=[pltpu.VMEM((1, LANES), F32)],
        compiler_params=_params("arbitrary"),
    )(dcum, fl, b)


def _conv_taps(w_ref, ext_ref, base, tile, reverse):
    acc = None
    for k in range(CONV_WIDTH):
        off = base + ((CONV_WIDTH - 1 - k) if reverse else k)
        term = w_ref[k:k + 1, :] * ext_ref[pl.ds(off, tile), :]
        acc = term if acc is None else acc + term
    return acc


def _conv_fwd(ag, w, cb, lg, lb, tile, name):
    s_len = ag.shape[0]
    c = CONV_CH

    def body(ag_ref, w_ref, cb_ref, lg_ref, lb_ref, u_ref, ext):
        @pl.when(pl.program_id(0) == 0)
        def _():
            ext[0:CONV_PAD, :] = jnp.zeros((CONV_PAD, c), F32)

        ext[CONV_PAD:CONV_PAD + tile, :] = ag_ref[:, 0:c] * _sigmoid(ag_ref[:, c:2 * c])
        u1 = _conv_taps(w_ref, ext, CONV_PAD - (CONV_WIDTH - 1), tile, False) + cb_ref[...]
        mu = jnp.mean(u1, axis=-1, keepdims=True)
        xc = u1 - mu
        y = xc * lax.rsqrt(jnp.mean(xc * xc, axis=-1, keepdims=True) + EPS) * lg_ref[...] + lb_ref[...]
        u_ref[...] = (y * _sigmoid(y)).astype(BF16)
        ext[0:CONV_PAD, :] = ext[tile:tile + CONV_PAD, :]

    return pl.pallas_call(
        body, name=name, grid=(s_len // tile,),
        in_specs=[pl.BlockSpec((tile, 2 * c), lambda i: (i, 0)), _full((CONV_PAD, c)), _full((1, c)), _full((1, c)),
                  _full((1, c))],
        out_specs=pl.BlockSpec((tile, c), lambda i: (i, 0)),
        out_shape=jax.ShapeDtypeStruct((s_len, c), BF16),
        scratch_shapes=[pltpu.VMEM((tile + CONV_PAD, c), F32)],
        compiler_params=_params("arbitrary"),
    )(ag, w, cb, lg, lb)


def _conv_bwd(ag, dcat, w, cb, lg, lb, tile, name):
    s_len = ag.shape[0]
    c = CONV_CH
    n_t = s_len // tile
    per = tile // CONV_PAD

    def body(ag_ref, halo_ref, du_ref, w_ref, cb_ref, lg_ref, lb_ref, dag_ref, dw_ref, dv_ref, ext, ext2):
        i = pl.program_id(0)
        t = n_t - 1 - i

        @pl.when(i == 0)
        def _():
            ext2[tile:tile + CONV_PAD, :] = jnp.zeros((CONV_PAD, c), F32)
            dw_ref[...] = jnp.zeros_like(dw_ref)
            dv_ref[...] = jnp.zeros_like(dv_ref)

        a = ag_ref[:, 0:c]
        sg = _sigmoid(ag_ref[:, c:2 * c])
        halo = halo_ref[:, 0:c] * _sigmoid(halo_ref[:, c:2 * c])
        ext[0:CONV_PAD, :] = jnp.where(t > 0, halo, 0.0)
        ext[CONV_PAD:CONV_PAD + tile, :] = a * sg
        u1 = _conv_taps(w_ref, ext, CONV_PAD - (CONV_WIDTH - 1), tile, False) + cb_ref[...]
        mu = jnp.mean(u1, axis=-1, keepdims=True)
        xc = u1 - mu
        rs = lax.rsqrt(jnp.mean(xc * xc, axis=-1, keepdims=True) + EPS)
        xhat = xc * rs
        y = xhat * lg_ref[...] + lb_ref[...]
        sy = _sigmoid(y)
        dy = du_ref[...] * (sy * (1.0 + y * (1.0 - sy)))
        dxh = dy * lg_ref[...]
        du1 = rs * (dxh - jnp.mean(dxh, axis=-1, keepdims=True) - xhat * jnp.mean(dxh * xhat, axis=-1, keepdims=True))
        dv_ref[0:1, :] += jnp.sum(du1, axis=0, keepdims=True)
        dv_ref[1:2, :] += jnp.sum(dy * xhat, axis=0, keepdims=True)
        dv_ref[2:3, :] += jnp.sum(dy, axis=0, keepdims=True)
        for k in range(CONV_WIDTH):
            off = CONV_PAD - (CONV_WIDTH - 1) + k
            dw_ref[k:k + 1, :] += jnp.sum(du1 * ext[pl.ds(off, tile), :], axis=0, keepdims=True)
        ext2[0:tile, :] = du1
        du0 = _conv_taps(w_ref, ext2, 0, tile, True)
        ext2[tile:tile + CONV_PAD, :] = du1[0:CONV_PAD, :]
        dag_ref[:, 0:c] = (du0 * sg).astype(BF16)
        dag_ref[:, c:2 * c] = (du0 * a * sg * (1.0 - sg)).astype(BF16)

    rev = lambda i: (n_t - 1 - i, 0)
    return pl.pallas_call(
        body, name=name, grid=(n_t,),
        in_specs=[pl.BlockSpec((tile, 2 * c), rev),
                  pl.BlockSpec((CONV_PAD, 2 * c), lambda i: (jnp.maximum((n_t - 1 - i) * per - 1, 0), 0)),
                  pl.BlockSpec((tile, c), rev), _full((CONV_PAD, c)), _full((1, c)), _full((1, c)), _full((1, c))],
        out_specs=[pl.BlockSpec((tile, 2 * c), rev), _full((CONV_PAD, c)), _full((8, c))],
        out_shape=[jax.ShapeDtypeStruct((s_len, 2 * c), BF16), jax.ShapeDtypeStruct((CONV_PAD, c), F32),
                   jax.ShapeDtypeStruct((8, c), F32)],
        scratch_shapes=[pltpu.VMEM((tile + CONV_PAD, c), F32), pltpu.VMEM((tile + CONV_PAD, c), F32)],
        compiler_params=_params("arbitrary"),
    )(ag, ag, dcat, w, cb, lg, lb)


def _lane_masks():
    lane = lax.broadcasted_iota(jnp.int32, (1, LANES), 1)
    return lane, (lane < FOX_HEAD_DIM, lane >= FOX_HEAD_DIM)


def _head_col(x, lane, h):
    return jnp.sum(jnp.where(lane == h, x, 0.0), axis=1, keepdims=True)


def _fox_fwd(qkv, cum, cumt, ta, name):
    s_len = qkv.shape[0]
    n_t = s_len // ta
    scale = FOX_HEAD_DIM ** -0.5

    def body(q_ref, k_ref, v_ref, cq_ref, ct_ref, o_ref, lse_ref):
        p = pl.program_id(0)
        i = pl.program_id(1)
        lane, masks = _lane_masks()
        q = q_ref[...]
        qh = [jnp.where(m, q, jnp.zeros_like(q)) for m in masks]
        cq = cq_ref[...]
        cqc = [_head_col(cq, lane, 2 * p + hh) for hh in range(2)]
        r = lax.broadcasted_iota(jnp.int32, (ta, ta), 0)
        c = lax.broadcasted_iota(jnp.int32, (ta, ta), 1)
        causal = c <= r

        def make_step(masked):
            def step(j, carry):
                off = pl.multiple_of(j * ta, ta)
                kt = k_ref[pl.ds(off, ta), :]
                vt = v_ref[pl.ds(off, ta), :]
                new = []
                for hh in range(2):
                    m, l, acc = carry[3 * hh:3 * hh + 3]
                    s = lax.dot_general(qh[hh], kt, NT_DIMS, preferred_element_type=F32) * scale
                    s = s + cqc[hh] - ct_ref[hh, j]
                    if masked:
                        s = jnp.where(causal, s, NEG_INF)
                    m_new = jnp.maximum(m, jnp.max(s, axis=1, keepdims=True))
                    alpha = jnp.exp(m - m_new)
                    pr = jnp.exp(s - m_new)
                    l = alpha * l + jnp.sum(pr, axis=1, keepdims=True)
                    acc = alpha * acc + jnp.dot(pr.astype(BF16), vt, preferred_element_type=F32)
                    new += [m_new, l, acc]
                return tuple(new)
            return step

        init = (jnp.full((ta, 1), NEG_INF, F32), jnp.zeros((ta, 1), F32), jnp.zeros((ta, LANES), F32)) * 2
        carry = lax.fori_loop(0, i, make_step(False), init)
        carry = make_step(True)(i, carry)
        m0, l0, a0, m1, l1, a1 = carry
        o_ref[...] = jnp.where(masks[0], a0 / l0, a1 / l1)
        lse_ref[...] = jnp.where(masks[0], m0 + jnp.log(l0), m1 + jnp.log(l1))

    blk = lambda col0: pl.BlockSpec((s_len, LANES), lambda p, i: (0, col0 + p))
    tile_spec = pl.BlockSpec((ta, LANES), lambda p, i: (i, p))
    return pl.pallas_call(
        body, name=name, grid=(N_PAIRS, n_t),
        in_specs=[tile_spec, blk(N_PAIRS), blk(2 * N_PAIRS), pl.BlockSpec((ta, LANES), lambda p, i: (i, 0)),
                  pl.BlockSpec((2, n_t, 1, ta), lambda p, i: (p, 0, 0, 0))],
        out_specs=[tile_spec, tile_spec],
        out_shape=[jax.ShapeDtypeStruct((s_len, FOX_WIDTH), F32), jax.ShapeDtypeStruct((s_len, FOX_WIDTH), F32)],
        compiler_params=_params("parallel", "parallel"),
    )(qkv, qkv, qkv, cum, cumt)


def _fox_dq(qkv, cum, cumt, o, dcat, lse, ta, name):
    s_len = qkv.shape[0]
    n_t = s_len // ta
    scale = FOX_HEAD_DIM ** -0.5

    def body(q_ref, k_ref, v_ref, cq_ref, ct_ref, o_ref, do_ref, lse_ref, dq_ref, dl_ref, dck_ref):
        p = pl.program_id(0)
        i = pl.program_id(1)
        lane, masks = _lane_masks()

        @pl.when(i == 0)
        def _():
            dck_ref[...] = jnp.zeros_like(dck_ref)

        q = q_ref[...]
        do = do_ref[...]
        prod = do * o_ref[...]
        dlt = [jnp.sum(jnp.where(m, prod, 0.0), axis=1, keepdims=True) for m in masks]
        dl_ref[...] = jnp.where(masks[0], dlt[0], dlt[1])
        dob = do.astype(BF16)
        qh = [jnp.where(m, q, jnp.zeros_like(q)) for m in masks]
        doh = [jnp.where(m, dob, jnp.zeros_like(dob)) for m in masks]
        lse = lse_ref[...]
        lse_h = [lse[:, 0:1], lse[:, FOX_HEAD_DIM:FOX_HEAD_DIM + 1]]
        cq = cq_ref[...]
        cqc = [_head_col(cq, lane, 2 * p + hh) for hh in range(2)]
        r = lax.broadcasted_iota(jnp.int32, (ta, ta), 0)
        c = lax.broadcasted_iota(jnp.int32, (ta, ta), 1)
        causal = c <= r

        def make_step(masked):
            def step(j, carry):
                off = pl.multiple_of(j * ta, ta)
                kt = k_ref[pl.ds(off, ta), :]
                vt = v_ref[pl.ds(off, ta), :]
                new = []
                for hh in range(2):
                    s = lax.dot_general(qh[hh], kt, NT_DIMS, preferred_element_type=F32) * scale
                    s = s + cqc[hh] - ct_ref[hh, j]
                    pr = jnp.exp(s - lse_h[hh])
                    if masked:
                        pr = jnp.where(causal, pr, 0.0)
                    dp = lax.dot_general(doh[hh], vt, NT_DIMS, preferred_element_type=F32)
                    ds = pr * (dp - dlt[hh])
                    dck_ref[hh, j] = dck_ref[hh, j] - jnp.sum(ds, axis=0, keepdims=True)
                    new.append(carry[hh] + jnp.dot(ds.astype(BF16), kt, preferred_element_type=F32))
                return tuple(new)
            return step

        init = (jnp.zeros((ta, LANES), F32),) * 2
        carry = lax.fori_loop(0, i, make_step(False), init)
        carry = make_step(True)(i, carry)
        dq_ref[...] = (jnp.where(masks[0], carry[0], carry[1]) * scale).astype(BF16)

    blk = lambda col0: pl.BlockSpec((s_len, LANES), lambda p, i: (0, col0 + p))
    tile_spec = pl.BlockSpec((ta, LANES), lambda p, i: (i, p))
    row4 = pl.BlockSpec((2, n_t, 1, ta), lambda p, i: (p, 0, 0, 0))
    return pl.pallas_call(
        body, name=name, grid=(N_PAIRS, n_t),
        in_specs=[tile_spec, blk(N_PAIRS), blk(2 * N_PAIRS), pl.BlockSpec((ta, LANES), lambda p, i: (i, 0)), row4,
                  tile_spec, pl.BlockSpec((ta, LANES), lambda p, i: (i, N_PAIRS + p)), tile_spec],
        out_specs=[tile_spec, tile_spec, row4],
        out_shape=[jax.ShapeDtypeStruct((s_len, FOX_WIDTH), BF16), jax.ShapeDtypeStruct((s_len, FOX_WIDTH), F32),
                   jax.ShapeDtypeStruct((FOX_HEADS, n_t, 1, ta), F32)],
        compiler_params=_params("parallel", "arbitrary"),
    )(qkv, qkv, qkv, cum, cumt, o, dcat, lse)


def _fox_dkv(qkv, cum, cumt, dcat, lset, deltat, ta, name):
    s_len = qkv.shape[0]
    n_t = s_len // ta
    scale = FOX_HEAD_DIM ** -0.5

    def body(k_ref, v_ref, q_ref, do_ref, ck_ref, ct_ref, lse_ref, dl_ref, dk_ref, dv_ref, dcq_ref):
        p = pl.program_id(0)
        j = pl.program_id(1)
        lane, masks = _lane_masks()

        @pl.when(j == 0)
        def _():
            dcq_ref[...] = jnp.zeros_like(dcq_ref)
        k = k_ref[...]
        v = v_ref[...]
        kh = [jnp.where(m, k, jnp.zeros_like(k)) for m in masks]
        vh = [jnp.where(m, v, jnp.zeros_like(v)) for m in masks]
        ck = ck_ref[...]
        ckc = [_head_col(ck, lane, 2 * p + hh) for hh in range(2)]
        r = lax.broadcasted_iota(jnp.int32, (ta, ta), 0)
        c = lax.broadcasted_iota(jnp.int32, (ta, ta), 1)
        causal = r <= c

        def make_step(masked):
            def step(i, carry):
                off = pl.multiple_of(i * ta, ta)
                qt = q_ref[pl.ds(off, ta), :]
                dot = do_ref[pl.ds(off, ta), :].astype(BF16)
                new = []
                for hh in range(2):
                    dk, dv = carry[2 * hh:2 * hh + 2]
                    st = lax.dot_general(kh[hh], qt, NT_DIMS, preferred_element_type=F32) * scale
                    st = st + ct_ref[hh, i] - ckc[hh]
                    pt = jnp.exp(st - lse_ref[hh, i])
                    if masked:
                        pt = jnp.where(causal, pt, 0.0)
                    dv = dv + jnp.dot(pt.astype(BF16), dot, preferred_element_type=F32)
                    dpt = lax.dot_general(vh[hh], dot, NT_DIMS, preferred_element_type=F32)
                    dst = pt * (dpt - dl_ref[hh, i])
                    dcq_ref[hh, i] = dcq_ref[hh, i] + jnp.sum(dst, axis=0, keepdims=True)
                    dk = dk + jnp.dot(dst.astype(BF16), qt, preferred_element_type=F32)
                    new += [dk, dv]
                return tuple(new)
            return step

        init = (jnp.zeros((ta, LANES), F32),) * 4
        carry = make_step(True)(j, init)
        carry = lax.fori_loop(j + 1, n_t, make_step(False), carry)
        dk_ref[...] = (jnp.where(masks[0], carry[0], carry[2]) * scale).astype(BF16)
        dv_ref[...] = jnp.where(masks[0], carry[1], carry[3]).astype(BF16)

    row4 = pl.BlockSpec((2, n_t, 1, ta), lambda p, j: (p, 0, 0, 0))
    out_spec = pl.BlockSpec((ta, LANES), lambda p, j: (j, p))
    return pl.pallas_call(
        body, name=name, grid=(N_PAIRS, n_t),
        in_specs=[pl.BlockSpec((ta, LANES), lambda p, j: (j, N_PAIRS + p)),
                  pl.BlockSpec((ta, LANES), lambda p, j: (j, 2 * N_PAIRS + p)),
                  pl.BlockSpec((s_len, LANES), lambda p, j: (0, p)),
                  pl.BlockSpec((s_len, LANES), lambda p, j: (0, N_PAIRS + p)),
                  pl.BlockSpec((ta, LANES), lambda p, j: (j, 0)), row4, row4, row4],
        out_specs=[out_spec, out_spec, row4],
        out_shape=[jax.ShapeDtypeStruct((s_len, FOX_WIDTH), BF16), jax.ShapeDtypeStruct((s_len, FOX_WIDTH), BF16),
                   jax.ShapeDtypeStruct((FOX_HEADS, n_t, 1, ta), F32)],
        compiler_params=_params("parallel", "arbitrary"),
    )(qkv, qkv, qkv, dcat, cum, cumt, lset, deltat)


def _mem_scores_t(q, kv, h):
    lo = h * MEM_HEAD_DIM
    st = lax.dot_general(kv[:, lo:lo + MEM_HEAD_DIM], q[:, lo:lo + MEM_HEAD_DIM], NT_DIMS,
                         preferred_element_type=F32) * (MEM_HEAD_DIM ** -0.5)
    e = jnp.exp(st - jnp.max(st, axis=0, keepdims=True))
    return e / jnp.sum(e, axis=0, keepdims=True)


def _memattn_fwd(q, kv, tile, name):
    s_len = q.shape[0]
    n_mem = kv.shape[0]

    def body(q_ref, kv_ref, o_ref):
        q = q_ref[...]
        kv = kv_ref[...]
        for h in range(MEM_HEADS):
            lo = h * MEM_HEAD_DIM
            pt = _mem_scores_t(q, kv, h).astype(BF16)
            vh = kv[:, MEM_INNER + lo:MEM_INNER + lo + MEM_HEAD_DIM]
            o_ref[:, lo:lo + MEM_HEAD_DIM] = lax.dot_general(pt, vh, TN_DIMS, preferred_element_type=F32).astype(BF16)

    return pl.pallas_call(
        body, name=name, grid=(s_len // tile,),
        in_specs=[pl.BlockSpec((tile, MEM_INNER), lambda i: (i, 0)), _full((n_mem, 2 * MEM_INNER))],
        out_specs=pl.BlockSpec((tile, MEM_INNER), lambda i: (i, 0)),
        out_shape=jax.ShapeDtypeStruct((s_len, MEM_INNER), BF16),
        compiler_params=_params("parallel"),
    )(q, kv)


def _memattn_bwd(q, kv, do, tile, name):
    s_len = q.shape[0]
    n_mem = kv.shape[0]
    scale = MEM_HEAD_DIM ** -0.5

    def body(q_ref, kv_ref, do_ref, dq_ref, dkv_ref):
        @pl.when(pl.program_id(0) == 0)
        def _():
            dkv_ref[...] = jnp.zeros_like(dkv_ref)

        q = q_ref[...]
        kv = kv_ref[...]
        do = do_ref[...]
        for h in range(MEM_HEADS):
            lo = h * MEM_HEAD_DIM
            qh = q[:, lo:lo + MEM_HEAD_DIM]
            kh = kv[:, lo:lo + MEM_HEAD_DIM]
            vh = kv[:, MEM_INNER + lo:MEM_INNER + lo + MEM_HEAD_DIM]
            doh = do[:, lo:lo + MEM_HEAD_DIM]
            pt = _mem_scores_t(q, kv, h)
            dkv_ref[:, MEM_INNER + lo:MEM_INNER + lo + MEM_HEAD_DIM] += jnp.dot(
                pt.astype(BF16), doh, preferred_element_type=F32)
            dpt = lax.dot_general(vh, doh, NT_DIMS, preferred_element_type=F32)
            dst = (pt * (dpt - jnp.sum(pt * dpt, axis=0, keepdims=True)) * scale).astype(BF16)
            dkv_ref[:, lo:lo + MEM_HEAD_DIM] += jnp.dot(dst, qh, preferred_element_type=F32)
            dq_ref[:, lo:lo + MEM_HEAD_DIM] = lax.dot_general(dst, kh, TN_DIMS, preferred_element_type=F32).astype(BF16)

    return pl.pallas_call(
        body, name=name, grid=(s_len // tile,),
        in_specs=[pl.BlockSpec((tile, MEM_INNER), lambda i: (i, 0)), _full((n_mem, 2 * MEM_INNER)),
                  pl.BlockSpec((tile, MEM_INNER), lambda i: (i, 0))],
        out_specs=[pl.BlockSpec((tile, MEM_INNER), lambda i: (i, 0)), _full((n_mem, 2 * MEM_INNER))],
        out_shape=[jax.ShapeDtypeStruct((s_len, MEM_INNER), BF16), jax.ShapeDtypeStruct((n_mem, 2 * MEM_INNER), F32)],
        compiler_params=_params("arbitrary"),
    )(q, kv, do)


def _loss_head(y, target, tile, name):
    s_len, d = y.shape

    def body(y_ref, t_ref, dy_ref, l_ref):
        @pl.when(pl.program_id(0) == 0)
        def _():
            l_ref[...] = jnp.zeros_like(l_ref)

        err = y_ref[...] - t_ref[...]
        dy_ref[...] = err * (1.0 / d)
        l_ref[...] += jnp.sum(err * err, axis=0, keepdims=True) * (0.5 / d)

    row = lambda i: (i, 0)
    return pl.pallas_call(
        body, name=name, grid=(s_len // tile,),
        in_specs=[pl.BlockSpec((tile, d), row), pl.BlockSpec((tile, d), row)],
        out_specs=[pl.BlockSpec((tile, d), row), _full((1, d))],
        out_shape=[jax.ShapeDtypeStruct((s_len, d), F32), jax.ShapeDtypeStruct((1, d), F32)],
        compiler_params=_params("arbitrary"),
    )(y, target)


def _attn_tile(s_len):
    return min(256, s_len)


def _rows4(a8, ta):
    return a8.reshape(FOX_HEADS, a8.shape[1] // ta, 1, ta)


def _head_rows(x, ta):
    return _rows4(x[:, ::FOX_HEAD_DIM].T, ta)


def _layer_fwd(x0, mem, w, l):
    s_len = x0.shape[0]
    tile = min(512, s_len)
    tile_ff = min(256, s_len)
    ta = _attn_tile(s_len)
    ident = lambda z: z
    sv = {"x0": x0}

    h1, ag, qkv, fl = _rms_matmul(
        x0, w["norm_mix_pre"], w["w_in_cat"],
        [(0, 2 * CONV_CH, [(F32, ident)]), (2 * CONV_CH, IN_MAIN, [(BF16, ident)]), (IN_MAIN, IN_CAT, [(F32, ident)])],
        tile, f"mix_in_{l}")
    cum, cum8 = _cumsum_fwd(fl, w["b_forget"], tile, f"cumsum_fwd_{l}")
    cumt = _rows4(cum8, ta)
    u3 = _conv_fwd(ag, w["conv_w"], w["conv_b"], w["conv_ln_g"], w["conv_ln_b"], tile, f"conv_fwd_{l}")
    o, lse = _fox_fwd(qkv, cum, cumt, ta, f"fox_fwd_{l}")
    cat = jnp.concatenate([u3, o.astype(BF16)], axis=1)
    y1, x1 = _matmul_resnorm(cat, w["w_out"], x0, w["norm_mix_post"], tile, f"mix_out_{l}")
    sv.update(h1=h1, ag=ag, qkv=qkv, fl=fl, cum=cum, cumt=cumt, o=o, lse=lse, cat=cat, y1=y1, x1=x1)

    h2, qm = _rms_matmul(x1, w["norm_mem_pre"], w["w_mq"], [(0, MEM_INNER, [(BF16, ident)])], tile, f"mem_q_{l}")
    mem_n, kv = _rms_matmul(mem, w["norm_memkv"], w["w_mkv"], [(0, 2 * MEM_INNER, [(BF16, ident)])],
                            mem.shape[0], f"mem_kv_{l}")
    om = _memattn_fwd(qm, kv, tile, f"mem_attn_fwd_{l}")
    y2, x2 = _matmul_resnorm(om, w["w_mo"], x1, w["norm_mem_post"], tile, f"mem_out_{l}")
    sv.update(h2=h2, qm=qm, mem_n=mem_n, kv=kv, om=om, y2=y2, x2=x2)

    relu2 = lambda z: jnp.square(jnp.maximum(z, 0.0))
    h3, pre, hid = _rms_matmul(x2, w["norm_mlp_pre"], w["w_up"], [(0, D_FF, [(BF16, ident), (BF16, relu2)])],
                               tile_ff, f"mlp_up_{l}")
    y3, x3 = _matmul_resnorm(hid, w["w_down"], x2, w["norm_mlp_post"], tile_ff, f"mlp_down_{l}")
    sv.update(h3=h3, pre=pre, hid=hid, y3=y3)
    return x3, sv


def _layer_bwd(dx3, mem, w, sv, l):
    s_len = dx3.shape[0]
    tile = min(512, s_len)
    tile_ff = min(256, s_len)
    ta = _attn_tile(s_len)
    tk = min(512, s_len)
    g = {}

    dy3, dpre, g["norm_mlp_post"] = _resnorm_bwd_mm(dx3, sv["y3"], w["norm_mlp_post"], w["w_down"], tile_ff,
                                                    f"mlp_down_bwd_{l}", BF16, pre=sv["pre"])
    g["w_down"] = _matmul_tn(sv["hid"], dy3, 512, tk, f"dw_down_{l}")
    dx2, g["norm_mlp_pre"] = _mm_prenorm_bwd(dpre, w["w_up"], sv["x2"], w["norm_mlp_pre"], dx3, tile_ff,
                                             f"mlp_up_bwd_{l}")
    g["w_up"] = _matmul_tn(sv["h3"], dpre, 256, tk, f"dw_up_{l}")

    dy2, dom, g["norm_mem_post"] = _resnorm_bwd_mm(dx2, sv["y2"], w["norm_mem_post"], w["w_mo"], tile,
                                                   f"mem_out_bwd_{l}", BF16)
    g["w_mo"] = _matmul_tn(sv["om"], dy2, 512, tk, f"dw_mo_{l}")
    dqm, dkv = _memattn_bwd(sv["qm"], sv["kv"], dom, tile, f"mem_attn_bwd_{l}")
    dkvb = dkv.astype(BF16)
    g["w_mq"] = _matmul_tn(sv["h2"], dqm, 512, tk, f"dw_mq_{l}")
    dx1, g["norm_mem_pre"] = _mm_prenorm_bwd(dqm, w["w_mq"], sv["x1"], w["norm_mem_pre"], dx2, tile, f"mem_q_bwd_{l}")
    n_mem = mem.shape[0]
    _, g["norm_memkv"] = _mm_prenorm_bwd(dkvb, w["w_mkv"], mem, w["norm_memkv"], None, n_mem, f"mem_kv_bwd_{l}")
    dw_mkv = _matmul_tn(sv["mem_n"], dkvb, 512, n_mem, f"dw_mkv_{l}")
    g["w_mk"] = dw_mkv[:, :MEM_INNER]
    g["w_mv"] = dw_mkv[:, MEM_INNER:]

    dy1, dcat, g["norm_mix_post"] = _resnorm_bwd_mm(dx1, sv["y1"], w["norm_mix_post"], w["w_out"], tile,
                                                    f"mix_out_bwd_{l}", F32)
    g["w_out"] = _matmul_tn(sv["cat"], dy1, 512, tk, f"dw_out_{l}")
    dq, delta, dck = _fox_dq(sv["qkv"], sv["cum"], sv["cumt"], sv["o"], dcat, sv["lse"], ta, f"fox_dq_{l}")
    dk, dv, dcq = _fox_dkv(sv["qkv"], sv["cum"], sv["cumt"], dcat, _head_rows(sv["lse"], ta), _head_rows(delta, ta),
                           ta, f"fox_dkv_{l}")
    dcum = jnp.pad((dcq + dck).reshape(FOX_HEADS, s_len).T, ((0, 0), (0, LANES - FOX_HEADS)))
    dfl, db = _cumsum_bwd(dcum, sv["fl"], w["b_forget"], tile, f"cumsum_bwd_{l}")
    g["b_forget"] = db[:, :FOX_HEADS]
    dag, dconv_w, dconv_v = _conv_bwd(sv["ag"], dcat, w["conv_w"], w["conv_b"], w["conv_ln_g"], w["conv_ln_b"], tile,
                                      f"conv_bwd_{l}")
    g["conv_w"] = dconv_w[:CONV_WIDTH]
    g["conv_b"], g["conv_ln_g"], g["conv_ln_b"] = dconv_v[0:1], dconv_v[1:2], dconv_v[2:3]
    dz = jnp.concatenate([dag, dq, dk, dv, dfl.astype(BF16)], axis=1)
    dx0, g["norm_mix_pre"] = _mm_prenorm_bwd(dz, w["w_in_cat"], sv["x0"], w["norm_mix_pre"], dx1, tile, f"mix_in_bwd_{l}")
    g["w_in"] = _matmul_tn(sv["h1"], dz, 256, tk, f"dw_in_{l}")[:, :IN_COLS]
    return dx0, g


def _local_step(x, mem, target, wl):
    saved = []
    h = x
    for l in range(DEPTH):
        h, sv = _layer_fwd(h, mem, wl[l], l)
        saved.append(sv)
    dh, loss_row = _loss_head(h, target, min(512, x.shape[0]), "loss_head")
    grads = [None] * DEPTH
    for l in reversed(range(DEPTH)):
        dh, grads[l] = _layer_bwd(dh, mem, wl[l], saved[l], l)
    return loss_row, dh, grads


def _exchange(src, same_block, name):
    rows, cols = src.shape[-2:]

    def body(src_ref, out_ref, send_sems, recv_sems, local_sem):
        x, y, c = lax.axis_index("x"), lax.axis_index("y"), lax.axis_index("c")
        me = 4 * x + 2 * y + c

        def mine(idx):
            return src_ref if same_block else src_ref.at[idx]

        local = pltpu.make_async_copy(mine(me), out_ref.at[me], local_sem)
        local.start()
        sends, recvs = [], []
        for k in range(1, N_DEV):
            px = 1 - x if k & 4 else x
            py = 1 - y if k & 2 else y
            pc = 1 - c if k & 1 else c
            peer = 4 * px + 2 * py + pc
            sends.append(pltpu.make_async_remote_copy(
                src_ref=mine(peer), dst_ref=out_ref.at[me], send_sem=send_sems.at[k - 1], recv_sem=recv_sems.at[k - 1],
                device_id=(px, py, pc), device_id_type=pl.DeviceIdType.MESH))
            recvs.append(pltpu.make_async_remote_copy(
                src_ref=mine(peer), dst_ref=out_ref.at[peer], send_sem=send_sems.at[k - 1], recv_sem=recv_sems.at[k - 1],
                device_id=(px, py, pc), device_id_type=pl.DeviceIdType.MESH))
        for cp in sends:
            cp.start()
        for cp in recvs:
            cp.wait_recv()
        for cp in sends:
            cp.wait_send()
        local.wait()

    return pl.pallas_call(
        body, name=name,
        in_specs=[pl.BlockSpec(memory_space=pltpu.HBM)],
        out_specs=pl.BlockSpec(memory_space=pltpu.HBM),
        out_shape=jax.ShapeDtypeStruct((N_DEV, rows, cols), src.dtype),
        scratch_shapes=[pltpu.SemaphoreType.DMA((N_DEV - 1,)), pltpu.SemaphoreType.DMA((N_DEV - 1,)),
                        pltpu.SemaphoreType.DMA],
    )(src)


def _sum_blocks(a, name):
    n, rows, cols = a.shape

    def body(a_ref, o_ref):
        acc = a_ref[0]
        for j in range(1, n):
            acc = acc + a_ref[j]
        o_ref[...] = acc

    return pl.pallas_call(
        body, name=name, in_specs=[_full((n, rows, cols))], out_specs=_full((rows, cols)),
        out_shape=jax.ShapeDtypeStruct((rows, cols), F32), grid=(1,),
    )(a)


def _adamw(gparts, w, m, v, tile, name):
    n, rows, cols = gparts.shape
    c1 = 1.0 - ADAM_B1
    c2 = 1.0 - ADAM_B2
    bc1 = 1.0 - ADAM_B1 ** ADAM_STEP
    bc2 = 1.0 - ADAM_B2 ** ADAM_STEP

    def body(gp_ref, w_ref, m_ref, v_ref, g_ref, d_ref, mo_ref, vo_ref):
        g = gp_ref[0].astype(F32)
        for j in range(1, n):
            g = g + gp_ref[j].astype(F32)
        g_ref[...] = g
        m_new = ADAM_B1 * m_ref[...] + c1 * g
        v_new = ADAM_B2 * v_ref[...] + c2 * (g * g)
        mo_ref[...] = m_new
        vo_ref[...] = v_new
        d_ref[...] = -ADAM_LR * ((m_new / bc1) / (jnp.sqrt(v_new / bc2) + ADAM_EPS) + ADAM_WD * w_ref[...])

    row = lambda i: (i, 0)
    spec = pl.BlockSpec((tile, cols), row)
    return pl.pallas_call(
        body, name=name, grid=(rows // tile,),
        in_specs=[pl.BlockSpec((n, tile, cols), lambda i: (0, i, 0)), spec, spec, spec],
        out_specs=[spec] * 4, out_shape=[jax.ShapeDtypeStruct((rows, cols), F32)] * 4,
        compiler_params=_params("parallel"),
    )(gparts, w, m, v)


def _pack_rows(parts, total_rows):
    flat = [p.reshape(-1, D_MODEL) for p in parts]
    used = sum(f.shape[0] for f in flat)
    if total_rows > used:
        flat.append(jnp.zeros((total_rows - used, D_MODEL), flat[0].dtype))
    return jnp.concatenate(flat, axis=0)


def _pack_big_shards(p, prefix, dtype):
    return _pack_rows([p[prefix + n][l].astype(dtype) for l in range(DEPTH) for n in BIG], PACK_ROWS)


def _unpack_big_shards(buf, shapes):
    out = {n: [] for n in BIG}
    off = 0
    for l in range(DEPTH):
        for n in BIG:
            out[n].append(buf[off:off + BIG_ROWS[n]].reshape(shapes[n]))
            off += BIG_ROWS[n]
    return {n: jnp.stack(v) for n, v in out.items()}


SHARD_AXIS = {"w_in": 1, "w_out": 0, "w_mq": 0, "w_mk": 0, "w_mv": 0, "w_mo": 1, "w_up": 1, "w_down": 0}


def _full_from_blocks(blocks, shard_shape, axis):
    sh = blocks.reshape((N_DEV,) + tuple(shard_shape))
    if axis == 0:
        return sh.reshape(N_DEV * shard_shape[0], shard_shape[1])
    return sh.transpose(1, 0, 2).reshape(shard_shape[0], N_DEV * shard_shape[1])


def _blocks_from_full(full, shard_shape, axis):
    if axis == 0:
        sh = full.reshape((N_DEV,) + tuple(shard_shape))
    else:
        sh = full.reshape(shard_shape[0], N_DEV, shard_shape[1]).transpose(1, 0, 2)
    return sh.reshape(N_DEV, -1, D_MODEL)


def _pad_lanes(v, n=D_MODEL):
    v = v.reshape(-1, v.shape[-1])
    return jnp.pad(v, ((0, 0), (0, n - v.shape[-1])))


def kernel(x, mem, norm_mix_pre, norm_mix_post, w_in, b_forget, conv_w, conv_b, conv_ln_g, conv_ln_b, w_out, norm_mem_pre, norm_mem_post, norm_memkv, w_mq, w_mk, w_mv, w_mo, norm_mlp_pre, norm_mlp_post, w_up, w_down, loss_target, m_norm_mix_pre, m_norm_mix_post, m_w_in, m_b_forget, m_conv_w, m_conv_b, m_conv_ln_g, m_conv_ln_b, m_w_out, m_norm_mem_pre, m_norm_mem_post, m_norm_memkv, m_w_mq, m_w_mk, m_w_mv, m_w_mo, m_norm_mlp_pre, m_norm_mlp_post, m_w_up, m_w_down, v_norm_mix_pre, v_norm_mix_post, v_w_in, v_b_forget, v_conv_w, v_conv_b, v_conv_ln_g, v_conv_ln_b, v_w_out, v_norm_mem_pre, v_norm_mem_post, v_norm_memkv, v_w_mq, v_w_mk, v_w_mv, v_w_mo, v_norm_mlp_pre, v_norm_mlp_post, v_w_up, v_w_down):
    p = dict(locals())
    names = ("norm_mix_pre", "norm_mix_post", "w_in", "b_forget", "conv_w", "conv_b", "conv_ln_g", "conv_ln_b", "w_out",
             "norm_mem_pre", "norm_mem_post", "norm_memkv", "w_mq", "w_mk", "w_mv", "w_mo", "norm_mlp_pre",
             "norm_mlp_post", "w_up", "w_down")
    me = 4 * lax.axis_index("x") + 2 * lax.axis_index("y") + lax.axis_index("c")
    shard_shapes = {n: p[n].shape[1:] for n in BIG}
    conv_cols = conv_w.shape[2]

    gathered = _exchange(_pack_big_shards(p, "", BF16), True, "gather_weights")
    conv_rows = DEPTH * CONV_PAD * conv_cols // D_MODEL
    conv_pack = _pack_rows([jnp.pad(conv_w, ((0, 0), (0, CONV_PAD - CONV_WIDTH), (0, 0)))], 8)
    conv_all = _exchange(conv_pack, True, "gather_conv")[:, :conv_rows].reshape(N_DEV, DEPTH, CONV_PAD, conv_cols)
    conv_full = conv_all.transpose(1, 2, 0, 3).reshape(DEPTH, CONV_PAD, N_DEV * conv_cols)

    wl = []
    off = 0
    for l in range(DEPTH):
        w = {}
        for n in BIG:
            w[n] = _full_from_blocks(gathered[:, off:off + BIG_ROWS[n]], shard_shapes[n], SHARD_AXIS[n])
            off += BIG_ROWS[n]
        w["w_in_cat"] = jnp.pad(w.pop("w_in"), ((0, 0), (0, IN_CAT - IN_COLS)))
        w["w_mkv"] = jnp.concatenate([w.pop("w_mk"), w.pop("w_mv")], axis=1)
        w["conv_w"] = conv_full[l]
        for n in VEC:
            w[n] = p[n][l][None, :]
        w["b_forget"] = _pad_lanes(w["b_forget"], LANES)
        wl.append(w)

    loss_row, grad_x, grads = _local_step(x[0], mem[0], loss_target[0], wl)

    gpack = jnp.concatenate(
        [_blocks_from_full(grads[l][n], shard_shapes[n], SHARD_AXIS[n]).astype(BF16) for l in range(DEPTH) for n in BIG]
        + [jnp.zeros((N_DEV, PACK_ROWS - DEPTH * ROWS_PER_LAYER, D_MODEL), BF16)], axis=1)
    gparts = _exchange(gpack, False, "scatter_grads")
    g_big, d_big, m_big, v_big = _adamw(gparts, _pack_big_shards(p, "", F32), _pack_big_shards(p, "m_", F32),
                                        _pack_big_shards(p, "v_", F32), PACK_TILE, "adamw_big")

    small = [loss_row] + [_pad_lanes(grads[l][n]) for l in range(DEPTH) for n in VEC]
    small += [_pad_lanes(grads[l]["conv_w"].reshape(1, -1), 16 * D_MODEL).reshape(16, D_MODEL) for l in range(DEPTH)]
    n_small = 1 + DEPTH * len(VEC) + 16 * DEPTH
    small_rows = -(-n_small // 8) * 8
    total = _sum_blocks(_exchange(_pack_rows(small, small_rows), True, "gather_small"), "sum_small")
    loss = jnp.sum(total[0])
    gv = {}
    row = 1
    for l in range(DEPTH):
        for n in VEC:
            gv[(n, l)] = total[row, :p[n].shape[1]]
            row += 1
    gconv = []
    for l in range(DEPTH):
        full = total[row:row + 16].reshape(-1)[:CONV_WIDTH * CONV_CH].reshape(CONV_WIDTH, CONV_CH)
        gconv.append(lax.dynamic_slice_in_dim(full, me * conv_cols, conv_cols, axis=1))
        row += 16

    def small_pack(prefix, conv):
        rows = [_pad_lanes(p[prefix + n][l][None, :]) for l in range(DEPTH) for n in VEC]
        rows += [_pad_lanes(conv[l].reshape(1, -1), 2 * D_MODEL).reshape(2, D_MODEL) for l in range(DEPTH)]
        return _pack_rows(rows, SMALL_ROWS)

    g_small = _pack_rows(
        [_pad_lanes(gv[(n, l)][None, :]) for l in range(DEPTH) for n in VEC]
        + [_pad_lanes(gconv[l].reshape(1, -1), 2 * D_MODEL).reshape(2, D_MODEL) for l in range(DEPTH)], SMALL_ROWS)
    gs, ds, ms, vs = _adamw(g_small[None], small_pack("", conv_w), small_pack("m_", m_conv_w),
                            small_pack("v_", v_conv_w), SMALL_ROWS, "adamw_small")

    def unpack_small(buf):
        out = {}
        row = 0
        for l in range(DEPTH):
            for n in VEC:
                out.setdefault(n, []).append(buf[row, :p[n].shape[1]])
                row += 1
        for l in range(DEPTH):
            out.setdefault("conv_w", []).append(buf[row:row + 2].reshape(-1)[:CONV_WIDTH * conv_cols]
                                                .reshape(CONV_WIDTH, conv_cols))
            row += 2
        return {n: jnp.stack(v) for n, v in out.items()}

    result = [loss, grad_x[None]]
    for big, small_buf in ((g_big, gs), (d_big, ds), (m_big, ms), (v_big, vs)):
        bigs = _unpack_big_shards(big, shard_shapes)
        smalls = unpack_small(small_buf)
        result += [bigs[n] if n in bigs else smalls[n] for n in names]
    return tuple(result)
```

```python
import functools

import jax
import jax.numpy as jnp
from jax import lax
from jax.experimental import pallas as pl
from jax.experimental.pallas import tpu as pltpu

F32 = jnp.float32
BF16 = jnp.bfloat16

N_DEV = 8
DEPTH = 2
D_MODEL = 1024
CONV_CH = 512
CONV_WIDTH = 31
CONV_PAD = 32
FOX_HEADS = 8
FOX_HEAD_DIM = 64
FOX_WIDTH = 512
N_PAIRS = 4
MEM_HEADS = 4
MEM_HEAD_DIM = 128
MEM_INNER = 512
D_FF = 4096
IN_MAIN = 2560
IN_COLS = 2568
IN_CAT = IN_MAIN + 128
LANES = 128
EPS = 1e-6
NEG_INF = -1e30

ADAM_LR = 0.001
ADAM_B1 = 0.9
ADAM_B2 = 0.999
ADAM_EPS = 1e-08
ADAM_WD = 0.01
ADAM_STEP = 10

NT_DIMS = (((1,), (1,)), ((), ()))
TN_DIMS = (((0,), (0,)), ((), ()))

BIG = ("w_in", "w_out", "w_mq", "w_mk", "w_mv", "w_mo", "w_up", "w_down")
ADAMW_TILE = {"w_in": 256, "w_out": 128, "w_mq": 128, "w_mk": 128, "w_mv": 128, "w_mo": 512, "w_up": 256, "w_down": 128}

VEC = ("norm_mix_pre", "norm_mix_post", "norm_mem_pre", "norm_mem_post", "norm_memkv", "norm_mlp_pre", "norm_mlp_post",
       "b_forget", "conv_b", "conv_ln_g", "conv_ln_b")
SMALL_ROWS = 32


def _sigmoid(x):
    return 1.0 / (1.0 + jnp.exp(-x))


def _rms(x, g):
    r = lax.rsqrt(jnp.mean(x * x, axis=-1, keepdims=True) + EPS)
    return x * r * g


def _rms_bwd(x, g, dh):
    r = lax.rsqrt(jnp.mean(x * x, axis=-1, keepdims=True) + EPS)
    gh = dh * g
    c = jnp.mean(gh * x, axis=-1, keepdims=True)
    dx = r * gh - x * (r * r * r * c)
    dg = jnp.sum(dh * (x * r), axis=0, keepdims=True)
    return dx, dg


def _full(shape):
    nd = len(shape)
    return pl.BlockSpec(shape, lambda *_: (0,) * nd)


def _params(*sem):
    return pltpu.CompilerParams(dimension_semantics=sem)


def _exchange_copies(src_refs, out_refs, same, send_sems, recv_sems, local_sems, with_recvs):
    x, y, c = lax.axis_index("x"), lax.axis_index("y"), lax.axis_index("c")
    me = 4 * x + 2 * y + c
    local, sends, recvs = [], [], []
    for a, (s_ref, o_ref) in enumerate(zip(src_refs, out_refs)):
        def mine(idx, s_ref=s_ref, whole=same[a]):
            return s_ref if whole else s_ref.at[idx]

        local.append(pltpu.make_async_copy(mine(me), o_ref.at[me], local_sems.at[a]))
        for k in range(1, N_DEV):
            px = 1 - x if k & 4 else x
            py = 1 - y if k & 2 else y
            pc = 1 - c if k & 1 else c
            peer = 4 * px + 2 * py + pc
            sem = a * (N_DEV - 1) + k - 1
            common = dict(send_sem=send_sems.at[sem], recv_sem=recv_sems.at[sem], device_id=(px, py, pc),
                          device_id_type=pl.DeviceIdType.MESH)
            sends.append(pltpu.make_async_remote_copy(src_ref=mine(peer), dst_ref=o_ref.at[me], **common))
            if with_recvs:
                recvs.append(pltpu.make_async_remote_copy(src_ref=mine(peer), dst_ref=o_ref.at[peer], **common))
    return local, sends, recvs


def _pcall(body, *, name, grid, in_specs, out_specs, out_shape, args, scratch_shapes=(), sem=(), host=None):
    if host is None:
        return pl.pallas_call(body, name=name, grid=grid, in_specs=in_specs, out_specs=out_specs, out_shape=out_shape,
                              scratch_shapes=list(scratch_shapes), compiler_params=_params(*sem))(*args)
    srcs, same = host
    n_in, n_out, n_scr, n_h = len(in_specs), len(out_specs), len(scratch_shapes), len(srcs)
    hbm = pl.BlockSpec(memory_space=pltpu.HBM)
    lands = [jax.ShapeDtypeStruct((N_DEV,) + (s.shape if whole else s.shape[1:]), s.dtype) for s, whole in zip(srcs, same)]

    def wrapped(*refs):
        ins, src_refs = refs[:n_in], refs[n_in:n_in + n_h]
        outs = refs[n_in + n_h:n_in + n_h + n_out]
        land_refs = refs[n_in + n_h + n_out:n_in + 2 * n_h + n_out]
        scr = refs[n_in + 2 * n_h + n_out:n_in + 2 * n_h + n_out + n_scr]
        sems = refs[n_in + 2 * n_h + n_out + n_scr:]
        first = functools.reduce(jnp.logical_and, [pl.program_id(d) == 0 for d in range(len(grid))])
        last = functools.reduce(jnp.logical_and, [pl.program_id(d) == grid[d] - 1 for d in range(len(grid))])

        @pl.when(first)
        def _():
            local, sends, _ = _exchange_copies(src_refs, land_refs, same, *sems, False)
            for cp in local + sends:
                cp.start()

        body(*ins, *outs, *scr)

        @pl.when(last)
        def _():
            local, sends, recvs = _exchange_copies(src_refs, land_refs, same, *sems, True)
            for cp in recvs:
                cp.wait_recv()
            for cp in sends:
                cp.wait_send()
            for cp in local:
                cp.wait()

    n_sem = n_h * (N_DEV - 1)
    res = pl.pallas_call(
        wrapped, name=name, grid=grid, in_specs=list(in_specs) + [hbm] * n_h, out_specs=list(out_specs) + [hbm] * n_h,
        out_shape=list(out_shape) + lands,
        scratch_shapes=list(scratch_shapes) + [pltpu.SemaphoreType.DMA((n_sem,)), pltpu.SemaphoreType.DMA((n_sem,)),
                                               pltpu.SemaphoreType.DMA((n_h,))],
        compiler_params=_params(*(("arbitrary",) * len(grid))),
    )(*args, *srcs)
    return list(res[:n_out]), list(res[n_out:])


def _exchange(srcs, same, name):
    def body():
        pass

    return _pcall(body, name=name, grid=(1,), in_specs=[], out_specs=[], out_shape=[], args=[], host=(srcs, same))[1]


def _rms_matmul(x, g, w, segs, tile, name, host=None):
    s_len, d = x.shape
    n = w.shape[1]
    chunk = 512

    def body(x_ref, g_ref, w_ref, h_ref, *outs):
        h = _rms(x_ref[...], g_ref[...]).astype(BF16)
        h_ref[...] = h
        oi = 0
        for c0, c1, fns in segs:
            for a in range(c0, c1, chunk):
                b = min(a + chunk, c1)
                z = jnp.dot(h, w_ref[:, a:b], preferred_element_type=F32)
                for k, (dt, fn) in enumerate(fns):
                    outs[oi + k][:, a - c0:b - c0] = fn(z).astype(dt)
            oi += len(fns)

    out_shape = [jax.ShapeDtypeStruct((s_len, d), BF16)]
    out_specs = [pl.BlockSpec((tile, d), lambda i: (i, 0))]
    for c0, c1, fns in segs:
        for dt, _ in fns:
            out_shape.append(jax.ShapeDtypeStruct((s_len, c1 - c0), dt))
            out_specs.append(pl.BlockSpec((tile, c1 - c0), lambda i: (i, 0)))
    return _pcall(
        body, name=name, grid=(s_len // tile,),
        in_specs=[pl.BlockSpec((tile, d), lambda i: (i, 0)), _full((1, d)), _full((d, n))],
        out_specs=out_specs, out_shape=out_shape, args=[x, g, w], sem=("parallel",), host=host)


def _matmul_resnorm(a, w, x, g, tile, name):
    s_len, k = a.shape
    d = w.shape[1]

    def body(a_ref, w_ref, x_ref, g_ref, y_ref, xo_ref):
        y = jnp.dot(a_ref[...], w_ref[...], preferred_element_type=F32)
        y_ref[...] = y
        xo_ref[...] = x_ref[...] + _rms(y, g_ref[...])

    row = lambda i: (i, 0)
    return pl.pallas_call(
        body, name=name, grid=(s_len // tile,),
        in_specs=[pl.BlockSpec((tile, k), row), _full((k, d)), pl.BlockSpec((tile, d), row), _full((1, d))],
        out_specs=[pl.BlockSpec((tile, d), row), pl.BlockSpec((tile, d), row)],
        out_shape=[jax.ShapeDtypeStruct((s_len, d), F32), jax.ShapeDtypeStruct((s_len, d), F32)],
        compiler_params=_params("parallel"),
    )(a, w, x, g)


def _resnorm_bwd_mm(dx, y, g, w, tile, name, out_dtype, pre=None):
    s_len, d = dx.shape
    k = w.shape[0]
    chunk = 512

    def body(*refs):
        if pre is None:
            dx_ref, y_ref, g_ref, w_ref, dy_ref, da_ref, dg_ref = refs
        else:
            dx_ref, y_ref, g_ref, w_ref, pre_ref, dy_ref, da_ref, dg_ref = refs
        dy, dg = _rms_bwd(y_ref[...], g_ref[...], dx_ref[...])
        dyb = dy.astype(BF16)
        dy_ref[...] = dyb

        @pl.when(pl.program_id(0) == 0)
        def _():
            dg_ref[...] = jnp.zeros_like(dg_ref)

        dg_ref[...] += dg
        for a in range(0, k, chunk):
            b = min(a + chunk, k)
            da = lax.dot_general(dyb, w_ref[a:b, :], NT_DIMS, preferred_element_type=F32)
            if pre is not None:
                da = da * (2.0 * jnp.maximum(pre_ref[:, a:b].astype(F32), 0.0))
            da_ref[:, a:b] = da.astype(out_dtype)

    row = lambda i: (i, 0)
    in_specs = [pl.BlockSpec((tile, d), row), pl.BlockSpec((tile, d), row), _full((1, d)), _full((k, d))]
    args = [dx, y, g, w]
    if pre is not None:
        in_specs.append(pl.BlockSpec((tile, k), row))
        args.append(pre)
    return pl.pallas_call(
        body, name=name, grid=(s_len // tile,), in_specs=in_specs,
        out_specs=[pl.BlockSpec((tile, d), row), pl.BlockSpec((tile, k), row), _full((1, d))],
        out_shape=[jax.ShapeDtypeStruct((s_len, d), BF16), jax.ShapeDtypeStruct((s_len, k), out_dtype),
                   jax.ShapeDtypeStruct((1, d), F32)],
        compiler_params=_params("arbitrary"),
    )(*args)


def _mm_prenorm_bwd(dz, w, x, g, dres, tile, name):
    s_len, n = dz.shape
    d = w.shape[0]

    def body(*refs):
        if dres is None:
            dz_ref, w_ref, x_ref, g_ref, dx_ref, dg_ref = refs
        else:
            dz_ref, w_ref, x_ref, g_ref, dres_ref, dx_ref, dg_ref = refs
        dh = lax.dot_general(dz_ref[...], w_ref[...], NT_DIMS, preferred_element_type=F32)
        dx, dg = _rms_bwd(x_ref[...], g_ref[...], dh)
        if dres is not None:
            dx = dx + dres_ref[...]
        dx_ref[...] = dx

        @pl.when(pl.program_id(0) == 0)
        def _():
            dg_ref[...] = jnp.zeros_like(dg_ref)

        dg_ref[...] += dg

    row = lambda i: (i, 0)
    in_specs = [pl.BlockSpec((tile, n), row), _full((d, n)), pl.BlockSpec((tile, d), row), _full((1, d))]
    args = [dz, w, x, g]
    if dres is not None:
        in_specs.append(pl.BlockSpec((tile, d), row))
        args.append(dres)
    return pl.pallas_call(
        body, name=name, grid=(s_len // tile,), in_specs=in_specs,
        out_specs=[pl.BlockSpec((tile, d), row), _full((1, d))],
        out_shape=[jax.ShapeDtypeStruct((s_len, d), F32), jax.ShapeDtypeStruct((1, d), F32)],
        compiler_params=_params("arbitrary"),
    )(*args)


def _matmul_tn(a, b, tm, tk, name):
    s_len, m = a.shape
    n = b.shape[1]

    def body(a_ref, b_ref, o_ref):
        @pl.when(pl.program_id(1) == 0)
        def _():
            o_ref[...] = jnp.zeros_like(o_ref)

        o_ref[...] += lax.dot_general(a_ref[...], b_ref[...], TN_DIMS, preferred_element_type=F32)

    return pl.pallas_call(
        body, name=name, grid=(m // tm, s_len // tk),
        in_specs=[pl.BlockSpec((tk, tm), lambda i, k: (k, i)), pl.BlockSpec((tk, n), lambda i, k: (k, 0))],
        out_specs=pl.BlockSpec((tm, n), lambda i, k: (i, 0)),
        out_shape=jax.ShapeDtypeStruct((m, n), F32),
        compiler_params=_params("parallel", "arbitrary"),
    )(a, b)


def _matmul_tn_shards(a, b, axis, tk, name):
    s_len, m = a.shape
    n = b.shape[1]
    r, c = (m // N_DEV, n) if axis == 0 else (m, n // N_DEV)
    n_k = s_len // tk

    def body(a_ref, b_ref, o_ref, acc):
        k = pl.program_id(1)

        @pl.when(k == 0)
        def _():
            acc[...] = jnp.zeros_like(acc)

        acc[...] += lax.dot_general(a_ref[...], b_ref[...], TN_DIMS, preferred_element_type=F32)

        @pl.when(k == n_k - 1)
        def _():
            o_ref[...] = acc[...].astype(BF16)

    if axis == 0:
        in_specs = [pl.BlockSpec((tk, r), lambda j, k: (k, j)), pl.BlockSpec((tk, n), lambda j, k: (k, 0))]
    else:
        in_specs = [pl.BlockSpec((tk, m), lambda j, k: (k, 0)), pl.BlockSpec((tk, c), lambda j, k: (k, j))]
    return pl.pallas_call(
        body, name=name, grid=(N_DEV, n_k), in_specs=in_specs,
        out_specs=pl.BlockSpec((None, r, c), lambda j, k: (j, 0, 0)),
        out_shape=jax.ShapeDtypeStruct((N_DEV, r, c), BF16),
        scratch_shapes=[pltpu.VMEM((r, c), F32)],
        compiler_params=_params("parallel", "arbitrary"),
    )(a, b)


def _cumsum_fwd(fl, b, tile, name):
    s_len = fl.shape[0]

    def body(fl_ref, b_ref, cum_ref, cumt_ref, carry):
        @pl.when(pl.program_id(0) == 0)
        def _():
            carry[...] = jnp.zeros_like(carry)

        xx = fl_ref[...] + b_ref[...]
        lf = jnp.minimum(xx, 0.0) - jnp.log1p(jnp.exp(-jnp.abs(xx)))
        r = lax.broadcasted_iota(jnp.int32, (tile, tile), 0)
        c = lax.broadcasted_iota(jnp.int32, (tile, tile), 1)
        tri = (c <= r).astype(F32)
        cs = jnp.dot(tri, lf, precision=lax.Precision.HIGHEST, preferred_element_type=F32) + carry[...]
        cum_ref[...] = cs
        cumt_ref[...] = cs.T[0:FOX_HEADS, :]
        carry[...] = cs[tile - 1:tile, :]

    return pl.pallas_call(
        body, name=name, grid=(s_len // tile,),
        in_specs=[pl.BlockSpec((tile, LANES), lambda i: (i, 0)), _full((1, LANES))],
        out_specs=[pl.BlockSpec((tile, LANES), lambda i: (i, 0)), pl.BlockSpec((FOX_HEADS, tile), lambda i: (0, i))],
        out_shape=[jax.ShapeDtypeStruct((s_len, LANES), F32), jax.ShapeDtypeStruct((FOX_HEADS, s_len), F32)],
        scratch_shapes=[pltpu.VMEM((1, LANES), F32)],
        compiler_params=_params("arbitrary"),
    )(fl, b)


def _cumsum_bwd(dcum, fl, b, tile, name):
    s_len = fl.shape[0]
    n_t = s_len // tile

    def body(dc_ref, fl_ref, b_ref, dfl_ref, db_ref, carry):
        @pl.when(pl.program_id(0) == 0)
        def _():
            carry[...] = jnp.zeros_like(carry)
            db_ref[...] = jnp.zeros_like(db_ref)

        r = lax.broadcasted_iota(jnp.int32, (tile, tile), 0)
        c = lax.broadcasted_iota(jnp.int32, (tile, tile), 1)
        tri = (c >= r).astype(F32)
        dl = jnp.dot(tri, dc_ref[...], precision=lax.Precision.HIGHEST, preferred_element_type=F32) + carry[...]
        carry[...] = dl[0:1, :]
        dfl = dl * _sigmoid(-(fl_ref[...] + b_ref[...]))
        dfl_ref[...] = dfl
        db_ref[...] += jnp.sum(dfl, axis=0, keepdims=True)

    rev = lambda i: (n_t - 1 - i, 0)
    return pl.pallas_call(
        body, name=name, grid=(n_t,),
        in_specs=[pl.BlockSpec((tile, LANES), rev), pl.BlockSpec((tile, LANES), rev), _full((1, LANES))],
        out_specs=[pl.BlockSpec((tile, LANES), rev), _full((1, LANES))],
        out_shape=[jax.ShapeDtypeStruct((s_len, LANES), F32), jax.ShapeDtypeStruct((1, LANES), F32)],
        scratch_shapes=[pltpu.VMEM((1, LANES), F32)],
        compiler_params=_params("arbitrary"),
    )(dcum, fl, b)


def _conv_taps(w_ref, ext_ref, base, tile, reverse):
    acc = None
    for k in range(CONV_WIDTH):
        off = base + ((CONV_WIDTH - 1 - k) if reverse else k)
        term = w_ref[k:k + 1, :] * ext_ref[pl.ds(off, tile), :]
        acc = term if acc is None else acc + term
    return acc


def _conv_fwd(ag, w, cb, lg, lb, tile, name):
    s_len = ag.shape[0]
    c = CONV_CH

    def body(ag_ref, w_ref, cb_ref, lg_ref, lb_ref, u_ref, ext):
        @pl.when(pl.program_id(0) == 0)
        def _():
            ext[0:CONV_PAD, :] = jnp.zeros((CONV_PAD, c), F32)

        ext[CONV_PAD:CONV_PAD + tile, :] = ag_ref[:, 0:c] * _sigmoid(ag_ref[:, c:2 * c])
        u1 = _conv_taps(w_ref, ext, CONV_PAD - (CONV_WIDTH - 1), tile, False) + cb_ref[...]
        mu = jnp.mean(u1, axis=-1, keepdims=True)
        xc = u1 - mu
        y = xc * lax.rsqrt(jnp.mean(xc * xc, axis=-1, keepdims=True) + EPS) * lg_ref[...] + lb_ref[...]
        u_ref[...] = (y * _sigmoid(y)).astype(BF16)
        ext[0:CONV_PAD, :] = ext[tile:tile + CONV_PAD, :]

    return pl.pallas_call(
        body, name=name, grid=(s_len // tile,),
        in_specs=[pl.BlockSpec((tile, 2 * c), lambda i: (i, 0)), _full((CONV_PAD, c)), _full((1, c)), _full((1, c)),
                  _full((1, c))],
        out_specs=pl.BlockSpec((tile, c), lambda i: (i, 0)),
        out_shape=jax.ShapeDtypeStruct((s_len, c), BF16),
        scratch_shapes=[pltpu.VMEM((tile + CONV_PAD, c), F32)],
        compiler_params=_params("arbitrary"),
    )(ag, w, cb, lg, lb)


def _conv_bwd(ag, dcat, w, cb, lg, lb, tile, name):
    s_len = ag.shape[0]
    c = CONV_CH
    n_t = s_len // tile
    per = tile // CONV_PAD

    def body(ag_ref, halo_ref, du_ref, w_ref, cb_ref, lg_ref, lb_ref, dag_ref, dw_ref, dv_ref, ext, ext2):
        i = pl.program_id(0)
        t = n_t - 1 - i

        @pl.when(i == 0)
        def _():
            ext2[tile:tile + CONV_PAD, :] = jnp.zeros((CONV_PAD, c), F32)
            dw_ref[...] = jnp.zeros_like(dw_ref)
            dv_ref[...] = jnp.zeros_like(dv_ref)

        a = ag_ref[:, 0:c]
        sg = _sigmoid(ag_ref[:, c:2 * c])
        halo = halo_ref[:, 0:c] * _sigmoid(halo_ref[:, c:2 * c])
        ext[0:CONV_PAD, :] = jnp.where(t > 0, halo, 0.0)
        ext[CONV_PAD:CONV_PAD + tile, :] = a * sg
        u1 = _conv_taps(w_ref, ext, CONV_PAD - (CONV_WIDTH - 1), tile, False) + cb_ref[...]
        mu = jnp.mean(u1, axis=-1, keepdims=True)
        xc = u1 - mu
        rs = lax.rsqrt(jnp.mean(xc * xc, axis=-1, keepdims=True) + EPS)
        xhat = xc * rs
        y = xhat * lg_ref[...] + lb_ref[...]
        sy = _sigmoid(y)
        dy = du_ref[...] * (sy * (1.0 + y * (1.0 - sy)))
        dxh = dy * lg_ref[...]
        du1 = rs * (dxh - jnp.mean(dxh, axis=-1, keepdims=True) - xhat * jnp.mean(dxh * xhat, axis=-1, keepdims=True))
        dv_ref[0:1, :] += jnp.sum(du1, axis=0, keepdims=True)
        dv_ref[1:2, :] += jnp.sum(dy * xhat, axis=0, keepdims=True)
        dv_ref[2:3, :] += jnp.sum(dy, axis=0, keepdims=True)
        for k in range(CONV_WIDTH):
            off = CONV_PAD - (CONV_WIDTH - 1) + k
            dw_ref[k:k + 1, :] += jnp.sum(du1 * ext[pl.ds(off, tile), :], axis=0, keepdims=True)
        ext2[0:tile, :] = du1
        du0 = _conv_taps(w_ref, ext2, 0, tile, True)
        ext2[tile:tile + CONV_PAD, :] = du1[0:CONV_PAD, :]
        dag_ref[:, 0:c] = (du0 * sg).astype(BF16)
        dag_ref[:, c:2 * c] = (du0 * a * sg * (1.0 - sg)).astype(BF16)

    rev = lambda i: (n_t - 1 - i, 0)
    return pl.pallas_call(
        body, name=name, grid=(n_t,),
        in_specs=[pl.BlockSpec((tile, 2 * c), rev),
                  pl.BlockSpec((CONV_PAD, 2 * c), lambda i: (jnp.maximum((n_t - 1 - i) * per - 1, 0), 0)),
                  pl.BlockSpec((tile, c), rev), _full((CONV_PAD, c)), _full((1, c)), _full((1, c)), _full((1, c))],
        out_specs=[pl.BlockSpec((tile, 2 * c), rev), _full((CONV_PAD, c)), _full((8, c))],
        out_shape=[jax.ShapeDtypeStruct((s_len, 2 * c), BF16), jax.ShapeDtypeStruct((CONV_PAD, c), F32),
                   jax.ShapeDtypeStruct((8, c), F32)],
        scratch_shapes=[pltpu.VMEM((tile + CONV_PAD, c), F32), pltpu.VMEM((tile + CONV_PAD, c), F32)],
        compiler_params=_params("arbitrary"),
    )(ag, ag, dcat, w, cb, lg, lb)


def _lane_masks():
    lane = lax.broadcasted_iota(jnp.int32, (1, LANES), 1)
    return lane, (lane < FOX_HEAD_DIM, lane >= FOX_HEAD_DIM)


def _head_col(x, lane, h):
    return jnp.sum(jnp.where(lane == h, x, 0.0), axis=1, keepdims=True)


def _fox_fwd(qkv, cum, cumt, ta, name, host=None):
    s_len = qkv.shape[0]
    n_t = s_len // ta
    scale = FOX_HEAD_DIM ** -0.5

    def body(q_ref, k_ref, v_ref, cq_ref, ct_ref, o_ref, lse_ref):
        p = pl.program_id(0)
        i = pl.program_id(1)
        lane, masks = _lane_masks()
        q = q_ref[...]
        qh = [jnp.where(m, q, jnp.zeros_like(q)) for m in masks]
        cq = cq_ref[...]
        cqc = [_head_col(cq, lane, 2 * p + hh) for hh in range(2)]
        r = lax.broadcasted_iota(jnp.int32, (ta, ta), 0)
        c = lax.broadcasted_iota(jnp.int32, (ta, ta), 1)
        causal = c <= r

        def make_step(masked):
            def step(j, carry):
                off = pl.multiple_of(j * ta, ta)
                kt = k_ref[pl.ds(off, ta), :]
                vt = v_ref[pl.ds(off, ta), :]
                new = []
                for hh in range(2):
                    m, l, acc = carry[3 * hh:3 * hh + 3]
                    s = lax.dot_general(qh[hh], kt, NT_DIMS, preferred_element_type=F32) * scale
                    s = s + cqc[hh] - ct_ref[hh, j]
                    if masked:
                        s = jnp.where(causal, s, NEG_INF)
                    m_new = jnp.maximum(m, jnp.max(s, axis=1, keepdims=True))
                    alpha = jnp.exp(m - m_new)
                    pr = jnp.exp(s - m_new)
                    l = alpha * l + jnp.sum(pr, axis=1, keepdims=True)
                    acc = alpha * acc + jnp.dot(pr.astype(BF16), vt, preferred_element_type=F32)
                    new += [m_new, l, acc]
                return tuple(new)
            return step

        init = (jnp.full((ta, 1), NEG_INF, F32), jnp.zeros((ta, 1), F32), jnp.zeros((ta, LANES), F32)) * 2
        carry = lax.fori_loop(0, i, make_step(False), init)
        carry = make_step(True)(i, carry)
        m0, l0, a0, m1, l1, a1 = carry
        o_ref[...] = jnp.where(masks[0], a0 / l0, a1 / l1)
        lse_ref[...] = jnp.where(masks[0], m0 + jnp.log(l0), m1 + jnp.log(l1))

    blk = lambda col0: pl.BlockSpec((s_len, LANES), lambda p, i: (0, col0 + p))
    tile_spec = pl.BlockSpec((ta, LANES), lambda p, i: (i, p))
    return _pcall(
        body, name=name, grid=(N_PAIRS, n_t),
        in_specs=[tile_spec, blk(N_PAIRS), blk(2 * N_PAIRS), pl.BlockSpec((ta, LANES), lambda p, i: (i, 0)),
                  pl.BlockSpec((2, n_t, 1, ta), lambda p, i: (p, 0, 0, 0))],
        out_specs=[tile_spec, tile_spec],
        out_shape=[jax.ShapeDtypeStruct((s_len, FOX_WIDTH), F32), jax.ShapeDtypeStruct((s_len, FOX_WIDTH), F32)],
        args=[qkv, qkv, qkv, cum, cumt], sem=("parallel", "parallel"), host=host)


def _fox_dq(qkv, cum, cumt, o, dcat, lse, ta, name, host=None):
    s_len = qkv.shape[0]
    n_t = s_len // ta
    scale = FOX_HEAD_DIM ** -0.5

    def body(q_ref, k_ref, v_ref, cq_ref, ct_ref, o_ref, do_ref, lse_ref, dq_ref, dl_ref, dck_ref):
        p = pl.program_id(0)
        i = pl.program_id(1)
        lane, masks = _lane_masks()

        @pl.when(i == 0)
        def _():
            dck_ref[...] = jnp.zeros_like(dck_ref)

        q = q_ref[...]
        do = do_ref[...]
        prod = do * o_ref[...]
        dlt = [jnp.sum(jnp.where(m, prod, 0.0), axis=1, keepdims=True) for m in masks]
        dl_ref[...] = jnp.where(masks[0], dlt[0], dlt[1])
        dob = do.astype(BF16)
        qh = [jnp.where(m, q, jnp.zeros_like(q)) for m in masks]
        doh = [jnp.where(m, dob, jnp.zeros_like(dob)) for m in masks]
        lse = lse_ref[...]
        lse_h = [lse[:, 0:1], lse[:, FOX_HEAD_DIM:FOX_HEAD_DIM + 1]]
        cq = cq_ref[...]
        cqc = [_head_col(cq, lane, 2 * p + hh) for hh in range(2)]
        r = lax.broadcasted_iota(jnp.int32, (ta, ta), 0)
        c = lax.broadcasted_iota(jnp.int32, (ta, ta), 1)
        causal = c <= r

        def make_step(masked):
            def step(j, carry):
                off = pl.multiple_of(j * ta, ta)
                kt = k_ref[pl.ds(off, ta), :]
                vt = v_ref[pl.ds(off, ta), :]
                new = []
                for hh in range(2):
                    s = lax.dot_general(qh[hh], kt, NT_DIMS, preferred_element_type=F32) * scale
                    s = s + cqc[hh] - ct_ref[hh, j]
                    pr = jnp.exp(s - lse_h[hh])
                    if masked:
                        pr = jnp.where(causal, pr, 0.0)
                    dp = lax.dot_general(doh[hh], vt, NT_DIMS, preferred_element_type=F32)
                    ds = pr * (dp - dlt[hh])
                    dck_ref[hh, j] = dck_ref[hh, j] - jnp.sum(ds, axis=0, keepdims=True)
                    new.append(carry[hh] + jnp.dot(ds.astype(BF16), kt, preferred_element_type=F32))
                return tuple(new)
            return step

        init = (jnp.zeros((ta, LANES), F32),) * 2
        carry = lax.fori_loop(0, i, make_step(False), init)
        carry = make_step(True)(i, carry)
        dq_ref[...] = (jnp.where(masks[0], carry[0], carry[1]) * scale).astype(BF16)

    blk = lambda col0: pl.BlockSpec((s_len, LANES), lambda p, i: (0, col0 + p))
    tile_spec = pl.BlockSpec((ta, LANES), lambda p, i: (i, p))
    row4 = pl.BlockSpec((2, n_t, 1, ta), lambda p, i: (p, 0, 0, 0))
    return _pcall(
        body, name=name, grid=(N_PAIRS, n_t),
        in_specs=[tile_spec, blk(N_PAIRS), blk(2 * N_PAIRS), pl.BlockSpec((ta, LANES), lambda p, i: (i, 0)), row4,
                  tile_spec, pl.BlockSpec((ta, LANES), lambda p, i: (i, N_PAIRS + p)), tile_spec],
        out_specs=[tile_spec, tile_spec, row4],
        out_shape=[jax.ShapeDtypeStruct((s_len, FOX_WIDTH), BF16), jax.ShapeDtypeStruct((s_len, FOX_WIDTH), F32),
                   jax.ShapeDtypeStruct((FOX_HEADS, n_t, 1, ta), F32)],
        args=[qkv, qkv, qkv, cum, cumt, o, dcat, lse], sem=("parallel", "arbitrary"), host=host)


def _fox_dkv(qkv, cum, cumt, dcat, lset, deltat, ta, name, host=None):
    s_len = qkv.shape[0]
    n_t = s_len // ta
    scale = FOX_HEAD_DIM ** -0.5

    def body(k_ref, v_ref, q_ref, do_ref, ck_ref, ct_ref, lse_ref, dl_ref, dk_ref, dv_ref, dcq_ref):
        p = pl.program_id(0)
        j = pl.program_id(1)
        lane, masks = _lane_masks()

        @pl.when(j == 0)
        def _():
            dcq_ref[...] = jnp.zeros_like(dcq_ref)
        k = k_ref[...]
        v = v_ref[...]
        kh = [jnp.where(m, k, jnp.zeros_like(k)) for m in masks]
        vh = [jnp.where(m, v, jnp.zeros_like(v)) for m in masks]
        ck = ck_ref[...]
        ckc = [_head_col(ck, lane, 2 * p + hh) for hh in range(2)]
        r = lax.broadcasted_iota(jnp.int32, (ta, ta), 0)
        c = lax.broadcasted_iota(jnp.int32, (ta, ta), 1)
        causal = r <= c

        def make_step(masked):
            def step(i, carry):
                off = pl.multiple_of(i * ta, ta)
                qt = q_ref[pl.ds(off, ta), :]
                dot = do_ref[pl.ds(off, ta), :].astype(BF16)
                new = []
                for hh in range(2):
                    dk, dv = carry[2 * hh:2 * hh + 2]
                    st = lax.dot_general(kh[hh], qt, NT_DIMS, preferred_element_type=F32) * scale
                    st = st + ct_ref[hh, i] - ckc[hh]
                    pt = jnp.exp(st - lse_ref[hh, i])
                    if masked:
                        pt = jnp.where(causal, pt, 0.0)
                    dv = dv + jnp.dot(pt.astype(BF16), dot, preferred_element_type=F32)
                    dpt = lax.dot_general(vh[hh], dot, NT_DIMS, preferred_element_type=F32)
                    dst = pt * (dpt - dl_ref[hh, i])
                    dcq_ref[hh, i] = dcq_ref[hh, i] + jnp.sum(dst, axis=0, keepdims=True)
                    dk = dk + jnp.dot(dst.astype(BF16), qt, preferred_element_type=F32)
                    new += [dk, dv]
                return tuple(new)
            return step

        init = (jnp.zeros((ta, LANES), F32),) * 4
        carry = make_step(True)(j, init)
        carry = lax.fori_loop(j + 1, n_t, make_step(False), carry)
        dk_ref[...] = (jnp.where(masks[0], carry[0], carry[2]) * scale).astype(BF16)
        dv_ref[...] = jnp.where(masks[0], carry[1], carry[3]).astype(BF16)

    row4 = pl.BlockSpec((2, n_t, 1, ta), lambda p, j: (p, 0, 0, 0))
    out_spec = pl.BlockSpec((ta, LANES), lambda p, j: (j, p))
    return _pcall(
        body, name=name, grid=(N_PAIRS, n_t),
        in_specs=[pl.BlockSpec((ta, LANES), lambda p, j: (j, N_PAIRS + p)),
                  pl.BlockSpec((ta, LANES), lambda p, j: (j, 2 * N_PAIRS + p)),
                  pl.BlockSpec((s_len, LANES), lambda p, j: (0, p)),
                  pl.BlockSpec((s_len, LANES), lambda p, j: (0, N_PAIRS + p)),
                  pl.BlockSpec((ta, LANES), lambda p, j: (j, 0)), row4, row4, row4],
        out_specs=[out_spec, out_spec, row4],
        out_shape=[jax.ShapeDtypeStruct((s_len, FOX_WIDTH), BF16), jax.ShapeDtypeStruct((s_len, FOX_WIDTH), BF16),
                   jax.ShapeDtypeStruct((FOX_HEADS, n_t, 1, ta), F32)],
        args=[qkv, qkv, qkv, dcat, cum, cumt, lset, deltat], sem=("parallel", "arbitrary"), host=host)


def _mem_scores_t(q, kv, h):
    lo = h * MEM_HEAD_DIM
    st = lax.dot_general(kv[:, lo:lo + MEM_HEAD_DIM], q[:, lo:lo + MEM_HEAD_DIM], NT_DIMS,
                         preferred_element_type=F32) * (MEM_HEAD_DIM ** -0.5)
    e = jnp.exp(st - jnp.max(st, axis=0, keepdims=True))
    return e / jnp.sum(e, axis=0, keepdims=True)


def _memattn_fwd(q, kv, tile, name):
    s_len = q.shape[0]
    n_mem = kv.shape[0]

    def body(q_ref, kv_ref, o_ref):
        q = q_ref[...]
        kv = kv_ref[...]
        for h in range(MEM_HEADS):
            lo = h * MEM_HEAD_DIM
            pt = _mem_scores_t(q, kv, h).astype(BF16)
            vh = kv[:, MEM_INNER + lo:MEM_INNER + lo + MEM_HEAD_DIM]
            o_ref[:, lo:lo + MEM_HEAD_DIM] = lax.dot_general(pt, vh, TN_DIMS, preferred_element_type=F32).astype(BF16)

    return pl.pallas_call(
        body, name=name, grid=(s_len // tile,),
        in_specs=[pl.BlockSpec((tile, MEM_INNER), lambda i: (i, 0)), _full((n_mem, 2 * MEM_INNER))],
        out_specs=pl.BlockSpec((tile, MEM_INNER), lambda i: (i, 0)),
        out_shape=jax.ShapeDtypeStruct((s_len, MEM_INNER), BF16),
        compiler_params=_params("parallel"),
    )(q, kv)


def _memattn_bwd(q, kv, do, tile, name):
    s_len = q.shape[0]
    n_mem = kv.shape[0]
    scale = MEM_HEAD_DIM ** -0.5

    def body(q_ref, kv_ref, do_ref, dq_ref, dkv_ref):
        @pl.when(pl.program_id(0) == 0)
        def _():
            dkv_ref[...] = jnp.zeros_like(dkv_ref)

        q = q_ref[...]
        kv = kv_ref[...]
        do = do_ref[...]
        for h in range(MEM_HEADS):
            lo = h * MEM_HEAD_DIM
            qh = q[:, lo:lo + MEM_HEAD_DIM]
            kh = kv[:, lo:lo + MEM_HEAD_DIM]
            vh = kv[:, MEM_INNER + lo:MEM_INNER + lo + MEM_HEAD_DIM]
            doh = do[:, lo:lo + MEM_HEAD_DIM]
            pt = _mem_scores_t(q, kv, h)
            dkv_ref[:, MEM_INNER + lo:MEM_INNER + lo + MEM_HEAD_DIM] += jnp.dot(
                pt.astype(BF16), doh, preferred_element_type=F32)
            dpt = lax.dot_general(vh, doh, NT_DIMS, preferred_element_type=F32)
            dst = (pt * (dpt - jnp.sum(pt * dpt, axis=0, keepdims=True)) * scale).astype(BF16)
            dkv_ref[:, lo:lo + MEM_HEAD_DIM] += jnp.dot(dst, qh, preferred_element_type=F32)
            dq_ref[:, lo:lo + MEM_HEAD_DIM] = lax.dot_general(dst, kh, TN_DIMS, preferred_element_type=F32).astype(BF16)

    return pl.pallas_call(
        body, name=name, grid=(s_len // tile,),
        in_specs=[pl.BlockSpec((tile, MEM_INNER), lambda i: (i, 0)), _full((n_mem, 2 * MEM_INNER)),
                  pl.BlockSpec((tile, MEM_INNER), lambda i: (i, 0))],
        out_specs=[pl.BlockSpec((tile, MEM_INNER), lambda i: (i, 0)), _full((n_mem, 2 * MEM_INNER))],
        out_shape=[jax.ShapeDtypeStruct((s_len, MEM_INNER), BF16), jax.ShapeDtypeStruct((n_mem, 2 * MEM_INNER), F32)],
        compiler_params=_params("arbitrary"),
    )(q, kv, do)


def _loss_head(y, target, tile, name):
    s_len, d = y.shape

    def body(y_ref, t_ref, dy_ref, l_ref):
        @pl.when(pl.program_id(0) == 0)
        def _():
            l_ref[...] = jnp.zeros_like(l_ref)

        err = y_ref[...] - t_ref[...]
        dy_ref[...] = err * (1.0 / d)
        l_ref[...] += jnp.sum(err * err, axis=0, keepdims=True) * (0.5 / d)

    row = lambda i: (i, 0)
    return pl.pallas_call(
        body, name=name, grid=(s_len // tile,),
        in_specs=[pl.BlockSpec((tile, d), row), pl.BlockSpec((tile, d), row)],
        out_specs=[pl.BlockSpec((tile, d), row), _full((1, d))],
        out_shape=[jax.ShapeDtypeStruct((s_len, d), F32), jax.ShapeDtypeStruct((1, d), F32)],
        compiler_params=_params("arbitrary"),
    )(y, target)


def _attn_tile(s_len):
    return min(256, s_len)


def _rows4(a8, ta):
    return a8.reshape(FOX_HEADS, a8.shape[1] // ta, 1, ta)


def _head_rows(x, ta):
    return _rows4(x[:, ::FOX_HEAD_DIM].T, ta)


REST = ("w_out", "w_mq", "w_mk", "w_mv", "w_mo", "w_up", "w_down")
SHARD_AXIS = {"w_in": 1, "w_out": 0, "w_mq": 0, "w_mk": 0, "w_mv": 0, "w_mo": 1, "w_up": 1, "w_down": 0}


def _full_from_shards(sh, axis):
    n, r, c = sh.shape
    if axis == 0:
        return sh.reshape(n * r, c)
    return sh.transpose(1, 0, 2).reshape(r, n * c)


def _rest_weights(lands):
    w = {n: _full_from_shards(sh, SHARD_AXIS[n]) for n, sh in zip(REST, lands)}
    w["w_mkv"] = jnp.concatenate([w.pop("w_mk"), w.pop("w_mv")], axis=1)
    return w


def _w_in_cat(land):
    return jnp.pad(_full_from_shards(land, 1), ((0, 0), (0, IN_CAT - IN_COLS)))


def _layer_fwd(x0, mem, w, l, rest_src=None, next_src=None):
    s_len = x0.shape[0]
    tile = min(512, s_len)
    tile_ff = min(256, s_len)
    ta = _attn_tile(s_len)
    ident = lambda z: z
    sv = {"x0": x0}

    h1, ag, qkv, fl = _rms_matmul(
        x0, w["norm_mix_pre"], w["w_in_cat"],
        [(0, 2 * CONV_CH, [(F32, ident)]), (2 * CONV_CH, IN_MAIN, [(BF16, ident)]), (IN_MAIN, IN_CAT, [(F32, ident)])],
        tile, f"mix_in_{l}")
    cum, cum8 = _cumsum_fwd(fl, w["b_forget"], tile, f"cumsum_fwd_{l}")
    cumt = _rows4(cum8, ta)
    u3 = _conv_fwd(ag, w["conv_w"], w["conv_b"], w["conv_ln_g"], w["conv_ln_b"], tile, f"conv_fwd_{l}")
    if rest_src is None:
        o, lse = _fox_fwd(qkv, cum, cumt, ta, f"fox_fwd_{l}")
    else:
        (o, lse), rest_land = _fox_fwd(qkv, cum, cumt, ta, f"fox_fwd_{l}", host=(rest_src, [True] * len(rest_src)))
        w = {**w, **_rest_weights(rest_land)}
    cat = jnp.concatenate([u3, o.astype(BF16)], axis=1)
    y1, x1 = _matmul_resnorm(cat, w["w_out"], x0, w["norm_mix_post"], tile, f"mix_out_{l}")
    sv.update(h1=h1, ag=ag, qkv=qkv, fl=fl, cum=cum, cumt=cumt, o=o, lse=lse, cat=cat, y1=y1, x1=x1)

    h2, qm = _rms_matmul(x1, w["norm_mem_pre"], w["w_mq"], [(0, MEM_INNER, [(BF16, ident)])], tile, f"mem_q_{l}")
    mem_n, kv = _rms_matmul(mem, w["norm_memkv"], w["w_mkv"], [(0, 2 * MEM_INNER, [(BF16, ident)])],
                            mem.shape[0], f"mem_kv_{l}")
    om = _memattn_fwd(qm, kv, tile, f"mem_attn_fwd_{l}")
    y2, x2 = _matmul_resnorm(om, w["w_mo"], x1, w["norm_mem_post"], tile, f"mem_out_{l}")
    sv.update(h2=h2, qm=qm, mem_n=mem_n, kv=kv, om=om, y2=y2, x2=x2)

    relu2 = lambda z: jnp.square(jnp.maximum(z, 0.0))
    up = _rms_matmul(x2, w["norm_mlp_pre"], w["w_up"], [(0, D_FF, [(BF16, ident), (BF16, relu2)])], tile_ff,
                     f"mlp_up_{l}", host=None if next_src is None else (next_src, [True] * len(next_src)))
    (h3, pre, hid), next_land = (up, None) if next_src is None else up
    y3, x3 = _matmul_resnorm(hid, w["w_down"], x2, w["norm_mlp_post"], tile_ff, f"mlp_down_{l}")
    sv.update(h3=h3, pre=pre, hid=hid, y3=y3)
    return x3, sv, w, next_land


def _layer_bwd(dx3, mem, w, sv, l, scatter_rest=False, dkv_src=None):
    s_len = dx3.shape[0]
    tile = min(512, s_len)
    tile_ff = min(256, s_len)
    ta = _attn_tile(s_len)
    tk = min(512, s_len)
    n_mem = mem.shape[0]
    g = {}

    dy3, dpre, g["norm_mlp_post"] = _resnorm_bwd_mm(dx3, sv["y3"], w["norm_mlp_post"], w["w_down"], tile_ff,
                                                    f"mlp_down_bwd_{l}", BF16, pre=sv["pre"])
    g["w_down"] = _matmul_tn_shards(sv["hid"], dy3, 0, tk, f"dw_down_{l}")
    dx2, g["norm_mlp_pre"] = _mm_prenorm_bwd(dpre, w["w_up"], sv["x2"], w["norm_mlp_pre"], dx3, tile_ff,
                                             f"mlp_up_bwd_{l}")
    g["w_up"] = _matmul_tn_shards(sv["h3"], dpre, 1, tk, f"dw_up_{l}")

    dy2, dom, g["norm_mem_post"] = _resnorm_bwd_mm(dx2, sv["y2"], w["norm_mem_post"], w["w_mo"], tile,
                                                   f"mem_out_bwd_{l}", BF16)
    g["w_mo"] = _matmul_tn_shards(sv["om"], dy2, 1, tk, f"dw_mo_{l}")
    dqm, dkv = _memattn_bwd(sv["qm"], sv["kv"], dom, tile, f"mem_attn_bwd_{l}")
    dkvb = dkv.astype(BF16)
    g["w_mq"] = _matmul_tn_shards(sv["h2"], dqm, 0, tk, f"dw_mq_{l}")
    dx1, g["norm_mem_pre"] = _mm_prenorm_bwd(dqm, w["w_mq"], sv["x1"], w["norm_mem_pre"], dx2, tile, f"mem_q_bwd_{l}")
    _, g["norm_memkv"] = _mm_prenorm_bwd(dkvb, w["w_mkv"], mem, w["norm_memkv"], None, n_mem, f"mem_kv_bwd_{l}")
    g["w_mk"] = _matmul_tn_shards(sv["mem_n"], dkvb[:, :MEM_INNER], 0, n_mem, f"dw_mk_{l}")
    g["w_mv"] = _matmul_tn_shards(sv["mem_n"], dkvb[:, MEM_INNER:], 0, n_mem, f"dw_mv_{l}")

    dy1, dcat, g["norm_mix_post"] = _resnorm_bwd_mm(dx1, sv["y1"], w["norm_mix_post"], w["w_out"], tile,
                                                    f"mix_out_bwd_{l}", F32)
    g["w_out"] = _matmul_tn_shards(sv["cat"], dy1, 0, tk, f"dw_out_{l}")
    rest_land = None
    if scatter_rest:
        rest = [g[n] for n in REST]
        (dq, delta, dck), rest_land = _fox_dq(sv["qkv"], sv["cum"], sv["cumt"], sv["o"], dcat, sv["lse"], ta,
                                              f"fox_dq_{l}", host=(rest, [False] * len(rest)))
    else:
        dq, delta, dck = _fox_dq(sv["qkv"], sv["cum"], sv["cumt"], sv["o"], dcat, sv["lse"], ta, f"fox_dq_{l}")
    dkv_res = _fox_dkv(sv["qkv"], sv["cum"], sv["cumt"], dcat, _head_rows(sv["lse"], ta), _head_rows(delta, ta), ta,
                       f"fox_dkv_{l}", host=None if dkv_src is None else (dkv_src, [False] * len(dkv_src)))
    (dk, dv, dcq), dkv_land = (dkv_res, None) if dkv_src is None else dkv_res
    dcum = jnp.pad((dcq + dck).reshape(FOX_HEADS, s_len).T, ((0, 0), (0, LANES - FOX_HEADS)))
    dfl, db = _cumsum_bwd(dcum, sv["fl"], w["b_forget"], tile, f"cumsum_bwd_{l}")
    g["b_forget"] = db[:, :FOX_HEADS]
    dag, dconv_w, dconv_v = _conv_bwd(sv["ag"], dcat, w["conv_w"], w["conv_b"], w["conv_ln_g"], w["conv_ln_b"], tile,
                                      f"conv_bwd_{l}")
    g["conv_w"] = dconv_w[:CONV_WIDTH]
    g["conv_b"], g["conv_ln_g"], g["conv_ln_b"] = dconv_v[0:1], dconv_v[1:2], dconv_v[2:3]
    dz = jnp.concatenate([dag, dq, dk, dv, dfl.astype(BF16)], axis=1)
    dx0, g["norm_mix_pre"] = _mm_prenorm_bwd(dz, w["w_in_cat"], sv["x0"], w["norm_mix_pre"], dx1, tile, f"mix_in_bwd_{l}")
    dw_in = _matmul_tn(sv["h1"], dz, 256, tk, f"dw_in_{l}")[:, :IN_COLS]
    g["w_in"] = dw_in.reshape(D_MODEL, N_DEV, IN_COLS // N_DEV).transpose(1, 0, 2).astype(BF16)
    return dx0, g, rest_land, dkv_land


def _sum_blocks(a, name):
    n, rows, cols = a.shape

    def body(a_ref, o_ref):
        acc = a_ref[0]
        for j in range(1, n):
            acc = acc + a_ref[j]
        o_ref[...] = acc

    return pl.pallas_call(
        body, name=name, in_specs=[_full((n, rows, cols))], out_specs=_full((rows, cols)),
        out_shape=jax.ShapeDtypeStruct((rows, cols), F32), grid=(1,),
    )(a)


def _adamw(gparts, w, m, v, tile, name):
    n, rows, cols = gparts.shape
    c1 = 1.0 - ADAM_B1
    c2 = 1.0 - ADAM_B2
    bc1 = 1.0 - ADAM_B1 ** ADAM_STEP
    bc2 = 1.0 - ADAM_B2 ** ADAM_STEP

    def body(gp_ref, w_ref, m_ref, v_ref, g_ref, d_ref, mo_ref, vo_ref):
        g = gp_ref[0].astype(F32)
        for j in range(1, n):
            g = g + gp_ref[j].astype(F32)
        g_ref[...] = g
        m_new = ADAM_B1 * m_ref[...] + c1 * g
        v_new = ADAM_B2 * v_ref[...] + c2 * (g * g)
        mo_ref[...] = m_new
        vo_ref[...] = v_new
        d_ref[...] = -ADAM_LR * ((m_new / bc1) / (jnp.sqrt(v_new / bc2) + ADAM_EPS) + ADAM_WD * w_ref[...])

    row = lambda i: (i, 0)
    spec = pl.BlockSpec((tile, cols), row)
    return pl.pallas_call(
        body, name=name, grid=(rows // tile,),
        in_specs=[pl.BlockSpec((n, tile, cols), lambda i: (0, i, 0)), spec, spec, spec],
        out_specs=[spec] * 4, out_shape=[jax.ShapeDtypeStruct((rows, cols), F32)] * 4,
        compiler_params=_params("parallel"),
    )(gparts, w, m, v)


def _pack_rows(parts, total_rows):
    flat = [p.reshape(-1, D_MODEL) for p in parts]
    used = sum(f.shape[0] for f in flat)
    if total_rows > used:
        flat.append(jnp.zeros((total_rows - used, D_MODEL), flat[0].dtype))
    return jnp.concatenate(flat, axis=0)


def _pad_lanes(v, n=D_MODEL):
    v = v.reshape(-1, v.shape[-1])
    return jnp.pad(v, ((0, 0), (0, n - v.shape[-1])))


def kernel(x, mem, norm_mix_pre, norm_mix_post, w_in, b_forget, conv_w, conv_b, conv_ln_g, conv_ln_b, w_out, norm_mem_pre, norm_mem_post, norm_memkv, w_mq, w_mk, w_mv, w_mo, norm_mlp_pre, norm_mlp_post, w_up, w_down, loss_target, m_norm_mix_pre, m_norm_mix_post, m_w_in, m_b_forget, m_conv_w, m_conv_b, m_conv_ln_g, m_conv_ln_b, m_w_out, m_norm_mem_pre, m_norm_mem_post, m_norm_memkv, m_w_mq, m_w_mk, m_w_mv, m_w_mo, m_norm_mlp_pre, m_norm_mlp_post, m_w_up, m_w_down, v_norm_mix_pre, v_norm_mix_post, v_w_in, v_b_forget, v_conv_w, v_conv_b, v_conv_ln_g, v_conv_ln_b, v_w_out, v_norm_mem_pre, v_norm_mem_post, v_norm_memkv, v_w_mq, v_w_mk, v_w_mv, v_w_mo, v_norm_mlp_pre, v_norm_mlp_post, v_w_up, v_w_down):
    p = dict(locals())
    names = ("norm_mix_pre", "norm_mix_post", "w_in", "b_forget", "conv_w", "conv_b", "conv_ln_g", "conv_ln_b", "w_out",
             "norm_mem_pre", "norm_mem_post", "norm_memkv", "w_mq", "w_mk", "w_mv", "w_mo", "norm_mlp_pre",
             "norm_mlp_post", "w_up", "w_down")
    me = 4 * lax.axis_index("x") + 2 * lax.axis_index("y") + lax.axis_index("c")
    conv_cols = conv_w.shape[2]
    s_len = x.shape[1]

    def shard(n, l):
        return p[n][l].astype(BF16)

    conv_pack = _pack_rows([jnp.pad(conv_w, ((0, 0), (0, CONV_PAD - CONV_WIDTH), (0, 0)))], 8)
    win_land, conv_land = _exchange([shard("w_in", 0), conv_pack], [True, True], "gather_first")
    conv_rows = DEPTH * CONV_PAD * conv_cols // D_MODEL
    conv_full = conv_land[:, :conv_rows].reshape(N_DEV, DEPTH, CONV_PAD, conv_cols)
    conv_full = conv_full.transpose(1, 2, 0, 3).reshape(DEPTH, CONV_PAD, N_DEV * conv_cols)

    def first_weights(l, land):
        w = {"w_in_cat": _w_in_cat(land), "conv_w": conv_full[l]}
        for n in VEC:
            w[n] = p[n][l][None, :]
        w["b_forget"] = _pad_lanes(w["b_forget"], LANES)
        return w

    h, sv0, w0, win1_land = _layer_fwd(x[0], mem[0], first_weights(0, win_land), 0,
                                       rest_src=[shard(n, 0) for n in REST], next_src=[shard("w_in", 1)])
    h, sv1, w1, _ = _layer_fwd(h, mem[0], first_weights(1, win1_land[0]), 1, rest_src=[shard(n, 1) for n in REST])
    dh, loss_row = _loss_head(h, loss_target[0], min(512, s_len), "loss_head")
    dh, g1, rest_g1, _ = _layer_bwd(dh, mem[0], w1, sv1, 1, scatter_rest=True)
    grad_x, g0, rest_g0, win1_g = _layer_bwd(dh, mem[0], w0, sv0, 0, scatter_rest=True, dkv_src=[g1["w_in"]])
    grads = [g0, g1]

    small = [loss_row] + [_pad_lanes(grads[l][n]) for l in range(DEPTH) for n in VEC]
    small += [_pad_lanes(grads[l]["conv_w"].reshape(1, -1), 16 * D_MODEL).reshape(16, D_MODEL) for l in range(DEPTH)]
    n_small = 1 + DEPTH * len(VEC) + 16 * DEPTH
    win0_g, small_land = _exchange([g0["w_in"], _pack_rows(small, -(-n_small // 8) * 8)], [False, True], "scatter_last")
    total = _sum_blocks(small_land, "sum_small")
    loss = jnp.sum(total[0])
    gv = {}
    row = 1
    for l in range(DEPTH):
        for n in VEC:
            gv[(n, l)] = total[row, :p[n].shape[1]]
            row += 1
    gconv = []
    for l in range(DEPTH):
        full = total[row:row + 16].reshape(-1)[:CONV_WIDTH * CONV_CH].reshape(CONV_WIDTH, CONV_CH)
        gconv.append(lax.dynamic_slice_in_dim(full, me * conv_cols, conv_cols, axis=1))
        row += 16

    def small_pack(vec, conv):
        rows = [_pad_lanes(vec(n, l)[None, :]) for l in range(DEPTH) for n in VEC]
        rows += [_pad_lanes(conv(l).reshape(1, -1), 2 * D_MODEL).reshape(2, D_MODEL) for l in range(DEPTH)]
        return _pack_rows(rows, SMALL_ROWS)

    def from_inputs(prefix):
        return small_pack(lambda n, l: p[prefix + n][l], lambda l: p[prefix + "conv_w"][l])

    g_small = small_pack(lambda n, l: gv[(n, l)], lambda l: gconv[l])
    small_out = _adamw(g_small[None], from_inputs(""), from_inputs("m_"), from_inputs("v_"), SMALL_ROWS, "adamw_small")

    def unpack_small(buf):
        out = {}
        row = 0
        for l in range(DEPTH):
            for n in VEC:
                out.setdefault(n, []).append(buf[row, :p[n].shape[1]])
                row += 1
        for l in range(DEPTH):
            out.setdefault("conv_w", []).append(buf[row:row + 2].reshape(-1)[:CONV_WIDTH * conv_cols]
                                                .reshape(CONV_WIDTH, conv_cols))
            row += 2
        return {n: jnp.stack(v) for n, v in out.items()}

    landed = {("w_in", 0): win0_g, ("w_in", 1): win1_g[0]}
    for i, n in enumerate(REST):
        landed[(n, 0)], landed[(n, 1)] = rest_g0[i], rest_g1[i]
    big_out = {}
    for n in BIG:
        per_layer = [_adamw(landed[(n, l)], p[n][l], p["m_" + n][l], p["v_" + n][l], ADAMW_TILE[n], f"adamw_{n}_{l}")
                     for l in range(DEPTH)]
        big_out[n] = [jnp.stack([per_layer[l][k] for l in range(DEPTH)]) for k in range(4)]

    result = [loss, grad_x[None]]
    for k in range(4):
        smalls = unpack_small(small_out[k])
        result += [big_out[n][k] if n in big_out else smalls[n] for n in names]
    return tuple(result)
```

```python
import functools

import jax
import jax.numpy as jnp
from jax import lax
from jax.experimental import pallas as pl
from jax.experimental.pallas import tpu as pltpu

F32 = jnp.float32
BF16 = jnp.bfloat16

N_DEV = 8
DEPTH = 2
D_MODEL = 1024
CONV_CH = 512
CONV_WIDTH = 31
CONV_PAD = 32
FOX_HEADS = 8
FOX_HEAD_DIM = 64
FOX_WIDTH = 512
N_PAIRS = 4
MEM_HEADS = 4
MEM_HEAD_DIM = 128
MEM_INNER = 512
D_FF = 4096
IN_MAIN = 2560
IN_COLS = 2568
IN_CAT = IN_MAIN + 128
LANES = 128
EPS = 1e-6
NEG_INF = -1e30

ADAM_LR = 0.001
ADAM_B1 = 0.9
ADAM_B2 = 0.999
ADAM_EPS = 1e-08
ADAM_WD = 0.01
ADAM_STEP = 10

NT_DIMS = (((1,), (1,)), ((), ()))
TN_DIMS = (((0,), (0,)), ((), ()))

BIG = ("w_in", "w_out", "w_mq", "w_mk", "w_mv", "w_mo", "w_up", "w_down")
ADAMW_TILE = {"w_in": 256, "w_out": 128, "w_mq": 128, "w_mk": 128, "w_mv": 128, "w_mo": 512, "w_up": 256, "w_down": 128}

VEC = ("norm_mix_pre", "norm_mix_post", "norm_mem_pre", "norm_mem_post", "norm_memkv", "norm_mlp_pre", "norm_mlp_post",
       "b_forget", "conv_b", "conv_ln_g", "conv_ln_b")
SMALL_ROWS = 32


def _sigmoid(x):
    return 1.0 / (1.0 + jnp.exp(-x))


def _rms(x, g):
    r = lax.rsqrt(jnp.mean(x * x, axis=-1, keepdims=True) + EPS)
    return x * r * g


def _rms_bwd(x, g, dh):
    r = lax.rsqrt(jnp.mean(x * x, axis=-1, keepdims=True) + EPS)
    gh = dh * g
    c = jnp.mean(gh * x, axis=-1, keepdims=True)
    dx = r * gh - x * (r * r * r * c)
    dg = jnp.sum(dh * (x * r), axis=0, keepdims=True)
    return dx, dg


def _full(shape):
    nd = len(shape)
    return pl.BlockSpec(shape, lambda *_: (0,) * nd)


def _params(*sem):
    return pltpu.CompilerParams(dimension_semantics=sem)


def _exchange_copies(src_refs, out_refs, same, send_sems, recv_sems, local_sems, with_recvs):
    x, y, c = lax.axis_index("x"), lax.axis_index("y"), lax.axis_index("c")
    me = 4 * x + 2 * y + c
    local, sends, recvs = [], [], []
    for a, (s_ref, o_ref) in enumerate(zip(src_refs, out_refs)):
        def mine(idx, s_ref=s_ref, whole=same[a]):
            return s_ref if whole else s_ref.at[idx]

        local.append(pltpu.make_async_copy(mine(me), o_ref.at[me], local_sems.at[a]))
        for k in range(1, N_DEV):
            px = 1 - x if k & 4 else x
            py = 1 - y if k & 2 else y
            pc = 1 - c if k & 1 else c
            peer = 4 * px + 2 * py + pc
            sem = a * (N_DEV - 1) + k - 1
            common = dict(send_sem=send_sems.at[sem], recv_sem=recv_sems.at[sem], device_id=(px, py, pc),
                          device_id_type=pl.DeviceIdType.MESH)
            sends.append(pltpu.make_async_remote_copy(src_ref=mine(peer), dst_ref=o_ref.at[me], **common))
            if with_recvs:
                recvs.append(pltpu.make_async_remote_copy(src_ref=mine(peer), dst_ref=o_ref.at[peer], **common))
    return local, sends, recvs


def _pcall(body, *, name, grid, in_specs, out_specs, out_shape, args, scratch_shapes=(), sem=(), host=None):
    if host is None:
        return pl.pallas_call(body, name=name, grid=grid, in_specs=in_specs, out_specs=out_specs, out_shape=out_shape,
                              scratch_shapes=list(scratch_shapes), compiler_params=_params(*sem))(*args)
    srcs, same = host
    n_in, n_out, n_scr, n_h = len(in_specs), len(out_specs), len(scratch_shapes), len(srcs)
    hbm = pl.BlockSpec(memory_space=pltpu.HBM)
    lands = [jax.ShapeDtypeStruct((N_DEV,) + (s.shape if whole else s.shape[1:]), s.dtype) for s, whole in zip(srcs, same)]

    def wrapped(*refs):
        ins, src_refs = refs[:n_in], refs[n_in:n_in + n_h]
        outs = refs[n_in + n_h:n_in + n_h + n_out]
        land_refs = refs[n_in + n_h + n_out:n_in + 2 * n_h + n_out]
        scr = refs[n_in + 2 * n_h + n_out:n_in + 2 * n_h + n_out + n_scr]
        sems = refs[n_in + 2 * n_h + n_out + n_scr:]
        first = functools.reduce(jnp.logical_and, [pl.program_id(d) == 0 for d in range(len(grid))])
        last = functools.reduce(jnp.logical_and, [pl.program_id(d) == grid[d] - 1 for d in range(len(grid))])

        @pl.when(first)
        def _():
            local, sends, _ = _exchange_copies(src_refs, land_refs, same, *sems, False)
            for cp in local + sends:
                cp.start()

        body(*ins, *outs, *scr)

        @pl.when(last)
        def _():
            local, sends, recvs = _exchange_copies(src_refs, land_refs, same, *sems, True)
            for cp in recvs:
                cp.wait_recv()
            for cp in sends:
                cp.wait_send()
            for cp in local:
                cp.wait()

    n_sem = n_h * (N_DEV - 1)
    res = pl.pallas_call(
        wrapped, name=name, grid=grid, in_specs=list(in_specs) + [hbm] * n_h, out_specs=list(out_specs) + [hbm] * n_h,
        out_shape=list(out_shape) + lands,
        scratch_shapes=list(scratch_shapes) + [pltpu.SemaphoreType.DMA((n_sem,)), pltpu.SemaphoreType.DMA((n_sem,)),
                                               pltpu.SemaphoreType.DMA((n_h,))],
        compiler_params=_params(*(("arbitrary",) * len(grid))),
    )(*args, *srcs)
    return list(res[:n_out]), list(res[n_out:])


def _exchange(srcs, same, name):
    def body():
        pass

    return _pcall(body, name=name, grid=(1,), in_specs=[], out_specs=[], out_shape=[], args=[], host=(srcs, same))[1]


def _rms_matmul(x, g, w, segs, tile, name, host=None):
    s_len, d = x.shape
    n = w.shape[1]
    chunk = 512

    def body(x_ref, g_ref, w_ref, h_ref, *outs):
        h = _rms(x_ref[...], g_ref[...]).astype(BF16)
        h_ref[...] = h
        oi = 0
        for c0, c1, fns in segs:
            for a in range(c0, c1, chunk):
                b = min(a + chunk, c1)
                z = jnp.dot(h, w_ref[:, a:b], preferred_element_type=F32)
                for k, (dt, fn) in enumerate(fns):
                    outs[oi + k][:, a - c0:b - c0] = fn(z).astype(dt)
            oi += len(fns)

    out_shape = [jax.ShapeDtypeStruct((s_len, d), BF16)]
    out_specs = [pl.BlockSpec((tile, d), lambda i: (i, 0))]
    for c0, c1, fns in segs:
        for dt, _ in fns:
            out_shape.append(jax.ShapeDtypeStruct((s_len, c1 - c0), dt))
            out_specs.append(pl.BlockSpec((tile, c1 - c0), lambda i: (i, 0)))
    return _pcall(
        body, name=name, grid=(s_len // tile,),
        in_specs=[pl.BlockSpec((tile, d), lambda i: (i, 0)), _full((1, d)), _full((d, n))],
        out_specs=out_specs, out_shape=out_shape, args=[x, g, w], sem=("parallel",), host=host)


def _matmul_resnorm(a, w, x, g, tile, name):
    s_len, k = a.shape
    d = w.shape[1]

    def body(a_ref, w_ref, x_ref, g_ref, y_ref, xo_ref):
        y = jnp.dot(a_ref[...], w_ref[...], preferred_element_type=F32)
        y_ref[...] = y
        xo_ref[...] = x_ref[...] + _rms(y, g_ref[...])

    row = lambda i: (i, 0)
    return pl.pallas_call(
        body, name=name, grid=(s_len // tile,),
        in_specs=[pl.BlockSpec((tile, k), row), _full((k, d)), pl.BlockSpec((tile, d), row), _full((1, d))],
        out_specs=[pl.BlockSpec((tile, d), row), pl.BlockSpec((tile, d), row)],
        out_shape=[jax.ShapeDtypeStruct((s_len, d), F32), jax.ShapeDtypeStruct((s_len, d), F32)],
        compiler_params=_params("parallel"),
    )(a, w, x, g)


def _resnorm_bwd_mm(dx, y, g, w, tile, name, out_dtype, pre=None):
    s_len, d = dx.shape
    k = w.shape[0]
    chunk = 512

    def body(*refs):
        if pre is None:
            dx_ref, y_ref, g_ref, w_ref, dy_ref, da_ref, dg_ref = refs
        else:
            dx_ref, y_ref, g_ref, w_ref, pre_ref, dy_ref, da_ref, dg_ref = refs
        dy, dg = _rms_bwd(y_ref[...], g_ref[...], dx_ref[...])
        dyb = dy.astype(BF16)
        dy_ref[...] = dyb

        @pl.when(pl.program_id(0) == 0)
        def _():
            dg_ref[...] = jnp.zeros_like(dg_ref)

        dg_ref[...] += dg
        for a in range(0, k, chunk):
            b = min(a + chunk, k)
            da = lax.dot_general(dyb, w_ref[a:b, :], NT_DIMS, preferred_element_type=F32)
            if pre is not None:
                da = da * (2.0 * jnp.maximum(pre_ref[:, a:b].astype(F32), 0.0))
            da_ref[:, a:b] = da.astype(out_dtype)

    row = lambda i: (i, 0)
    in_specs = [pl.BlockSpec((tile, d), row), pl.BlockSpec((tile, d), row), _full((1, d)), _full((k, d))]
    args = [dx, y, g, w]
    if pre is not None:
        in_specs.append(pl.BlockSpec((tile, k), row))
        args.append(pre)
    return pl.pallas_call(
        body, name=name, grid=(s_len // tile,), in_specs=in_specs,
        out_specs=[pl.BlockSpec((tile, d), row), pl.BlockSpec((tile, k), row), _full((1, d))],
        out_shape=[jax.ShapeDtypeStruct((s_len, d), BF16), jax.ShapeDtypeStruct((s_len, k), out_dtype),
                   jax.ShapeDtypeStruct((1, d), F32)],
        compiler_params=_params("arbitrary"),
    )(*args)


def _mm_prenorm_bwd(dz, w, x, g, dres, tile, name):
    s_len, n = dz.shape
    d = w.shape[0]

    def body(*refs):
        if dres is None:
            dz_ref, w_ref, x_ref, g_ref, dx_ref, dg_ref = refs
        else:
            dz_ref, w_ref, x_ref, g_ref, dres_ref, dx_ref, dg_ref = refs
        dh = lax.dot_general(dz_ref[...], w_ref[...], NT_DIMS, preferred_element_type=F32)
        dx, dg = _rms_bwd(x_ref[...], g_ref[...], dh)
        if dres is not None:
            dx = dx + dres_ref[...]
        dx_ref[...] = dx

        @pl.when(pl.program_id(0) == 0)
        def _():
            dg_ref[...] = jnp.zeros_like(dg_ref)

        dg_ref[...] += dg

    row = lambda i: (i, 0)
    in_specs = [pl.BlockSpec((tile, n), row), _full((d, n)), pl.BlockSpec((tile, d), row), _full((1, d))]
    args = [dz, w, x, g]
    if dres is not None:
        in_specs.append(pl.BlockSpec((tile, d), row))
        args.append(dres)
    return pl.pallas_call(
        body, name=name, grid=(s_len // tile,), in_specs=in_specs,
        out_specs=[pl.BlockSpec((tile, d), row), _full((1, d))],
        out_shape=[jax.ShapeDtypeStruct((s_len, d), F32), jax.ShapeDtypeStruct((1, d), F32)],
        compiler_params=_params("arbitrary"),
    )(*args)


def _matmul_tn(a, b, tm, tk, name):
    s_len, m = a.shape
    n = b.shape[1]

    def body(a_ref, b_ref, o_ref):
        @pl.when(pl.program_id(1) == 0)
        def _():
            o_ref[...] = jnp.zeros_like(o_ref)

        o_ref[...] += lax.dot_general(a_ref[...], b_ref[...], TN_DIMS, preferred_element_type=F32)

    return pl.pallas_call(
        body, name=name, grid=(m // tm, s_len // tk),
        in_specs=[pl.BlockSpec((tk, tm), lambda i, k: (k, i)), pl.BlockSpec((tk, n), lambda i, k: (k, 0))],
        out_specs=pl.BlockSpec((tm, n), lambda i, k: (i, 0)),
        out_shape=jax.ShapeDtypeStruct((m, n), F32),
        compiler_params=_params("parallel", "arbitrary"),
    )(a, b)


def _matmul_tn_shards(a, b, axis, tk, name):
    s_len, m = a.shape
    n = b.shape[1]
    r, c = (m // N_DEV, n) if axis == 0 else (m, n // N_DEV)
    n_k = s_len // tk
    tm = max(r, min(m, (1 << 20) // n)) if axis == 0 else min(m, (1 << 20) // n)

    def body(a_ref, b_ref, o_ref, acc):
        k = pl.program_id(1)

        @pl.when(k == 0)
        def _():
            acc[...] = jnp.zeros_like(acc)

        acc[...] += lax.dot_general(a_ref[...], b_ref[...], TN_DIMS, preferred_element_type=F32)

        @pl.when(k == n_k - 1)
        def _():
            if axis == 0:
                o_ref[...] = acc[...].reshape(tm // r, r, c).astype(BF16)
            else:
                for j in range(N_DEV):
                    o_ref[j] = acc[:, j * c:(j + 1) * c].astype(BF16)

    if axis == 0:
        out_spec = pl.BlockSpec((tm // r, r, c), lambda i, k: (i, 0, 0))
    else:
        out_spec = pl.BlockSpec((N_DEV, tm, c), lambda i, k: (0, i, 0))
    return pl.pallas_call(
        body, name=name, grid=(m // tm, n_k),
        in_specs=[pl.BlockSpec((tk, tm), lambda i, k: (k, i)), pl.BlockSpec((tk, n), lambda i, k: (k, 0))],
        out_specs=out_spec, out_shape=jax.ShapeDtypeStruct((N_DEV, r, c), BF16),
        scratch_shapes=[pltpu.VMEM((tm, n), F32)],
        compiler_params=_params("parallel", "arbitrary"),
    )(a, b)


def _cumsum_fwd(fl, b, tile, name):
    s_len = fl.shape[0]

    def body(fl_ref, b_ref, cum_ref, cumt_ref, carry):
        @pl.when(pl.program_id(0) == 0)
        def _():
            carry[...] = jnp.zeros_like(carry)

        xx = fl_ref[...] + b_ref[...]
        lf = jnp.minimum(xx, 0.0) - jnp.log1p(jnp.exp(-jnp.abs(xx)))
        r = lax.broadcasted_iota(jnp.int32, (tile, tile), 0)
        c = lax.broadcasted_iota(jnp.int32, (tile, tile), 1)
        tri = (c <= r).astype(F32)
        cs = jnp.dot(tri, lf, precision=lax.Precision.HIGHEST, preferred_element_type=F32) + carry[...]
        cum_ref[...] = cs
        cumt_ref[...] = cs.T[0:FOX_HEADS, :]
        carry[...] = cs[tile - 1:tile, :]

    return pl.pallas_call(
        body, name=name, grid=(s_len // tile,),
        in_specs=[pl.BlockSpec((tile, LANES), lambda i: (i, 0)), _full((1, LANES))],
        out_specs=[pl.BlockSpec((tile, LANES), lambda i: (i, 0)), pl.BlockSpec((FOX_HEADS, tile), lambda i: (0, i))],
        out_shape=[jax.ShapeDtypeStruct((s_len, LANES), F32), jax.ShapeDtypeStruct((FOX_HEADS, s_len), F32)],
        scratch_shapes=[pltpu.VMEM((1, LANES), F32)],
        compiler_params=_params("arbitrary"),
    )(fl, b)


def _cumsum_bwd(dcum, fl, b, tile, name):
    s_len = fl.shape[0]
    n_t = s_len // tile

    def body(dc_ref, fl_ref, b_ref, dfl_ref, db_ref, carry):
        @pl.when(pl.program_id(0) == 0)
        def _():
            carry[...] = jnp.zeros_like(carry)
            db_ref[...] = jnp.zeros_like(db_ref)

        r = lax.broadcasted_iota(jnp.int32, (tile, tile), 0)
        c = lax.broadcasted_iota(jnp.int32, (tile, tile), 1)
        tri = (c >= r).astype(F32)
        dl = jnp.dot(tri, dc_ref[...], precision=lax.Precision.HIGHEST, preferred_element_type=F32) + carry[...]
        carry[...] = dl[0:1, :]
        dfl = dl * _sigmoid(-(fl_ref[...] + b_ref[...]))
        dfl_ref[...] = dfl
        db_ref[...] += jnp.sum(dfl, axis=0, keepdims=True)

    rev = lambda i: (n_t - 1 - i, 0)
    return pl.pallas_call(
        body, name=name, grid=(n_t,),
        in_specs=[pl.BlockSpec((tile, LANES), rev), pl.BlockSpec((tile, LANES), rev), _full((1, LANES))],
        out_specs=[pl.BlockSpec((tile, LANES), rev), _full((1, LANES))],
        out_shape=[jax.ShapeDtypeStruct((s_len, LANES), F32), jax.ShapeDtypeStruct((1, LANES), F32)],
        scratch_shapes=[pltpu.VMEM((1, LANES), F32)],
        compiler_params=_params("arbitrary"),
    )(dcum, fl, b)


def _conv_taps(w_ref, ext_ref, base, tile, reverse):
    acc = None
    for k in range(CONV_WIDTH):
        off = base + ((CONV_WIDTH - 1 - k) if reverse else k)
        term = w_ref[k:k + 1, :] * ext_ref[pl.ds(off, tile), :]
        acc = term if acc is None else acc + term
    return acc


def _conv_fwd(ag, w, cb, lg, lb, tile, name):
    s_len = ag.shape[0]
    c = CONV_CH

    def body(ag_ref, w_ref, cb_ref, lg_ref, lb_ref, u_ref, ext):
        @pl.when(pl.program_id(0) == 0)
        def _():
            ext[0:CONV_PAD, :] = jnp.zeros((CONV_PAD, c), F32)

        ext[CONV_PAD:CONV_PAD + tile, :] = ag_ref[:, 0:c] * _sigmoid(ag_ref[:, c:2 * c])
        u1 = _conv_taps(w_ref, ext, CONV_PAD - (CONV_WIDTH - 1), tile, False) + cb_ref[...]
        mu = jnp.mean(u1, axis=-1, keepdims=True)
        xc = u1 - mu
        y = xc * lax.rsqrt(jnp.mean(xc * xc, axis=-1, keepdims=True) + EPS) * lg_ref[...] + lb_ref[...]
        u_ref[...] = (y * _sigmoid(y)).astype(BF16)
        ext[0:CONV_PAD, :] = ext[tile:tile + CONV_PAD, :]

    return pl.pallas_call(
        body, name=name, grid=(s_len // tile,),
        in_specs=[pl.BlockSpec((tile, 2 * c), lambda i: (i, 0)), _full((CONV_PAD, c)), _full((1, c)), _full((1, c)),
                  _full((1, c))],
        out_specs=pl.BlockSpec((tile, c), lambda i: (i, 0)),
        out_shape=jax.ShapeDtypeStruct((s_len, c), BF16),
        scratch_shapes=[pltpu.VMEM((tile + CONV_PAD, c), F32)],
        compiler_params=_params("arbitrary"),
    )(ag, w, cb, lg, lb)


def _conv_bwd(ag, dcat, w, cb, lg, lb, tile, name):
    s_len = ag.shape[0]
    c = CONV_CH
    n_t = s_len // tile
    per = tile // CONV_PAD

    def body(ag_ref, halo_ref, du_ref, w_ref, cb_ref, lg_ref, lb_ref, dag_ref, dw_ref, dv_ref, ext, ext2):
        i = pl.program_id(0)
        t = n_t - 1 - i

        @pl.when(i == 0)
        def _():
            ext2[tile:tile + CONV_PAD, :] = jnp.zeros((CONV_PAD, c), F32)
            dw_ref[...] = jnp.zeros_like(dw_ref)
            dv_ref[...] = jnp.zeros_like(dv_ref)

        a = ag_ref[:, 0:c]
        sg = _sigmoid(ag_ref[:, c:2 * c])
        halo = halo_ref[:, 0:c] * _sigmoid(halo_ref[:, c:2 * c])
        ext[0:CONV_PAD, :] = jnp.where(t > 0, halo, 0.0)
        ext[CONV_PAD:CONV_PAD + tile, :] = a * sg
        u1 = _conv_taps(w_ref, ext, CONV_PAD - (CONV_WIDTH - 1), tile, False) + cb_ref[...]
        mu = jnp.mean(u1, axis=-1, keepdims=True)
        xc = u1 - mu
        rs = lax.rsqrt(jnp.mean(xc * xc, axis=-1, keepdims=True) + EPS)
        xhat = xc * rs
        y = xhat * lg_ref[...] + lb_ref[...]
        sy = _sigmoid(y)
        dy = du_ref[...] * (sy * (1.0 + y * (1.0 - sy)))
        dxh = dy * lg_ref[...]
        du1 = rs * (dxh - jnp.mean(dxh, axis=-1, keepdims=True) - xhat * jnp.mean(dxh * xhat, axis=-1, keepdims=True))
        dv_ref[0:1, :] += jnp.sum(du1, axis=0, keepdims=True)
        dv_ref[1:2, :] += jnp.sum(dy * xhat, axis=0, keepdims=True)
        dv_ref[2:3, :] += jnp.sum(dy, axis=0, keepdims=True)
        for k in range(CONV_WIDTH):
            off = CONV_PAD - (CONV_WIDTH - 1) + k
            dw_ref[k:k + 1, :] += jnp.sum(du1 * ext[pl.ds(off, tile), :], axis=0, keepdims=True)
        ext2[0:tile, :] = du1
        du0 = _conv_taps(w_ref, ext2, 0, tile, True)
        ext2[tile:tile + CONV_PAD, :] = du1[0:CONV_PAD, :]
        dag_ref[:, 0:c] = (du0 * sg).astype(BF16)
        dag_ref[:, c:2 * c] = (du0 * a * sg * (1.0 - sg)).astype(BF16)

    rev = lambda i: (n_t - 1 - i, 0)
    return pl.pallas_call(
        body, name=name, grid=(n_t,),
        in_specs=[pl.BlockSpec((tile, 2 * c), rev),
                  pl.BlockSpec((CONV_PAD, 2 * c), lambda i: (jnp.maximum((n_t - 1 - i) * per - 1, 0), 0)),
                  pl.BlockSpec((tile, c), rev), _full((CONV_PAD, c)), _full((1, c)), _full((1, c)), _full((1, c))],
        out_specs=[pl.BlockSpec((tile, 2 * c), rev), _full((CONV_PAD, c)), _full((8, c))],
        out_shape=[jax.ShapeDtypeStruct((s_len, 2 * c), BF16), jax.ShapeDtypeStruct((CONV_PAD, c), F32),
                   jax.ShapeDtypeStruct((8, c), F32)],
        scratch_shapes=[pltpu.VMEM((tile + CONV_PAD, c), F32), pltpu.VMEM((tile + CONV_PAD, c), F32)],
        compiler_params=_params("arbitrary"),
    )(ag, ag, dcat, w, cb, lg, lb)


def _lane_masks():
    lane = lax.broadcasted_iota(jnp.int32, (1, LANES), 1)
    return lane, (lane < FOX_HEAD_DIM, lane >= FOX_HEAD_DIM)


def _head_col(x, lane, h):
    return jnp.sum(jnp.where(lane == h, x, 0.0), axis=1, keepdims=True)


def _fox_fwd(qkv, cum, cumt, ta, name, host=None):
    s_len = qkv.shape[0]
    n_t = s_len // ta
    scale = FOX_HEAD_DIM ** -0.5

    def body(q_ref, k_ref, v_ref, cq_ref, ct_ref, o_ref, lse_ref):
        p = pl.program_id(0)
        i = pl.program_id(1)
        lane, masks = _lane_masks()
        q = q_ref[...]
        qh = [jnp.where(m, q, jnp.zeros_like(q)) for m in masks]
        cq = cq_ref[...]
        cqc = [_head_col(cq, lane, 2 * p + hh) for hh in range(2)]
        r = lax.broadcasted_iota(jnp.int32, (ta, ta), 0)
        c = lax.broadcasted_iota(jnp.int32, (ta, ta), 1)
        causal = c <= r

        def make_step(masked):
            def step(j, carry):
                off = pl.multiple_of(j * ta, ta)
                kt = k_ref[pl.ds(off, ta), :]
                vt = v_ref[pl.ds(off, ta), :]
                new = []
                for hh in range(2):
                    m, l, acc = carry[3 * hh:3 * hh + 3]
                    s = lax.dot_general(qh[hh], kt, NT_DIMS, preferred_element_type=F32) * scale
                    s = s + cqc[hh] - ct_ref[hh, j]
                    if masked:
                        s = jnp.where(causal, s, NEG_INF)
                    m_new = jnp.maximum(m, jnp.max(s, axis=1, keepdims=True))
                    alpha = jnp.exp(m - m_new)
                    pr = jnp.exp(s - m_new)
                    l = alpha * l + jnp.sum(pr, axis=1, keepdims=True)
                    acc = alpha * acc + jnp.dot(pr.astype(BF16), vt, preferred_element_type=F32)
                    new += [m_new, l, acc]
                return tuple(new)
            return step

        init = (jnp.full((ta, 1), NEG_INF, F32), jnp.zeros((ta, 1), F32), jnp.zeros((ta, LANES), F32)) * 2
        carry = lax.fori_loop(0, i, make_step(False), init)
        carry = make_step(True)(i, carry)
        m0, l0, a0, m1, l1, a1 = carry
        o_ref[...] = jnp.where(masks[0], a0 / l0, a1 / l1)
        lse_ref[...] = jnp.where(masks[0], m0 + jnp.log(l0), m1 + jnp.log(l1))

    blk = lambda col0: pl.BlockSpec((s_len, LANES), lambda p, i: (0, col0 + p))
    tile_spec = pl.BlockSpec((ta, LANES), lambda p, i: (i, p))
    return _pcall(
        body, name=name, grid=(N_PAIRS, n_t),
        in_specs=[tile_spec, blk(N_PAIRS), blk(2 * N_PAIRS), pl.BlockSpec((ta, LANES), lambda p, i: (i, 0)),
                  pl.BlockSpec((2, n_t, 1, ta), lambda p, i: (p, 0, 0, 0))],
        out_specs=[tile_spec, tile_spec],
        out_shape=[jax.ShapeDtypeStruct((s_len, FOX_WIDTH), F32), jax.ShapeDtypeStruct((s_len, FOX_WIDTH), F32)],
        args=[qkv, qkv, qkv, cum, cumt], sem=("parallel", "parallel"), host=host)


def _fox_dq(qkv, cum, cumt, o, dcat, lse, ta, name, host=None):
    s_len = qkv.shape[0]
    n_t = s_len // ta
    scale = FOX_HEAD_DIM ** -0.5

    def body(q_ref, k_ref, v_ref, cq_ref, ct_ref, o_ref, do_ref, lse_ref, dq_ref, dl_ref, dck_ref):
        p = pl.program_id(0)
        i = pl.program_id(1)
        lane, masks = _lane_masks()

        @pl.when(i == 0)
        def _():
            dck_ref[...] = jnp.zeros_like(dck_ref)

        q = q_ref[...]
        do = do_ref[...]
        prod = do * o_ref[...]
        dlt = [jnp.sum(jnp.where(m, prod, 0.0), axis=1, keepdims=True) for m in masks]
        dl_ref[...] = jnp.where(masks[0], dlt[0], dlt[1])
        dob = do.astype(BF16)
        qh = [jnp.where(m, q, jnp.zeros_like(q)) for m in masks]
        doh = [jnp.where(m, dob, jnp.zeros_like(dob)) for m in masks]
        lse = lse_ref[...]
        lse_h = [lse[:, 0:1], lse[:, FOX_HEAD_DIM:FOX_HEAD_DIM + 1]]
        cq = cq_ref[...]
        cqc = [_head_col(cq, lane, 2 * p + hh) for hh in range(2)]
        r = lax.broadcasted_iota(jnp.int32, (ta, ta), 0)
        c = lax.broadcasted_iota(jnp.int32, (ta, ta), 1)
        causal = c <= r

        def make_step(masked):
            def step(j, carry):
                off = pl.multiple_of(j * ta, ta)
                kt = k_ref[pl.ds(off, ta), :]
                vt = v_ref[pl.ds(off, ta), :]
                new = []
                for hh in range(2):
                    s = lax.dot_general(qh[hh], kt, NT_DIMS, preferred_element_type=F32) * scale
                    s = s + cqc[hh] - ct_ref[hh, j]
                    pr = jnp.exp(s - lse_h[hh])
                    if masked:
                        pr = jnp.where(causal, pr, 0.0)
                    dp = lax.dot_general(doh[hh], vt, NT_DIMS, preferred_element_type=F32)
                    ds = pr * (dp - dlt[hh])
                    dck_ref[hh, j] = dck_ref[hh, j] - jnp.sum(ds, axis=0, keepdims=True)
                    new.append(carry[hh] + jnp.dot(ds.astype(BF16), kt, preferred_element_type=F32))
                return tuple(new)
            return step

        init = (jnp.zeros((ta, LANES), F32),) * 2
        carry = lax.fori_loop(0, i, make_step(False), init)
        carry = make_step(True)(i, carry)
        dq_ref[...] = (jnp.where(masks[0], carry[0], carry[1]) * scale).astype(BF16)

    blk = lambda col0: pl.BlockSpec((s_len, LANES), lambda p, i: (0, col0 + p))
    tile_spec = pl.BlockSpec((ta, LANES), lambda p, i: (i, p))
    row4 = pl.BlockSpec((2, n_t, 1, ta), lambda p, i: (p, 0, 0, 0))
    return _pcall(
        body, name=name, grid=(N_PAIRS, n_t),
        in_specs=[tile_spec, blk(N_PAIRS), blk(2 * N_PAIRS), pl.BlockSpec((ta, LANES), lambda p, i: (i, 0)), row4,
                  tile_spec, pl.BlockSpec((ta, LANES), lambda p, i: (i, N_PAIRS + p)), tile_spec],
        out_specs=[tile_spec, tile_spec, row4],
        out_shape=[jax.ShapeDtypeStruct((s_len, FOX_WIDTH), BF16), jax.ShapeDtypeStruct((s_len, FOX_WIDTH), F32),
                   jax.ShapeDtypeStruct((FOX_HEADS, n_t, 1, ta), F32)],
        args=[qkv, qkv, qkv, cum, cumt, o, dcat, lse], sem=("parallel", "arbitrary"), host=host)


def _fox_dkv(qkv, cum, cumt, dcat, lset, deltat, ta, name, host=None):
    s_len = qkv.shape[0]
    n_t = s_len // ta
    scale = FOX_HEAD_DIM ** -0.5

    def body(k_ref, v_ref, q_ref, do_ref, ck_ref, ct_ref, lse_ref, dl_ref, dk_ref, dv_ref, dcq_ref):
        p = pl.program_id(0)
        j = pl.program_id(1)
        lane, masks = _lane_masks()

        @pl.when(j == 0)
        def _():
            dcq_ref[...] = jnp.zeros_like(dcq_ref)
        k = k_ref[...]
        v = v_ref[...]
        kh = [jnp.where(m, k, jnp.zeros_like(k)) for m in masks]
        vh = [jnp.where(m, v, jnp.zeros_like(v)) for m in masks]
        ck = ck_ref[...]
        ckc = [_head_col(ck, lane, 2 * p + hh) for hh in range(2)]
        r = lax.broadcasted_iota(jnp.int32, (ta, ta), 0)
        c = lax.broadcasted_iota(jnp.int32, (ta, ta), 1)
        causal = r <= c

        def make_step(masked):
            def step(i, carry):
                off = pl.multiple_of(i * ta, ta)
                qt = q_ref[pl.ds(off, ta), :]
                dot = do_ref[pl.ds(off, ta), :].astype(BF16)
                new = []
                for hh in range(2):
                    dk, dv = carry[2 * hh:2 * hh + 2]
                    st = lax.dot_general(kh[hh], qt, NT_DIMS, preferred_element_type=F32) * scale
                    st = st + ct_ref[hh, i] - ckc[hh]
                    pt = jnp.exp(st - lse_ref[hh, i])
                    if masked:
                        pt = jnp.where(causal, pt, 0.0)
                    dv = dv + jnp.dot(pt.astype(BF16), dot, preferred_element_type=F32)
                    dpt = lax.dot_general(vh[hh], dot, NT_DIMS, preferred_element_type=F32)
                    dst = pt * (dpt - dl_ref[hh, i])
                    dcq_ref[hh, i] = dcq_ref[hh, i] + jnp.sum(dst, axis=0, keepdims=True)
                    dk = dk + jnp.dot(dst.astype(BF16), qt, preferred_element_type=F32)
                    new += [dk, dv]
                return tuple(new)
            return step

        init = (jnp.zeros((ta, LANES), F32),) * 4
        carry = make_step(True)(j, init)
        carry = lax.fori_loop(j + 1, n_t, make_step(False), carry)
        dk_ref[...] = (jnp.where(masks[0], carry[0], carry[2]) * scale).astype(BF16)
        dv_ref[...] = jnp.where(masks[0], carry[1], carry[3]).astype(BF16)

    row4 = pl.BlockSpec((2, n_t, 1, ta), lambda p, j: (p, 0, 0, 0))
    out_spec = pl.BlockSpec((ta, LANES), lambda p, j: (j, p))
    return _pcall(
        body, name=name, grid=(N_PAIRS, n_t),
        in_specs=[pl.BlockSpec((ta, LANES), lambda p, j: (j, N_PAIRS + p)),
                  pl.BlockSpec((ta, LANES), lambda p, j: (j, 2 * N_PAIRS + p)),
                  pl.BlockSpec((s_len, LANES), lambda p, j: (0, p)),
                  pl.BlockSpec((s_len, LANES), lambda p, j: (0, N_PAIRS + p)),
                  pl.BlockSpec((ta, LANES), lambda p, j: (j, 0)), row4, row4, row4],
        out_specs=[out_spec, out_spec, row4],
        out_shape=[jax.ShapeDtypeStruct((s_len, FOX_WIDTH), BF16), jax.ShapeDtypeStruct((s_len, FOX_WIDTH), BF16),
                   jax.ShapeDtypeStruct((FOX_HEADS, n_t, 1, ta), F32)],
        args=[qkv, qkv, qkv, dcat, cum, cumt, lset, deltat], sem=("parallel", "arbitrary"), host=host)


Q_SCALE = FOX_HEAD_DIM ** -0.5


def _split3(x):
    hi = x.astype(BF16).astype(F32)
    r = x - hi
    mid = r.astype(BF16).astype(F32)
    lo = (r - mid).astype(BF16).astype(F32)
    return hi, mid, lo


def _in_lanes(lane, lo, n):
    return (lane >= lo) & (lane < lo + n)


def _put3(lane, lo, parts, rest):
    return jnp.where(lane == lo, parts[0], jnp.where(lane == lo + 1, parts[1], jnp.where(lane == lo + 2, parts[2], rest)))


def _spare_lane(h):
    return FOX_HEAD_DIM if h % 2 == 0 else 0


def _shift_div(i, num, den):
    return i * (num // den) if num >= den else lax.shift_right_logical(i, (den // num).bit_length() - 1)


def _foxa_prep(qkv, cum, tile, name):
    s_len = qkv.shape[0]

    def body(q_ref, k_ref, v_ref, c_ref, qa_ref, ka_ref, va_ref):
        lane = lax.broadcasted_iota(jnp.int32, (1, LANES), 1)
        cum_t = c_ref[...]
        for h in range(FOX_HEADS):
            e = _spare_lane(h)
            head = ~_in_lanes(lane, e, FOX_HEAD_DIM)
            blk = slice(LANES * (h // 2), LANES * (h // 2) + LANES)
            out = slice(LANES * h, LANES * h + LANES)
            c3 = _split3(_head_col(cum_t, lane, h))
            ex_q = _put3(lane, e, c3, jnp.where(_in_lanes(lane, e + 3, 3), 1.0, 0.0))
            qa_ref[:, out] = jnp.where(head, q_ref[:, blk].astype(F32) * Q_SCALE, ex_q).astype(BF16)
            ones = jnp.where(_in_lanes(lane, e, 3) | _in_lanes(lane, e + 6, 3), 1.0, 0.0)
            ex_k = _put3(lane, e + 3, [-c for c in c3], ones)
            ka_ref[:, out] = jnp.where(head, k_ref[:, blk].astype(F32), ex_k).astype(BF16)
            ex_v = jnp.where(_in_lanes(lane, e, 3), 1.0, 0.0)
            va_ref[:, out] = jnp.where(head, v_ref[:, blk].astype(F32), ex_v).astype(BF16)

    col = lambda c: pl.BlockSpec((tile, FOX_WIDTH), lambda i: (i, c))
    wide = pl.BlockSpec((tile, 2 * FOX_WIDTH), lambda i: (i, 0))
    return pl.pallas_call(
        body, name=name, grid=(s_len // tile,),
        in_specs=[col(0), col(1), col(2), pl.BlockSpec((tile, LANES), lambda i: (i, 0))],
        out_specs=[wide, wide, wide], out_shape=[jax.ShapeDtypeStruct((s_len, 2 * FOX_WIDTH), BF16)] * 3,
        compiler_params=_params("parallel"),
    )(qkv, qkv, qkv, cum)


def _foxa_prep_bwd(qa, lse, o, dcat, tile, name):
    s_len = qa.shape[0]

    def body(qa_ref, lse_ref, o_ref, do_ref, qb_ref, doa_ref):
        lane = lax.broadcasted_iota(jnp.int32, (1, LANES), 1)
        for h in range(FOX_HEADS):
            e = _spare_lane(h)
            head = ~_in_lanes(lane, e, FOX_HEAD_DIM)
            blk = slice(LANES * (h // 2), LANES * (h // 2) + LANES)
            out = slice(LANES * h, LANES * h + LANES)
            do = do_ref[:, blk]
            delta = jnp.sum(jnp.where(head, do * o_ref[:, blk], 0.0), axis=1, keepdims=True)
            doa_ref[:, out] = _put3(lane, e, _split3(-delta), jnp.where(head, do, 0.0)).astype(BF16)
            at = LANES * (h // 2) + FOX_HEAD_DIM - e
            l3 = _split3(-lse_ref[:, at:at + 1])
            qb_ref[:, out] = _put3(lane, e + 6, l3, qa_ref[:, out].astype(F32)).astype(BF16)

    wide = pl.BlockSpec((tile, 2 * FOX_WIDTH), lambda i: (i, 0))
    half = pl.BlockSpec((tile, FOX_WIDTH), lambda i: (i, 0))
    return pl.pallas_call(
        body, name=name, grid=(s_len // tile,),
        in_specs=[wide, half, half, pl.BlockSpec((tile, FOX_WIDTH), lambda i: (i, 1))],
        out_specs=[wide, wide], out_shape=[jax.ShapeDtypeStruct((s_len, 2 * FOX_WIDTH), BF16)] * 2,
        compiler_params=_params("parallel"),
    )(qa, lse, o, dcat)


def _foxa_fwd(qa, ka, va, tq, tk, name, host=None):
    s_len = qa.shape[0]

    def body(q_ref, k_ref, v_ref, o_ref, lse_ref):
        i = pl.program_id(1)
        lane = lax.broadcasted_iota(jnp.int32, (1, LANES), 1)
        cols = [slice(LANES * hh, LANES * hh + LANES) for hh in range(2)]
        qh = [q_ref[:, c] for c in cols]

        def scores(j):
            off = pl.multiple_of(j * tk, tk)
            return [lax.dot_general(qh[hh], k_ref[pl.ds(off, tk), cols[hh]], NT_DIMS, preferred_element_type=F32)
                    for hh in range(2)]

        def update(j, s, m, acc, mask):
            off = pl.multiple_of(j * tk, tk)
            m_out, acc_out = [], []
            for hh in range(2):
                sh = s[hh] if mask is None else jnp.where(mask, s[hh], NEG_INF)
                m_new = jnp.maximum(m[hh], jnp.max(sh, axis=1, keepdims=True))
                pr = jnp.exp(sh - m_new).astype(BF16)
                acc_out.append(jnp.exp(m[hh] - m_new) * acc[hh]
                               + jnp.dot(pr, v_ref[pl.ds(off, tk), cols[hh]], preferred_element_type=F32))
                m_out.append(m_new)
            return m_out, acc_out

        def step(j, carry):
            s_next = scores(j + 1)
            m, acc = update(j, carry[0:2], carry[2:4], carry[4:6], None)
            return (*s_next, *m, *acc)

        n_full = _shift_div(i, tq, tk)
        n_part = max(tq // tk, 1)
        qi = lax.broadcasted_iota(jnp.int32, (tq, tk), 0) + i * tq
        ki = lax.broadcasted_iota(jnp.int32, (tq, tk), 1)
        carry = (*scores(0), *([jnp.full((tq, 1), NEG_INF, F32)] * 2), *([jnp.zeros((tq, LANES), F32)] * 2))
        carry = lax.fori_loop(0, n_full, step, carry)
        s, m, acc = list(carry[0:2]), list(carry[2:4]), list(carry[4:6])
        for jj in range(n_part):
            s_next = scores(n_full + jj + 1) if jj < n_part - 1 else None
            m, acc = update(n_full + jj, s, m, acc, ki + (n_full + jj) * tk <= qi)
            s = s_next
        res = []
        for hh in range(2):
            l = acc[hh][:, _spare_lane(hh):_spare_lane(hh) + 1]
            res.append((acc[hh] / l, m[hh] + jnp.log(l)))
        low = lane < FOX_HEAD_DIM
        o_ref[...] = jnp.where(low, res[0][0], res[1][0])
        lse_ref[...] = jnp.where(low, res[0][1], res[1][1])

    pair = pl.BlockSpec((s_len, 2 * LANES), lambda p, i: (0, p))
    out = pl.BlockSpec((tq, LANES), lambda p, i: (i, p))
    return _pcall(
        body, name=name, grid=(N_PAIRS, s_len // tq),
        in_specs=[pl.BlockSpec((tq, 2 * LANES), lambda p, i: (i, p)), pair, pair],
        out_specs=[out, out], out_shape=[jax.ShapeDtypeStruct((s_len, FOX_WIDTH), F32)] * 2,
        args=[qa, ka, va], sem=("parallel", "parallel"), host=host)


def _foxa_dq(qb, ka, va, doa, tq, tk, name, host=None):
    s_len = qb.shape[0]

    def body(q_ref, k_ref, v_ref, do_ref, dq_ref, rs_ref):
        i = pl.program_id(1)
        lane = lax.broadcasted_iota(jnp.int32, (1, LANES), 1)
        cols = [slice(LANES * hh, LANES * hh + LANES) for hh in range(2)]
        qh = [q_ref[:, c] for c in cols]
        doh = [do_ref[:, c] for c in cols]

        def update(j, acc, mask):
            off = pl.multiple_of(j * tk, tk)
            out = []
            for hh in range(2):
                kt = k_ref[pl.ds(off, tk), cols[hh]]
                pr = jnp.exp(lax.dot_general(qh[hh], kt, NT_DIMS, preferred_element_type=F32))
                if mask is not None:
                    pr = jnp.where(mask, pr, 0.0)
                ds = pr * lax.dot_general(doh[hh], v_ref[pl.ds(off, tk), cols[hh]], NT_DIMS, preferred_element_type=F32)
                out.append(acc[hh] + jnp.dot(ds.astype(BF16), kt, preferred_element_type=F32))
            return tuple(out)

        n_full = _shift_div(i, tq, tk)
        qi = lax.broadcasted_iota(jnp.int32, (tq, tk), 0) + i * tq
        ki = lax.broadcasted_iota(jnp.int32, (tq, tk), 1)
        acc = lax.fori_loop(0, n_full, lambda j, a: update(j, a, None), (jnp.zeros((tq, LANES), F32),) * 2)
        for jj in range(max(tq // tk, 1)):
            acc = update(n_full + jj, acc, ki + (n_full + jj) * tk <= qi)
        low = lane < FOX_HEAD_DIM
        dq_ref[...] = (jnp.where(low, acc[0], acc[1]) * Q_SCALE).astype(BF16)
        rs_ref[...] = jnp.where(low, acc[0][:, _spare_lane(0):_spare_lane(0) + 1], acc[1][:, _spare_lane(1):_spare_lane(1) + 1])

    pair = pl.BlockSpec((s_len, 2 * LANES), lambda p, i: (0, p))
    tile2 = pl.BlockSpec((tq, 2 * LANES), lambda p, i: (i, p))
    out = pl.BlockSpec((tq, LANES), lambda p, i: (i, p))
    return _pcall(
        body, name=name, grid=(N_PAIRS, s_len // tq), in_specs=[tile2, pair, pair, tile2], out_specs=[out, out],
        out_shape=[jax.ShapeDtypeStruct((s_len, FOX_WIDTH), BF16), jax.ShapeDtypeStruct((s_len, FOX_WIDTH), F32)],
        args=[qb, ka, va, doa], sem=("parallel", "parallel"), host=host)


def _foxa_dkv(qb, ka, va, doa, tq, tk, name, host=None):
    s_len = qb.shape[0]
    n_q = s_len // tq

    def body(k_ref, v_ref, q_ref, do_ref, dk_ref, dv_ref, cs_ref):
        j = pl.program_id(1)
        lane = lax.broadcasted_iota(jnp.int32, (1, LANES), 1)
        cols = [slice(LANES * hh, LANES * hh + LANES) for hh in range(2)]
        kh = [k_ref[:, c] for c in cols]
        vh = [v_ref[:, c] for c in cols]

        def update(i, acc, mask):
            off = pl.multiple_of(i * tq, tq)
            out = []
            for hh in range(2):
                qt = q_ref[pl.ds(off, tq), cols[hh]]
                dot = do_ref[pl.ds(off, tq), cols[hh]]
                pt = jnp.exp(lax.dot_general(kh[hh], qt, NT_DIMS, preferred_element_type=F32))
                if mask is not None:
                    pt = jnp.where(mask, pt, 0.0)
                dv = acc[2 * hh + 1] + jnp.dot(pt.astype(BF16), dot, preferred_element_type=F32)
                dst = pt * lax.dot_general(vh[hh], dot, NT_DIMS, preferred_element_type=F32)
                out += [acc[2 * hh] + jnp.dot(dst.astype(BF16), qt, preferred_element_type=F32), dv]
            return tuple(out)

        i0 = _shift_div(j, tk, tq)
        n_part = max(tk // tq, 1)
        ki = lax.broadcasted_iota(jnp.int32, (tk, tq), 0) + j * tk
        qi = lax.broadcasted_iota(jnp.int32, (tk, tq), 1)
        acc = (jnp.zeros((tk, LANES), F32),) * 4
        for ii in range(n_part):
            acc = update(i0 + ii, acc, ki <= qi + (i0 + ii) * tq)
        acc = lax.fori_loop(i0 + n_part, n_q, lambda i, a: update(i, a, None), acc)
        low = lane < FOX_HEAD_DIM
        dk_ref[...] = jnp.where(low, acc[0], acc[2]).astype(BF16)
        dv_ref[...] = jnp.where(low, acc[1], acc[3]).astype(BF16)
        cs_ref[...] = jnp.where(low, acc[0][:, _spare_lane(0) + 3:_spare_lane(0) + 4],
                                acc[2][:, _spare_lane(1) + 3:_spare_lane(1) + 4])

    pair = pl.BlockSpec((s_len, 2 * LANES), lambda p, j: (0, p))
    tile2 = pl.BlockSpec((tk, 2 * LANES), lambda p, j: (j, p))
    out = pl.BlockSpec((tk, LANES), lambda p, j: (j, p))
    return _pcall(
        body, name=name, grid=(N_PAIRS, s_len // tk), in_specs=[tile2, tile2, pair, pair], out_specs=[out, out, out],
        out_shape=[jax.ShapeDtypeStruct((s_len, FOX_WIDTH), BF16), jax.ShapeDtypeStruct((s_len, FOX_WIDTH), BF16),
                   jax.ShapeDtypeStruct((s_len, FOX_WIDTH), F32)],
        args=[ka, va, qb, doa], sem=("parallel", "parallel"), host=host)


def _mem_scores_t(q, kv, h):
    lo = h * MEM_HEAD_DIM
    st = lax.dot_general(kv[:, lo:lo + MEM_HEAD_DIM], q[:, lo:lo + MEM_HEAD_DIM], NT_DIMS,
                         preferred_element_type=F32) * (MEM_HEAD_DIM ** -0.5)
    e = jnp.exp(st - jnp.max(st, axis=0, keepdims=True))
    return e / jnp.sum(e, axis=0, keepdims=True)


def _memattn_fwd(q, kv, tile, name):
    s_len = q.shape[0]
    n_mem = kv.shape[0]

    def body(q_ref, kv_ref, o_ref):
        q = q_ref[...]
        kv = kv_ref[...]
        for h in range(MEM_HEADS):
            lo = h * MEM_HEAD_DIM
            pt = _mem_scores_t(q, kv, h).astype(BF16)
            vh = kv[:, MEM_INNER + lo:MEM_INNER + lo + MEM_HEAD_DIM]
            o_ref[:, lo:lo + MEM_HEAD_DIM] = lax.dot_general(pt, vh, TN_DIMS, preferred_element_type=F32).astype(BF16)

    return pl.pallas_call(
        body, name=name, grid=(s_len // tile,),
        in_specs=[pl.BlockSpec((tile, MEM_INNER), lambda i: (i, 0)), _full((n_mem, 2 * MEM_INNER))],
        out_specs=pl.BlockSpec((tile, MEM_INNER), lambda i: (i, 0)),
        out_shape=jax.ShapeDtypeStruct((s_len, MEM_INNER), BF16),
        compiler_params=_params("parallel"),
    )(q, kv)


def _memattn_bwd(q, kv, do, tile, name):
    s_len = q.shape[0]
    n_mem = kv.shape[0]
    scale = MEM_HEAD_DIM ** -0.5

    def body(q_ref, kv_ref, do_ref, dq_ref, dkv_ref):
        @pl.when(pl.program_id(0) == 0)
        def _():
            dkv_ref[...] = jnp.zeros_like(dkv_ref)

        q = q_ref[...]
        kv = kv_ref[...]
        do = do_ref[...]
        for h in range(MEM_HEADS):
            lo = h * MEM_HEAD_DIM
            qh = q[:, lo:lo + MEM_HEAD_DIM]
            kh = kv[:, lo:lo + MEM_HEAD_DIM]
            vh = kv[:, MEM_INNER + lo:MEM_INNER + lo + MEM_HEAD_DIM]
            doh = do[:, lo:lo + MEM_HEAD_DIM]
            pt = _mem_scores_t(q, kv, h)
            dkv_ref[:, MEM_INNER + lo:MEM_INNER + lo + MEM_HEAD_DIM] += jnp.dot(
                pt.astype(BF16), doh, preferred_element_type=F32)
            dpt = lax.dot_general(vh, doh, NT_DIMS, preferred_element_type=F32)
            dst = (pt * (dpt - jnp.sum(pt * dpt, axis=0, keepdims=True)) * scale).astype(BF16)
            dkv_ref[:, lo:lo + MEM_HEAD_DIM] += jnp.dot(dst, qh, preferred_element_type=F32)
            dq_ref[:, lo:lo + MEM_HEAD_DIM] = lax.dot_general(dst, kh, TN_DIMS, preferred_element_type=F32).astype(BF16)

    return pl.pallas_call(
        body, name=name, grid=(s_len // tile,),
        in_specs=[pl.BlockSpec((tile, MEM_INNER), lambda i: (i, 0)), _full((n_mem, 2 * MEM_INNER)),
                  pl.BlockSpec((tile, MEM_INNER), lambda i: (i, 0))],
        out_specs=[pl.BlockSpec((tile, MEM_INNER), lambda i: (i, 0)), _full((n_mem, 2 * MEM_INNER))],
        out_shape=[jax.ShapeDtypeStruct((s_len, MEM_INNER), BF16), jax.ShapeDtypeStruct((n_mem, 2 * MEM_INNER), F32)],
        compiler_params=_params("arbitrary"),
    )(q, kv, do)


def _loss_head(y, target, tile, name):
    s_len, d = y.shape

    def body(y_ref, t_ref, dy_ref, l_ref):
        @pl.when(pl.program_id(0) == 0)
        def _():
            l_ref[...] = jnp.zeros_like(l_ref)

        err = y_ref[...] - t_ref[...]
        dy_ref[...] = err * (1.0 / d)
        l_ref[...] += jnp.sum(err * err, axis=0, keepdims=True) * (0.5 / d)

    row = lambda i: (i, 0)
    return pl.pallas_call(
        body, name=name, grid=(s_len // tile,),
        in_specs=[pl.BlockSpec((tile, d), row), pl.BlockSpec((tile, d), row)],
        out_specs=[pl.BlockSpec((tile, d), row), _full((1, d))],
        out_shape=[jax.ShapeDtypeStruct((s_len, d), F32), jax.ShapeDtypeStruct((1, d), F32)],
        compiler_params=_params("arbitrary"),
    )(y, target)


def _attn_tile(s_len):
    return min(256, s_len // 2)


def _rows4(a8, ta):
    return a8.reshape(FOX_HEADS, a8.shape[1] // ta, 1, ta)


def _head_rows(x, ta):
    return _rows4(x[:, ::FOX_HEAD_DIM].T, ta)


REST = ("w_out", "w_mq", "w_mk", "w_mv", "w_mo", "w_up", "w_down")
SHARD_AXIS = {"w_in": 1, "w_out": 0, "w_mq": 0, "w_mk": 0, "w_mv": 0, "w_mo": 1, "w_up": 1, "w_down": 0}


def _full_from_shards(sh, axis):
    n, r, c = sh.shape
    if axis == 0:
        return sh.reshape(n * r, c)
    return sh.transpose(1, 0, 2).reshape(r, n * c)


def _rest_weights(lands):
    w = {n: _full_from_shards(sh, SHARD_AXIS[n]) for n, sh in zip(REST, lands)}
    w["w_mkv"] = jnp.concatenate([w.pop("w_mk"), w.pop("w_mv")], axis=1)
    return w


def _w_in_cat(land):
    return jnp.pad(_full_from_shards(land, 1), ((0, 0), (0, IN_CAT - IN_COLS)))


def _layer_fwd(x0, mem, w, l, rest_src=None, next_src=None):
    s_len = x0.shape[0]
    tile = min(512, s_len)
    tile_ff = min(256, s_len)
    ta = _attn_tile(s_len)
    ident = lambda z: z
    sv = {"x0": x0}

    h1, ag, qkv, fl = _rms_matmul(
        x0, w["norm_mix_pre"], w["w_in_cat"],
        [(0, 2 * CONV_CH, [(F32, ident)]), (2 * CONV_CH, IN_MAIN, [(BF16, ident)]), (IN_MAIN, IN_CAT, [(F32, ident)])],
        tile, f"mix_in_{l}")
    cum, cum8 = _cumsum_fwd(fl, w["b_forget"], tile, f"cumsum_fwd_{l}")
    u3 = _conv_fwd(ag, w["conv_w"], w["conv_b"], w["conv_ln_g"], w["conv_ln_b"], tile, f"conv_fwd_{l}")
    qa, ka, va = _foxa_prep(qkv, cum, tile, f"fox_prep_{l}")
    if rest_src is None:
        o, lse = _foxa_fwd(qa, ka, va, ta, 2 * ta, f"fox_fwd_{l}")
    else:
        (o, lse), rest_land = _foxa_fwd(qa, ka, va, ta, 2 * ta, f"fox_fwd_{l}", host=(rest_src, [True] * len(rest_src)))
        w = {**w, **_rest_weights(rest_land)}
    cat = jnp.concatenate([u3, o.astype(BF16)], axis=1)
    y1, x1 = _matmul_resnorm(cat, w["w_out"], x0, w["norm_mix_post"], tile, f"mix_out_{l}")
    sv.update(h1=h1, ag=ag, fl=fl, qa=qa, ka=ka, va=va, o=o, lse=lse, cat=cat, y1=y1, x1=x1)

    h2, qm = _rms_matmul(x1, w["norm_mem_pre"], w["w_mq"], [(0, MEM_INNER, [(BF16, ident)])], tile, f"mem_q_{l}")
    mem_n, kv = _rms_matmul(mem, w["norm_memkv"], w["w_mkv"], [(0, 2 * MEM_INNER, [(BF16, ident)])],
                            mem.shape[0], f"mem_kv_{l}")
    om = _memattn_fwd(qm, kv, tile, f"mem_attn_fwd_{l}")
    y2, x2 = _matmul_resnorm(om, w["w_mo"], x1, w["norm_mem_post"], tile, f"mem_out_{l}")
    sv.update(h2=h2, qm=qm, mem_n=mem_n, kv=kv, om=om, y2=y2, x2=x2)

    relu2 = lambda z: jnp.square(jnp.maximum(z, 0.0))
    up = _rms_matmul(x2, w["norm_mlp_pre"], w["w_up"], [(0, D_FF, [(BF16, ident), (BF16, relu2)])], tile_ff,
                     f"mlp_up_{l}", host=None if next_src is None else (next_src, [True] * len(next_src)))
    (h3, pre, hid), next_land = (up, None) if next_src is None else up
    y3, x3 = _matmul_resnorm(hid, w["w_down"], x2, w["norm_mlp_post"], tile_ff, f"mlp_down_{l}")
    sv.update(h3=h3, pre=pre, hid=hid, y3=y3)
    return x3, sv, w, next_land


def _layer_bwd(dx3, mem, w, sv, l, scatter_rest=False, dkv_src=None):
    s_len = dx3.shape[0]
    tile = min(512, s_len)
    tile_ff = min(256, s_len)
    ta = _attn_tile(s_len)
    tk = min(512, s_len)
    n_mem = mem.shape[0]
    g = {}

    dy3, dpre, g["norm_mlp_post"] = _resnorm_bwd_mm(dx3, sv["y3"], w["norm_mlp_post"], w["w_down"], tile_ff,
                                                    f"mlp_down_bwd_{l}", BF16, pre=sv["pre"])
    g["w_down"] = _matmul_tn_shards(sv["hid"], dy3, 0, tk, f"dw_down_{l}")
    dx2, g["norm_mlp_pre"] = _mm_prenorm_bwd(dpre, w["w_up"], sv["x2"], w["norm_mlp_pre"], dx3, tile_ff,
                                             f"mlp_up_bwd_{l}")
    g["w_up"] = _matmul_tn_shards(sv["h3"], dpre, 1, tk, f"dw_up_{l}")

    dy2, dom, g["norm_mem_post"] = _resnorm_bwd_mm(dx2, sv["y2"], w["norm_mem_post"], w["w_mo"], tile,
                                                   f"mem_out_bwd_{l}", BF16)
    g["w_mo"] = _matmul_tn_shards(sv["om"], dy2, 1, tk, f"dw_mo_{l}")
    dqm, dkv = _memattn_bwd(sv["qm"], sv["kv"], dom, tile, f"mem_attn_bwd_{l}")
    dkvb = dkv.astype(BF16)
    g["w_mq"] = _matmul_tn_shards(sv["h2"], dqm, 0, tk, f"dw_mq_{l}")
    dx1, g["norm_mem_pre"] = _mm_prenorm_bwd(dqm, w["w_mq"], sv["x1"], w["norm_mem_pre"], dx2, tile, f"mem_q_bwd_{l}")
    _, g["norm_memkv"] = _mm_prenorm_bwd(dkvb, w["w_mkv"], mem, w["norm_memkv"], None, n_mem, f"mem_kv_bwd_{l}")
    g["w_mk"] = _matmul_tn_shards(sv["mem_n"], dkvb[:, :MEM_INNER], 0, n_mem, f"dw_mk_{l}")
    g["w_mv"] = _matmul_tn_shards(sv["mem_n"], dkvb[:, MEM_INNER:], 0, n_mem, f"dw_mv_{l}")

    dy1, dcat, g["norm_mix_post"] = _resnorm_bwd_mm(dx1, sv["y1"], w["norm_mix_post"], w["w_out"], tile,
                                                    f"mix_out_bwd_{l}", F32)
    g["w_out"] = _matmul_tn_shards(sv["cat"], dy1, 0, tk, f"dw_out_{l}")
    qb, doa = _foxa_prep_bwd(sv["qa"], sv["lse"], sv["o"], dcat, tile, f"fox_prep_bwd_{l}")
    rest_land = None
    if scatter_rest:
        rest = [g[n] for n in REST]
        (dq, rs), rest_land = _foxa_dq(qb, sv["ka"], sv["va"], doa, ta, 2 * ta, f"fox_dq_{l}",
                                       host=(rest, [False] * len(rest)))
    else:
        dq, rs = _foxa_dq(qb, sv["ka"], sv["va"], doa, ta, 2 * ta, f"fox_dq_{l}")
    dkv_res = _foxa_dkv(qb, sv["ka"], sv["va"], doa, 2 * ta, ta, f"fox_dkv_{l}",
                        host=None if dkv_src is None else (dkv_src, [False] * len(dkv_src)))
    (dk, dv, cs), dkv_land = (dkv_res, None) if dkv_src is None else dkv_res
    dcum = jnp.pad((rs - cs)[:, ::FOX_HEAD_DIM], ((0, 0), (0, LANES - FOX_HEADS)))
    dfl, db = _cumsum_bwd(dcum, sv["fl"], w["b_forget"], tile, f"cumsum_bwd_{l}")
    g["b_forget"] = db[:, :FOX_HEADS]
    dag, dconv_w, dconv_v = _conv_bwd(sv["ag"], dcat, w["conv_w"], w["conv_b"], w["conv_ln_g"], w["conv_ln_b"], tile,
                                      f"conv_bwd_{l}")
    g["conv_w"] = dconv_w[:CONV_WIDTH]
    g["conv_b"], g["conv_ln_g"], g["conv_ln_b"] = dconv_v[0:1], dconv_v[1:2], dconv_v[2:3]
    dz = jnp.concatenate([dag, dq, dk, dv, dfl.astype(BF16)], axis=1)
    dx0, g["norm_mix_pre"] = _mm_prenorm_bwd(dz, w["w_in_cat"], sv["x0"], w["norm_mix_pre"], dx1, tile, f"mix_in_bwd_{l}")
    dw_in = _matmul_tn(sv["h1"], dz, 256, tk, f"dw_in_{l}")[:, :IN_COLS]
    g["w_in"] = dw_in.reshape(D_MODEL, N_DEV, IN_COLS // N_DEV).transpose(1, 0, 2).astype(BF16)
    return dx0, g, rest_land, dkv_land


def _sum_blocks(a, name):
    n, rows, cols = a.shape

    def body(a_ref, o_ref):
        acc = a_ref[0]
        for j in range(1, n):
            acc = acc + a_ref[j]
        o_ref[...] = acc

    return pl.pallas_call(
        body, name=name, in_specs=[_full((n, rows, cols))], out_specs=_full((rows, cols)),
        out_shape=jax.ShapeDtypeStruct((rows, cols), F32), grid=(1,),
    )(a)


def _adamw(gparts, w, m, v, tile, name):
    n, rows, cols = gparts.shape
    c1 = 1.0 - ADAM_B1
    c2 = 1.0 - ADAM_B2
    bc1 = 1.0 - ADAM_B1 ** ADAM_STEP
    bc2 = 1.0 - ADAM_B2 ** ADAM_STEP

    def body(gp_ref, w_ref, m_ref, v_ref, g_ref, d_ref, mo_ref, vo_ref):
        g = gp_ref[0].astype(F32)
        for j in range(1, n):
            g = g + gp_ref[j].astype(F32)
        g_ref[...] = g
        m_new = ADAM_B1 * m_ref[...] + c1 * g
        v_new = ADAM_B2 * v_ref[...] + c2 * (g * g)
        mo_ref[...] = m_new
        vo_ref[...] = v_new
        d_ref[...] = -ADAM_LR * ((m_new / bc1) / (jnp.sqrt(v_new / bc2) + ADAM_EPS) + ADAM_WD * w_ref[...])

    row = lambda i: (i, 0)
    spec = pl.BlockSpec((tile, cols), row)
    return pl.pallas_call(
        body, name=name, grid=(rows // tile,),
        in_specs=[pl.BlockSpec((n, tile, cols), lambda i: (0, i, 0)), spec, spec, spec],
        out_specs=[spec] * 4, out_shape=[jax.ShapeDtypeStruct((rows, cols), F32)] * 4,
        compiler_params=_params("parallel"),
    )(gparts, w, m, v)


def _pack_rows(parts, total_rows):
    flat = [p.reshape(-1, D_MODEL) for p in parts]
    used = sum(f.shape[0] for f in flat)
    if total_rows > used:
        flat.append(jnp.zeros((total_rows - used, D_MODEL), flat[0].dtype))
    return jnp.concatenate(flat, axis=0)


def _pad_lanes(v, n=D_MODEL):
    v = v.reshape(-1, v.shape[-1])
    return jnp.pad(v, ((0, 0), (0, n - v.shape[-1])))


def kernel(x, mem, norm_mix_pre, norm_mix_post, w_in, b_forget, conv_w, conv_b, conv_ln_g, conv_ln_b, w_out, norm_mem_pre, norm_mem_post, norm_memkv, w_mq, w_mk, w_mv, w_mo, norm_mlp_pre, norm_mlp_post, w_up, w_down, loss_target, m_norm_mix_pre, m_norm_mix_post, m_w_in, m_b_forget, m_conv_w, m_conv_b, m_conv_ln_g, m_conv_ln_b, m_w_out, m_norm_mem_pre, m_norm_mem_post, m_norm_memkv, m_w_mq, m_w_mk, m_w_mv, m_w_mo, m_norm_mlp_pre, m_norm_mlp_post, m_w_up, m_w_down, v_norm_mix_pre, v_norm_mix_post, v_w_in, v_b_forget, v_conv_w, v_conv_b, v_conv_ln_g, v_conv_ln_b, v_w_out, v_norm_mem_pre, v_norm_mem_post, v_norm_memkv, v_w_mq, v_w_mk, v_w_mv, v_w_mo, v_norm_mlp_pre, v_norm_mlp_post, v_w_up, v_w_down):
    p = dict(locals())
    names = ("norm_mix_pre", "norm_mix_post", "w_in", "b_forget", "conv_w", "conv_b", "conv_ln_g", "conv_ln_b", "w_out",
             "norm_mem_pre", "norm_mem_post", "norm_memkv", "w_mq", "w_mk", "w_mv", "w_mo", "norm_mlp_pre",
             "norm_mlp_post", "w_up", "w_down")
    me = 4 * lax.axis_index("x") + 2 * lax.axis_index("y") + lax.axis_index("c")
    conv_cols = conv_w.shape[2]
    s_len = x.shape[1]

    def shard(n, l):
        return p[n][l].astype(BF16)

    conv_pack = _pack_rows([jnp.pad(conv_w, ((0, 0), (0, CONV_PAD - CONV_WIDTH), (0, 0)))], 8)
    win_land, conv_land = _exchange([shard("w_in", 0), conv_pack], [True, True], "gather_first")
    conv_rows = DEPTH * CONV_PAD * conv_cols // D_MODEL
    conv_full = conv_land[:, :conv_rows].reshape(N_DEV, DEPTH, CONV_PAD, conv_cols)
    conv_full = conv_full.transpose(1, 2, 0, 3).reshape(DEPTH, CONV_PAD, N_DEV * conv_cols)

    def first_weights(l, land):
        w = {"w_in_cat": _w_in_cat(land), "conv_w": conv_full[l]}
        for n in VEC:
            w[n] = p[n][l][None, :]
        w["b_forget"] = _pad_lanes(w["b_forget"], LANES)
        return w

    h, sv0, w0, win1_land = _layer_fwd(x[0], mem[0], first_weights(0, win_land), 0,
                                       rest_src=[shard(n, 0) for n in REST], next_src=[shard("w_in", 1)])
    h, sv1, w1, _ = _layer_fwd(h, mem[0], first_weights(1, win1_land[0]), 1, rest_src=[shard(n, 1) for n in REST])
    dh, loss_row = _loss_head(h, loss_target[0], min(512, s_len), "loss_head")
    dh, g1, rest_g1, _ = _layer_bwd(dh, mem[0], w1, sv1, 1, scatter_rest=True)
    grad_x, g0, rest_g0, win1_g = _layer_bwd(dh, mem[0], w0, sv0, 0, scatter_rest=True, dkv_src=[g1["w_in"]])
    grads = [g0, g1]

    small = [loss_row] + [_pad_lanes(grads[l][n]) for l in range(DEPTH) for n in VEC]
    small += [_pad_lanes(grads[l]["conv_w"].reshape(1, -1), 16 * D_MODEL).reshape(16, D_MODEL) for l in range(DEPTH)]
    n_small = 1 + DEPTH * len(VEC) + 16 * DEPTH
    win0_g, small_land = _exchange([g0["w_in"], _pack_rows(small, -(-n_small // 8) * 8)], [False, True], "scatter_last")
    total = _sum_blocks(small_land, "sum_small")
    loss = jnp.sum(total[0])
    gv = {}
    row = 1
    for l in range(DEPTH):
        for n in VEC:
            gv[(n, l)] = total[row, :p[n].shape[1]]
            row += 1
    gconv = []
    for l in range(DEPTH):
        full = total[row:row + 16].reshape(-1)[:CONV_WIDTH * CONV_CH].reshape(CONV_WIDTH, CONV_CH)
        gconv.append(lax.dynamic_slice_in_dim(full, me * conv_cols, conv_cols, axis=1))
        row += 16

    def small_pack(vec, conv):
        rows = [_pad_lanes(vec(n, l)[None, :]) for l in range(DEPTH) for n in VEC]
        rows += [_pad_lanes(conv(l).reshape(1, -1), 2 * D_MODEL).reshape(2, D_MODEL) for l in range(DEPTH)]
        return _pack_rows(rows, SMALL_ROWS)

    def from_inputs(prefix):
        return small_pack(lambda n, l: p[prefix + n][l], lambda l: p[prefix + "conv_w"][l])

    g_small = small_pack(lambda n, l: gv[(n, l)], lambda l: gconv[l])
    small_out = _adamw(g_small[None], from_inputs(""), from_inputs("m_"), from_inputs("v_"), SMALL_ROWS, "adamw_small")

    def unpack_small(buf):
        out = {}
        row = 0
        for l in range(DEPTH):
            for n in VEC:
                out.setdefault(n, []).append(buf[row, :p[n].shape[1]])
                row += 1
        for l in range(DEPTH):
            out.setdefault("conv_w", []).append(buf[row:row + 2].reshape(-1)[:CONV_WIDTH * conv_cols]
                                                .reshape(CONV_WIDTH, conv_cols))
            row += 2
        return {n: jnp.stack(v) for n, v in out.items()}

    landed = {("w_in", 0): win0_g, ("w_in", 1): win1_g[0]}
    for i, n in enumerate(REST):
        landed[(n, 0)], landed[(n, 1)] = rest_g0[i], rest_g1[i]
    big_out = {}
    for n in BIG:
        per_layer = [_adamw(landed[(n, l)], p[n][l], p["m_" + n][l], p["v_" + n][l], ADAMW_TILE[n], f"adamw_{n}_{l}")
                     for l in range(DEPTH)]
        big_out[n] = [jnp.stack([per_layer[l][k] for l in range(DEPTH)]) for k in range(4)]

    result = [loss, grad_x[None]]
    for k in range(4):
        smalls = unpack_small(small_out[k])
        result += [big_out[n][k] if n in big_out else smalls[n] for n in names]
    return tuple(result)
```

```python
import functools

import jax
import jax.numpy as jnp
from jax import lax
from jax.experimental import pallas as pl
from jax.experimental.pallas import tpu as pltpu

F32 = jnp.float32
BF16 = jnp.bfloat16

N_DEV = 8
DEPTH = 2
D_MODEL = 1024
CONV_CH = 512
CONV_WIDTH = 31
CONV_PAD = 32
FOX_HEADS = 8
FOX_HEAD_DIM = 64
FOX_WIDTH = 512
N_PAIRS = 4
MEM_HEADS = 4
MEM_HEAD_DIM = 128
MEM_INNER = 512
D_FF = 4096
IN_MAIN = 2560
IN_COLS = 2568
IN_CAT = IN_MAIN + 128
LANES = 128
EPS = 1e-6
NEG_INF = -1e30

ADAM_LR = 0.001
ADAM_B1 = 0.9
ADAM_B2 = 0.999
ADAM_EPS = 1e-08
ADAM_WD = 0.01
ADAM_STEP = 10

NT_DIMS = (((1,), (1,)), ((), ()))
TN_DIMS = (((0,), (0,)), ((), ()))

BIG = ("w_in", "w_out", "w_mq", "w_mk", "w_mv", "w_mo", "w_up", "w_down")
ADAMW_TILE = {"w_in": 256, "w_out": 128, "w_mq": 128, "w_mk": 128, "w_mv": 128, "w_mo": 512, "w_up": 256, "w_down": 128}

VEC_1024 = ("norm_mix_pre", "norm_mix_post", "norm_mem_pre", "norm_mem_post", "norm_memkv", "norm_mlp_pre", "norm_mlp_post")
VEC_512 = ("conv_b", "conv_ln_g", "conv_ln_b")
VEC = VEC_1024 + VEC_512 + ("b_forget",)
SMALL_ROWS = 32


def _sigmoid(x):
    return 1.0 / (1.0 + jnp.exp(-x))


def _rms(x, g):
    r = lax.rsqrt(jnp.mean(x * x, axis=-1, keepdims=True) + EPS)
    return x * r * g


def _rms_bwd(x, g, dh):
    r = lax.rsqrt(jnp.mean(x * x, axis=-1, keepdims=True) + EPS)
    gh = dh * g
    c = jnp.mean(gh * x, axis=-1, keepdims=True)
    dx = r * gh - x * (r * r * r * c)
    dg = jnp.sum(dh * (x * r), axis=0, keepdims=True)
    return dx, dg


def _full(shape):
    nd = len(shape)
    return pl.BlockSpec(shape, lambda *_: (0,) * nd)


def _params(*sem):
    return pltpu.CompilerParams(dimension_semantics=sem)


def _exchange_copies(src_refs, out_refs, same, send_sems, recv_sems, local_sems, with_recvs):
    x, y, c = lax.axis_index("x"), lax.axis_index("y"), lax.axis_index("c")
    me = 4 * x + 2 * y + c
    local, sends, recvs = [], [], []
    for a, (s_ref, o_ref) in enumerate(zip(src_refs, out_refs)):
        def mine(idx, s_ref=s_ref, whole=same[a]):
            return s_ref if whole else s_ref.at[idx]

        local.append(pltpu.make_async_copy(mine(me), o_ref.at[me], local_sems.at[a]))
        for k in range(1, N_DEV):
            px = 1 - x if k & 4 else x
            py = 1 - y if k & 2 else y
            pc = 1 - c if k & 1 else c
            peer = 4 * px + 2 * py + pc
            sem = a * (N_DEV - 1) + k - 1
            common = dict(send_sem=send_sems.at[sem], recv_sem=recv_sems.at[sem], device_id=(px, py, pc),
                          device_id_type=pl.DeviceIdType.MESH)
            sends.append(pltpu.make_async_remote_copy(src_ref=mine(peer), dst_ref=o_ref.at[me], **common))
            if with_recvs:
                recvs.append(pltpu.make_async_remote_copy(src_ref=mine(peer), dst_ref=o_ref.at[peer], **common))
    return local, sends, recvs


def _pcall(body, *, name, grid, in_specs, out_specs, out_shape, args, scratch_shapes=(), sem=(), host=None):
    if host is None:
        return pl.pallas_call(body, name=name, grid=grid, in_specs=in_specs, out_specs=out_specs, out_shape=out_shape,
                              scratch_shapes=list(scratch_shapes), compiler_params=_params(*sem))(*args)
    srcs, same = host
    n_in, n_out, n_scr, n_h = len(in_specs), len(out_specs), len(scratch_shapes), len(srcs)
    hbm = pl.BlockSpec(memory_space=pltpu.HBM)
    lands = [jax.ShapeDtypeStruct((N_DEV,) + (s.shape if whole else s.shape[1:]), s.dtype) for s, whole in zip(srcs, same)]

    def wrapped(*refs):
        ins, src_refs = refs[:n_in], refs[n_in:n_in + n_h]
        outs = refs[n_in + n_h:n_in + n_h + n_out]
        land_refs = refs[n_in + n_h + n_out:n_in + 2 * n_h + n_out]
        scr = refs[n_in + 2 * n_h + n_out:n_in + 2 * n_h + n_out + n_scr]
        sems = refs[n_in + 2 * n_h + n_out + n_scr:]
        first = functools.reduce(jnp.logical_and, [pl.program_id(d) == 0 for d in range(len(grid))])
        last = functools.reduce(jnp.logical_and, [pl.program_id(d) == grid[d] - 1 for d in range(len(grid))])

        @pl.when(first)
        def _():
            local, sends, _ = _exchange_copies(src_refs, land_refs, same, *sems, False)
            for cp in local + sends:
                cp.start()

        body(*ins, *outs, *scr)

        @pl.when(last)
        def _():
            local, sends, recvs = _exchange_copies(src_refs, land_refs, same, *sems, True)
            for cp in recvs:
                cp.wait_recv()
            for cp in sends:
                cp.wait_send()
            for cp in local:
                cp.wait()

    n_sem = n_h * (N_DEV - 1)
    res = pl.pallas_call(
        wrapped, name=name, grid=grid, in_specs=list(in_specs) + [hbm] * n_h, out_specs=list(out_specs) + [hbm] * n_h,
        out_shape=list(out_shape) + lands,
        scratch_shapes=list(scratch_shapes) + [pltpu.SemaphoreType.DMA((n_sem,)), pltpu.SemaphoreType.DMA((n_sem,)),
                                               pltpu.SemaphoreType.DMA((n_h,))],
        compiler_params=_params(*(("arbitrary",) * len(grid))),
    )(*args, *srcs)
    return list(res[:n_out]), list(res[n_out:])


def _exchange(srcs, same, name):
    def body():
        pass

    return _pcall(body, name=name, grid=(1,), in_specs=[], out_specs=[], out_shape=[], args=[], host=(srcs, same))[1]


def _rms_matmul(x, g, w, segs, tile, name, host=None):
    s_len, d = x.shape
    n = w.shape[1]
    chunk = 512

    def body(x_ref, g_ref, w_ref, h_ref, *outs):
        h = _rms(x_ref[...], g_ref[...]).astype(BF16)
        h_ref[...] = h
        oi = 0
        for c0, c1, fns in segs:
            for a in range(c0, c1, chunk):
                b = min(a + chunk, c1)
                z = jnp.dot(h, w_ref[:, a:b], preferred_element_type=F32)
                for k, (dt, fn) in enumerate(fns):
                    outs[oi + k][:, a - c0:b - c0] = fn(z).astype(dt)
            oi += len(fns)

    out_shape = [jax.ShapeDtypeStruct((s_len, d), BF16)]
    out_specs = [pl.BlockSpec((tile, d), lambda i: (i, 0))]
    for c0, c1, fns in segs:
        for dt, _ in fns:
            out_shape.append(jax.ShapeDtypeStruct((s_len, c1 - c0), dt))
            out_specs.append(pl.BlockSpec((tile, c1 - c0), lambda i: (i, 0)))
    return _pcall(
        body, name=name, grid=(s_len // tile,),
        in_specs=[pl.BlockSpec((tile, d), lambda i: (i, 0)), _full((1, d)), _full((d, n))],
        out_specs=out_specs, out_shape=out_shape, args=[x, g, w], sem=("parallel",), host=host)


def _matmul_resnorm(a, w, x, g, tile, name):
    s_len, k = a.shape
    d = w.shape[1]

    def body(a_ref, w_ref, x_ref, g_ref, y_ref, xo_ref):
        y = jnp.dot(a_ref[...], w_ref[...], preferred_element_type=F32)
        y_ref[...] = y
        xo_ref[...] = x_ref[...] + _rms(y, g_ref[...])

    row = lambda i: (i, 0)
    return pl.pallas_call(
        body, name=name, grid=(s_len // tile,),
        in_specs=[pl.BlockSpec((tile, k), row), _full((k, d)), pl.BlockSpec((tile, d), row), _full((1, d))],
        out_specs=[pl.BlockSpec((tile, d), row), pl.BlockSpec((tile, d), row)],
        out_shape=[jax.ShapeDtypeStruct((s_len, d), F32), jax.ShapeDtypeStruct((s_len, d), F32)],
        compiler_params=_params("parallel"),
    )(a, w, x, g)


def _resnorm_bwd_mm(dx, y, g, w, tile, name, out_dtype, pre=None):
    s_len, d = dx.shape
    k = w.shape[0]
    chunk = 512

    def body(*refs):
        if pre is None:
            dx_ref, y_ref, g_ref, w_ref, dy_ref, da_ref, dg_ref = refs
        else:
            dx_ref, y_ref, g_ref, w_ref, pre_ref, dy_ref, da_ref, dg_ref = refs
        dy, dg = _rms_bwd(y_ref[...], g_ref[...], dx_ref[...])
        dyb = dy.astype(BF16)
        dy_ref[...] = dyb

        @pl.when(pl.program_id(0) == 0)
        def _():
            dg_ref[...] = jnp.zeros_like(dg_ref)

        dg_ref[...] += dg
        for a in range(0, k, chunk):
            b = min(a + chunk, k)
            da = lax.dot_general(dyb, w_ref[a:b, :], NT_DIMS, preferred_element_type=F32)
            if pre is not None:
                da = da * (2.0 * jnp.maximum(pre_ref[:, a:b].astype(F32), 0.0))
            da_ref[:, a:b] = da.astype(out_dtype)

    row = lambda i: (i, 0)
    in_specs = [pl.BlockSpec((tile, d), row), pl.BlockSpec((tile, d), row), _full((1, d)), _full((k, d))]
    args = [dx, y, g, w]
    if pre is not None:
        in_specs.append(pl.BlockSpec((tile, k), row))
        args.append(pre)
    return pl.pallas_call(
        body, name=name, grid=(s_len // tile,), in_specs=in_specs,
        out_specs=[pl.BlockSpec((tile, d), row), pl.BlockSpec((tile, k), row), _full((1, d))],
        out_shape=[jax.ShapeDtypeStruct((s_len, d), BF16), jax.ShapeDtypeStruct((s_len, k), out_dtype),
                   jax.ShapeDtypeStruct((1, d), F32)],
        compiler_params=_params("arbitrary"),
    )(*args)


def _mm_prenorm_bwd(dz, w, x, g, dres, tile, name, host=None):
    s_len, n = dz.shape
    d = w.shape[0]

    def body(*refs):
        if dres is None:
            dz_ref, w_ref, x_ref, g_ref, dx_ref, dg_ref = refs
        else:
            dz_ref, w_ref, x_ref, g_ref, dres_ref, dx_ref, dg_ref = refs
        dh = lax.dot_general(dz_ref[...], w_ref[...], NT_DIMS, preferred_element_type=F32)
        dx, dg = _rms_bwd(x_ref[...], g_ref[...], dh)
        if dres is not None:
            dx = dx + dres_ref[...]
        dx_ref[...] = dx

        @pl.when(pl.program_id(0) == 0)
        def _():
            dg_ref[...] = jnp.zeros_like(dg_ref)

        dg_ref[...] += dg

    row = lambda i: (i, 0)
    in_specs = [pl.BlockSpec((tile, n), row), _full((d, n)), pl.BlockSpec((tile, d), row), _full((1, d))]
    args = [dz, w, x, g]
    if dres is not None:
        in_specs.append(pl.BlockSpec((tile, d), row))
        args.append(dres)
    return _pcall(
        body, name=name, grid=(s_len // tile,), in_specs=in_specs,
        out_specs=[pl.BlockSpec((tile, d), row), _full((1, d))],
        out_shape=[jax.ShapeDtypeStruct((s_len, d), F32), jax.ShapeDtypeStruct((1, d), F32)],
        args=args, sem=("arbitrary",), host=host)


def _matmul_tn(a, b, tm, tk, name):
    s_len, m = a.shape
    n = b.shape[1]

    def body(a_ref, b_ref, o_ref):
        @pl.when(pl.program_id(1) == 0)
        def _():
            o_ref[...] = jnp.zeros_like(o_ref)

        o_ref[...] += lax.dot_general(a_ref[...], b_ref[...], TN_DIMS, preferred_element_type=F32)

    return pl.pallas_call(
        body, name=name, grid=(m // tm, s_len // tk),
        in_specs=[pl.BlockSpec((tk, tm), lambda i, k: (k, i)), pl.BlockSpec((tk, n), lambda i, k: (k, 0))],
        out_specs=pl.BlockSpec((tm, n), lambda i, k: (i, 0)),
        out_shape=jax.ShapeDtypeStruct((m, n), F32),
        compiler_params=_params("parallel", "arbitrary"),
    )(a, b)


def _matmul_tn_shards(a, b, axis, tk, name):
    s_len, m = a.shape
    n = b.shape[1]
    r, c = (m // N_DEV, n) if axis == 0 else (m, n // N_DEV)
    n_k = s_len // tk
    tm = max(r, min(m, (1 << 20) // n)) if axis == 0 else min(m, (1 << 20) // n)

    def body(a_ref, b_ref, o_ref, acc):
        k = pl.program_id(1)

        @pl.when(k == 0)
        def _():
            acc[...] = jnp.zeros_like(acc)

        acc[...] += lax.dot_general(a_ref[...], b_ref[...], TN_DIMS, preferred_element_type=F32)

        @pl.when(k == n_k - 1)
        def _():
            if axis == 0:
                o_ref[...] = acc[...].reshape(tm // r, r, c).astype(BF16)
            else:
                for j in range(N_DEV):
                    o_ref[j] = acc[:, j * c:(j + 1) * c].astype(BF16)

    if axis == 0:
        out_spec = pl.BlockSpec((tm // r, r, c), lambda i, k: (i, 0, 0))
    else:
        out_spec = pl.BlockSpec((N_DEV, tm, c), lambda i, k: (0, i, 0))
    return pl.pallas_call(
        body, name=name, grid=(m // tm, n_k),
        in_specs=[pl.BlockSpec((tk, tm), lambda i, k: (k, i)), pl.BlockSpec((tk, n), lambda i, k: (k, 0))],
        out_specs=out_spec, out_shape=jax.ShapeDtypeStruct((N_DEV, r, c), BF16),
        scratch_shapes=[pltpu.VMEM((tm, n), F32)],
        compiler_params=_params("parallel", "arbitrary"),
    )(a, b)


def _cumsum_fwd(fl, b, tile, name):
    s_len = fl.shape[0]

    def body(fl_ref, b_ref, cum_ref, carry):
        @pl.when(pl.program_id(0) == 0)
        def _():
            carry[...] = jnp.zeros_like(carry)

        xx = fl_ref[...] + b_ref[...]
        lf = jnp.minimum(xx, 0.0) - jnp.log1p(jnp.exp(-jnp.abs(xx)))
        r = lax.broadcasted_iota(jnp.int32, (tile, tile), 0)
        c = lax.broadcasted_iota(jnp.int32, (tile, tile), 1)
        tri = (c <= r).astype(F32)
        cs = jnp.dot(tri, lf, precision=lax.Precision.HIGHEST, preferred_element_type=F32) + carry[...]
        cum_ref[...] = cs
        carry[...] = cs[tile - 1:tile, :]

    return pl.pallas_call(
        body, name=name, grid=(s_len // tile,),
        in_specs=[pl.BlockSpec((tile, LANES), lambda i: (i, 0)), _full((1, LANES))],
        out_specs=pl.BlockSpec((tile, LANES), lambda i: (i, 0)),
        out_shape=jax.ShapeDtypeStruct((s_len, LANES), F32),
        scratch_shapes=[pltpu.VMEM((1, LANES), F32)],
        compiler_params=_params("arbitrary"),
    )(fl, b)


def _cumsum_bwd(rs, cs, fl, b, tile, name):
    s_len = fl.shape[0]
    n_t = s_len // tile

    def body(rs_ref, cs_ref, fl_ref, b_ref, dfl_ref, db_ref, carry):
        @pl.when(pl.program_id(0) == 0)
        def _():
            carry[...] = jnp.zeros_like(carry)
            db_ref[...] = jnp.zeros_like(db_ref)

        r = lax.broadcasted_iota(jnp.int32, (tile, tile), 0)
        c = lax.broadcasted_iota(jnp.int32, (tile, tile), 1)
        tri = (c >= r).astype(F32)
        pick = (lax.broadcasted_iota(jnp.int32, (FOX_WIDTH, LANES), 0)
                == FOX_HEAD_DIM * lax.broadcasted_iota(jnp.int32, (FOX_WIDTH, LANES), 1)).astype(F32)
        dc = jnp.dot(rs_ref[...] - cs_ref[...], pick, precision=lax.Precision.HIGHEST, preferred_element_type=F32)
        dl = jnp.dot(tri, dc, precision=lax.Precision.HIGHEST, preferred_element_type=F32) + carry[...]
        carry[...] = dl[0:1, :]
        dfl = dl * _sigmoid(-(fl_ref[...] + b_ref[...]))
        dfl_ref[...] = dfl
        db_ref[...] += jnp.sum(dfl, axis=0, keepdims=True)

    rev = lambda i: (n_t - 1 - i, 0)
    return pl.pallas_call(
        body, name=name, grid=(n_t,),
        in_specs=[pl.BlockSpec((tile, FOX_WIDTH), rev), pl.BlockSpec((tile, FOX_WIDTH), rev), pl.BlockSpec((tile, LANES), rev),
                  _full((1, LANES))],
        out_specs=[pl.BlockSpec((tile, LANES), rev), _full((1, LANES))],
        out_shape=[jax.ShapeDtypeStruct((s_len, LANES), F32), jax.ShapeDtypeStruct((1, LANES), F32)],
        scratch_shapes=[pltpu.VMEM((1, LANES), F32)],
        compiler_params=_params("arbitrary"),
    )(rs, cs, fl, b)


SUBLANES = 8
CONV_ROWS = 64


def _phase_copies(src, dst, rows):
    for p in range(SUBLANES):
        dst[p] = src[pl.ds(p, rows), :]


def _phase_rows(extp_ref, off, r0, rows):
    p = off % SUBLANES
    return extp_ref[p, pl.ds(pl.multiple_of(r0 + (off - p), SUBLANES), rows), :]


def _conv_taps(w_ref, extp_ref, base, r0, rows, reverse):
    acc = None
    for k in range(CONV_WIDTH):
        off = base + ((CONV_WIDTH - 1 - k) if reverse else k)
        term = w_ref[k:k + 1, :] * _phase_rows(extp_ref, off, r0, rows)
        acc = term if acc is None else acc + term
    return acc


def _fold_rows(x):
    out = x[0:SUBLANES]
    for i in range(1, x.shape[0] // SUBLANES):
        out = out + x[i * SUBLANES:(i + 1) * SUBLANES]
    return out


def _conv_fwd(ag, w, cb, lg, lb, tile, name):
    s_len = ag.shape[0]
    c = CONV_CH
    rb = tile

    def body(ag_ref, w_ref, cb_ref, lg_ref, lb_ref, u_ref, ext, extp):
        @pl.when(pl.program_id(0) == 0)
        def _():
            ext[0:CONV_PAD, :] = jnp.zeros((CONV_PAD, c), F32)
            ext[tile + CONV_PAD:tile + CONV_PAD + SUBLANES, :] = jnp.zeros((SUBLANES, c), F32)

        ext[CONV_PAD:CONV_PAD + tile, :] = ag_ref[:, 0:c] * _sigmoid(ag_ref[:, c:2 * c])
        _phase_copies(ext, extp, tile + CONV_PAD)

        def block(b, carry):
            r0 = pl.multiple_of(b * rb, rb)
            u1 = _conv_taps(w_ref, extp, CONV_PAD - (CONV_WIDTH - 1), r0, rb, False) + cb_ref[...]
            mu = jnp.mean(u1, axis=-1, keepdims=True)
            xc = u1 - mu
            y = xc * lax.rsqrt(jnp.mean(xc * xc, axis=-1, keepdims=True) + EPS) * lg_ref[...] + lb_ref[...]
            u_ref[pl.ds(r0, rb), :] = (y * _sigmoid(y)).astype(BF16)
            return carry

        lax.fori_loop(0, tile // rb, block, 0)
        ext[0:CONV_PAD, :] = ext[tile:tile + CONV_PAD, :]

    return pl.pallas_call(
        body, name=name, grid=(s_len // tile,),
        in_specs=[pl.BlockSpec((tile, 2 * c), lambda i: (i, 0)), _full((CONV_PAD, c)), _full((1, c)), _full((1, c)),
                  _full((1, c))],
        out_specs=pl.BlockSpec((tile, c), lambda i: (i, 0)),
        out_shape=jax.ShapeDtypeStruct((s_len, c), BF16),
        scratch_shapes=[pltpu.VMEM((tile + CONV_PAD + SUBLANES, c), F32), pltpu.VMEM((SUBLANES, tile + CONV_PAD, c), F32)],
        compiler_params=_params("arbitrary"),
    )(ag, w, cb, lg, lb)


def _conv_bwd(ag, dcat, w, cb, lg, lb, tile, name):
    s_len = ag.shape[0]
    c = CONV_CH
    n_t = s_len // tile
    per = tile // CONV_PAD
    rb = min(CONV_ROWS, tile)

    def body(ag_ref, halo_ref, du_ref, w_ref, cb_ref, lg_ref, lb_ref, dag_ref, dw_ref, dv_ref, ext, ext2, extp, dwacc):
        i = pl.program_id(0)
        t = n_t - 1 - i

        @pl.when(i == 0)
        def _():
            ext2[tile:tile + CONV_PAD + SUBLANES, :] = jnp.zeros((CONV_PAD + SUBLANES, c), F32)
            ext[tile + CONV_PAD:tile + CONV_PAD + SUBLANES, :] = jnp.zeros((SUBLANES, c), F32)
            dwacc[...] = jnp.zeros_like(dwacc)
            dv_ref[...] = jnp.zeros_like(dv_ref)

        halo = halo_ref[:, 0:c] * _sigmoid(halo_ref[:, c:2 * c])
        ext[0:CONV_PAD, :] = jnp.where(t > 0, halo, 0.0)
        ext[CONV_PAD:CONV_PAD + tile, :] = ag_ref[:, 0:c] * _sigmoid(ag_ref[:, c:2 * c])
        _phase_copies(ext, extp, tile + CONV_PAD)
        base = CONV_PAD - (CONV_WIDTH - 1)

        def block1(b, carry):
            r0 = pl.multiple_of(b * rb, rb)
            u1 = _conv_taps(w_ref, extp, base, r0, rb, False) + cb_ref[...]
            mu = jnp.mean(u1, axis=-1, keepdims=True)
            xc = u1 - mu
            rs = lax.rsqrt(jnp.mean(xc * xc, axis=-1, keepdims=True) + EPS)
            xhat = xc * rs
            y = xhat * lg_ref[...] + lb_ref[...]
            sy = _sigmoid(y)
            dy = du_ref[pl.ds(r0, rb), :] * (sy * (1.0 + y * (1.0 - sy)))
            dxh = dy * lg_ref[...]
            du1 = rs * (dxh - jnp.mean(dxh, axis=-1, keepdims=True) - xhat * jnp.mean(dxh * xhat, axis=-1, keepdims=True))
            dv_ref[0:1, :] += jnp.sum(du1, axis=0, keepdims=True)
            dv_ref[1:2, :] += jnp.sum(dy * xhat, axis=0, keepdims=True)
            dv_ref[2:3, :] += jnp.sum(dy, axis=0, keepdims=True)
            for k in range(CONV_WIDTH):
                dwacc[k] += _fold_rows(du1 * _phase_rows(extp, base + k, r0, rb))
            ext2[pl.ds(r0, rb), :] = du1
            return carry

        lax.fori_loop(0, tile // rb, block1, 0)
        _phase_copies(ext2, extp, tile + CONV_PAD)

        def block2(b, carry):
            r0 = pl.multiple_of(b * rb, rb)
            du0 = _conv_taps(w_ref, extp, 0, r0, rb, True)
            a = ag_ref[pl.ds(r0, rb), 0:c]
            sg = _sigmoid(ag_ref[pl.ds(r0, rb), c:2 * c])
            dag_ref[pl.ds(r0, rb), 0:c] = (du0 * sg).astype(BF16)
            dag_ref[pl.ds(r0, rb), c:2 * c] = (du0 * a * sg * (1.0 - sg)).astype(BF16)
            return carry

        lax.fori_loop(0, tile // rb, block2, 0)
        ext2[tile:tile + CONV_PAD, :] = ext2[0:CONV_PAD, :]

        @pl.when(i == n_t - 1)
        def _():
            for k in range(CONV_WIDTH):
                dw_ref[k:k + 1, :] = jnp.sum(dwacc[k], axis=0, keepdims=True)
            dw_ref[CONV_WIDTH:CONV_PAD, :] = jnp.zeros((CONV_PAD - CONV_WIDTH, c), F32)

    rev = lambda i: (n_t - 1 - i, 0)
    return pl.pallas_call(
        body, name=name, grid=(n_t,),
        in_specs=[pl.BlockSpec((tile, 2 * c), rev),
                  pl.BlockSpec((CONV_PAD, 2 * c), lambda i: (jnp.maximum((n_t - 1 - i) * per - 1, 0), 0)),
                  pl.BlockSpec((tile, c), rev), _full((CONV_PAD, c)), _full((1, c)), _full((1, c)), _full((1, c))],
        out_specs=[pl.BlockSpec((tile, 2 * c), rev), _full((CONV_PAD, c)), _full((8, c))],
        out_shape=[jax.ShapeDtypeStruct((s_len, 2 * c), BF16), jax.ShapeDtypeStruct((CONV_PAD, c), F32),
                   jax.ShapeDtypeStruct((8, c), F32)],
        scratch_shapes=[pltpu.VMEM((tile + CONV_PAD + SUBLANES, c), F32), pltpu.VMEM((tile + CONV_PAD + SUBLANES, c), F32),
                        pltpu.VMEM((SUBLANES, tile + CONV_PAD, c), F32), pltpu.VMEM((CONV_PAD, SUBLANES, c), F32)],
        compiler_params=_params("arbitrary"),
    )(ag, ag, dcat, w, cb, lg, lb)


Q_SCALE = FOX_HEAD_DIM ** -0.5


def _head_col(x, lane, h):
    return jnp.sum(jnp.where(lane == h, x, 0.0), axis=1, keepdims=True)


def _split3(x):
    hi = x.astype(BF16).astype(F32)
    r = x - hi
    mid = r.astype(BF16).astype(F32)
    lo = (r - mid).astype(BF16).astype(F32)
    return hi, mid, lo


def _in_lanes(lane, lo, n):
    return (lane >= lo) & (lane < lo + n)


def _put3(lane, lo, parts, rest):
    return jnp.where(lane == lo, parts[0], jnp.where(lane == lo + 1, parts[1], jnp.where(lane == lo + 2, parts[2], rest)))


def _spare_lane(h):
    return FOX_HEAD_DIM if h % 2 == 0 else 0


def _shift_div(i, num, den):
    return i * (num // den) if num >= den else lax.shift_right_logical(i, (den // num).bit_length() - 1)


def _foxa_prep(qkv, cum, tile, name):
    s_len = qkv.shape[0]

    def body(q_ref, k_ref, v_ref, c_ref, qa_ref, ka_ref, va_ref):
        lane = lax.broadcasted_iota(jnp.int32, (1, LANES), 1)
        cum_t = c_ref[...]
        for h in range(FOX_HEADS):
            e = _spare_lane(h)
            head = ~_in_lanes(lane, e, FOX_HEAD_DIM)
            blk = slice(LANES * (h // 2), LANES * (h // 2) + LANES)
            out = slice(LANES * h, LANES * h + LANES)
            c3 = _split3(_head_col(cum_t, lane, h))
            ex_q = _put3(lane, e, c3, jnp.where(_in_lanes(lane, e + 3, 3), 1.0, 0.0))
            qa_ref[:, out] = jnp.where(head, q_ref[:, blk].astype(F32) * Q_SCALE, ex_q).astype(BF16)
            ones = jnp.where(_in_lanes(lane, e, 3) | _in_lanes(lane, e + 6, 3), 1.0, 0.0)
            ex_k = _put3(lane, e + 3, [-c for c in c3], ones)
            ka_ref[:, out] = jnp.where(head, k_ref[:, blk].astype(F32), ex_k).astype(BF16)
            ex_v = jnp.where(_in_lanes(lane, e, 3), 1.0, 0.0)
            va_ref[:, out] = jnp.where(head, v_ref[:, blk].astype(F32), ex_v).astype(BF16)

    col = lambda c: pl.BlockSpec((tile, FOX_WIDTH), lambda i: (i, c))
    wide = pl.BlockSpec((tile, 2 * FOX_WIDTH), lambda i: (i, 0))
    return pl.pallas_call(
        body, name=name, grid=(s_len // tile,),
        in_specs=[col(0), col(1), col(2), pl.BlockSpec((tile, LANES), lambda i: (i, 0))],
        out_specs=[wide, wide, wide], out_shape=[jax.ShapeDtypeStruct((s_len, 2 * FOX_WIDTH), BF16)] * 3,
        compiler_params=_params("parallel"),
    )(qkv, qkv, qkv, cum)


def _foxa_prep_bwd(qa, lse, o, dcat, tile, name):
    s_len = qa.shape[0]

    def body(qa_ref, lse_ref, o_ref, do_ref, qb_ref, doa_ref):
        lane = lax.broadcasted_iota(jnp.int32, (1, LANES), 1)
        for h in range(FOX_HEADS):
            e = _spare_lane(h)
            head = ~_in_lanes(lane, e, FOX_HEAD_DIM)
            blk = slice(LANES * (h // 2), LANES * (h // 2) + LANES)
            out = slice(LANES * h, LANES * h + LANES)
            do = do_ref[:, blk]
            delta = jnp.sum(jnp.where(head, do * o_ref[:, blk], 0.0), axis=1, keepdims=True)
            doa_ref[:, out] = _put3(lane, e, _split3(-delta), jnp.where(head, do, 0.0)).astype(BF16)
            at = LANES * (h // 2) + FOX_HEAD_DIM - e
            l3 = _split3(-lse_ref[:, at:at + 1])
            qb_ref[:, out] = _put3(lane, e + 6, l3, qa_ref[:, out].astype(F32)).astype(BF16)

    wide = pl.BlockSpec((tile, 2 * FOX_WIDTH), lambda i: (i, 0))
    half = pl.BlockSpec((tile, FOX_WIDTH), lambda i: (i, 0))
    return pl.pallas_call(
        body, name=name, grid=(s_len // tile,),
        in_specs=[wide, half, half, pl.BlockSpec((tile, FOX_WIDTH), lambda i: (i, 1))],
        out_specs=[wide, wide], out_shape=[jax.ShapeDtypeStruct((s_len, 2 * FOX_WIDTH), BF16)] * 2,
        compiler_params=_params("parallel"),
    )(qa, lse, o, dcat)


def _foxa_fwd(qa, ka, va, tq, tk, name, host=None):
    s_len = qa.shape[0]

    def body(q_ref, k_ref, v_ref, o_ref, lse_ref):
        i = pl.program_id(1)
        lane = lax.broadcasted_iota(jnp.int32, (1, LANES), 1)
        cols = [slice(LANES * hh, LANES * hh + LANES) for hh in range(2)]
        qh = [q_ref[:, c] for c in cols]

        def scores(j):
            off = pl.multiple_of(j * tk, tk)
            return [lax.dot_general(qh[hh], k_ref[pl.ds(off, tk), cols[hh]], NT_DIMS, preferred_element_type=F32)
                    for hh in range(2)]

        def update(j, s, m, acc, mask):
            off = pl.multiple_of(j * tk, tk)
            m_out, acc_out = [], []
            for hh in range(2):
                sh = s[hh] if mask is None else jnp.where(mask, s[hh], NEG_INF)
                m_new = jnp.maximum(m[hh], jnp.max(sh, axis=1, keepdims=True))
                pr = jnp.exp(sh - m_new).astype(BF16)
                acc_out.append(jnp.exp(m[hh] - m_new) * acc[hh]
                               + jnp.dot(pr, v_ref[pl.ds(off, tk), cols[hh]], preferred_element_type=F32))
                m_out.append(m_new)
            return m_out, acc_out

        def step(j, carry):
            s_next = scores(j + 1)
            m, acc = update(j, carry[0:2], carry[2:4], carry[4:6], None)
            return (*s_next, *m, *acc)

        n_full = _shift_div(i, tq, tk)
        n_part = max(tq // tk, 1)
        qi = lax.broadcasted_iota(jnp.int32, (tq, tk), 0) + i * tq
        ki = lax.broadcasted_iota(jnp.int32, (tq, tk), 1)
        carry = (*scores(0), *([jnp.full((tq, 1), NEG_INF, F32)] * 2), *([jnp.zeros((tq, LANES), F32)] * 2))
        carry = lax.fori_loop(0, n_full, step, carry)
        s, m, acc = list(carry[0:2]), list(carry[2:4]), list(carry[4:6])
        for jj in range(n_part):
            s_next = scores(n_full + jj + 1) if jj < n_part - 1 else None
            m, acc = update(n_full + jj, s, m, acc, ki + (n_full + jj) * tk <= qi)
            s = s_next
        res = []
        for hh in range(2):
            l = acc[hh][:, _spare_lane(hh):_spare_lane(hh) + 1]
            res.append((acc[hh] / l, m[hh] + jnp.log(l)))
        low = lane < FOX_HEAD_DIM
        o_ref[...] = jnp.where(low, res[0][0], res[1][0])
        lse_ref[...] = jnp.where(low, res[0][1], res[1][1])

    pair = pl.BlockSpec((s_len, 2 * LANES), lambda p, i: (0, p))
    out = pl.BlockSpec((tq, LANES), lambda p, i: (i, p))
    return _pcall(
        body, name=name, grid=(N_PAIRS, s_len // tq),
        in_specs=[pl.BlockSpec((tq, 2 * LANES), lambda p, i: (i, p)), pair, pair],
        out_specs=[out, out], out_shape=[jax.ShapeDtypeStruct((s_len, FOX_WIDTH), F32)] * 2,
        args=[qa, ka, va], sem=("parallel", "parallel"), host=host)


def _foxa_dq(qb, ka, va, doa, tq, tk, name, host=None):
    s_len = qb.shape[0]

    def body(q_ref, k_ref, v_ref, do_ref, dq_ref, rs_ref):
        i = pl.program_id(1)
        lane = lax.broadcasted_iota(jnp.int32, (1, LANES), 1)
        cols = [slice(LANES * hh, LANES * hh + LANES) for hh in range(2)]
        qh = [q_ref[:, c] for c in cols]
        doh = [do_ref[:, c] for c in cols]

        def update(j, acc, mask):
            off = pl.multiple_of(j * tk, tk)
            out = []
            for hh in range(2):
                kt = k_ref[pl.ds(off, tk), cols[hh]]
                pr = jnp.exp(lax.dot_general(qh[hh], kt, NT_DIMS, preferred_element_type=F32))
                if mask is not None:
                    pr = jnp.where(mask, pr, 0.0)
                ds = pr * lax.dot_general(doh[hh], v_ref[pl.ds(off, tk), cols[hh]], NT_DIMS, preferred_element_type=F32)
                out.append(acc[hh] + jnp.dot(ds.astype(BF16), kt, preferred_element_type=F32))
            return tuple(out)

        n_full = _shift_div(i, tq, tk)
        qi = lax.broadcasted_iota(jnp.int32, (tq, tk), 0) + i * tq
        ki = lax.broadcasted_iota(jnp.int32, (tq, tk), 1)
        acc = lax.fori_loop(0, n_full, lambda j, a: update(j, a, None), (jnp.zeros((tq, LANES), F32),) * 2)
        for jj in range(max(tq // tk, 1)):
            acc = update(n_full + jj, acc, ki + (n_full + jj) * tk <= qi)
        low = lane < FOX_HEAD_DIM
        dq_ref[...] = (jnp.where(low, acc[0], acc[1]) * Q_SCALE).astype(BF16)
        rs_ref[...] = jnp.where(low, acc[0][:, _spare_lane(0):_spare_lane(0) + 1], acc[1][:, _spare_lane(1):_spare_lane(1) + 1])

    pair = pl.BlockSpec((s_len, 2 * LANES), lambda p, i: (0, p))
    tile2 = pl.BlockSpec((tq, 2 * LANES), lambda p, i: (i, p))
    out = pl.BlockSpec((tq, LANES), lambda p, i: (i, p))
    return _pcall(
        body, name=name, grid=(N_PAIRS, s_len // tq), in_specs=[tile2, pair, pair, tile2], out_specs=[out, out],
        out_shape=[jax.ShapeDtypeStruct((s_len, FOX_WIDTH), BF16), jax.ShapeDtypeStruct((s_len, FOX_WIDTH), F32)],
        args=[qb, ka, va, doa], sem=("parallel", "parallel"), host=host)


def _foxa_dkv(qb, ka, va, doa, tq, tk, name, host=None):
    s_len = qb.shape[0]
    n_q = s_len // tq

    def body(k_ref, v_ref, q_ref, do_ref, dk_ref, dv_ref, cs_ref):
        j = pl.program_id(1)
        lane = lax.broadcasted_iota(jnp.int32, (1, LANES), 1)
        cols = [slice(LANES * hh, LANES * hh + LANES) for hh in range(2)]
        kh = [k_ref[:, c] for c in cols]
        vh = [v_ref[:, c] for c in cols]

        def update(i, acc, mask):
            off = pl.multiple_of(i * tq, tq)
            out = []
            for hh in range(2):
                qt = q_ref[pl.ds(off, tq), cols[hh]]
                dot = do_ref[pl.ds(off, tq), cols[hh]]
                pt = jnp.exp(lax.dot_general(kh[hh], qt, NT_DIMS, preferred_element_type=F32))
                if mask is not None:
                    pt = jnp.where(mask, pt, 0.0)
                dv = acc[2 * hh + 1] + jnp.dot(pt.astype(BF16), dot, preferred_element_type=F32)
                dst = pt * lax.dot_general(vh[hh], dot, NT_DIMS, preferred_element_type=F32)
                out += [acc[2 * hh] + jnp.dot(dst.astype(BF16), qt, preferred_element_type=F32), dv]
            return tuple(out)

        i0 = _shift_div(j, tk, tq)
        n_part = max(tk // tq, 1)
        ki = lax.broadcasted_iota(jnp.int32, (tk, tq), 0) + j * tk
        qi = lax.broadcasted_iota(jnp.int32, (tk, tq), 1)
        acc = (jnp.zeros((tk, LANES), F32),) * 4
        for ii in range(n_part):
            acc = update(i0 + ii, acc, ki <= qi + (i0 + ii) * tq)
        acc = lax.fori_loop(i0 + n_part, n_q, lambda i, a: update(i, a, None), acc)
        low = lane < FOX_HEAD_DIM
        dk_ref[...] = jnp.where(low, acc[0], acc[2]).astype(BF16)
        dv_ref[...] = jnp.where(low, acc[1], acc[3]).astype(BF16)
        cs_ref[...] = jnp.where(low, acc[0][:, _spare_lane(0) + 3:_spare_lane(0) + 4],
                                acc[2][:, _spare_lane(1) + 3:_spare_lane(1) + 4])

    pair = pl.BlockSpec((s_len, 2 * LANES), lambda p, j: (0, p))
    tile2 = pl.BlockSpec((tk, 2 * LANES), lambda p, j: (j, p))
    out = pl.BlockSpec((tk, LANES), lambda p, j: (j, p))
    return _pcall(
        body, name=name, grid=(N_PAIRS, s_len // tk), in_specs=[tile2, tile2, pair, pair], out_specs=[out, out, out],
        out_shape=[jax.ShapeDtypeStruct((s_len, FOX_WIDTH), BF16), jax.ShapeDtypeStruct((s_len, FOX_WIDTH), BF16),
                   jax.ShapeDtypeStruct((s_len, FOX_WIDTH), F32)],
        args=[ka, va, qb, doa], sem=("parallel", "parallel"), host=host)


def _mem_scores_t(q, kv, h):
    lo = h * MEM_HEAD_DIM
    st = lax.dot_general(kv[:, lo:lo + MEM_HEAD_DIM], q[:, lo:lo + MEM_HEAD_DIM], NT_DIMS,
                         preferred_element_type=F32) * (MEM_HEAD_DIM ** -0.5)
    e = jnp.exp(st - jnp.max(st, axis=0, keepdims=True))
    return e / jnp.sum(e, axis=0, keepdims=True)


def _memattn_fwd(q, kv, tile, name):
    s_len = q.shape[0]
    n_mem = kv.shape[0]

    def body(q_ref, kv_ref, o_ref):
        q = q_ref[...]
        kv = kv_ref[...]
        for h in range(MEM_HEADS):
            lo = h * MEM_HEAD_DIM
            pt = _mem_scores_t(q, kv, h).astype(BF16)
            vh = kv[:, MEM_INNER + lo:MEM_INNER + lo + MEM_HEAD_DIM]
            o_ref[:, lo:lo + MEM_HEAD_DIM] = lax.dot_general(pt, vh, TN_DIMS, preferred_element_type=F32).astype(BF16)

    return pl.pallas_call(
        body, name=name, grid=(s_len // tile,),
        in_specs=[pl.BlockSpec((tile, MEM_INNER), lambda i: (i, 0)), _full((n_mem, 2 * MEM_INNER))],
        out_specs=pl.BlockSpec((tile, MEM_INNER), lambda i: (i, 0)),
        out_shape=jax.ShapeDtypeStruct((s_len, MEM_INNER), BF16),
        compiler_params=_params("parallel"),
    )(q, kv)


def _memattn_bwd(q, kv, do, tile, name):
    s_len = q.shape[0]
    n_mem = kv.shape[0]
    scale = MEM_HEAD_DIM ** -0.5

    def body(q_ref, kv_ref, do_ref, dq_ref, dkv_ref):
        @pl.when(pl.program_id(0) == 0)
        def _():
            dkv_ref[...] = jnp.zeros_like(dkv_ref)

        q = q_ref[...]
        kv = kv_ref[...]
        do = do_ref[...]
        for h in range(MEM_HEADS):
            lo = h * MEM_HEAD_DIM
            qh = q[:, lo:lo + MEM_HEAD_DIM]
            kh = kv[:, lo:lo + MEM_HEAD_DIM]
            vh = kv[:, MEM_INNER + lo:MEM_INNER + lo + MEM_HEAD_DIM]
            doh = do[:, lo:lo + MEM_HEAD_DIM]
            pt = _mem_scores_t(q, kv, h)
            dkv_ref[:, MEM_INNER + lo:MEM_INNER + lo + MEM_HEAD_DIM] += jnp.dot(
                pt.astype(BF16), doh, preferred_element_type=F32)
            dpt = lax.dot_general(vh, doh, NT_DIMS, preferred_element_type=F32)
            dst = (pt * (dpt - jnp.sum(pt * dpt, axis=0, keepdims=True)) * scale).astype(BF16)
            dkv_ref[:, lo:lo + MEM_HEAD_DIM] += jnp.dot(dst, qh, preferred_element_type=F32)
            dq_ref[:, lo:lo + MEM_HEAD_DIM] = lax.dot_general(dst, kh, TN_DIMS, preferred_element_type=F32).astype(BF16)

    return pl.pallas_call(
        body, name=name, grid=(s_len // tile,),
        in_specs=[pl.BlockSpec((tile, MEM_INNER), lambda i: (i, 0)), _full((n_mem, 2 * MEM_INNER)),
                  pl.BlockSpec((tile, MEM_INNER), lambda i: (i, 0))],
        out_specs=[pl.BlockSpec((tile, MEM_INNER), lambda i: (i, 0)), _full((n_mem, 2 * MEM_INNER))],
        out_shape=[jax.ShapeDtypeStruct((s_len, MEM_INNER), BF16), jax.ShapeDtypeStruct((n_mem, 2 * MEM_INNER), F32)],
        compiler_params=_params("arbitrary"),
    )(q, kv, do)


def _loss_head(y, target, tile, name):
    s_len, d = y.shape

    def body(y_ref, t_ref, dy_ref, l_ref):
        @pl.when(pl.program_id(0) == 0)
        def _():
            l_ref[...] = jnp.zeros_like(l_ref)

        err = y_ref[...] - t_ref[...]
        dy_ref[...] = err * (1.0 / d)
        l_ref[...] += jnp.sum(err * err, axis=0, keepdims=True) * (0.5 / d)

    row = lambda i: (i, 0)
    return pl.pallas_call(
        body, name=name, grid=(s_len // tile,),
        in_specs=[pl.BlockSpec((tile, d), row), pl.BlockSpec((tile, d), row)],
        out_specs=[pl.BlockSpec((tile, d), row), _full((1, d))],
        out_shape=[jax.ShapeDtypeStruct((s_len, d), F32), jax.ShapeDtypeStruct((1, d), F32)],
        compiler_params=_params("arbitrary"),
    )(y, target)


def _attn_tile(s_len):
    return min(256, s_len // 2)


REST = ("w_out", "w_mq", "w_mk", "w_mv", "w_mo", "w_up", "w_down")
SHARD_AXIS = {"w_in": 1, "w_out": 0, "w_mq": 0, "w_mk": 0, "w_mv": 0, "w_mo": 1, "w_up": 1, "w_down": 0}


def _full_from_shards(sh, axis):
    n, r, c = sh.shape
    if axis == 0:
        return sh.reshape(n * r, c)
    return sh.transpose(1, 0, 2).reshape(r, n * c)


def _rest_weights(lands):
    w = {n: _full_from_shards(sh, SHARD_AXIS[n]) for n, sh in zip(REST, lands)}
    w["w_mkv"] = jnp.concatenate([w.pop("w_mk"), w.pop("w_mv")], axis=1)
    return w


def _w_in_cat(land):
    return jnp.pad(_full_from_shards(land, 1), ((0, 0), (0, IN_CAT - IN_COLS)))


def _layer_fwd(x0, mem, w, l, rest_src=None, next_src=None):
    s_len = x0.shape[0]
    tile = min(512, s_len)
    tile_ff = min(256, s_len)
    ta = _attn_tile(s_len)
    ident = lambda z: z
    sv = {"x0": x0}

    h1, ag, qkv, fl = _rms_matmul(
        x0, w["norm_mix_pre"], w["w_in_cat"],
        [(0, 2 * CONV_CH, [(F32, ident)]), (2 * CONV_CH, IN_MAIN, [(BF16, ident)]), (IN_MAIN, IN_CAT, [(F32, ident)])],
        tile, f"mix_in_{l}")
    cum = _cumsum_fwd(fl, w["b_forget"], tile, f"cumsum_fwd_{l}")
    u3 = _conv_fwd(ag, w["conv_w"], w["conv_b"], w["conv_ln_g"], w["conv_ln_b"], tile, f"conv_fwd_{l}")
    qa, ka, va = _foxa_prep(qkv, cum, tile, f"fox_prep_{l}")
    if rest_src is None:
        o, lse = _foxa_fwd(qa, ka, va, ta, 2 * ta, f"fox_fwd_{l}")
    else:
        (o, lse), rest_land = _foxa_fwd(qa, ka, va, ta, 2 * ta, f"fox_fwd_{l}", host=(rest_src, [True] * len(rest_src)))
        w = {**w, **_rest_weights(rest_land)}
    cat = jnp.concatenate([u3, o.astype(BF16)], axis=1)
    y1, x1 = _matmul_resnorm(cat, w["w_out"], x0, w["norm_mix_post"], tile, f"mix_out_{l}")
    sv.update(h1=h1, ag=ag, fl=fl, qa=qa, ka=ka, va=va, o=o, lse=lse, cat=cat, y1=y1, x1=x1)

    h2, qm = _rms_matmul(x1, w["norm_mem_pre"], w["w_mq"], [(0, MEM_INNER, [(BF16, ident)])], tile, f"mem_q_{l}")
    mem_n, kv = _rms_matmul(mem, w["norm_memkv"], w["w_mkv"], [(0, 2 * MEM_INNER, [(BF16, ident)])],
                            mem.shape[0], f"mem_kv_{l}")
    om = _memattn_fwd(qm, kv, tile, f"mem_attn_fwd_{l}")
    y2, x2 = _matmul_resnorm(om, w["w_mo"], x1, w["norm_mem_post"], tile, f"mem_out_{l}")
    sv.update(h2=h2, qm=qm, mem_n=mem_n, kv=kv, om=om, y2=y2, x2=x2)

    relu2 = lambda z: jnp.square(jnp.maximum(z, 0.0))
    up = _rms_matmul(x2, w["norm_mlp_pre"], w["w_up"], [(0, D_FF, [(BF16, ident), (BF16, relu2)])], tile_ff,
                     f"mlp_up_{l}", host=None if next_src is None else (next_src, [True] * len(next_src)))
    (h3, pre, hid), next_land = (up, None) if next_src is None else up
    y3, x3 = _matmul_resnorm(hid, w["w_down"], x2, w["norm_mlp_post"], tile_ff, f"mlp_down_{l}")
    sv.update(h3=h3, pre=pre, hid=hid, y3=y3)
    return x3, sv, w, next_land


def _layer_bwd(dx3, mem, w, sv, l, scatter_rest=False, dkv_src=None, scatter_w_in=False):
    s_len = dx3.shape[0]
    tile = min(512, s_len)
    tile_ff = min(256, s_len)
    ta = _attn_tile(s_len)
    tk = min(512, s_len)
    n_mem = mem.shape[0]
    g = {}

    dy3, dpre, g["norm_mlp_post"] = _resnorm_bwd_mm(dx3, sv["y3"], w["norm_mlp_post"], w["w_down"], tile_ff,
                                                    f"mlp_down_bwd_{l}", BF16, pre=sv["pre"])
    g["w_down"] = _matmul_tn_shards(sv["hid"], dy3, 0, tk, f"dw_down_{l}")
    dx2, g["norm_mlp_pre"] = _mm_prenorm_bwd(dpre, w["w_up"], sv["x2"], w["norm_mlp_pre"], dx3, tile_ff,
                                             f"mlp_up_bwd_{l}")
    g["w_up"] = _matmul_tn_shards(sv["h3"], dpre, 1, tk, f"dw_up_{l}")

    dy2, dom, g["norm_mem_post"] = _resnorm_bwd_mm(dx2, sv["y2"], w["norm_mem_post"], w["w_mo"], tile,
                                                   f"mem_out_bwd_{l}", BF16)
    g["w_mo"] = _matmul_tn_shards(sv["om"], dy2, 1, tk, f"dw_mo_{l}")
    dqm, dkv = _memattn_bwd(sv["qm"], sv["kv"], dom, tile, f"mem_attn_bwd_{l}")
    dkvb = dkv.astype(BF16)
    g["w_mq"] = _matmul_tn_shards(sv["h2"], dqm, 0, tk, f"dw_mq_{l}")
    dx1, g["norm_mem_pre"] = _mm_prenorm_bwd(dqm, w["w_mq"], sv["x1"], w["norm_mem_pre"], dx2, tile, f"mem_q_bwd_{l}")
    _, g["norm_memkv"] = _mm_prenorm_bwd(dkvb, w["w_mkv"], mem, w["norm_memkv"], None, n_mem, f"mem_kv_bwd_{l}")
    g["w_mk"] = _matmul_tn_shards(sv["mem_n"], dkvb[:, :MEM_INNER], 0, n_mem, f"dw_mk_{l}")
    g["w_mv"] = _matmul_tn_shards(sv["mem_n"], dkvb[:, MEM_INNER:], 0, n_mem, f"dw_mv_{l}")

    dy1, dcat, g["norm_mix_post"] = _resnorm_bwd_mm(dx1, sv["y1"], w["norm_mix_post"], w["w_out"], tile,
                                                    f"mix_out_bwd_{l}", F32)
    g["w_out"] = _matmul_tn_shards(sv["cat"], dy1, 0, tk, f"dw_out_{l}")
    qb, doa = _foxa_prep_bwd(sv["qa"], sv["lse"], sv["o"], dcat, tile, f"fox_prep_bwd_{l}")
    rest_land = None
    if scatter_rest:
        rest = [g[n] for n in REST]
        (dq, rs), rest_land = _foxa_dq(qb, sv["ka"], sv["va"], doa, ta, 2 * ta, f"fox_dq_{l}",
                                       host=(rest, [False] * len(rest)))
    else:
        dq, rs = _foxa_dq(qb, sv["ka"], sv["va"], doa, ta, 2 * ta, f"fox_dq_{l}")
    dkv_res = _foxa_dkv(qb, sv["ka"], sv["va"], doa, 2 * ta, ta, f"fox_dkv_{l}",
                        host=None if dkv_src is None else (dkv_src, [False] * len(dkv_src)))
    (dk, dv, cs), dkv_land = (dkv_res, None) if dkv_src is None else dkv_res
    dfl, db = _cumsum_bwd(rs, cs, sv["fl"], w["b_forget"], tile, f"cumsum_bwd_{l}")
    g["b_forget"] = db[:, :FOX_HEADS]
    dag, dconv_w, dconv_v = _conv_bwd(sv["ag"], dcat, w["conv_w"], w["conv_b"], w["conv_ln_g"], w["conv_ln_b"], tile,
                                      f"conv_bwd_{l}")
    g["conv_w"] = dconv_w[:CONV_WIDTH]
    g["conv_b"], g["conv_ln_g"], g["conv_ln_b"] = dconv_v[0:1], dconv_v[1:2], dconv_v[2:3]
    dz = jnp.concatenate([dag, dq, dk, dv, dfl.astype(BF16)], axis=1)
    dw_in = _matmul_tn(sv["h1"], dz, 256, tk, f"dw_in_{l}")[:, :IN_COLS]
    g["w_in"] = dw_in.reshape(D_MODEL, N_DEV, IN_COLS // N_DEV).transpose(1, 0, 2).astype(BF16)
    res = _mm_prenorm_bwd(dz, w["w_in_cat"], sv["x0"], w["norm_mix_pre"], dx1, tile, f"mix_in_bwd_{l}",
                          host=([g["w_in"]], [False]) if scatter_w_in else None)
    (dx0, g["norm_mix_pre"]), w_in_land = res if scatter_w_in else (res, None)
    return dx0, g, rest_land, dkv_land, w_in_land


def _sum_blocks(a, name):
    n, rows, cols = a.shape

    def body(a_ref, o_ref):
        acc = a_ref[0]
        for j in range(1, n):
            acc = acc + a_ref[j]
        o_ref[...] = acc

    return pl.pallas_call(
        body, name=name, in_specs=[_full((n, rows, cols))], out_specs=_full((rows, cols)),
        out_shape=jax.ShapeDtypeStruct((rows, cols), F32), grid=(1,),
    )(a)


def _adamw(gparts, w, m, v, tile, name):
    n_l, rows, cols = w.shape
    n = gparts[0].shape[0]
    n_t = rows // tile
    c1 = 1.0 - ADAM_B1
    c2 = 1.0 - ADAM_B2
    bc1 = 1.0 - ADAM_B1 ** ADAM_STEP
    bc2 = 1.0 - ADAM_B2 ** ADAM_STEP

    def body(*refs):
        gp_refs, (w_ref, m_ref, v_ref, g_ref, d_ref, mo_ref, vo_ref) = refs[:n_l], refs[n_l:]
        layer = pl.program_id(0)
        g = None
        for l, gp_ref in enumerate(gp_refs):
            gl = gp_ref[0].astype(F32)
            for j in range(1, n):
                gl = gl + gp_ref[j].astype(F32)
            g = gl if g is None else jnp.where(layer == l, gl, g)
        g_ref[...] = g
        m_new = ADAM_B1 * m_ref[...] + c1 * g
        v_new = ADAM_B2 * v_ref[...] + c2 * (g * g)
        mo_ref[...] = m_new
        vo_ref[...] = v_new
        d_ref[...] = -ADAM_LR * ((m_new / bc1) / (jnp.sqrt(v_new / bc2) + ADAM_EPS) + ADAM_WD * w_ref[...])

    def gp_spec(l):
        return pl.BlockSpec((n, tile, cols), lambda L, i: (0, jnp.where(L < l, 0, jnp.where(L > l, n_t - 1, i)), 0))

    spec = pl.BlockSpec((None, tile, cols), lambda L, i: (L, i, 0))
    return pl.pallas_call(
        body, name=name, grid=(n_l, n_t),
        in_specs=[gp_spec(l) for l in range(n_l)] + [spec, spec, spec],
        out_specs=[spec] * 4, out_shape=[jax.ShapeDtypeStruct((n_l, rows, cols), F32)] * 4,
        compiler_params=_params("arbitrary", "arbitrary"),
    )(*gparts, w, m, v)


def _pack_rows(parts, total_rows):
    flat = [p.reshape(-1, D_MODEL) for p in parts]
    used = sum(f.shape[0] for f in flat)
    if total_rows > used:
        flat.append(jnp.zeros((total_rows - used, D_MODEL), flat[0].dtype))
    return jnp.concatenate(flat, axis=0)


def kernel(x, mem, norm_mix_pre, norm_mix_post, w_in, b_forget, conv_w, conv_b, conv_ln_g, conv_ln_b, w_out, norm_mem_pre, norm_mem_post, norm_memkv, w_mq, w_mk, w_mv, w_mo, norm_mlp_pre, norm_mlp_post, w_up, w_down, loss_target, m_norm_mix_pre, m_norm_mix_post, m_w_in, m_b_forget, m_conv_w, m_conv_b, m_conv_ln_g, m_conv_ln_b, m_w_out, m_norm_mem_pre, m_norm_mem_post, m_norm_memkv, m_w_mq, m_w_mk, m_w_mv, m_w_mo, m_norm_mlp_pre, m_norm_mlp_post, m_w_up, m_w_down, v_norm_mix_pre, v_norm_mix_post, v_w_in, v_b_forget, v_conv_w, v_conv_b, v_conv_ln_g, v_conv_ln_b, v_w_out, v_norm_mem_pre, v_norm_mem_post, v_norm_memkv, v_w_mq, v_w_mk, v_w_mv, v_w_mo, v_norm_mlp_pre, v_norm_mlp_post, v_w_up, v_w_down):
    p = dict(locals())
    names = ("norm_mix_pre", "norm_mix_post", "w_in", "b_forget", "conv_w", "conv_b", "conv_ln_g", "conv_ln_b", "w_out",
             "norm_mem_pre", "norm_mem_post", "norm_memkv", "w_mq", "w_mk", "w_mv", "w_mo", "norm_mlp_pre",
             "norm_mlp_post", "w_up", "w_down")
    me = 4 * lax.axis_index("x") + 2 * lax.axis_index("y") + lax.axis_index("c")
    conv_cols = conv_w.shape[2]
    s_len = x.shape[1]
    bf = {n: p[n].astype(BF16) for n in BIG}

    conv_pack = _pack_rows([jnp.pad(conv_w, ((0, 0), (0, CONV_PAD - CONV_WIDTH), (0, 0)))], 8)
    win_land, conv_land = _exchange([bf["w_in"][0], conv_pack], [True, True], "gather_first")
    conv_rows = DEPTH * CONV_PAD * conv_cols // D_MODEL
    conv_full = conv_land[:, :conv_rows].reshape(N_DEV, DEPTH, CONV_PAD, conv_cols)
    conv_full = conv_full.transpose(1, 2, 0, 3).reshape(DEPTH, CONV_PAD, N_DEV * conv_cols)
    b_forget_pad = jnp.pad(b_forget, ((0, 0), (0, LANES - FOX_HEADS)))

    def first_weights(l, land):
        w = {"w_in_cat": _w_in_cat(land), "conv_w": conv_full[l], "b_forget": b_forget_pad[l:l + 1]}
        for n in VEC[:-1]:
            w[n] = p[n][l:l + 1]
        return w

    h, sv0, w0, win1_land = _layer_fwd(x[0], mem[0], first_weights(0, win_land), 0,
                                       rest_src=[bf[n][0] for n in REST], next_src=[bf["w_in"][1]])
    h, sv1, w1, _ = _layer_fwd(h, mem[0], first_weights(1, win1_land[0]), 1, rest_src=[bf[n][1] for n in REST])
    dh, loss_row = _loss_head(h, loss_target[0], min(512, s_len), "loss_head")
    dh, g1, rest_g1, _, _ = _layer_bwd(dh, mem[0], w1, sv1, 1, scatter_rest=True)
    grad_x, g0, rest_g0, win1_g, win0_g = _layer_bwd(dh, mem[0], w0, sv0, 0, scatter_rest=True, dkv_src=[g1["w_in"]],
                                                     scatter_w_in=True)

    def small_rows(get):
        rows = [jnp.concatenate([get(n) for n in VEC_1024], axis=0),
                jnp.concatenate([get(n) for n in VEC_512], axis=0).reshape(len(VEC_512), D_MODEL),
                jnp.pad(get("b_forget").reshape(1, -1), ((0, 0), (0, D_MODEL - DEPTH * FOX_HEADS)))]
        return jnp.concatenate(rows, axis=0)

    def tap_rows(conv):
        return jnp.pad(conv.reshape(1, -1), ((0, 0), (0, 4 * D_MODEL - conv.size))).reshape(4, D_MODEL)

    n_vec_rows = DEPTH * len(VEC_1024) + len(VEC_512) + 1
    part = small_rows(lambda n: jnp.concatenate([g0[n], g1[n]], axis=0))
    conv_part = jnp.stack([g0["conv_w"], g1["conv_w"]]).reshape(CONV_WIDTH, D_MODEL)
    n_part = 1 + n_vec_rows + CONV_WIDTH
    pad_rows = -n_part % 8
    small_land = _exchange([jnp.concatenate([loss_row, part, conv_part, jnp.zeros((pad_rows, D_MODEL), F32)], axis=0)],
                           [True], "gather_small")[0]
    total = _sum_blocks(small_land, "sum_small")
    loss = jnp.sum(total[0])
    conv_g = total[1 + n_vec_rows:n_part].reshape(DEPTH, CONV_WIDTH, CONV_CH)
    conv_g = lax.dynamic_slice_in_dim(conv_g, me * conv_cols, conv_cols, axis=2)
    fill = jnp.zeros((SMALL_ROWS - n_vec_rows - 4, D_MODEL), F32)

    def small_pack(vec_rows, conv):
        return jnp.concatenate([vec_rows, tap_rows(conv), fill], axis=0)

    small_out = _adamw([small_pack(total[1:1 + n_vec_rows], conv_g)[None]],
                       *[small_pack(small_rows(lambda n: p[pre + n]), p[pre + "conv_w"])[None] for pre in ("", "m_", "v_")],
                       SMALL_ROWS, "adamw_small")

    def unpack_small(buf):
        out = {}
        for k, n in enumerate(VEC_1024):
            out[n] = buf[DEPTH * k:DEPTH * (k + 1)]
        at = DEPTH * len(VEC_1024)
        for k, n in enumerate(VEC_512):
            out[n] = buf[at + k].reshape(DEPTH, CONV_CH)
        at += len(VEC_512)
        out["b_forget"] = buf[at, :DEPTH * FOX_HEADS].reshape(DEPTH, FOX_HEADS)
        out["conv_w"] = buf[at + 1:at + 5].reshape(-1)[:DEPTH * CONV_WIDTH * conv_cols].reshape(DEPTH, CONV_WIDTH, conv_cols)
        return out

    landed = {"w_in": [win0_g[0], win1_g[0]]}
    for i, n in enumerate(REST):
        landed[n] = [rest_g0[i], rest_g1[i]]
    big_out = {n: _adamw(landed[n], p[n], p["m_" + n], p["v_" + n], ADAMW_TILE[n], f"adamw_{n}") for n in BIG}

    result = [loss, grad_x[None]]
    for k in range(4):
        smalls = unpack_small(small_out[k][0])
        result += [big_out[n][k] if n in big_out else smalls[n] for n in names]
    return tuple(result)
```

```python
import functools

import jax
import jax.numpy as jnp
from jax import lax
from jax.experimental import pallas as pl
from jax.experimental.pallas import tpu as pltpu

F32 = jnp.float32
BF16 = jnp.bfloat16

N_DEV = 8
DEPTH = 2
D_MODEL = 1024
CONV_CH = 512
CONV_WIDTH = 31
CONV_PAD = 32
FOX_HEADS = 8
FOX_HEAD_DIM = 64
FOX_WIDTH = 512
N_PAIRS = 4
MEM_HEADS = 4
MEM_HEAD_DIM = 128
MEM_INNER = 512
D_FF = 4096
IN_MAIN = 2560
IN_COLS = 2568
IN_CAT = IN_MAIN + 128
LANES = 128
EPS = 1e-6
NEG_INF = -1e30

ADAM_LR = 0.001
ADAM_B1 = 0.9
ADAM_B2 = 0.999
ADAM_EPS = 1e-08
ADAM_WD = 0.01
ADAM_STEP = 10

NT_DIMS = (((1,), (1,)), ((), ()))
TN_DIMS = (((0,), (0,)), ((), ()))

BIG = ("w_in", "w_out", "w_mq", "w_mk", "w_mv", "w_mo", "w_up", "w_down")
ADAMW_TILE = {"w_in": 256, "w_out": 128, "w_mq": 128, "w_mk": 128, "w_mv": 128, "w_mo": 512, "w_up": 256, "w_down": 128}

VEC_1024 = ("norm_mix_pre", "norm_mix_post", "norm_mem_pre", "norm_mem_post", "norm_memkv", "norm_mlp_pre", "norm_mlp_post")
VEC_512 = ("conv_b", "conv_ln_g", "conv_ln_b")
VEC = VEC_1024 + VEC_512 + ("b_forget",)
SMALL_ROWS = 32


def _sigmoid(x):
    return 1.0 / (1.0 + jnp.exp(-x))


def _rms(x, g):
    r = lax.rsqrt(jnp.mean(x * x, axis=-1, keepdims=True) + EPS)
    return x * r * g


def _rms_bwd(x, g, dh):
    r = lax.rsqrt(jnp.mean(x * x, axis=-1, keepdims=True) + EPS)
    gh = dh * g
    c = jnp.mean(gh * x, axis=-1, keepdims=True)
    dx = r * gh - x * (r * r * r * c)
    dg = jnp.sum(dh * (x * r), axis=0, keepdims=True)
    return dx, dg


def _full(shape):
    nd = len(shape)
    return pl.BlockSpec(shape, lambda *_: (0,) * nd)


def _params(*sem):
    return pltpu.CompilerParams(dimension_semantics=sem)


def _exchange_copies(src_refs, out_refs, same, send_sems, recv_sems, local_sems, with_recvs):
    x, y, c = lax.axis_index("x"), lax.axis_index("y"), lax.axis_index("c")
    me = 4 * x + 2 * y + c
    local, sends, recvs = [], [], []
    for a, (s_ref, o_ref) in enumerate(zip(src_refs, out_refs)):
        def mine(idx, s_ref=s_ref, whole=same[a]):
            return s_ref if whole else s_ref.at[idx]

        local.append(pltpu.make_async_copy(mine(me), o_ref.at[me], local_sems.at[a]))
        for k in range(1, N_DEV):
            px = 1 - x if k & 4 else x
            py = 1 - y if k & 2 else y
            pc = 1 - c if k & 1 else c
            peer = 4 * px + 2 * py + pc
            sem = a * (N_DEV - 1) + k - 1
            common = dict(send_sem=send_sems.at[sem], recv_sem=recv_sems.at[sem], device_id=(px, py, pc),
                          device_id_type=pl.DeviceIdType.MESH)
            sends.append(pltpu.make_async_remote_copy(src_ref=mine(peer), dst_ref=o_ref.at[me], **common))
            if with_recvs:
                recvs.append(pltpu.make_async_remote_copy(src_ref=mine(peer), dst_ref=o_ref.at[peer], **common))
    return local, sends, recvs


def _gather_copies(src_refs, out_refs, send_sems, recv_sems, local_sems, phase):
    x, y, c = lax.axis_index("x"), lax.axis_index("y"), lax.axis_index("c")
    sibling = (x, y, 1 - c)
    chips = [(1 - x, y), (x, 1 - y), (1 - x, 1 - y)]

    def idx(px, py, pc):
        return 4 * px + 2 * py + pc

    local, first, arrive, passed, final = [], [], [], [], []
    for a, (s_ref, o_ref) in enumerate(zip(src_refs, out_refs)):
        def cp(k, src, block, to, a=a, o_ref=o_ref):
            sem = a * (N_DEV - 1) + k
            return pltpu.make_async_remote_copy(src_ref=src, dst_ref=o_ref.at[block], send_sem=send_sems.at[sem],
                                                recv_sem=recv_sems.at[sem], device_id=to, device_id_type=pl.DeviceIdType.MESH)

        me = idx(x, y, c)
        if phase != 1:
            local.append(pltpu.make_async_copy(s_ref, o_ref.at[me], local_sems.at[a]))
            first.append(cp(0, s_ref, me, sibling))
        if phase == 2:
            final.append(cp(0, s_ref, idx(x, y, 1 - c), sibling))
        for j, chip in enumerate(chips):
            theirs = idx(*chip, c)
            if phase != 1:
                first.append(cp(1 + j, s_ref, me, (*chip, c)))
            if phase == 1:
                arrive.append(cp(1 + j, s_ref, theirs, (*chip, c)))
            if phase != 0:
                passed.append(cp(4 + j, o_ref.at[theirs], theirs, sibling))
            if phase == 2:
                final.append(cp(4 + j, s_ref, idx(*chip, 1 - c), sibling))
    return local, first, arrive, passed, final


def _pcall(body, *, name, grid, in_specs, out_specs, out_shape, args, scratch_shapes=(), sem=(), host=None):
    if host is None:
        return pl.pallas_call(body, name=name, grid=grid, in_specs=in_specs, out_specs=out_specs, out_shape=out_shape,
                              scratch_shapes=list(scratch_shapes), compiler_params=_params(*sem))(*args)
    srcs, same = host
    n_in, n_out, n_scr, n_h = len(in_specs), len(out_specs), len(scratch_shapes), len(srcs)
    hbm = pl.BlockSpec(memory_space=pltpu.HBM)
    lands = [jax.ShapeDtypeStruct((N_DEV,) + (s.shape if whole else s.shape[1:]), s.dtype) for s, whole in zip(srcs, same)]

    def wrapped(*refs):
        ins, src_refs = refs[:n_in], refs[n_in:n_in + n_h]
        outs = refs[n_in + n_h:n_in + n_h + n_out]
        land_refs = refs[n_in + n_h + n_out:n_in + 2 * n_h + n_out]
        scr = refs[n_in + 2 * n_h + n_out:n_in + 2 * n_h + n_out + n_scr]
        sems = refs[n_in + 2 * n_h + n_out + n_scr:]
        ids = [pl.program_id(d) for d in range(len(grid))]
        first = functools.reduce(jnp.logical_and, [i == 0 for i in ids])
        last = functools.reduce(jnp.logical_and, [i == n - 1 for i, n in zip(ids, grid)])
        later = functools.reduce(jnp.logical_and, [ids[0] == (3 * grid[0]) // 4] + [i == 0 for i in ids[1:]])

        if all(same):
            @pl.when(first)
            def _():
                local, sends, _, _, _ = _gather_copies(src_refs, land_refs, *sems, 0)
                for cp in local + sends:
                    cp.start()

            body(*ins, *outs, *scr)

            @pl.when(later)
            def _():
                _, _, arrive, passed, _ = _gather_copies(src_refs, land_refs, *sems, 1)
                for cp in arrive:
                    cp.wait_recv()
                for cp in passed:
                    cp.start()

            @pl.when(last)
            def _():
                local, sends, _, passed, final = _gather_copies(src_refs, land_refs, *sems, 2)
                for cp in final:
                    cp.wait_recv()
                for cp in sends + passed:
                    cp.wait_send()
                for cp in local:
                    cp.wait()
        else:
            @pl.when(first)
            def _():
                local, sends, _ = _exchange_copies(src_refs, land_refs, same, *sems, False)
                for cp in local + sends:
                    cp.start()

            body(*ins, *outs, *scr)

            @pl.when(last)
            def _():
                local, sends, recvs = _exchange_copies(src_refs, land_refs, same, *sems, True)
                for cp in recvs:
                    cp.wait_recv()
                for cp in sends:
                    cp.wait_send()
                for cp in local:
                    cp.wait()

    n_sem = n_h * (N_DEV - 1)
    res = pl.pallas_call(
        wrapped, name=name, grid=grid, in_specs=list(in_specs) + [hbm] * n_h, out_specs=list(out_specs) + [hbm] * n_h,
        out_shape=list(out_shape) + lands,
        scratch_shapes=list(scratch_shapes) + [pltpu.SemaphoreType.DMA((n_sem,)), pltpu.SemaphoreType.DMA((n_sem,)),
                                               pltpu.SemaphoreType.DMA((n_h,))],
        compiler_params=_params(*(("arbitrary",) * len(grid))),
    )(*args, *srcs)
    return list(res[:n_out]), list(res[n_out:])


def _exchange(srcs, same, name):
    def body():
        pass

    return _pcall(body, name=name, grid=(1,), in_specs=[], out_specs=[], out_shape=[], args=[], host=(srcs, same))[1]


def _rms_matmul(x, g, w, segs, tile, name, host=None):
    s_len, d = x.shape
    n = w.shape[1]
    chunk = 512

    def body(x_ref, g_ref, w_ref, h_ref, *outs):
        h = _rms(x_ref[...], g_ref[...]).astype(BF16)
        h_ref[...] = h
        oi = 0
        for c0, c1, fns in segs:
            for a in range(c0, c1, chunk):
                b = min(a + chunk, c1)
                z = jnp.dot(h, w_ref[:, a:b], preferred_element_type=F32)
                for k, (dt, fn) in enumerate(fns):
                    outs[oi + k][:, a - c0:b - c0] = fn(z).astype(dt)
            oi += len(fns)

    out_shape = [jax.ShapeDtypeStruct((s_len, d), BF16)]
    out_specs = [pl.BlockSpec((tile, d), lambda i: (i, 0))]
    for c0, c1, fns in segs:
        for dt, _ in fns:
            out_shape.append(jax.ShapeDtypeStruct((s_len, c1 - c0), dt))
            out_specs.append(pl.BlockSpec((tile, c1 - c0), lambda i: (i, 0)))
    return _pcall(
        body, name=name, grid=(s_len // tile,),
        in_specs=[pl.BlockSpec((tile, d), lambda i: (i, 0)), _full((1, d)), _full((d, n))],
        out_specs=out_specs, out_shape=out_shape, args=[x, g, w], sem=("parallel",), host=host)


def _matmul_resnorm(a, w, x, g, tile, name):
    s_len, k = a.shape
    d = w.shape[1]

    def body(a_ref, w_ref, x_ref, g_ref, y_ref, xo_ref):
        y = jnp.dot(a_ref[...], w_ref[...], preferred_element_type=F32)
        y_ref[...] = y
        xo_ref[...] = x_ref[...] + _rms(y, g_ref[...])

    row = lambda i: (i, 0)
    return pl.pallas_call(
        body, name=name, grid=(s_len // tile,),
        in_specs=[pl.BlockSpec((tile, k), row), _full((k, d)), pl.BlockSpec((tile, d), row), _full((1, d))],
        out_specs=[pl.BlockSpec((tile, d), row), pl.BlockSpec((tile, d), row)],
        out_shape=[jax.ShapeDtypeStruct((s_len, d), F32), jax.ShapeDtypeStruct((s_len, d), F32)],
        compiler_params=_params("parallel"),
    )(a, w, x, g)


def _resnorm_bwd_mm(dx, y, g, w, tile, name, out_dtype, pre=None):
    s_len, d = dx.shape
    k = w.shape[0]
    chunk = 512

    def body(*refs):
        if pre is None:
            dx_ref, y_ref, g_ref, w_ref, dy_ref, da_ref, dg_ref = refs
        else:
            dx_ref, y_ref, g_ref, w_ref, pre_ref, dy_ref, da_ref, dg_ref = refs
        dy, dg = _rms_bwd(y_ref[...], g_ref[...], dx_ref[...])
        dyb = dy.astype(BF16)
        dy_ref[...] = dyb

        @pl.when(pl.program_id(0) == 0)
        def _():
            dg_ref[...] = jnp.zeros_like(dg_ref)

        dg_ref[...] += dg
        for a in range(0, k, chunk):
            b = min(a + chunk, k)
            da = lax.dot_general(dyb, w_ref[a:b, :], NT_DIMS, preferred_element_type=F32)
            if pre is not None:
                da = da * (2.0 * jnp.maximum(pre_ref[:, a:b].astype(F32), 0.0))
            da_ref[:, a:b] = da.astype(out_dtype)

    row = lambda i: (i, 0)
    in_specs = [pl.BlockSpec((tile, d), row), pl.BlockSpec((tile, d), row), _full((1, d)), _full((k, d))]
    args = [dx, y, g, w]
    if pre is not None:
        in_specs.append(pl.BlockSpec((tile, k), row))
        args.append(pre)
    return pl.pallas_call(
        body, name=name, grid=(s_len // tile,), in_specs=in_specs,
        out_specs=[pl.BlockSpec((tile, d), row), pl.BlockSpec((tile, k), row), _full((1, d))],
        out_shape=[jax.ShapeDtypeStruct((s_len, d), BF16), jax.ShapeDtypeStruct((s_len, k), out_dtype),
                   jax.ShapeDtypeStruct((1, d), F32)],
        compiler_params=_params("arbitrary"),
    )(*args)


def _mm_prenorm_bwd(dz, w, x, g, dres, tile, name, host=None):
    s_len, n = dz.shape
    d = w.shape[0]

    def body(*refs):
        if dres is None:
            dz_ref, w_ref, x_ref, g_ref, dx_ref, dg_ref = refs
        else:
            dz_ref, w_ref, x_ref, g_ref, dres_ref, dx_ref, dg_ref = refs
        dh = lax.dot_general(dz_ref[...], w_ref[...], NT_DIMS, preferred_element_type=F32)
        dx, dg = _rms_bwd(x_ref[...], g_ref[...], dh)
        if dres is not None:
            dx = dx + dres_ref[...]
        dx_ref[...] = dx

        @pl.when(pl.program_id(0) == 0)
        def _():
            dg_ref[...] = jnp.zeros_like(dg_ref)

        dg_ref[...] += dg

    row = lambda i: (i, 0)
    in_specs = [pl.BlockSpec((tile, n), row), _full((d, n)), pl.BlockSpec((tile, d), row), _full((1, d))]
    args = [dz, w, x, g]
    if dres is not None:
        in_specs.append(pl.BlockSpec((tile, d), row))
        args.append(dres)
    return _pcall(
        body, name=name, grid=(s_len // tile,), in_specs=in_specs,
        out_specs=[pl.BlockSpec((tile, d), row), _full((1, d))],
        out_shape=[jax.ShapeDtypeStruct((s_len, d), F32), jax.ShapeDtypeStruct((1, d), F32)],
        args=args, sem=("arbitrary",), host=host)


def _matmul_tn(a, b, tm, tk, name):
    s_len, m = a.shape
    n = b.shape[1]

    def body(a_ref, b_ref, o_ref):
        @pl.when(pl.program_id(1) == 0)
        def _():
            o_ref[...] = jnp.zeros_like(o_ref)

        o_ref[...] += lax.dot_general(a_ref[...], b_ref[...], TN_DIMS, preferred_element_type=F32)

    return pl.pallas_call(
        body, name=name, grid=(m // tm, s_len // tk),
        in_specs=[pl.BlockSpec((tk, tm), lambda i, k: (k, i)), pl.BlockSpec((tk, n), lambda i, k: (k, 0))],
        out_specs=pl.BlockSpec((tm, n), lambda i, k: (i, 0)),
        out_shape=jax.ShapeDtypeStruct((m, n), F32),
        compiler_params=_params("parallel", "arbitrary"),
    )(a, b)


def _matmul_tn_shards(a, b, axis, tk, name):
    s_len, m = a.shape
    n = b.shape[1]
    r, c = (m // N_DEV, n) if axis == 0 else (m, n // N_DEV)
    n_k = s_len // tk
    tm = max(r, min(m, (1 << 20) // n)) if axis == 0 else min(m, (1 << 20) // n)

    def body(a_ref, b_ref, o_ref, acc):
        k = pl.program_id(1)

        @pl.when(k == 0)
        def _():
            acc[...] = jnp.zeros_like(acc)

        acc[...] += lax.dot_general(a_ref[...], b_ref[...], TN_DIMS, preferred_element_type=F32)

        @pl.when(k == n_k - 1)
        def _():
            if axis == 0:
                o_ref[...] = acc[...].reshape(tm // r, r, c).astype(BF16)
            else:
                for j in range(N_DEV):
                    o_ref[j] = acc[:, j * c:(j + 1) * c].astype(BF16)

    if axis == 0:
        out_spec = pl.BlockSpec((tm // r, r, c), lambda i, k: (i, 0, 0))
    else:
        out_spec = pl.BlockSpec((N_DEV, tm, c), lambda i, k: (0, i, 0))
    return pl.pallas_call(
        body, name=name, grid=(m // tm, n_k),
        in_specs=[pl.BlockSpec((tk, tm), lambda i, k: (k, i)), pl.BlockSpec((tk, n), lambda i, k: (k, 0))],
        out_specs=out_spec, out_shape=jax.ShapeDtypeStruct((N_DEV, r, c), BF16),
        scratch_shapes=[pltpu.VMEM((tm, n), F32)],
        compiler_params=_params("parallel", "arbitrary"),
    )(a, b)


def _cumsum_fwd(fl, b, tile, name):
    s_len = fl.shape[0]

    def body(fl_ref, b_ref, cum_ref, carry):
        @pl.when(pl.program_id(0) == 0)
        def _():
            carry[...] = jnp.zeros_like(carry)

        xx = fl_ref[...] + b_ref[...]
        lf = jnp.minimum(xx, 0.0) - jnp.log1p(jnp.exp(-jnp.abs(xx)))
        r = lax.broadcasted_iota(jnp.int32, (tile, tile), 0)
        c = lax.broadcasted_iota(jnp.int32, (tile, tile), 1)
        tri = (c <= r).astype(F32)
        cs = jnp.dot(tri, lf, precision=lax.Precision.HIGHEST, preferred_element_type=F32) + carry[...]
        cum_ref[...] = cs
        carry[...] = cs[tile - 1:tile, :]

    return pl.pallas_call(
        body, name=name, grid=(s_len // tile,),
        in_specs=[pl.BlockSpec((tile, LANES), lambda i: (i, 0)), _full((1, LANES))],
        out_specs=pl.BlockSpec((tile, LANES), lambda i: (i, 0)),
        out_shape=jax.ShapeDtypeStruct((s_len, LANES), F32),
        scratch_shapes=[pltpu.VMEM((1, LANES), F32)],
        compiler_params=_params("arbitrary"),
    )(fl, b)


def _cumsum_bwd(rs, cs, fl, b, tile, name):
    s_len = fl.shape[0]
    n_t = s_len // tile

    def body(rs_ref, cs_ref, fl_ref, b_ref, dfl_ref, db_ref, carry):
        @pl.when(pl.program_id(0) == 0)
        def _():
            carry[...] = jnp.zeros_like(carry)
            db_ref[...] = jnp.zeros_like(db_ref)

        r = lax.broadcasted_iota(jnp.int32, (tile, tile), 0)
        c = lax.broadcasted_iota(jnp.int32, (tile, tile), 1)
        tri = (c >= r).astype(F32)
        lane = lax.broadcasted_iota(jnp.int32, (1, LANES), 1)
        dc = jnp.zeros((tile, LANES), F32)
        for p in range(N_PAIRS):
            blk = rs_ref[:, LANES * p:LANES * (p + 1)] - cs_ref[:, LANES * p:LANES * (p + 1)]
            dc = jnp.where(lane == 2 * p, blk, dc)
            dc = jnp.where(lane == 2 * p + 1, pltpu.roll(blk, FOX_HEAD_DIM, axis=1), dc)
        dl = jnp.dot(tri, dc, precision=lax.Precision.HIGHEST, preferred_element_type=F32) + carry[...]
        carry[...] = dl[0:1, :]
        dfl = dl * _sigmoid(-(fl_ref[...] + b_ref[...]))
        dfl_ref[...] = dfl
        db_ref[...] += jnp.sum(dfl, axis=0, keepdims=True)

    rev = lambda i: (n_t - 1 - i, 0)
    return pl.pallas_call(
        body, name=name, grid=(n_t,),
        in_specs=[pl.BlockSpec((tile, FOX_WIDTH), rev), pl.BlockSpec((tile, FOX_WIDTH), rev), pl.BlockSpec((tile, LANES), rev),
                  _full((1, LANES))],
        out_specs=[pl.BlockSpec((tile, LANES), rev), _full((1, LANES))],
        out_shape=[jax.ShapeDtypeStruct((s_len, LANES), F32), jax.ShapeDtypeStruct((1, LANES), F32)],
        scratch_shapes=[pltpu.VMEM((1, LANES), F32)],
        compiler_params=_params("arbitrary"),
    )(rs, cs, fl, b)


SUBLANES = 8
CONV_ROWS = 64


def _phase_copies(src, dst, rows):
    for p in range(SUBLANES):
        dst[p] = src[pl.ds(p, rows), :]


def _phase_rows(extp_ref, off, r0, rows):
    p = off % SUBLANES
    return extp_ref[p, pl.ds(pl.multiple_of(r0 + (off - p), SUBLANES), rows), :]


def _conv_taps(w_ref, extp_ref, base, r0, rows, reverse):
    acc = None
    for k in range(CONV_WIDTH):
        off = base + ((CONV_WIDTH - 1 - k) if reverse else k)
        term = w_ref[k:k + 1, :] * _phase_rows(extp_ref, off, r0, rows)
        acc = term if acc is None else acc + term
    return acc


def _fold_rows(x):
    out = x[0:SUBLANES]
    for i in range(1, x.shape[0] // SUBLANES):
        out = out + x[i * SUBLANES:(i + 1) * SUBLANES]
    return out


def _conv_fwd(ag, w, cb, lg, lb, tile, name):
    s_len = ag.shape[0]
    c = CONV_CH
    rb = tile

    def body(ag_ref, w_ref, cb_ref, lg_ref, lb_ref, u_ref, ext, extp):
        @pl.when(pl.program_id(0) == 0)
        def _():
            ext[0:CONV_PAD, :] = jnp.zeros((CONV_PAD, c), F32)
            ext[tile + CONV_PAD:tile + CONV_PAD + SUBLANES, :] = jnp.zeros((SUBLANES, c), F32)

        ext[CONV_PAD:CONV_PAD + tile, :] = ag_ref[:, 0:c] * _sigmoid(ag_ref[:, c:2 * c])
        _phase_copies(ext, extp, tile + CONV_PAD)

        def block(b, carry):
            r0 = pl.multiple_of(b * rb, rb)
            u1 = _conv_taps(w_ref, extp, CONV_PAD - (CONV_WIDTH - 1), r0, rb, False) + cb_ref[...]
            mu = jnp.mean(u1, axis=-1, keepdims=True)
            xc = u1 - mu
            y = xc * lax.rsqrt(jnp.mean(xc * xc, axis=-1, keepdims=True) + EPS) * lg_ref[...] + lb_ref[...]
            u_ref[pl.ds(r0, rb), :] = (y * _sigmoid(y)).astype(BF16)
            return carry

        lax.fori_loop(0, tile // rb, block, 0)
        ext[0:CONV_PAD, :] = ext[tile:tile + CONV_PAD, :]

    return pl.pallas_call(
        body, name=name, grid=(s_len // tile,),
        in_specs=[pl.BlockSpec((tile, 2 * c), lambda i: (i, 0)), _full((CONV_PAD, c)), _full((1, c)), _full((1, c)),
                  _full((1, c))],
        out_specs=pl.BlockSpec((tile, c), lambda i: (i, 0)),
        out_shape=jax.ShapeDtypeStruct((s_len, c), BF16),
        scratch_shapes=[pltpu.VMEM((tile + CONV_PAD + SUBLANES, c), F32), pltpu.VMEM((SUBLANES, tile + CONV_PAD, c), F32)],
        compiler_params=_params("arbitrary"),
    )(ag, w, cb, lg, lb)


def _conv_bwd(ag, dcat, w, cb, lg, lb, tile, name):
    s_len = ag.shape[0]
    c = CONV_CH
    n_t = s_len // tile
    per = tile // CONV_PAD
    rb = min(CONV_ROWS, tile)

    def body(ag_ref, halo_ref, du_ref, w_ref, cb_ref, lg_ref, lb_ref, dag_ref, dw_ref, dv_ref, ext, ext2, extp, dwacc):
        i = pl.program_id(0)
        t = n_t - 1 - i

        @pl.when(i == 0)
        def _():
            ext2[tile:tile + CONV_PAD + SUBLANES, :] = jnp.zeros((CONV_PAD + SUBLANES, c), F32)
            ext[tile + CONV_PAD:tile + CONV_PAD + SUBLANES, :] = jnp.zeros((SUBLANES, c), F32)
            dwacc[...] = jnp.zeros_like(dwacc)
            dv_ref[...] = jnp.zeros_like(dv_ref)

        halo = halo_ref[:, 0:c] * _sigmoid(halo_ref[:, c:2 * c])
        ext[0:CONV_PAD, :] = jnp.where(t > 0, halo, 0.0)
        ext[CONV_PAD:CONV_PAD + tile, :] = ag_ref[:, 0:c] * _sigmoid(ag_ref[:, c:2 * c])
        _phase_copies(ext, extp, tile + CONV_PAD)
        base = CONV_PAD - (CONV_WIDTH - 1)

        def block1(b, carry):
            r0 = pl.multiple_of(b * rb, rb)
            u1 = _conv_taps(w_ref, extp, base, r0, rb, False) + cb_ref[...]
            mu = jnp.mean(u1, axis=-1, keepdims=True)
            xc = u1 - mu
            rs = lax.rsqrt(jnp.mean(xc * xc, axis=-1, keepdims=True) + EPS)
            xhat = xc * rs
            y = xhat * lg_ref[...] + lb_ref[...]
            sy = _sigmoid(y)
            dy = du_ref[pl.ds(r0, rb), :] * (sy * (1.0 + y * (1.0 - sy)))
            dxh = dy * lg_ref[...]
            du1 = rs * (dxh - jnp.mean(dxh, axis=-1, keepdims=True) - xhat * jnp.mean(dxh * xhat, axis=-1, keepdims=True))
            dv_ref[0:1, :] += jnp.sum(du1, axis=0, keepdims=True)
            dv_ref[1:2, :] += jnp.sum(dy * xhat, axis=0, keepdims=True)
            dv_ref[2:3, :] += jnp.sum(dy, axis=0, keepdims=True)
            for k in range(CONV_WIDTH):
                dwacc[k] += _fold_rows(du1 * _phase_rows(extp, base + k, r0, rb))
            ext2[pl.ds(r0, rb), :] = du1
            return carry

        lax.fori_loop(0, tile // rb, block1, 0)
        _phase_copies(ext2, extp, tile + CONV_PAD)

        def block2(b, carry):
            r0 = pl.multiple_of(b * rb, rb)
            du0 = _conv_taps(w_ref, extp, 0, r0, rb, True)
            a = ag_ref[pl.ds(r0, rb), 0:c]
            sg = _sigmoid(ag_ref[pl.ds(r0, rb), c:2 * c])
            dag_ref[pl.ds(r0, rb), 0:c] = (du0 * sg).astype(BF16)
            dag_ref[pl.ds(r0, rb), c:2 * c] = (du0 * a * sg * (1.0 - sg)).astype(BF16)
            return carry

        lax.fori_loop(0, tile // rb, block2, 0)
        ext2[tile:tile + CONV_PAD, :] = ext2[0:CONV_PAD, :]

        @pl.when(i == n_t - 1)
        def _():
            for k in range(CONV_WIDTH):
                dw_ref[k:k + 1, :] = jnp.sum(dwacc[k], axis=0, keepdims=True)
            dw_ref[CONV_WIDTH:CONV_PAD, :] = jnp.zeros((CONV_PAD - CONV_WIDTH, c), F32)

    rev = lambda i: (n_t - 1 - i, 0)
    return pl.pallas_call(
        body, name=name, grid=(n_t,),
        in_specs=[pl.BlockSpec((tile, 2 * c), rev),
                  pl.BlockSpec((CONV_PAD, 2 * c), lambda i: (jnp.maximum((n_t - 1 - i) * per - 1, 0), 0)),
                  pl.BlockSpec((tile, c), rev), _full((CONV_PAD, c)), _full((1, c)), _full((1, c)), _full((1, c))],
        out_specs=[pl.BlockSpec((tile, 2 * c), rev), _full((CONV_PAD, c)), _full((8, c))],
        out_shape=[jax.ShapeDtypeStruct((s_len, 2 * c), BF16), jax.ShapeDtypeStruct((CONV_PAD, c), F32),
                   jax.ShapeDtypeStruct((8, c), F32)],
        scratch_shapes=[pltpu.VMEM((tile + CONV_PAD + SUBLANES, c), F32), pltpu.VMEM((tile + CONV_PAD + SUBLANES, c), F32),
                        pltpu.VMEM((SUBLANES, tile + CONV_PAD, c), F32), pltpu.VMEM((CONV_PAD, SUBLANES, c), F32)],
        compiler_params=_params("arbitrary"),
    )(ag, ag, dcat, w, cb, lg, lb)


Q_SCALE = FOX_HEAD_DIM ** -0.5


def _head_col(x, lane, h):
    return jnp.sum(jnp.where(lane == h, x, 0.0), axis=1, keepdims=True)


def _split3(x):
    hi = x.astype(BF16).astype(F32)
    r = x - hi
    mid = r.astype(BF16).astype(F32)
    lo = (r - mid).astype(BF16).astype(F32)
    return hi, mid, lo


def _in_lanes(lane, lo, n):
    return (lane >= lo) & (lane < lo + n)


def _put3(lane, lo, parts, rest):
    return jnp.where(lane == lo, parts[0], jnp.where(lane == lo + 1, parts[1], jnp.where(lane == lo + 2, parts[2], rest)))


def _spare_lane(h):
    return FOX_HEAD_DIM if h % 2 == 0 else 0


def _shift_div(i, num, den):
    return i * (num // den) if num >= den else lax.shift_right_logical(i, (den // num).bit_length() - 1)


def _foxa_prep(qkv, cum, tile, name):
    s_len = qkv.shape[0]

    def body(q_ref, k_ref, v_ref, c_ref, qa_ref, ka_ref, va_ref):
        lane = lax.broadcasted_iota(jnp.int32, (1, LANES), 1)
        cum_t = c_ref[...]
        for h in range(FOX_HEADS):
            e = _spare_lane(h)
            head = ~_in_lanes(lane, e, FOX_HEAD_DIM)
            blk = slice(LANES * (h // 2), LANES * (h // 2) + LANES)
            out = slice(LANES * h, LANES * h + LANES)
            c3 = _split3(_head_col(cum_t, lane, h))
            ex_q = _put3(lane, e, c3, jnp.where(_in_lanes(lane, e + 3, 3), 1.0, 0.0))
            qa_ref[:, out] = jnp.where(head, q_ref[:, blk].astype(F32) * Q_SCALE, ex_q).astype(BF16)
            ones = jnp.where(_in_lanes(lane, e, 3) | _in_lanes(lane, e + 6, 3), 1.0, 0.0)
            ex_k = _put3(lane, e + 3, [-c for c in c3], ones)
            ka_ref[:, out] = jnp.where(head, k_ref[:, blk].astype(F32), ex_k).astype(BF16)
            ex_v = jnp.where(_in_lanes(lane, e, 3), 1.0, 0.0)
            va_ref[:, out] = jnp.where(head, v_ref[:, blk].astype(F32), ex_v).astype(BF16)

    col = lambda c: pl.BlockSpec((tile, FOX_WIDTH), lambda i: (i, c))
    wide = pl.BlockSpec((tile, 2 * FOX_WIDTH), lambda i: (i, 0))
    return pl.pallas_call(
        body, name=name, grid=(s_len // tile,),
        in_specs=[col(0), col(1), col(2), pl.BlockSpec((tile, LANES), lambda i: (i, 0))],
        out_specs=[wide, wide, wide], out_shape=[jax.ShapeDtypeStruct((s_len, 2 * FOX_WIDTH), BF16)] * 3,
        compiler_params=_params("parallel"),
    )(qkv, qkv, qkv, cum)


def _foxa_prep_bwd(qa, lse, o, dcat, tile, name):
    s_len = qa.shape[0]

    def body(qa_ref, lse_ref, o_ref, do_ref, qb_ref, doa_ref):
        lane = lax.broadcasted_iota(jnp.int32, (1, LANES), 1)
        for h in range(FOX_HEADS):
            e = _spare_lane(h)
            head = ~_in_lanes(lane, e, FOX_HEAD_DIM)
            blk = slice(LANES * (h // 2), LANES * (h // 2) + LANES)
            out = slice(LANES * h, LANES * h + LANES)
            do = do_ref[:, blk]
            delta = jnp.sum(jnp.where(head, do * o_ref[:, blk], 0.0), axis=1, keepdims=True)
            doa_ref[:, out] = _put3(lane, e, _split3(-delta), jnp.where(head, do, 0.0)).astype(BF16)
            at = LANES * (h // 2) + FOX_HEAD_DIM - e
            l3 = _split3(-lse_ref[:, at:at + 1])
            qb_ref[:, out] = _put3(lane, e + 6, l3, qa_ref[:, out].astype(F32)).astype(BF16)

    wide = pl.BlockSpec((tile, 2 * FOX_WIDTH), lambda i: (i, 0))
    half = pl.BlockSpec((tile, FOX_WIDTH), lambda i: (i, 0))
    return pl.pallas_call(
        body, name=name, grid=(s_len // tile,),
        in_specs=[wide, half, half, pl.BlockSpec((tile, FOX_WIDTH), lambda i: (i, 1))],
        out_specs=[wide, wide], out_shape=[jax.ShapeDtypeStruct((s_len, 2 * FOX_WIDTH), BF16)] * 2,
        compiler_params=_params("parallel"),
    )(qa, lse, o, dcat)


def _foxa_fwd(qa, ka, va, tq, tk, name, host=None):
    s_len = qa.shape[0]

    def body(q_ref, k_ref, v_ref, o_ref, lse_ref):
        i = pl.program_id(1)
        lane = lax.broadcasted_iota(jnp.int32, (1, LANES), 1)
        cols = [slice(LANES * hh, LANES * hh + LANES) for hh in range(2)]
        qh = [q_ref[:, c] for c in cols]

        def scores(j):
            off = pl.multiple_of(j * tk, tk)
            return [lax.dot_general(qh[hh], k_ref[pl.ds(off, tk), cols[hh]], NT_DIMS, preferred_element_type=F32)
                    for hh in range(2)]

        def update(j, s, m, acc, mask):
            off = pl.multiple_of(j * tk, tk)
            m_out, acc_out = [], []
            for hh in range(2):
                sh = s[hh] if mask is None else jnp.where(mask, s[hh], NEG_INF)
                m_new = jnp.maximum(m[hh], jnp.max(sh, axis=1, keepdims=True))
                pr = jnp.exp(sh - m_new).astype(BF16)
                acc_out.append(jnp.exp(m[hh] - m_new) * acc[hh]
                               + jnp.dot(pr, v_ref[pl.ds(off, tk), cols[hh]], preferred_element_type=F32))
                m_out.append(m_new)
            return m_out, acc_out

        def step(j, carry):
            s_next = scores(j + 1)
            m, acc = update(j, carry[0:2], carry[2:4], carry[4:6], None)
            return (*s_next, *m, *acc)

        n_full = _shift_div(i, tq, tk)
        n_part = max(tq // tk, 1)
        qi = lax.broadcasted_iota(jnp.int32, (tq, tk), 0) + i * tq
        ki = lax.broadcasted_iota(jnp.int32, (tq, tk), 1)
        carry = (*scores(0), *([jnp.full((tq, 1), NEG_INF, F32)] * 2), *([jnp.zeros((tq, LANES), F32)] * 2))
        carry = lax.fori_loop(0, n_full, step, carry)
        s, m, acc = list(carry[0:2]), list(carry[2:4]), list(carry[4:6])
        for jj in range(n_part):
            s_next = scores(n_full + jj + 1) if jj < n_part - 1 else None
            m, acc = update(n_full + jj, s, m, acc, ki + (n_full + jj) * tk <= qi)
            s = s_next
        res = []
        for hh in range(2):
            l = acc[hh][:, _spare_lane(hh):_spare_lane(hh) + 1]
            res.append((acc[hh] / l, m[hh] + jnp.log(l)))
        low = lane < FOX_HEAD_DIM
        o_ref[...] = jnp.where(low, res[0][0], res[1][0])
        lse_ref[...] = jnp.where(low, res[0][1], res[1][1])

    pair = pl.BlockSpec((s_len, 2 * LANES), lambda p, i: (0, p))
    out = pl.BlockSpec((tq, LANES), lambda p, i: (i, p))
    return _pcall(
        body, name=name, grid=(N_PAIRS, s_len // tq),
        in_specs=[pl.BlockSpec((tq, 2 * LANES), lambda p, i: (i, p)), pair, pair],
        out_specs=[out, out], out_shape=[jax.ShapeDtypeStruct((s_len, FOX_WIDTH), F32)] * 2,
        args=[qa, ka, va], sem=("parallel", "parallel"), host=host)


def _foxa_dq(qb, ka, va, doa, tq, tk, name, host=None):
    s_len = qb.shape[0]

    def body(q_ref, k_ref, v_ref, do_ref, dq_ref, rs_ref):
        i = pl.program_id(1)
        lane = lax.broadcasted_iota(jnp.int32, (1, LANES), 1)
        cols = [slice(LANES * hh, LANES * hh + LANES) for hh in range(2)]
        qh = [q_ref[:, c] for c in cols]
        doh = [do_ref[:, c] for c in cols]

        def update(j, acc, mask):
            off = pl.multiple_of(j * tk, tk)
            out = []
            for hh in range(2):
                kt = k_ref[pl.ds(off, tk), cols[hh]]
                pr = jnp.exp(lax.dot_general(qh[hh], kt, NT_DIMS, preferred_element_type=F32))
                if mask is not None:
                    pr = jnp.where(mask, pr, 0.0)
                ds = pr * lax.dot_general(doh[hh], v_ref[pl.ds(off, tk), cols[hh]], NT_DIMS, preferred_element_type=F32)
                out.append(acc[hh] + jnp.dot(ds.astype(BF16), kt, preferred_element_type=F32))
            return tuple(out)

        n_full = _shift_div(i, tq, tk)
        qi = lax.broadcasted_iota(jnp.int32, (tq, tk), 0) + i * tq
        ki = lax.broadcasted_iota(jnp.int32, (tq, tk), 1)
        acc = lax.fori_loop(0, n_full, lambda j, a: update(j, a, None), (jnp.zeros((tq, LANES), F32),) * 2)
        for jj in range(max(tq // tk, 1)):
            acc = update(n_full + jj, acc, ki + (n_full + jj) * tk <= qi)
        low = lane < FOX_HEAD_DIM
        dq_ref[...] = (jnp.where(low, acc[0], acc[1]) * Q_SCALE).astype(BF16)
        rs_ref[...] = jnp.where(low, acc[0][:, _spare_lane(0):_spare_lane(0) + 1], acc[1][:, _spare_lane(1):_spare_lane(1) + 1])

    pair = pl.BlockSpec((s_len, 2 * LANES), lambda p, i: (0, p))
    tile2 = pl.BlockSpec((tq, 2 * LANES), lambda p, i: (i, p))
    out = pl.BlockSpec((tq, LANES), lambda p, i: (i, p))
    return _pcall(
        body, name=name, grid=(N_PAIRS, s_len // tq), in_specs=[tile2, pair, pair, tile2], out_specs=[out, out],
        out_shape=[jax.ShapeDtypeStruct((s_len, FOX_WIDTH), BF16), jax.ShapeDtypeStruct((s_len, FOX_WIDTH), F32)],
        args=[qb, ka, va, doa], sem=("parallel", "parallel"), host=host)


def _foxa_dkv(qb, ka, va, doa, tq, tk, name, host=None):
    s_len = qb.shape[0]
    n_q = s_len // tq

    def body(k_ref, v_ref, q_ref, do_ref, dk_ref, dv_ref, cs_ref):
        j = pl.program_id(1)
        lane = lax.broadcasted_iota(jnp.int32, (1, LANES), 1)
        cols = [slice(LANES * hh, LANES * hh + LANES) for hh in range(2)]
        kh = [k_ref[:, c] for c in cols]
        vh = [v_ref[:, c] for c in cols]

        def update(i, acc, mask):
            off = pl.multiple_of(i * tq, tq)
            out = []
            for hh in range(2):
                qt = q_ref[pl.ds(off, tq), cols[hh]]
                dot = do_ref[pl.ds(off, tq), cols[hh]]
                pt = jnp.exp(lax.dot_general(kh[hh], qt, NT_DIMS, preferred_element_type=F32))
                if mask is not None:
                    pt = jnp.where(mask, pt, 0.0)
                dv = acc[2 * hh + 1] + jnp.dot(pt.astype(BF16), dot, preferred_element_type=F32)
                dst = pt * lax.dot_general(vh[hh], dot, NT_DIMS, preferred_element_type=F32)
                out += [acc[2 * hh] + jnp.dot(dst.astype(BF16), qt, preferred_element_type=F32), dv]
            return tuple(out)

        i0 = _shift_div(j, tk, tq)
        n_part = max(tk // tq, 1)
        ki = lax.broadcasted_iota(jnp.int32, (tk, tq), 0) + j * tk
        qi = lax.broadcasted_iota(jnp.int32, (tk, tq), 1)
        acc = (jnp.zeros((tk, LANES), F32),) * 4
        for ii in range(n_part):
            acc = update(i0 + ii, acc, ki <= qi + (i0 + ii) * tq)
        acc = lax.fori_loop(i0 + n_part, n_q, lambda i, a: update(i, a, None), acc)
        low = lane < FOX_HEAD_DIM
        dk_ref[...] = jnp.where(low, acc[0], acc[2]).astype(BF16)
        dv_ref[...] = jnp.where(low, acc[1], acc[3]).astype(BF16)
        cs_ref[...] = jnp.where(low, acc[0][:, _spare_lane(0) + 3:_spare_lane(0) + 4],
                                acc[2][:, _spare_lane(1) + 3:_spare_lane(1) + 4])

    pair = pl.BlockSpec((s_len, 2 * LANES), lambda p, j: (0, p))
    tile2 = pl.BlockSpec((tk, 2 * LANES), lambda p, j: (j, p))
    out = pl.BlockSpec((tk, LANES), lambda p, j: (j, p))
    return _pcall(
        body, name=name, grid=(N_PAIRS, s_len // tk), in_specs=[tile2, tile2, pair, pair], out_specs=[out, out, out],
        out_shape=[jax.ShapeDtypeStruct((s_len, FOX_WIDTH), BF16), jax.ShapeDtypeStruct((s_len, FOX_WIDTH), BF16),
                   jax.ShapeDtypeStruct((s_len, FOX_WIDTH), F32)],
        args=[ka, va, qb, doa], sem=("parallel", "parallel"), host=host)


def _mem_scores_t(q, kv, h):
    lo = h * MEM_HEAD_DIM
    st = lax.dot_general(kv[:, lo:lo + MEM_HEAD_DIM], q[:, lo:lo + MEM_HEAD_DIM], NT_DIMS,
                         preferred_element_type=F32) * (MEM_HEAD_DIM ** -0.5)
    e = jnp.exp(st - jnp.max(st, axis=0, keepdims=True))
    return e / jnp.sum(e, axis=0, keepdims=True)


def _memattn_fwd(q, kv, tile, name):
    s_len = q.shape[0]
    n_mem = kv.shape[0]

    def body(q_ref, kv_ref, o_ref):
        q = q_ref[...]
        kv = kv_ref[...]
        for h in range(MEM_HEADS):
            lo = h * MEM_HEAD_DIM
            pt = _mem_scores_t(q, kv, h).astype(BF16)
            vh = kv[:, MEM_INNER + lo:MEM_INNER + lo + MEM_HEAD_DIM]
            o_ref[:, lo:lo + MEM_HEAD_DIM] = lax.dot_general(pt, vh, TN_DIMS, preferred_element_type=F32).astype(BF16)

    return pl.pallas_call(
        body, name=name, grid=(s_len // tile,),
        in_specs=[pl.BlockSpec((tile, MEM_INNER), lambda i: (i, 0)), _full((n_mem, 2 * MEM_INNER))],
        out_specs=pl.BlockSpec((tile, MEM_INNER), lambda i: (i, 0)),
        out_shape=jax.ShapeDtypeStruct((s_len, MEM_INNER), BF16),
        compiler_params=_params("parallel"),
    )(q, kv)


def _memattn_bwd(q, kv, do, tile, name):
    s_len = q.shape[0]
    n_mem = kv.shape[0]
    scale = MEM_HEAD_DIM ** -0.5

    def body(q_ref, kv_ref, do_ref, dq_ref, dkv_ref):
        @pl.when(pl.program_id(0) == 0)
        def _():
            dkv_ref[...] = jnp.zeros_like(dkv_ref)

        q = q_ref[...]
        kv = kv_ref[...]
        do = do_ref[...]
        for h in range(MEM_HEADS):
            lo = h * MEM_HEAD_DIM
            qh = q[:, lo:lo + MEM_HEAD_DIM]
            kh = kv[:, lo:lo + MEM_HEAD_DIM]
            vh = kv[:, MEM_INNER + lo:MEM_INNER + lo + MEM_HEAD_DIM]
            doh = do[:, lo:lo + MEM_HEAD_DIM]
            pt = _mem_scores_t(q, kv, h)
            dkv_ref[:, MEM_INNER + lo:MEM_INNER + lo + MEM_HEAD_DIM] += jnp.dot(
                pt.astype(BF16), doh, preferred_element_type=F32)
            dpt = lax.dot_general(vh, doh, NT_DIMS, preferred_element_type=F32)
            dst = (pt * (dpt - jnp.sum(pt * dpt, axis=0, keepdims=True)) * scale).astype(BF16)
            dkv_ref[:, lo:lo + MEM_HEAD_DIM] += jnp.dot(dst, qh, preferred_element_type=F32)
            dq_ref[:, lo:lo + MEM_HEAD_DIM] = lax.dot_general(dst, kh, TN_DIMS, preferred_element_type=F32).astype(BF16)

    return pl.pallas_call(
        body, name=name, grid=(s_len // tile,),
        in_specs=[pl.BlockSpec((tile, MEM_INNER), lambda i: (i, 0)), _full((n_mem, 2 * MEM_INNER)),
                  pl.BlockSpec((tile, MEM_INNER), lambda i: (i, 0))],
        out_specs=[pl.BlockSpec((tile, MEM_INNER), lambda i: (i, 0)), _full((n_mem, 2 * MEM_INNER))],
        out_shape=[jax.ShapeDtypeStruct((s_len, MEM_INNER), BF16), jax.ShapeDtypeStruct((n_mem, 2 * MEM_INNER), F32)],
        compiler_params=_params("arbitrary"),
    )(q, kv, do)


def _loss_head(y, target, tile, name):
    s_len, d = y.shape

    def body(y_ref, t_ref, dy_ref, l_ref):
        @pl.when(pl.program_id(0) == 0)
        def _():
            l_ref[...] = jnp.zeros_like(l_ref)

        err = y_ref[...] - t_ref[...]
        dy_ref[...] = err * (1.0 / d)
        l_ref[...] += jnp.sum(err * err, axis=0, keepdims=True) * (0.5 / d)

    row = lambda i: (i, 0)
    return pl.pallas_call(
        body, name=name, grid=(s_len // tile,),
        in_specs=[pl.BlockSpec((tile, d), row), pl.BlockSpec((tile, d), row)],
        out_specs=[pl.BlockSpec((tile, d), row), _full((1, d))],
        out_shape=[jax.ShapeDtypeStruct((s_len, d), F32), jax.ShapeDtypeStruct((1, d), F32)],
        compiler_params=_params("arbitrary"),
    )(y, target)


def _attn_tile(s_len):
    return min(256, s_len // 2)


REST = ("w_out", "w_mq", "w_mk", "w_mv", "w_mo", "w_up", "w_down")
REST_DQ = ("w_up", "w_down")
REST_DKV = ("w_out", "w_mq", "w_mk", "w_mv", "w_mo")
SHARD_AXIS = {"w_in": 1, "w_out": 0, "w_mq": 0, "w_mk": 0, "w_mv": 0, "w_mo": 1, "w_up": 1, "w_down": 0}


def _full_from_shards(sh, axis):
    n, r, c = sh.shape
    if axis == 0:
        return sh.reshape(n * r, c)
    return sh.transpose(1, 0, 2).reshape(r, n * c)


def _rest_weights(lands):
    w = {n: _full_from_shards(sh, SHARD_AXIS[n]) for n, sh in zip(REST, lands)}
    w["w_mkv"] = jnp.concatenate([w.pop("w_mk"), w.pop("w_mv")], axis=1)
    return w


def _w_in_cat(land):
    return jnp.pad(_full_from_shards(land, 1), ((0, 0), (0, IN_CAT - IN_COLS)))


def _layer_fwd(x0, mem, w, l, rest_src=None, next_src=None):
    s_len = x0.shape[0]
    tile = min(512, s_len)
    tile_ff = min(256, s_len)
    ta = _attn_tile(s_len)
    ident = lambda z: z
    sv = {"x0": x0}

    h1, ag, qkv, fl = _rms_matmul(
        x0, w["norm_mix_pre"], w["w_in_cat"],
        [(0, 2 * CONV_CH, [(F32, ident)]), (2 * CONV_CH, IN_MAIN, [(BF16, ident)]), (IN_MAIN, IN_CAT, [(F32, ident)])],
        tile, f"mix_in_{l}")
    cum = _cumsum_fwd(fl, w["b_forget"], tile, f"cumsum_fwd_{l}")
    u3 = _conv_fwd(ag, w["conv_w"], w["conv_b"], w["conv_ln_g"], w["conv_ln_b"], tile, f"conv_fwd_{l}")
    qa, ka, va = _foxa_prep(qkv, cum, tile, f"fox_prep_{l}")
    next_land = None
    if rest_src is None:
        o, lse = _foxa_fwd(qa, ka, va, ta, 2 * ta, f"fox_fwd_{l}")
    else:
        srcs = list(rest_src) + list(next_src or [])
        (o, lse), lands = _foxa_fwd(qa, ka, va, ta, 2 * ta, f"fox_fwd_{l}", host=(srcs, [True] * len(srcs)))
        w = {**w, **_rest_weights(lands[:len(REST)])}
        next_land = lands[len(REST):]
    cat = jnp.concatenate([u3, o.astype(BF16)], axis=1)
    y1, x1 = _matmul_resnorm(cat, w["w_out"], x0, w["norm_mix_post"], tile, f"mix_out_{l}")
    sv.update(h1=h1, ag=ag, fl=fl, qa=qa, ka=ka, va=va, o=o, lse=lse, cat=cat, y1=y1, x1=x1)

    h2, qm = _rms_matmul(x1, w["norm_mem_pre"], w["w_mq"], [(0, MEM_INNER, [(BF16, ident)])], tile, f"mem_q_{l}")
    mem_n, kv = _rms_matmul(mem, w["norm_memkv"], w["w_mkv"], [(0, 2 * MEM_INNER, [(BF16, ident)])],
                            mem.shape[0], f"mem_kv_{l}")
    om = _memattn_fwd(qm, kv, tile, f"mem_attn_fwd_{l}")
    y2, x2 = _matmul_resnorm(om, w["w_mo"], x1, w["norm_mem_post"], tile, f"mem_out_{l}")
    sv.update(h2=h2, qm=qm, mem_n=mem_n, kv=kv, om=om, y2=y2, x2=x2)

    relu2 = lambda z: jnp.square(jnp.maximum(z, 0.0))
    h3, pre, hid = _rms_matmul(x2, w["norm_mlp_pre"], w["w_up"], [(0, D_FF, [(BF16, ident), (BF16, relu2)])], tile_ff,
                               f"mlp_up_{l}")
    y3, x3 = _matmul_resnorm(hid, w["w_down"], x2, w["norm_mlp_post"], tile_ff, f"mlp_down_{l}")
    sv.update(h3=h3, pre=pre, hid=hid, y3=y3)
    return x3, sv, w, next_land


def _layer_bwd(dx3, mem, w, sv, l, scatter_rest=False, dkv_src=None, scatter_w_in=False):
    s_len = dx3.shape[0]
    tile = min(512, s_len)
    tile_ff = min(256, s_len)
    ta = _attn_tile(s_len)
    tk = min(512, s_len)
    n_mem = mem.shape[0]
    g = {}

    dy3, dpre, g["norm_mlp_post"] = _resnorm_bwd_mm(dx3, sv["y3"], w["norm_mlp_post"], w["w_down"], tile_ff,
                                                    f"mlp_down_bwd_{l}", BF16, pre=sv["pre"])
    g["w_down"] = _matmul_tn_shards(sv["hid"], dy3, 0, tk, f"dw_down_{l}")
    dx2, g["norm_mlp_pre"] = _mm_prenorm_bwd(dpre, w["w_up"], sv["x2"], w["norm_mlp_pre"], dx3, tile_ff,
                                             f"mlp_up_bwd_{l}")
    g["w_up"] = _matmul_tn_shards(sv["h3"], dpre, 1, tk, f"dw_up_{l}")

    dy2, dom, g["norm_mem_post"] = _resnorm_bwd_mm(dx2, sv["y2"], w["norm_mem_post"], w["w_mo"], tile,
                                                   f"mem_out_bwd_{l}", BF16)
    g["w_mo"] = _matmul_tn_shards(sv["om"], dy2, 1, tk, f"dw_mo_{l}")
    dqm, dkv = _memattn_bwd(sv["qm"], sv["kv"], dom, tile, f"mem_attn_bwd_{l}")
    dkvb = dkv.astype(BF16)
    g["w_mq"] = _matmul_tn_shards(sv["h2"], dqm, 0, tk, f"dw_mq_{l}")
    dx1, g["norm_mem_pre"] = _mm_prenorm_bwd(dqm, w["w_mq"], sv["x1"], w["norm_mem_pre"], dx2, tile, f"mem_q_bwd_{l}")
    _, g["norm_memkv"] = _mm_prenorm_bwd(dkvb, w["w_mkv"], mem, w["norm_memkv"], None, n_mem, f"mem_kv_bwd_{l}")
    g["w_mk"] = _matmul_tn_shards(sv["mem_n"], dkvb[:, :MEM_INNER], 0, n_mem, f"dw_mk_{l}")
    g["w_mv"] = _matmul_tn_shards(sv["mem_n"], dkvb[:, MEM_INNER:], 0, n_mem, f"dw_mv_{l}")

    dy1, dcat, g["norm_mix_post"] = _resnorm_bwd_mm(dx1, sv["y1"], w["norm_mix_post"], w["w_out"], tile,
                                                    f"mix_out_bwd_{l}", F32)
    g["w_out"] = _matmul_tn_shards(sv["cat"], dy1, 0, tk, f"dw_out_{l}")
    qb, doa = _foxa_prep_bwd(sv["qa"], sv["lse"], sv["o"], dcat, tile, f"fox_prep_bwd_{l}")
    rest_land, dkv_land = None, None
    if scatter_rest:
        with_dq = [g[n] for n in REST_DQ]
        (dq, rs), land_dq = _foxa_dq(qb, sv["ka"], sv["va"], doa, ta, 2 * ta, f"fox_dq_{l}",
                                     host=(with_dq, [False] * len(with_dq)))
        with_dkv = [g[n] for n in REST_DKV] + list(dkv_src or [])
        (dk, dv, cs), land_dkv = _foxa_dkv(qb, sv["ka"], sv["va"], doa, 2 * ta, ta, f"fox_dkv_{l}",
                                           host=(with_dkv, [False] * len(with_dkv)))
        by_name = dict(zip(REST_DQ + REST_DKV, land_dq + land_dkv))
        rest_land = [by_name[n] for n in REST]
        dkv_land = land_dkv[len(REST_DKV):]
    else:
        dq, rs = _foxa_dq(qb, sv["ka"], sv["va"], doa, ta, 2 * ta, f"fox_dq_{l}")
        dk, dv, cs = _foxa_dkv(qb, sv["ka"], sv["va"], doa, 2 * ta, ta, f"fox_dkv_{l}")
    dfl, db = _cumsum_bwd(rs, cs, sv["fl"], w["b_forget"], tile, f"cumsum_bwd_{l}")
    g["b_forget"] = db[:, :FOX_HEADS]
    dag, dconv_w, dconv_v = _conv_bwd(sv["ag"], dcat, w["conv_w"], w["conv_b"], w["conv_ln_g"], w["conv_ln_b"], tile,
                                      f"conv_bwd_{l}")
    g["conv_w"] = dconv_w[:CONV_WIDTH]
    g["conv_b"], g["conv_ln_g"], g["conv_ln_b"] = dconv_v[0:1], dconv_v[1:2], dconv_v[2:3]
    dz = jnp.concatenate([dag, dq, dk, dv, dfl.astype(BF16)], axis=1)
    dw_in = _matmul_tn(sv["h1"], dz, 512, tk, f"dw_in_{l}")[:, :IN_COLS]
    g["w_in"] = dw_in.reshape(D_MODEL, N_DEV, IN_COLS // N_DEV).transpose(1, 0, 2).astype(BF16)
    res = _mm_prenorm_bwd(dz, w["w_in_cat"], sv["x0"], w["norm_mix_pre"], dx1, tile, f"mix_in_bwd_{l}",
                          host=([g["w_in"]], [False]) if scatter_w_in else None)
    (dx0, g["norm_mix_pre"]), w_in_land = res if scatter_w_in else (res, None)
    return dx0, g, rest_land, dkv_land, w_in_land


def _sum_blocks(a, name):
    n, rows, cols = a.shape

    def body(a_ref, o_ref):
        acc = a_ref[0]
        for j in range(1, n):
            acc = acc + a_ref[j]
        o_ref[...] = acc

    return pl.pallas_call(
        body, name=name, in_specs=[_full((n, rows, cols))], out_specs=_full((rows, cols)),
        out_shape=jax.ShapeDtypeStruct((rows, cols), F32), grid=(1,),
    )(a)


def _adamw(gparts, w, m, v, tile, name):
    n_l, rows, cols = w.shape
    n = gparts[0].shape[0]
    n_t = rows // tile
    c1 = 1.0 - ADAM_B1
    c2 = 1.0 - ADAM_B2
    bc1 = 1.0 - ADAM_B1 ** ADAM_STEP
    bc2 = 1.0 - ADAM_B2 ** ADAM_STEP

    def body(*refs):
        gp_refs, (w_ref, m_ref, v_ref, g_ref, d_ref, mo_ref, vo_ref) = refs[:n_l], refs[n_l:]
        layer = pl.program_id(0)
        g = None
        for l, gp_ref in enumerate(gp_refs):
            gl = gp_ref[0].astype(F32)
            for j in range(1, n):
                gl = gl + gp_ref[j].astype(F32)
            g = gl if g is None else jnp.where(layer == l, gl, g)
        g_ref[...] = g
        m_new = ADAM_B1 * m_ref[...] + c1 * g
        v_new = ADAM_B2 * v_ref[...] + c2 * (g * g)
        mo_ref[...] = m_new
        vo_ref[...] = v_new
        d_ref[...] = -ADAM_LR * ((m_new / bc1) / (jnp.sqrt(v_new / bc2) + ADAM_EPS) + ADAM_WD * w_ref[...])

    def gp_spec(l):
        return pl.BlockSpec((n, tile, cols), lambda L, i: (0, jnp.where(L < l, 0, jnp.where(L > l, n_t - 1, i)), 0))

    spec = pl.BlockSpec((None, tile, cols), lambda L, i: (L, i, 0))
    return pl.pallas_call(
        body, name=name, grid=(n_l, n_t),
        in_specs=[gp_spec(l) for l in range(n_l)] + [spec, spec, spec],
        out_specs=[spec] * 4, out_shape=[jax.ShapeDtypeStruct((n_l, rows, cols), F32)] * 4,
        compiler_params=_params("arbitrary", "arbitrary"),
    )(*gparts, w, m, v)


def _pack_rows(parts, total_rows):
    flat = [p.reshape(-1, D_MODEL) for p in parts]
    used = sum(f.shape[0] for f in flat)
    if total_rows > used:
        flat.append(jnp.zeros((total_rows - used, D_MODEL), flat[0].dtype))
    return jnp.concatenate(flat, axis=0)


def kernel(x, mem, norm_mix_pre, norm_mix_post, w_in, b_forget, conv_w, conv_b, conv_ln_g, conv_ln_b, w_out, norm_mem_pre, norm_mem_post, norm_memkv, w_mq, w_mk, w_mv, w_mo, norm_mlp_pre, norm_mlp_post, w_up, w_down, loss_target, m_norm_mix_pre, m_norm_mix_post, m_w_in, m_b_forget, m_conv_w, m_conv_b, m_conv_ln_g, m_conv_ln_b, m_w_out, m_norm_mem_pre, m_norm_mem_post, m_norm_memkv, m_w_mq, m_w_mk, m_w_mv, m_w_mo, m_norm_mlp_pre, m_norm_mlp_post, m_w_up, m_w_down, v_norm_mix_pre, v_norm_mix_post, v_w_in, v_b_forget, v_conv_w, v_conv_b, v_conv_ln_g, v_conv_ln_b, v_w_out, v_norm_mem_pre, v_norm_mem_post, v_norm_memkv, v_w_mq, v_w_mk, v_w_mv, v_w_mo, v_norm_mlp_pre, v_norm_mlp_post, v_w_up, v_w_down):
    p = dict(locals())
    names = ("norm_mix_pre", "norm_mix_post", "w_in", "b_forget", "conv_w", "conv_b", "conv_ln_g", "conv_ln_b", "w_out",
             "norm_mem_pre", "norm_mem_post", "norm_memkv", "w_mq", "w_mk", "w_mv", "w_mo", "norm_mlp_pre",
             "norm_mlp_post", "w_up", "w_down")
    me = 4 * lax.axis_index("x") + 2 * lax.axis_index("y") + lax.axis_index("c")
    conv_cols = conv_w.shape[2]
    s_len = x.shape[1]
    bf = {n: p[n].astype(BF16) for n in BIG}

    conv_pack = _pack_rows([jnp.pad(conv_w, ((0, 0), (0, CONV_PAD - CONV_WIDTH), (0, 0)))], 8)
    win_land, conv_land = _exchange([bf["w_in"][0], conv_pack], [True, True], "gather_first")
    conv_rows = DEPTH * CONV_PAD * conv_cols // D_MODEL
    conv_full = conv_land[:, :conv_rows].reshape(N_DEV, DEPTH, CONV_PAD, conv_cols)
    conv_full = conv_full.transpose(1, 2, 0, 3).reshape(DEPTH, CONV_PAD, N_DEV * conv_cols)
    b_forget_pad = jnp.pad(b_forget, ((0, 0), (0, LANES - FOX_HEADS)))

    def first_weights(l, land):
        w = {"w_in_cat": _w_in_cat(land), "conv_w": conv_full[l], "b_forget": b_forget_pad[l:l + 1]}
        for n in VEC[:-1]:
            w[n] = p[n][l:l + 1]
        return w

    h, sv0, w0, win1_land = _layer_fwd(x[0], mem[0], first_weights(0, win_land), 0,
                                       rest_src=[bf[n][0] for n in REST], next_src=[bf["w_in"][1]])
    h, sv1, w1, _ = _layer_fwd(h, mem[0], first_weights(1, win1_land[0]), 1, rest_src=[bf[n][1] for n in REST])
    dh, loss_row = _loss_head(h, loss_target[0], min(512, s_len), "loss_head")
    dh, g1, rest_g1, _, _ = _layer_bwd(dh, mem[0], w1, sv1, 1, scatter_rest=True)
    grad_x, g0, rest_g0, win1_g, win0_g = _layer_bwd(dh, mem[0], w0, sv0, 0, scatter_rest=True, dkv_src=[g1["w_in"]],
                                                     scatter_w_in=True)

    def small_rows(get):
        rows = [jnp.concatenate([get(n) for n in VEC_1024], axis=0),
                jnp.concatenate([get(n) for n in VEC_512], axis=0).reshape(len(VEC_512), D_MODEL),
                jnp.pad(get("b_forget").reshape(1, -1), ((0, 0), (0, D_MODEL - DEPTH * FOX_HEADS)))]
        return jnp.concatenate(rows, axis=0)

    def tap_rows(conv):
        return jnp.pad(conv.reshape(1, -1), ((0, 0), (0, 4 * D_MODEL - conv.size))).reshape(4, D_MODEL)

    n_vec_rows = DEPTH * len(VEC_1024) + len(VEC_512) + 1
    part = small_rows(lambda n: jnp.concatenate([g0[n], g1[n]], axis=0))
    conv_part = jnp.stack([g0["conv_w"], g1["conv_w"]]).reshape(CONV_WIDTH, D_MODEL)
    n_part = 1 + n_vec_rows + CONV_WIDTH
    pad_rows = -n_part % 8
    small_land = _exchange([jnp.concatenate([loss_row, part, conv_part, jnp.zeros((pad_rows, D_MODEL), F32)], axis=0)],
                           [True], "gather_small")[0]
    total = _sum_blocks(small_land, "sum_small")
    loss = jnp.sum(total[0])
    conv_g = total[1 + n_vec_rows:n_part].reshape(DEPTH, CONV_WIDTH, CONV_CH)
    conv_g = lax.dynamic_slice_in_dim(conv_g, me * conv_cols, conv_cols, axis=2)
    fill = jnp.zeros((SMALL_ROWS - n_vec_rows - 4, D_MODEL), F32)

    def small_pack(vec_rows, conv):
        return jnp.concatenate([vec_rows, tap_rows(conv), fill], axis=0)

    small_out = _adamw([small_pack(total[1:1 + n_vec_rows], conv_g)[None]],
                       *[small_pack(small_rows(lambda n: p[pre + n]), p[pre + "conv_w"])[None] for pre in ("", "m_", "v_")],
                       SMALL_ROWS, "adamw_small")

    def unpack_small(buf):
        out = {}
        for k, n in enumerate(VEC_1024):
            out[n] = buf[DEPTH * k:DEPTH * (k + 1)]
        at = DEPTH * len(VEC_1024)
        for k, n in enumerate(VEC_512):
            out[n] = buf[at + k].reshape(DEPTH, CONV_CH)
        at += len(VEC_512)
        out["b_forget"] = buf[at, :DEPTH * FOX_HEADS].reshape(DEPTH, FOX_HEADS)
        out["conv_w"] = buf[at + 1:at + 5].reshape(-1)[:DEPTH * CONV_WIDTH * conv_cols].reshape(DEPTH, CONV_WIDTH, conv_cols)
        return out

    landed = {"w_in": [win0_g[0], win1_g[0]]}
    for i, n in enumerate(REST):
        landed[n] = [rest_g0[i], rest_g1[i]]
    big_out = {n: _adamw(landed[n], p[n], p["m_" + n], p["v_" + n], ADAMW_TILE[n], f"adamw_{n}") for n in BIG}

    result = [loss, grad_x[None]]
    for k in range(4):
        smalls = unpack_small(small_out[k][0])
        result += [big_out[n][k] if n in big_out else smalls[n] for n in names]
    return tuple(result)
```

```python
import functools

import jax
import jax.numpy as jnp
from jax import lax
from jax.experimental import pallas as pl
from jax.experimental.pallas import tpu as pltpu

F32 = jnp.float32
BF16 = jnp.bfloat16

N_DEV = 8
DEPTH = 2
D_MODEL = 1024
CONV_CH = 512
CONV_WIDTH = 31
CONV_PAD = 32
FOX_HEADS = 8
FOX_HEAD_DIM = 64
FOX_WIDTH = 512
N_PAIRS = 4
MEM_HEADS = 4
MEM_HEAD_DIM = 128
MEM_INNER = 512
D_FF = 4096
IN_MAIN = 2560
IN_COLS = 2568
IN_CAT = IN_MAIN + 128
LANES = 128
EPS = 1e-6
NEG_INF = -1e30

ADAM_LR = 0.001
ADAM_B1 = 0.9
ADAM_B2 = 0.999
ADAM_EPS = 1e-08
ADAM_WD = 0.01
ADAM_STEP = 10

NT_DIMS = (((1,), (1,)), ((), ()))
TN_DIMS = (((0,), (0,)), ((), ()))

BIG = ("w_in", "w_out", "w_mq", "w_mk", "w_mv", "w_mo", "w_up", "w_down")
ADAMW_TILE = {"w_in": 256, "w_out": 128, "w_mq": 128, "w_mk": 128, "w_mv": 128, "w_mo": 512, "w_up": 256, "w_down": 128}

VEC_1024 = ("norm_mix_pre", "norm_mix_post", "norm_mem_pre", "norm_mem_post", "norm_memkv", "norm_mlp_pre", "norm_mlp_post")
VEC_512 = ("conv_b", "conv_ln_g", "conv_ln_b")
VEC = VEC_1024 + VEC_512 + ("b_forget",)
SMALL_ROWS = 32


def _sigmoid(x):
    return 1.0 / (1.0 + jnp.exp(-x))


def _rms(x, g):
    r = lax.rsqrt(jnp.mean(x * x, axis=-1, keepdims=True) + EPS)
    return x * r * g


def _rms_bwd(x, g, dh):
    r = lax.rsqrt(jnp.mean(x * x, axis=-1, keepdims=True) + EPS)
    gh = dh * g
    c = jnp.mean(gh * x, axis=-1, keepdims=True)
    dx = r * gh - x * (r * r * r * c)
    dg = jnp.sum(dh * (x * r), axis=0, keepdims=True)
    return dx, dg


def _full(shape):
    nd = len(shape)
    return pl.BlockSpec(shape, lambda *_: (0,) * nd)


def _params(*sem):
    return pltpu.CompilerParams(dimension_semantics=sem)


def _exchange_copies(src_refs, out_refs, same, send_sems, recv_sems, local_sems, with_recvs):
    x, y, c = lax.axis_index("x"), lax.axis_index("y"), lax.axis_index("c")
    me = 4 * x + 2 * y + c
    local, sends, recvs = [], [], []
    for a, (s_ref, o_ref) in enumerate(zip(src_refs, out_refs)):
        def mine(idx, s_ref=s_ref, whole=same[a]):
            return s_ref if whole else s_ref.at[idx]

        local.append(pltpu.make_async_copy(mine(me), o_ref.at[me], local_sems.at[a]))
        for k in range(1, N_DEV):
            px = 1 - x if k & 4 else x
            py = 1 - y if k & 2 else y
            pc = 1 - c if k & 1 else c
            peer = 4 * px + 2 * py + pc
            sem = a * (N_DEV - 1) + k - 1
            common = dict(send_sem=send_sems.at[sem], recv_sem=recv_sems.at[sem], device_id=(px, py, pc),
                          device_id_type=pl.DeviceIdType.MESH)
            sends.append(pltpu.make_async_remote_copy(src_ref=mine(peer), dst_ref=o_ref.at[me], **common))
            if with_recvs:
                recvs.append(pltpu.make_async_remote_copy(src_ref=mine(peer), dst_ref=o_ref.at[peer], **common))
    return local, sends, recvs


def _gather_copies(src_refs, out_refs, send_sems, recv_sems, local_sems, phase):
    x, y, c = lax.axis_index("x"), lax.axis_index("y"), lax.axis_index("c")
    sibling = (x, y, 1 - c)
    chips = [(1 - x, y), (x, 1 - y), (1 - x, 1 - y)]

    def idx(px, py, pc):
        return 4 * px + 2 * py + pc

    local, first, arrive, passed, final = [], [], [], [], []
    for a, (s_ref, o_ref) in enumerate(zip(src_refs, out_refs)):
        def cp(k, src, block, to, a=a, o_ref=o_ref):
            sem = a * (N_DEV - 1) + k
            return pltpu.make_async_remote_copy(src_ref=src, dst_ref=o_ref.at[block], send_sem=send_sems.at[sem],
                                                recv_sem=recv_sems.at[sem], device_id=to, device_id_type=pl.DeviceIdType.MESH)

        me = idx(x, y, c)
        if phase != 1:
            local.append(pltpu.make_async_copy(s_ref, o_ref.at[me], local_sems.at[a]))
            first.append(cp(0, s_ref, me, sibling))
        if phase == 2:
            final.append(cp(0, s_ref, idx(x, y, 1 - c), sibling))
        for j, chip in enumerate(chips):
            theirs = idx(*chip, c)
            if phase != 1:
                first.append(cp(1 + j, s_ref, me, (*chip, c)))
            if phase == 1:
                arrive.append(cp(1 + j, s_ref, theirs, (*chip, c)))
            if phase != 0:
                passed.append(cp(4 + j, o_ref.at[theirs], theirs, sibling))
            if phase == 2:
                final.append(cp(4 + j, s_ref, idx(*chip, 1 - c), sibling))
    return local, first, arrive, passed, final


def _pcall(body, *, name, grid, in_specs, out_specs, out_shape, args, scratch_shapes=(), sem=(), host=None):
    if host is None:
        return pl.pallas_call(body, name=name, grid=grid, in_specs=in_specs, out_specs=out_specs, out_shape=out_shape,
                              scratch_shapes=list(scratch_shapes), compiler_params=_params(*sem))(*args)
    srcs, same = host
    n_in, n_out, n_scr, n_h = len(in_specs), len(out_specs), len(scratch_shapes), len(srcs)
    hbm = pl.BlockSpec(memory_space=pltpu.HBM)
    lands = [jax.ShapeDtypeStruct((N_DEV,) + (s.shape if whole else s.shape[1:]), s.dtype) for s, whole in zip(srcs, same)]

    def wrapped(*refs):
        ins, src_refs = refs[:n_in], refs[n_in:n_in + n_h]
        outs = refs[n_in + n_h:n_in + n_h + n_out]
        land_refs = refs[n_in + n_h + n_out:n_in + 2 * n_h + n_out]
        scr = refs[n_in + 2 * n_h + n_out:n_in + 2 * n_h + n_out + n_scr]
        sems = refs[n_in + 2 * n_h + n_out + n_scr:]
        ids = [pl.program_id(d) for d in range(len(grid))]
        first = functools.reduce(jnp.logical_and, [i == 0 for i in ids])
        last = functools.reduce(jnp.logical_and, [i == n - 1 for i, n in zip(ids, grid)])
        later = functools.reduce(jnp.logical_and, [ids[0] == (3 * grid[0]) // 4] + [i == 0 for i in ids[1:]])

        if all(same):
            @pl.when(first)
            def _():
                local, sends, _, _, _ = _gather_copies(src_refs, land_refs, *sems, 0)
                for cp in local + sends:
                    cp.start()

            body(*ins, *outs, *scr)

            @pl.when(later)
            def _():
                _, _, arrive, passed, _ = _gather_copies(src_refs, land_refs, *sems, 1)
                for cp in arrive:
                    cp.wait_recv()
                for cp in passed:
                    cp.start()

            @pl.when(last)
            def _():
                local, sends, _, passed, final = _gather_copies(src_refs, land_refs, *sems, 2)
                for cp in final:
                    cp.wait_recv()
                for cp in sends + passed:
                    cp.wait_send()
                for cp in local:
                    cp.wait()
        else:
            @pl.when(first)
            def _():
                local, sends, _ = _exchange_copies(src_refs, land_refs, same, *sems, False)
                for cp in local + sends:
                    cp.start()

            body(*ins, *outs, *scr)

            @pl.when(last)
            def _():
                local, sends, recvs = _exchange_copies(src_refs, land_refs, same, *sems, True)
                for cp in recvs:
                    cp.wait_recv()
                for cp in sends:
                    cp.wait_send()
                for cp in local:
                    cp.wait()

    n_sem = n_h * (N_DEV - 1)
    res = pl.pallas_call(
        wrapped, name=name, grid=grid, in_specs=list(in_specs) + [hbm] * n_h, out_specs=list(out_specs) + [hbm] * n_h,
        out_shape=list(out_shape) + lands,
        scratch_shapes=list(scratch_shapes) + [pltpu.SemaphoreType.DMA((n_sem,)), pltpu.SemaphoreType.DMA((n_sem,)),
                                               pltpu.SemaphoreType.DMA((n_h,))],
        compiler_params=_params(*(("arbitrary",) * len(grid))),
    )(*args, *srcs)
    return list(res[:n_out]), list(res[n_out:])


def _exchange(srcs, same, name):
    def body():
        pass

    return _pcall(body, name=name, grid=(1,), in_specs=[], out_specs=[], out_shape=[], args=[], host=(srcs, same))[1]


def _rms_matmul(x, g, w, segs, tile, name, host=None):
    s_len, d = x.shape
    n = w.shape[1]
    chunk = 512

    def body(x_ref, g_ref, w_ref, h_ref, *outs):
        h = _rms(x_ref[...], g_ref[...]).astype(BF16)
        h_ref[...] = h
        oi = 0
        for c0, c1, fns in segs:
            for a in range(c0, c1, chunk):
                b = min(a + chunk, c1)
                z = jnp.dot(h, w_ref[:, a:b], preferred_element_type=F32)
                for k, (dt, fn) in enumerate(fns):
                    outs[oi + k][:, a - c0:b - c0] = fn(z).astype(dt)
            oi += len(fns)

    out_shape = [jax.ShapeDtypeStruct((s_len, d), BF16)]
    out_specs = [pl.BlockSpec((tile, d), lambda i: (i, 0))]
    for c0, c1, fns in segs:
        for dt, _ in fns:
            out_shape.append(jax.ShapeDtypeStruct((s_len, c1 - c0), dt))
            out_specs.append(pl.BlockSpec((tile, c1 - c0), lambda i: (i, 0)))
    return _pcall(
        body, name=name, grid=(s_len // tile,),
        in_specs=[pl.BlockSpec((tile, d), lambda i: (i, 0)), _full((1, d)), _full((d, n))],
        out_specs=out_specs, out_shape=out_shape, args=[x, g, w], sem=("parallel",), host=host)


def _matmul_resnorm(a, w, x, g, tile, name):
    s_len, k = a.shape
    d = w.shape[1]

    def body(a_ref, w_ref, x_ref, g_ref, y_ref, xo_ref):
        y = jnp.dot(a_ref[...], w_ref[...], preferred_element_type=F32)
        y_ref[...] = y
        xo_ref[...] = x_ref[...] + _rms(y, g_ref[...])

    row = lambda i: (i, 0)
    return pl.pallas_call(
        body, name=name, grid=(s_len // tile,),
        in_specs=[pl.BlockSpec((tile, k), row), _full((k, d)), pl.BlockSpec((tile, d), row), _full((1, d))],
        out_specs=[pl.BlockSpec((tile, d), row), pl.BlockSpec((tile, d), row)],
        out_shape=[jax.ShapeDtypeStruct((s_len, d), F32), jax.ShapeDtypeStruct((s_len, d), F32)],
        compiler_params=_params("parallel"),
    )(a, w, x, g)


def _resnorm_bwd_mm(dx, y, g, w, tile, name, out_dtype, pre=None):
    s_len, d = dx.shape
    k = w.shape[0]
    chunk = 512

    def body(*refs):
        if pre is None:
            dx_ref, y_ref, g_ref, w_ref, dy_ref, da_ref, dg_ref = refs
        else:
            dx_ref, y_ref, g_ref, w_ref, pre_ref, dy_ref, da_ref, dg_ref = refs
        dy, dg = _rms_bwd(y_ref[...], g_ref[...], dx_ref[...])
        dyb = dy.astype(BF16)
        dy_ref[...] = dyb

        @pl.when(pl.program_id(0) == 0)
        def _():
            dg_ref[...] = jnp.zeros_like(dg_ref)

        dg_ref[...] += dg
        for a in range(0, k, chunk):
            b = min(a + chunk, k)
            da = lax.dot_general(dyb, w_ref[a:b, :], NT_DIMS, preferred_element_type=F32)
            if pre is not None:
                da = da * (2.0 * jnp.maximum(pre_ref[:, a:b].astype(F32), 0.0))
            da_ref[:, a:b] = da.astype(out_dtype)

    row = lambda i: (i, 0)
    in_specs = [pl.BlockSpec((tile, d), row), pl.BlockSpec((tile, d), row), _full((1, d)), _full((k, d))]
    args = [dx, y, g, w]
    if pre is not None:
        in_specs.append(pl.BlockSpec((tile, k), row))
        args.append(pre)
    return pl.pallas_call(
        body, name=name, grid=(s_len // tile,), in_specs=in_specs,
        out_specs=[pl.BlockSpec((tile, d), row), pl.BlockSpec((tile, k), row), _full((1, d))],
        out_shape=[jax.ShapeDtypeStruct((s_len, d), BF16), jax.ShapeDtypeStruct((s_len, k), out_dtype),
                   jax.ShapeDtypeStruct((1, d), F32)],
        compiler_params=_params("arbitrary"),
    )(*args)


def _mm_prenorm_bwd(dz, w, x, g, dres, tile, name, host=None):
    pieces = list(dz) if isinstance(dz, (list, tuple)) else [dz]
    n_p = len(pieces)
    widths = [p.shape[1] for p in pieces]
    s_len, n = pieces[0].shape[0], sum(widths)
    d = w.shape[0]

    def body(*refs):
        dz_refs, rest = refs[:n_p], refs[n_p:]
        if dres is None:
            w_ref, x_ref, g_ref, dx_ref, dg_ref = rest
        else:
            w_ref, x_ref, g_ref, dres_ref, dx_ref, dg_ref = rest
        dh, off = None, 0
        for dz_ref, width in zip(dz_refs, widths):
            part = lax.dot_general(dz_ref[...], w_ref[:, off:off + width], NT_DIMS, preferred_element_type=F32)
            dh = part if dh is None else dh + part
            off += width
        dx, dg = _rms_bwd(x_ref[...], g_ref[...], dh)
        if dres is not None:
            dx = dx + dres_ref[...]
        dx_ref[...] = dx

        @pl.when(pl.program_id(0) == 0)
        def _():
            dg_ref[...] = jnp.zeros_like(dg_ref)

        dg_ref[...] += dg

    row = lambda i: (i, 0)
    in_specs = [pl.BlockSpec((tile, width), row) for width in widths]
    in_specs += [_full((d, n)), pl.BlockSpec((tile, d), row), _full((1, d))]
    args = pieces + [w, x, g]
    if dres is not None:
        in_specs.append(pl.BlockSpec((tile, d), row))
        args.append(dres)
    return _pcall(
        body, name=name, grid=(s_len // tile,), in_specs=in_specs,
        out_specs=[pl.BlockSpec((tile, d), row), _full((1, d))],
        out_shape=[jax.ShapeDtypeStruct((s_len, d), F32), jax.ShapeDtypeStruct((1, d), F32)],
        args=args, sem=("arbitrary",), host=host)


def _matmul_tn(a, bs, tm, tk, name):
    s_len, m = a.shape
    widths = [b.shape[1] for b in bs]
    n = sum(widths)

    def body(a_ref, *refs):
        b_refs, o_ref = refs[:-1], refs[-1]

        @pl.when(pl.program_id(1) == 0)
        def _():
            o_ref[...] = jnp.zeros_like(o_ref)

        a_tile = a_ref[...]
        off = 0
        for b_ref, width in zip(b_refs, widths):
            o_ref[:, off:off + width] += lax.dot_general(a_tile, b_ref[...], TN_DIMS, preferred_element_type=F32)
            off += width

    return pl.pallas_call(
        body, name=name, grid=(m // tm, s_len // tk),
        in_specs=[pl.BlockSpec((tk, tm), lambda i, k: (k, i))]
        + [pl.BlockSpec((tk, width), lambda i, k: (k, 0)) for width in widths],
        out_specs=pl.BlockSpec((tm, n), lambda i, k: (i, 0)),
        out_shape=jax.ShapeDtypeStruct((m, n), F32),
        compiler_params=_params("parallel", "arbitrary"),
    )(a, *bs)


def _matmul_tn_shards(a, b, axis, tk, name):
    s_len, m = a.shape
    n = b.shape[1]
    r, c = (m // N_DEV, n) if axis == 0 else (m, n // N_DEV)
    n_k = s_len // tk
    tm = max(r, min(m, (1 << 20) // n)) if axis == 0 else min(m, (1 << 20) // n)

    def body(a_ref, b_ref, o_ref, acc):
        k = pl.program_id(1)

        @pl.when(k == 0)
        def _():
            acc[...] = jnp.zeros_like(acc)

        acc[...] += lax.dot_general(a_ref[...], b_ref[...], TN_DIMS, preferred_element_type=F32)

        @pl.when(k == n_k - 1)
        def _():
            if axis == 0:
                o_ref[...] = acc[...].reshape(tm // r, r, c).astype(BF16)
            else:
                for j in range(N_DEV):
                    o_ref[j] = acc[:, j * c:(j + 1) * c].astype(BF16)

    if axis == 0:
        out_spec = pl.BlockSpec((tm // r, r, c), lambda i, k: (i, 0, 0))
    else:
        out_spec = pl.BlockSpec((N_DEV, tm, c), lambda i, k: (0, i, 0))
    return pl.pallas_call(
        body, name=name, grid=(m // tm, n_k),
        in_specs=[pl.BlockSpec((tk, tm), lambda i, k: (k, i)), pl.BlockSpec((tk, n), lambda i, k: (k, 0))],
        out_specs=out_spec, out_shape=jax.ShapeDtypeStruct((N_DEV, r, c), BF16),
        scratch_shapes=[pltpu.VMEM((tm, n), F32)],
        compiler_params=_params("parallel", "arbitrary"),
    )(a, b)


def _cumsum_bwd(rs, cs, fl, b, tile, name):
    s_len = fl.shape[0]
    n_t = s_len // tile

    def body(rs_ref, cs_ref, fl_ref, b_ref, dfl_ref, db_ref, carry):
        @pl.when(pl.program_id(0) == 0)
        def _():
            carry[...] = jnp.zeros_like(carry)
            db_ref[...] = jnp.zeros_like(db_ref)

        r = lax.broadcasted_iota(jnp.int32, (tile, tile), 0)
        c = lax.broadcasted_iota(jnp.int32, (tile, tile), 1)
        tri = (c >= r).astype(F32)
        lane = lax.broadcasted_iota(jnp.int32, (1, LANES), 1)
        dc = jnp.zeros((tile, LANES), F32)
        for p in range(N_PAIRS):
            blk = rs_ref[:, LANES * p:LANES * (p + 1)] - cs_ref[:, LANES * p:LANES * (p + 1)]
            dc = jnp.where(lane == 2 * p, blk, dc)
            dc = jnp.where(lane == 2 * p + 1, pltpu.roll(blk, FOX_HEAD_DIM, axis=1), dc)
        dl = jnp.dot(tri, dc, precision=lax.Precision.HIGHEST, preferred_element_type=F32) + carry[...]
        carry[...] = dl[0:1, :]
        dfl = dl * _sigmoid(-(fl_ref[...] + b_ref[...]))
        dfl_ref[...] = dfl
        db_ref[...] += jnp.sum(dfl, axis=0, keepdims=True)

    rev = lambda i: (n_t - 1 - i, 0)
    return pl.pallas_call(
        body, name=name, grid=(n_t,),
        in_specs=[pl.BlockSpec((tile, FOX_WIDTH), rev), pl.BlockSpec((tile, FOX_WIDTH), rev), pl.BlockSpec((tile, LANES), rev),
                  _full((1, LANES))],
        out_specs=[pl.BlockSpec((tile, LANES), rev), _full((1, LANES))],
        out_shape=[jax.ShapeDtypeStruct((s_len, LANES), F32), jax.ShapeDtypeStruct((1, LANES), F32)],
        scratch_shapes=[pltpu.VMEM((1, LANES), F32)],
        compiler_params=_params("arbitrary"),
    )(rs, cs, fl, b)


SUBLANES = 8
CONV_ROWS = 64


def _phase_copies(src, dst, rows):
    for p in range(SUBLANES):
        dst[p] = src[pl.ds(p, rows), :]


def _phase_rows(extp_ref, off, r0, rows):
    p = off % SUBLANES
    return extp_ref[p, pl.ds(pl.multiple_of(r0 + (off - p), SUBLANES), rows), :]


def _conv_taps(w_ref, extp_ref, base, r0, rows, reverse):
    acc = None
    for k in range(CONV_WIDTH):
        off = base + ((CONV_WIDTH - 1 - k) if reverse else k)
        term = w_ref[k:k + 1, :] * _phase_rows(extp_ref, off, r0, rows)
        acc = term if acc is None else acc + term
    return acc


def _fold_rows(x):
    out = x[0:SUBLANES]
    for i in range(1, x.shape[0] // SUBLANES):
        out = out + x[i * SUBLANES:(i + 1) * SUBLANES]
    return out


def _conv_fwd(ag, w, cb, lg, lb, tile, name):
    s_len = ag.shape[0]
    c = CONV_CH
    rb = tile

    def body(ag_ref, w_ref, cb_ref, lg_ref, lb_ref, u_ref, ext, extp):
        @pl.when(pl.program_id(0) == 0)
        def _():
            ext[0:CONV_PAD, :] = jnp.zeros((CONV_PAD, c), F32)
            ext[tile + CONV_PAD:tile + CONV_PAD + SUBLANES, :] = jnp.zeros((SUBLANES, c), F32)

        ext[CONV_PAD:CONV_PAD + tile, :] = ag_ref[:, 0:c] * _sigmoid(ag_ref[:, c:2 * c])
        _phase_copies(ext, extp, tile + CONV_PAD)

        def block(b, carry):
            r0 = pl.multiple_of(b * rb, rb)
            u1 = _conv_taps(w_ref, extp, CONV_PAD - (CONV_WIDTH - 1), r0, rb, False) + cb_ref[...]
            mu = jnp.mean(u1, axis=-1, keepdims=True)
            xc = u1 - mu
            y = xc * lax.rsqrt(jnp.mean(xc * xc, axis=-1, keepdims=True) + EPS) * lg_ref[...] + lb_ref[...]
            u_ref[pl.ds(r0, rb), :] = (y * _sigmoid(y)).astype(BF16)
            return carry

        lax.fori_loop(0, tile // rb, block, 0)
        ext[0:CONV_PAD, :] = ext[tile:tile + CONV_PAD, :]

    return pl.pallas_call(
        body, name=name, grid=(s_len // tile,),
        in_specs=[pl.BlockSpec((tile, 2 * c), lambda i: (i, 0)), _full((CONV_PAD, c)), _full((1, c)), _full((1, c)),
                  _full((1, c))],
        out_specs=pl.BlockSpec((tile, c), lambda i: (i, 0)),
        out_shape=jax.ShapeDtypeStruct((s_len, c), BF16),
        scratch_shapes=[pltpu.VMEM((tile + CONV_PAD + SUBLANES, c), F32), pltpu.VMEM((SUBLANES, tile + CONV_PAD, c), F32)],
        compiler_params=_params("arbitrary"),
    )(ag, w, cb, lg, lb)


def _conv_bwd(ag, dcat, w, cb, lg, lb, tile, name):
    s_len = ag.shape[0]
    c = CONV_CH
    n_t = s_len // tile
    per = tile // CONV_PAD
    rb = min(CONV_ROWS, tile)

    def body(ag_ref, halo_ref, du_ref, w_ref, cb_ref, lg_ref, lb_ref, dag_ref, dw_ref, dv_ref, ext, ext2, extp, dwacc):
        i = pl.program_id(0)
        t = n_t - 1 - i

        @pl.when(i == 0)
        def _():
            ext2[tile:tile + CONV_PAD + SUBLANES, :] = jnp.zeros((CONV_PAD + SUBLANES, c), F32)
            ext[tile + CONV_PAD:tile + CONV_PAD + SUBLANES, :] = jnp.zeros((SUBLANES, c), F32)
            dwacc[...] = jnp.zeros_like(dwacc)
            dv_ref[...] = jnp.zeros_like(dv_ref)

        halo = halo_ref[:, 0:c] * _sigmoid(halo_ref[:, c:2 * c])
        ext[0:CONV_PAD, :] = jnp.where(t > 0, halo, 0.0)
        ext[CONV_PAD:CONV_PAD + tile, :] = ag_ref[:, 0:c] * _sigmoid(ag_ref[:, c:2 * c])
        _phase_copies(ext, extp, tile + CONV_PAD)
        base = CONV_PAD - (CONV_WIDTH - 1)

        def block1(b, carry):
            r0 = pl.multiple_of(b * rb, rb)
            u1 = _conv_taps(w_ref, extp, base, r0, rb, False) + cb_ref[...]
            mu = jnp.mean(u1, axis=-1, keepdims=True)
            xc = u1 - mu
            rs = lax.rsqrt(jnp.mean(xc * xc, axis=-1, keepdims=True) + EPS)
            xhat = xc * rs
            y = xhat * lg_ref[...] + lb_ref[...]
            sy = _sigmoid(y)
            dy = du_ref[pl.ds(r0, rb), :] * (sy * (1.0 + y * (1.0 - sy)))
            dxh = dy * lg_ref[...]
            du1 = rs * (dxh - jnp.mean(dxh, axis=-1, keepdims=True) - xhat * jnp.mean(dxh * xhat, axis=-1, keepdims=True))
            dv_ref[0:1, :] += jnp.sum(du1, axis=0, keepdims=True)
            dv_ref[1:2, :] += jnp.sum(dy * xhat, axis=0, keepdims=True)
            dv_ref[2:3, :] += jnp.sum(dy, axis=0, keepdims=True)
            for k in range(CONV_WIDTH):
                dwacc[k] += _fold_rows(du1 * _phase_rows(extp, base + k, r0, rb))
            ext2[pl.ds(r0, rb), :] = du1
            return carry

        lax.fori_loop(0, tile // rb, block1, 0)
        _phase_copies(ext2, extp, tile + CONV_PAD)

        def block2(b, carry):
            r0 = pl.multiple_of(b * rb, rb)
            du0 = _conv_taps(w_ref, extp, 0, r0, rb, True)
            a = ag_ref[pl.ds(r0, rb), 0:c]
            sg = _sigmoid(ag_ref[pl.ds(r0, rb), c:2 * c])
            dag_ref[pl.ds(r0, rb), 0:c] = (du0 * sg).astype(BF16)
            dag_ref[pl.ds(r0, rb), c:2 * c] = (du0 * a * sg * (1.0 - sg)).astype(BF16)
            return carry

        lax.fori_loop(0, tile // rb, block2, 0)
        ext2[tile:tile + CONV_PAD, :] = ext2[0:CONV_PAD, :]

        @pl.when(i == n_t - 1)
        def _():
            for k in range(CONV_WIDTH):
                dw_ref[k:k + 1, :] = jnp.sum(dwacc[k], axis=0, keepdims=True)
            dw_ref[CONV_WIDTH:CONV_PAD, :] = jnp.zeros((CONV_PAD - CONV_WIDTH, c), F32)

    rev = lambda i: (n_t - 1 - i, 0)
    return pl.pallas_call(
        body, name=name, grid=(n_t,),
        in_specs=[pl.BlockSpec((tile, 2 * c), rev),
                  pl.BlockSpec((CONV_PAD, 2 * c), lambda i: (jnp.maximum((n_t - 1 - i) * per - 1, 0), 0)),
                  pl.BlockSpec((tile, c), rev), _full((CONV_PAD, c)), _full((1, c)), _full((1, c)), _full((1, c))],
        out_specs=[pl.BlockSpec((tile, 2 * c), rev), _full((CONV_PAD, c)), _full((8, c))],
        out_shape=[jax.ShapeDtypeStruct((s_len, 2 * c), BF16), jax.ShapeDtypeStruct((CONV_PAD, c), F32),
                   jax.ShapeDtypeStruct((8, c), F32)],
        scratch_shapes=[pltpu.VMEM((tile + CONV_PAD + SUBLANES, c), F32), pltpu.VMEM((tile + CONV_PAD + SUBLANES, c), F32),
                        pltpu.VMEM((SUBLANES, tile + CONV_PAD, c), F32), pltpu.VMEM((CONV_PAD, SUBLANES, c), F32)],
        compiler_params=_params("arbitrary"),
    )(ag, ag, dcat, w, cb, lg, lb)


Q_SCALE = FOX_HEAD_DIM ** -0.5


def _head_col(x, lane, h):
    return jnp.sum(jnp.where(lane == h, x, 0.0), axis=1, keepdims=True)


def _split3(x):
    hi = x.astype(BF16).astype(F32)
    r = x - hi
    mid = r.astype(BF16).astype(F32)
    lo = (r - mid).astype(BF16).astype(F32)
    return hi, mid, lo


def _in_lanes(lane, lo, n):
    return (lane >= lo) & (lane < lo + n)


def _put3(lane, lo, parts, rest):
    return jnp.where(lane == lo, parts[0], jnp.where(lane == lo + 1, parts[1], jnp.where(lane == lo + 2, parts[2], rest)))


def _spare_lane(h):
    return FOX_HEAD_DIM if h % 2 == 0 else 0


def _shift_div(i, num, den):
    return i * (num // den) if num >= den else lax.shift_right_logical(i, (den // num).bit_length() - 1)


def _foxa_prep(qkv, fl, b, tile, name):
    s_len = qkv.shape[0]

    def body(q_ref, k_ref, v_ref, fl_ref, b_ref, qa_ref, ka_ref, va_ref, carry):
        @pl.when(pl.program_id(0) == 0)
        def _():
            carry[...] = jnp.zeros_like(carry)

        xx = fl_ref[...] + b_ref[...]
        lf = jnp.minimum(xx, 0.0) - jnp.log1p(jnp.exp(-jnp.abs(xx)))
        tri = (lax.broadcasted_iota(jnp.int32, (tile, tile), 1) <= lax.broadcasted_iota(jnp.int32, (tile, tile), 0))
        cum_t = jnp.dot(tri.astype(F32), lf, precision=lax.Precision.HIGHEST, preferred_element_type=F32) + carry[...]
        carry[...] = cum_t[tile - 1:tile, :]
        lane = lax.broadcasted_iota(jnp.int32, (1, LANES), 1)
        for h in range(FOX_HEADS):
            e = _spare_lane(h)
            head = ~_in_lanes(lane, e, FOX_HEAD_DIM)
            blk = slice(LANES * (h // 2), LANES * (h // 2) + LANES)
            out = slice(LANES * h, LANES * h + LANES)
            c3 = _split3(_head_col(cum_t, lane, h))
            ex_q = _put3(lane, e, c3, jnp.where(_in_lanes(lane, e + 3, 3), 1.0, 0.0))
            qa_ref[:, out] = jnp.where(head, q_ref[:, blk].astype(F32) * Q_SCALE, ex_q).astype(BF16)
            ones = jnp.where(_in_lanes(lane, e, 3) | _in_lanes(lane, e + 6, 3), 1.0, 0.0)
            ex_k = _put3(lane, e + 3, [-c for c in c3], ones)
            ka_ref[:, out] = jnp.where(head, k_ref[:, blk].astype(F32), ex_k).astype(BF16)
            ex_v = jnp.where(_in_lanes(lane, e, 3), 1.0, 0.0)
            va_ref[:, out] = jnp.where(head, v_ref[:, blk].astype(F32), ex_v).astype(BF16)

    col = lambda c: pl.BlockSpec((tile, FOX_WIDTH), lambda i: (i, c))
    wide = pl.BlockSpec((tile, 2 * FOX_WIDTH), lambda i: (i, 0))
    return pl.pallas_call(
        body, name=name, grid=(s_len // tile,),
        in_specs=[col(0), col(1), col(2), pl.BlockSpec((tile, LANES), lambda i: (i, 0)), _full((1, LANES))],
        out_specs=[wide, wide, wide], out_shape=[jax.ShapeDtypeStruct((s_len, 2 * FOX_WIDTH), BF16)] * 3,
        scratch_shapes=[pltpu.VMEM((1, LANES), F32)],
        compiler_params=_params("arbitrary"),
    )(qkv, qkv, qkv, fl, b)


def _foxa_prep_bwd(qa, lse, o, dcat, tile, name):
    s_len = qa.shape[0]

    def body(qa_ref, lse_ref, o_ref, do_ref, qb_ref, doa_ref):
        lane = lax.broadcasted_iota(jnp.int32, (1, LANES), 1)
        for h in range(FOX_HEADS):
            e = _spare_lane(h)
            head = ~_in_lanes(lane, e, FOX_HEAD_DIM)
            blk = slice(LANES * (h // 2), LANES * (h // 2) + LANES)
            out = slice(LANES * h, LANES * h + LANES)
            do = do_ref[:, blk]
            delta = jnp.sum(jnp.where(head, do * o_ref[:, blk], 0.0), axis=1, keepdims=True)
            doa_ref[:, out] = _put3(lane, e, _split3(-delta), jnp.where(head, do, 0.0)).astype(BF16)
            at = LANES * (h // 2) + FOX_HEAD_DIM - e
            l3 = _split3(-lse_ref[:, at:at + 1])
            qb_ref[:, out] = _put3(lane, e + 6, l3, qa_ref[:, out].astype(F32)).astype(BF16)

    wide = pl.BlockSpec((tile, 2 * FOX_WIDTH), lambda i: (i, 0))
    half = pl.BlockSpec((tile, FOX_WIDTH), lambda i: (i, 0))
    return pl.pallas_call(
        body, name=name, grid=(s_len // tile,),
        in_specs=[wide, half, half, pl.BlockSpec((tile, FOX_WIDTH), lambda i: (i, 1))],
        out_specs=[wide, wide], out_shape=[jax.ShapeDtypeStruct((s_len, 2 * FOX_WIDTH), BF16)] * 2,
        compiler_params=_params("parallel"),
    )(qa, lse, o, dcat)


def _foxa_fwd(qa, ka, va, tq, tk, name, host=None):
    s_len = qa.shape[0]

    def body(q_ref, k_ref, v_ref, o_ref, lse_ref):
        i = pl.program_id(1)
        lane = lax.broadcasted_iota(jnp.int32, (1, LANES), 1)
        cols = [slice(LANES * hh, LANES * hh + LANES) for hh in range(2)]
        qh = [q_ref[:, c] for c in cols]

        def scores(j):
            off = pl.multiple_of(j * tk, tk)
            return [lax.dot_general(qh[hh], k_ref[pl.ds(off, tk), cols[hh]], NT_DIMS, preferred_element_type=F32)
                    for hh in range(2)]

        def update(j, s, m, acc, mask):
            off = pl.multiple_of(j * tk, tk)
            m_out, acc_out = [], []
            for hh in range(2):
                sh = s[hh] if mask is None else jnp.where(mask, s[hh], NEG_INF)
                m_new = jnp.maximum(m[hh], jnp.max(sh, axis=1, keepdims=True))
                pr = jnp.exp(sh - m_new).astype(BF16)
                acc_out.append(jnp.exp(m[hh] - m_new) * acc[hh]
                               + jnp.dot(pr, v_ref[pl.ds(off, tk), cols[hh]], preferred_element_type=F32))
                m_out.append(m_new)
            return m_out, acc_out

        def step(j, carry):
            s_next = scores(j + 1)
            m, acc = update(j, carry[0:2], carry[2:4], carry[4:6], None)
            return (*s_next, *m, *acc)

        n_full = _shift_div(i, tq, tk)
        n_part = max(tq // tk, 1)
        qi = lax.broadcasted_iota(jnp.int32, (tq, tk), 0) + i * tq
        ki = lax.broadcasted_iota(jnp.int32, (tq, tk), 1)
        carry = (*scores(0), *([jnp.full((tq, 1), NEG_INF, F32)] * 2), *([jnp.zeros((tq, LANES), F32)] * 2))
        carry = lax.fori_loop(0, n_full, step, carry)
        s, m, acc = list(carry[0:2]), list(carry[2:4]), list(carry[4:6])
        for jj in range(n_part):
            s_next = scores(n_full + jj + 1) if jj < n_part - 1 else None
            m, acc = update(n_full + jj, s, m, acc, ki + (n_full + jj) * tk <= qi)
            s = s_next
        res = []
        for hh in range(2):
            l = acc[hh][:, _spare_lane(hh):_spare_lane(hh) + 1]
            res.append((acc[hh] / l, m[hh] + jnp.log(l)))
        low = lane < FOX_HEAD_DIM
        o_ref[...] = jnp.where(low, res[0][0], res[1][0])
        lse_ref[...] = jnp.where(low, res[0][1], res[1][1])

    pair = pl.BlockSpec((s_len, 2 * LANES), lambda p, i: (0, p))
    out = pl.BlockSpec((tq, LANES), lambda p, i: (i, p))
    return _pcall(
        body, name=name, grid=(N_PAIRS, s_len // tq),
        in_specs=[pl.BlockSpec((tq, 2 * LANES), lambda p, i: (i, p)), pair, pair],
        out_specs=[out, out], out_shape=[jax.ShapeDtypeStruct((s_len, FOX_WIDTH), F32)] * 2,
        args=[qa, ka, va], sem=("parallel", "parallel"), host=host)


def _foxa_dq(qb, ka, va, doa, tq, tk, name, host=None):
    s_len = qb.shape[0]

    def body(q_ref, k_ref, v_ref, do_ref, dq_ref, rs_ref):
        i = pl.program_id(1)
        lane = lax.broadcasted_iota(jnp.int32, (1, LANES), 1)
        cols = [slice(LANES * hh, LANES * hh + LANES) for hh in range(2)]
        qh = [q_ref[:, c] for c in cols]
        doh = [do_ref[:, c] for c in cols]

        def update(j, acc, mask):
            off = pl.multiple_of(j * tk, tk)
            out = []
            for hh in range(2):
                kt = k_ref[pl.ds(off, tk), cols[hh]]
                pr = jnp.exp(lax.dot_general(qh[hh], kt, NT_DIMS, preferred_element_type=F32))
                if mask is not None:
                    pr = jnp.where(mask, pr, 0.0)
                ds = pr * lax.dot_general(doh[hh], v_ref[pl.ds(off, tk), cols[hh]], NT_DIMS, preferred_element_type=F32)
                out.append(acc[hh] + jnp.dot(ds.astype(BF16), kt, preferred_element_type=F32))
            return tuple(out)

        n_full = _shift_div(i, tq, tk)
        qi = lax.broadcasted_iota(jnp.int32, (tq, tk), 0) + i * tq
        ki = lax.broadcasted_iota(jnp.int32, (tq, tk), 1)
        acc = lax.fori_loop(0, n_full, lambda j, a: update(j, a, None), (jnp.zeros((tq, LANES), F32),) * 2)
        for jj in range(max(tq // tk, 1)):
            acc = update(n_full + jj, acc, ki + (n_full + jj) * tk <= qi)
        low = lane < FOX_HEAD_DIM
        dq_ref[...] = (jnp.where(low, acc[0], acc[1]) * Q_SCALE).astype(BF16)
        rs_ref[...] = jnp.where(low, acc[0][:, _spare_lane(0):_spare_lane(0) + 1], acc[1][:, _spare_lane(1):_spare_lane(1) + 1])

    pair = pl.BlockSpec((s_len, 2 * LANES), lambda p, i: (0, p))
    tile2 = pl.BlockSpec((tq, 2 * LANES), lambda p, i: (i, p))
    out = pl.BlockSpec((tq, LANES), lambda p, i: (i, p))
    return _pcall(
        body, name=name, grid=(N_PAIRS, s_len // tq), in_specs=[tile2, pair, pair, tile2], out_specs=[out, out],
        out_shape=[jax.ShapeDtypeStruct((s_len, FOX_WIDTH), BF16), jax.ShapeDtypeStruct((s_len, FOX_WIDTH), F32)],
        args=[qb, ka, va, doa], sem=("parallel", "parallel"), host=host)


def _foxa_dkv(qb, ka, va, doa, tq, tk, name, host=None):
    s_len = qb.shape[0]
    n_q = s_len // tq

    def body(k_ref, v_ref, q_ref, do_ref, dk_ref, dv_ref, cs_ref):
        j = pl.program_id(1)
        lane = lax.broadcasted_iota(jnp.int32, (1, LANES), 1)
        cols = [slice(LANES * hh, LANES * hh + LANES) for hh in range(2)]
        kh = [k_ref[:, c] for c in cols]
        vh = [v_ref[:, c] for c in cols]

        def update(i, acc, mask):
            off = pl.multiple_of(i * tq, tq)
            out = []
            for hh in range(2):
                qt = q_ref[pl.ds(off, tq), cols[hh]]
                dot = do_ref[pl.ds(off, tq), cols[hh]]
                pt = jnp.exp(lax.dot_general(kh[hh], qt, NT_DIMS, preferred_element_type=F32))
                if mask is not None:
                    pt = jnp.where(mask, pt, 0.0)
                dv = acc[2 * hh + 1] + jnp.dot(pt.astype(BF16), dot, preferred_element_type=F32)
                dst = pt * lax.dot_general(vh[hh], dot, NT_DIMS, preferred_element_type=F32)
                out += [acc[2 * hh] + jnp.dot(dst.astype(BF16), qt, preferred_element_type=F32), dv]
            return tuple(out)

        i0 = _shift_div(j, tk, tq)
        n_part = max(tk // tq, 1)
        ki = lax.broadcasted_iota(jnp.int32, (tk, tq), 0) + j * tk
        qi = lax.broadcasted_iota(jnp.int32, (tk, tq), 1)
        acc = (jnp.zeros((tk, LANES), F32),) * 4
        for ii in range(n_part):
            acc = update(i0 + ii, acc, ki <= qi + (i0 + ii) * tq)
        acc = lax.fori_loop(i0 + n_part, n_q, lambda i, a: update(i, a, None), acc)
        low = lane < FOX_HEAD_DIM
        dk_ref[...] = jnp.where(low, acc[0], acc[2]).astype(BF16)
        dv_ref[...] = jnp.where(low, acc[1], acc[3]).astype(BF16)
        cs_ref[...] = jnp.where(low, acc[0][:, _spare_lane(0) + 3:_spare_lane(0) + 4],
                                acc[2][:, _spare_lane(1) + 3:_spare_lane(1) + 4])

    pair = pl.BlockSpec((s_len, 2 * LANES), lambda p, j: (0, p))
    tile2 = pl.BlockSpec((tk, 2 * LANES), lambda p, j: (j, p))
    out = pl.BlockSpec((tk, LANES), lambda p, j: (j, p))
    return _pcall(
        body, name=name, grid=(N_PAIRS, s_len // tk), in_specs=[tile2, tile2, pair, pair], out_specs=[out, out, out],
        out_shape=[jax.ShapeDtypeStruct((s_len, FOX_WIDTH), BF16), jax.ShapeDtypeStruct((s_len, FOX_WIDTH), BF16),
                   jax.ShapeDtypeStruct((s_len, FOX_WIDTH), F32)],
        args=[ka, va, qb, doa], sem=("parallel", "parallel"), host=host)


def _mem_scores_t(q, kv, h):
    lo = h * MEM_HEAD_DIM
    st = lax.dot_general(kv[:, lo:lo + MEM_HEAD_DIM], q[:, lo:lo + MEM_HEAD_DIM], NT_DIMS,
                         preferred_element_type=F32) * (MEM_HEAD_DIM ** -0.5)
    e = jnp.exp(st - jnp.max(st, axis=0, keepdims=True))
    return e / jnp.sum(e, axis=0, keepdims=True)


def _memattn_fwd(q, kv, tile, name):
    s_len = q.shape[0]
    n_mem = kv.shape[0]

    def body(q_ref, kv_ref, o_ref):
        q = q_ref[...]
        kv = kv_ref[...]
        for h in range(MEM_HEADS):
            lo = h * MEM_HEAD_DIM
            pt = _mem_scores_t(q, kv, h).astype(BF16)
            vh = kv[:, MEM_INNER + lo:MEM_INNER + lo + MEM_HEAD_DIM]
            o_ref[:, lo:lo + MEM_HEAD_DIM] = lax.dot_general(pt, vh, TN_DIMS, preferred_element_type=F32).astype(BF16)

    return pl.pallas_call(
        body, name=name, grid=(s_len // tile,),
        in_specs=[pl.BlockSpec((tile, MEM_INNER), lambda i: (i, 0)), _full((n_mem, 2 * MEM_INNER))],
        out_specs=pl.BlockSpec((tile, MEM_INNER), lambda i: (i, 0)),
        out_shape=jax.ShapeDtypeStruct((s_len, MEM_INNER), BF16),
        compiler_params=_params("parallel"),
    )(q, kv)


def _memattn_bwd(q, kv, do, tile, name):
    s_len = q.shape[0]
    n_mem = kv.shape[0]
    scale = MEM_HEAD_DIM ** -0.5

    def body(q_ref, kv_ref, do_ref, dq_ref, dkv_ref):
        @pl.when(pl.program_id(0) == 0)
        def _():
            dkv_ref[...] = jnp.zeros_like(dkv_ref)

        q = q_ref[...]
        kv = kv_ref[...]
        do = do_ref[...]
        for h in range(MEM_HEADS):
            lo = h * MEM_HEAD_DIM
            qh = q[:, lo:lo + MEM_HEAD_DIM]
            kh = kv[:, lo:lo + MEM_HEAD_DIM]
            vh = kv[:, MEM_INNER + lo:MEM_INNER + lo + MEM_HEAD_DIM]
            doh = do[:, lo:lo + MEM_HEAD_DIM]
            pt = _mem_scores_t(q, kv, h)
            dkv_ref[:, MEM_INNER + lo:MEM_INNER + lo + MEM_HEAD_DIM] += jnp.dot(
                pt.astype(BF16), doh, preferred_element_type=F32)
            dpt = lax.dot_general(vh, doh, NT_DIMS, preferred_element_type=F32)
            dst = (pt * (dpt - jnp.sum(pt * dpt, axis=0, keepdims=True)) * scale).astype(BF16)
            dkv_ref[:, lo:lo + MEM_HEAD_DIM] += jnp.dot(dst, qh, preferred_element_type=F32)
            dq_ref[:, lo:lo + MEM_HEAD_DIM] = lax.dot_general(dst, kh, TN_DIMS, preferred_element_type=F32).astype(BF16)

    return pl.pallas_call(
        body, name=name, grid=(s_len // tile,),
        in_specs=[pl.BlockSpec((tile, MEM_INNER), lambda i: (i, 0)), _full((n_mem, 2 * MEM_INNER)),
                  pl.BlockSpec((tile, MEM_INNER), lambda i: (i, 0))],
        out_specs=[pl.BlockSpec((tile, MEM_INNER), lambda i: (i, 0)), _full((n_mem, 2 * MEM_INNER))],
        out_shape=[jax.ShapeDtypeStruct((s_len, MEM_INNER), BF16), jax.ShapeDtypeStruct((n_mem, 2 * MEM_INNER), F32)],
        compiler_params=_params("arbitrary"),
    )(q, kv, do)


def _loss_head(y, target, tile, name):
    s_len, d = y.shape

    def body(y_ref, t_ref, dy_ref, l_ref):
        @pl.when(pl.program_id(0) == 0)
        def _():
            l_ref[...] = jnp.zeros_like(l_ref)

        err = y_ref[...] - t_ref[...]
        dy_ref[...] = err * (1.0 / d)
        l_ref[...] += jnp.sum(err * err, axis=0, keepdims=True) * (0.5 / d)

    row = lambda i: (i, 0)
    return pl.pallas_call(
        body, name=name, grid=(s_len // tile,),
        in_specs=[pl.BlockSpec((tile, d), row), pl.BlockSpec((tile, d), row)],
        out_specs=[pl.BlockSpec((tile, d), row), _full((1, d))],
        out_shape=[jax.ShapeDtypeStruct((s_len, d), F32), jax.ShapeDtypeStruct((1, d), F32)],
        compiler_params=_params("arbitrary"),
    )(y, target)


def _attn_tile(s_len):
    return min(256, s_len // 2)


REST = ("w_out", "w_mq", "w_mk", "w_mv", "w_mo", "w_up", "w_down")
REST_DQ = ("w_up", "w_down")
REST_DKV = ("w_out", "w_mq", "w_mk", "w_mv", "w_mo")
SHARD_AXIS = {"w_in": 1, "w_out": 0, "w_mq": 0, "w_mk": 0, "w_mv": 0, "w_mo": 1, "w_up": 1, "w_down": 0}


def _full_from_shards(sh, axis):
    n, r, c = sh.shape
    if axis == 0:
        return sh.reshape(n * r, c)
    return sh.transpose(1, 0, 2).reshape(r, n * c)


def _rest_weights(lands):
    w = {n: _full_from_shards(sh, SHARD_AXIS[n]) for n, sh in zip(REST, lands)}
    w["w_mkv"] = jnp.concatenate([w.pop("w_mk"), w.pop("w_mv")], axis=1)
    return w


def _w_in_cat(land):
    return jnp.pad(_full_from_shards(land, 1), ((0, 0), (0, IN_CAT - IN_COLS)))


def _layer_fwd(x0, mem, w, l, rest_src=None, next_src=None):
    s_len = x0.shape[0]
    tile = min(512, s_len)
    tile_ff = min(256, s_len)
    ta = _attn_tile(s_len)
    ident = lambda z: z
    sv = {"x0": x0}

    h1, ag, qkv, fl = _rms_matmul(
        x0, w["norm_mix_pre"], w["w_in_cat"],
        [(0, 2 * CONV_CH, [(F32, ident)]), (2 * CONV_CH, IN_MAIN, [(BF16, ident)]), (IN_MAIN, IN_CAT, [(F32, ident)])],
        tile, f"mix_in_{l}")
    u3 = _conv_fwd(ag, w["conv_w"], w["conv_b"], w["conv_ln_g"], w["conv_ln_b"], tile, f"conv_fwd_{l}")
    qa, ka, va = _foxa_prep(qkv, fl, w["b_forget"], tile, f"fox_prep_{l}")
    next_land = None
    if rest_src is None:
        o, lse = _foxa_fwd(qa, ka, va, ta, 2 * ta, f"fox_fwd_{l}")
    else:
        srcs = list(rest_src) + list(next_src or [])
        (o, lse), lands = _foxa_fwd(qa, ka, va, ta, 2 * ta, f"fox_fwd_{l}", host=(srcs, [True] * len(srcs)))
        w = {**w, **_rest_weights(lands[:len(REST)])}
        next_land = lands[len(REST):]
    cat = jnp.concatenate([u3, o.astype(BF16)], axis=1)
    y1, x1 = _matmul_resnorm(cat, w["w_out"], x0, w["norm_mix_post"], tile, f"mix_out_{l}")
    sv.update(h1=h1, ag=ag, fl=fl, qa=qa, ka=ka, va=va, o=o, lse=lse, cat=cat, y1=y1, x1=x1)

    h2, qm = _rms_matmul(x1, w["norm_mem_pre"], w["w_mq"], [(0, MEM_INNER, [(BF16, ident)])], tile, f"mem_q_{l}")
    mem_n, kv = _rms_matmul(mem, w["norm_memkv"], w["w_mkv"], [(0, 2 * MEM_INNER, [(BF16, ident)])],
                            mem.shape[0], f"mem_kv_{l}")
    om = _memattn_fwd(qm, kv, tile, f"mem_attn_fwd_{l}")
    y2, x2 = _matmul_resnorm(om, w["w_mo"], x1, w["norm_mem_post"], tile, f"mem_out_{l}")
    sv.update(h2=h2, qm=qm, mem_n=mem_n, kv=kv, om=om, y2=y2, x2=x2)

    relu2 = lambda z: jnp.square(jnp.maximum(z, 0.0))
    h3, pre, hid = _rms_matmul(x2, w["norm_mlp_pre"], w["w_up"], [(0, D_FF, [(BF16, ident), (BF16, relu2)])], tile_ff,
                               f"mlp_up_{l}")
    y3, x3 = _matmul_resnorm(hid, w["w_down"], x2, w["norm_mlp_post"], tile_ff, f"mlp_down_{l}")
    sv.update(h3=h3, pre=pre, hid=hid, y3=y3)
    return x3, sv, w, next_land


def _layer_bwd(dx3, mem, w, sv, l, scatter_rest=False, dkv_src=None, scatter_w_in=False):
    s_len = dx3.shape[0]
    tile = min(512, s_len)
    tile_ff = min(256, s_len)
    ta = _attn_tile(s_len)
    tk = min(512, s_len)
    n_mem = mem.shape[0]
    g = {}

    dy3, dpre, g["norm_mlp_post"] = _resnorm_bwd_mm(dx3, sv["y3"], w["norm_mlp_post"], w["w_down"], tile_ff,
                                                    f"mlp_down_bwd_{l}", BF16, pre=sv["pre"])
    g["w_down"] = _matmul_tn_shards(sv["hid"], dy3, 0, tk, f"dw_down_{l}")
    dx2, g["norm_mlp_pre"] = _mm_prenorm_bwd(dpre, w["w_up"], sv["x2"], w["norm_mlp_pre"], dx3, tile_ff,
                                             f"mlp_up_bwd_{l}")
    g["w_up"] = _matmul_tn_shards(sv["h3"], dpre, 1, tk, f"dw_up_{l}")

    dy2, dom, g["norm_mem_post"] = _resnorm_bwd_mm(dx2, sv["y2"], w["norm_mem_post"], w["w_mo"], tile,
                                                   f"mem_out_bwd_{l}", BF16)
    g["w_mo"] = _matmul_tn_shards(sv["om"], dy2, 1, tk, f"dw_mo_{l}")
    dqm, dkv = _memattn_bwd(sv["qm"], sv["kv"], dom, tile, f"mem_attn_bwd_{l}")
    dkvb = dkv.astype(BF16)
    g["w_mq"] = _matmul_tn_shards(sv["h2"], dqm, 0, tk, f"dw_mq_{l}")
    dx1, g["norm_mem_pre"] = _mm_prenorm_bwd(dqm, w["w_mq"], sv["x1"], w["norm_mem_pre"], dx2, tile, f"mem_q_bwd_{l}")
    _, g["norm_memkv"] = _mm_prenorm_bwd(dkvb, w["w_mkv"], mem, w["norm_memkv"], None, n_mem, f"mem_kv_bwd_{l}")
    g["w_mk"] = _matmul_tn_shards(sv["mem_n"], dkvb[:, :MEM_INNER], 0, n_mem, f"dw_mk_{l}")
    g["w_mv"] = _matmul_tn_shards(sv["mem_n"], dkvb[:, MEM_INNER:], 0, n_mem, f"dw_mv_{l}")

    dy1, dcat, g["norm_mix_post"] = _resnorm_bwd_mm(dx1, sv["y1"], w["norm_mix_post"], w["w_out"], tile,
                                                    f"mix_out_bwd_{l}", F32)
    g["w_out"] = _matmul_tn_shards(sv["cat"], dy1, 0, tk, f"dw_out_{l}")
    qb, doa = _foxa_prep_bwd(sv["qa"], sv["lse"], sv["o"], dcat, tile, f"fox_prep_bwd_{l}")
    rest_land, dkv_land = None, None
    if scatter_rest:
        with_dq = [g[n] for n in REST_DQ]
        (dq, rs), land_dq = _foxa_dq(qb, sv["ka"], sv["va"], doa, ta, 2 * ta, f"fox_dq_{l}",
                                     host=(with_dq, [False] * len(with_dq)))
        with_dkv = [g[n] for n in REST_DKV] + list(dkv_src or [])
        (dk, dv, cs), land_dkv = _foxa_dkv(qb, sv["ka"], sv["va"], doa, 2 * ta, ta, f"fox_dkv_{l}",
                                           host=(with_dkv, [False] * len(with_dkv)))
        by_name = dict(zip(REST_DQ + REST_DKV, land_dq + land_dkv))
        rest_land = [by_name[n] for n in REST]
        dkv_land = land_dkv[len(REST_DKV):]
    else:
        dq, rs = _foxa_dq(qb, sv["ka"], sv["va"], doa, ta, 2 * ta, f"fox_dq_{l}")
        dk, dv, cs = _foxa_dkv(qb, sv["ka"], sv["va"], doa, 2 * ta, ta, f"fox_dkv_{l}")
    dfl, db = _cumsum_bwd(rs, cs, sv["fl"], w["b_forget"], tile, f"cumsum_bwd_{l}")
    g["b_forget"] = db[:, :FOX_HEADS]
    dag, dconv_w, dconv_v = _conv_bwd(sv["ag"], dcat, w["conv_w"], w["conv_b"], w["conv_ln_g"], w["conv_ln_b"], tile,
                                      f"conv_bwd_{l}")
    g["conv_w"] = dconv_w[:CONV_WIDTH]
    g["conv_b"], g["conv_ln_g"], g["conv_ln_b"] = dconv_v[0:1], dconv_v[1:2], dconv_v[2:3]
    dz = [dag, dq, dk, dv, dfl.astype(BF16)]
    dw_in = _matmul_tn(sv["h1"], dz, 512, tk, f"dw_in_{l}")[:, :IN_COLS]
    g["w_in"] = dw_in.reshape(D_MODEL, N_DEV, IN_COLS // N_DEV).transpose(1, 0, 2).astype(BF16)
    res = _mm_prenorm_bwd(dz, w["w_in_cat"], sv["x0"], w["norm_mix_pre"], dx1, tile, f"mix_in_bwd_{l}",
                          host=([g["w_in"]], [False]) if scatter_w_in else None)
    (dx0, g["norm_mix_pre"]), w_in_land = res if scatter_w_in else (res, None)
    return dx0, g, rest_land, dkv_land, w_in_land


def _sum_blocks(a, name):
    n, rows, cols = a.shape

    def body(a_ref, o_ref):
        acc = a_ref[0]
        for j in range(1, n):
            acc = acc + a_ref[j]
        o_ref[...] = acc

    return pl.pallas_call(
        body, name=name, in_specs=[_full((n, rows, cols))], out_specs=_full((rows, cols)),
        out_shape=jax.ShapeDtypeStruct((rows, cols), F32), grid=(1,),
    )(a)


def _adamw(gparts, w, m, v, tile, name):
    n_l, rows, cols = w.shape
    n = gparts[0].shape[0]
    n_t = rows // tile
    c1 = 1.0 - ADAM_B1
    c2 = 1.0 - ADAM_B2
    bc1 = 1.0 - ADAM_B1 ** ADAM_STEP
    bc2 = 1.0 - ADAM_B2 ** ADAM_STEP

    def body(*refs):
        gp_refs, (w_ref, m_ref, v_ref, g_ref, d_ref, mo_ref, vo_ref) = refs[:n_l], refs[n_l:]
        layer = pl.program_id(0)
        g = None
        for l, gp_ref in enumerate(gp_refs):
            gl = gp_ref[0].astype(F32)
            for j in range(1, n):
                gl = gl + gp_ref[j].astype(F32)
            g = gl if g is None else jnp.where(layer == l, gl, g)
        g_ref[...] = g
        m_new = ADAM_B1 * m_ref[...] + c1 * g
        v_new = ADAM_B2 * v_ref[...] + c2 * (g * g)
        mo_ref[...] = m_new
        vo_ref[...] = v_new
        d_ref[...] = -ADAM_LR * ((m_new / bc1) / (jnp.sqrt(v_new / bc2) + ADAM_EPS) + ADAM_WD * w_ref[...])

    def gp_spec(l):
        return pl.BlockSpec((n, tile, cols), lambda L, i: (0, jnp.where(L < l, 0, jnp.where(L > l, n_t - 1, i)), 0))

    spec = pl.BlockSpec((None, tile, cols), lambda L, i: (L, i, 0))
    return pl.pallas_call(
        body, name=name, grid=(n_l, n_t),
        in_specs=[gp_spec(l) for l in range(n_l)] + [spec, spec, spec],
        out_specs=[spec] * 4, out_shape=[jax.ShapeDtypeStruct((n_l, rows, cols), F32)] * 4,
        compiler_params=_params("arbitrary", "arbitrary"),
    )(*gparts, w, m, v)


def _pack_rows(parts, total_rows):
    flat = [p.reshape(-1, D_MODEL) for p in parts]
    used = sum(f.shape[0] for f in flat)
    if total_rows > used:
        flat.append(jnp.zeros((total_rows - used, D_MODEL), flat[0].dtype))
    return jnp.concatenate(flat, axis=0)


def kernel(x, mem, norm_mix_pre, norm_mix_post, w_in, b_forget, conv_w, conv_b, conv_ln_g, conv_ln_b, w_out, norm_mem_pre, norm_mem_post, norm_memkv, w_mq, w_mk, w_mv, w_mo, norm_mlp_pre, norm_mlp_post, w_up, w_down, loss_target, m_norm_mix_pre, m_norm_mix_post, m_w_in, m_b_forget, m_conv_w, m_conv_b, m_conv_ln_g, m_conv_ln_b, m_w_out, m_norm_mem_pre, m_norm_mem_post, m_norm_memkv, m_w_mq, m_w_mk, m_w_mv, m_w_mo, m_norm_mlp_pre, m_norm_mlp_post, m_w_up, m_w_down, v_norm_mix_pre, v_norm_mix_post, v_w_in, v_b_forget, v_conv_w, v_conv_b, v_conv_ln_g, v_conv_ln_b, v_w_out, v_norm_mem_pre, v_norm_mem_post, v_norm_memkv, v_w_mq, v_w_mk, v_w_mv, v_w_mo, v_norm_mlp_pre, v_norm_mlp_post, v_w_up, v_w_down):
    p = dict(locals())
    names = ("norm_mix_pre", "norm_mix_post", "w_in", "b_forget", "conv_w", "conv_b", "conv_ln_g", "conv_ln_b", "w_out",
             "norm_mem_pre", "norm_mem_post", "norm_memkv", "w_mq", "w_mk", "w_mv", "w_mo", "norm_mlp_pre",
             "norm_mlp_post", "w_up", "w_down")
    me = 4 * lax.axis_index("x") + 2 * lax.axis_index("y") + lax.axis_index("c")
    conv_cols = conv_w.shape[2]
    s_len = x.shape[1]
    bf = {n: p[n].astype(BF16) for n in BIG}

    conv_pack = _pack_rows([jnp.pad(conv_w, ((0, 0), (0, CONV_PAD - CONV_WIDTH), (0, 0)))], 8)
    win_land, conv_land = _exchange([bf["w_in"][0], conv_pack], [True, True], "gather_first")
    conv_rows = DEPTH * CONV_PAD * conv_cols // D_MODEL
    conv_full = conv_land[:, :conv_rows].reshape(N_DEV, DEPTH, CONV_PAD, conv_cols)
    conv_full = conv_full.transpose(1, 2, 0, 3).reshape(DEPTH, CONV_PAD, N_DEV * conv_cols)
    b_forget_pad = jnp.pad(b_forget, ((0, 0), (0, LANES - FOX_HEADS)))

    def first_weights(l, land):
        w = {"w_in_cat": _w_in_cat(land), "conv_w": conv_full[l], "b_forget": b_forget_pad[l:l + 1]}
        for n in VEC[:-1]:
            w[n] = p[n][l:l + 1]
        return w

    h, sv0, w0, win1_land = _layer_fwd(x[0], mem[0], first_weights(0, win_land), 0,
                                       rest_src=[bf[n][0] for n in REST], next_src=[bf["w_in"][1]])
    h, sv1, w1, _ = _layer_fwd(h, mem[0], first_weights(1, win1_land[0]), 1, rest_src=[bf[n][1] for n in REST])
    dh, loss_row = _loss_head(h, loss_target[0], min(512, s_len), "loss_head")
    dh, g1, rest_g1, _, _ = _layer_bwd(dh, mem[0], w1, sv1, 1, scatter_rest=True)
    grad_x, g0, rest_g0, win1_g, win0_g = _layer_bwd(dh, mem[0], w0, sv0, 0, scatter_rest=True, dkv_src=[g1["w_in"]],
                                                     scatter_w_in=True)

    def small_rows(get):
        rows = [jnp.concatenate([get(n) for n in VEC_1024], axis=0),
                jnp.concatenate([get(n) for n in VEC_512], axis=0).reshape(len(VEC_512), D_MODEL),
                jnp.pad(get("b_forget").reshape(1, -1), ((0, 0), (0, D_MODEL - DEPTH * FOX_HEADS)))]
        return jnp.concatenate(rows, axis=0)

    def tap_rows(conv):
        return jnp.pad(conv.reshape(1, -1), ((0, 0), (0, 4 * D_MODEL - conv.size))).reshape(4, D_MODEL)

    n_vec_rows = DEPTH * len(VEC_1024) + len(VEC_512) + 1
    part = small_rows(lambda n: jnp.concatenate([g0[n], g1[n]], axis=0))
    conv_part = jnp.stack([g0["conv_w"], g1["conv_w"]]).reshape(CONV_WIDTH, D_MODEL)
    n_part = 1 + n_vec_rows + CONV_WIDTH
    pad_rows = -n_part % 8
    small_land = _exchange([jnp.concatenate([loss_row, part, conv_part, jnp.zeros((pad_rows, D_MODEL), F32)], axis=0)],
                           [True], "gather_small")[0]
    total = _sum_blocks(small_land, "sum_small")
    loss = jnp.sum(total[0])
    conv_g = total[1 + n_vec_rows:n_part].reshape(DEPTH, CONV_WIDTH, CONV_CH)
    conv_g = lax.dynamic_slice_in_dim(conv_g, me * conv_cols, conv_cols, axis=2)
    fill = jnp.zeros((SMALL_ROWS - n_vec_rows - 4, D_MODEL), F32)

    def small_pack(vec_rows, conv):
        return jnp.concatenate([vec_rows, tap_rows(conv), fill], axis=0)

    small_out = _adamw([small_pack(total[1:1 + n_vec_rows], conv_g)[None]],
                       *[small_pack(small_rows(lambda n: p[pre + n]), p[pre + "conv_w"])[None] for pre in ("", "m_", "v_")],
                       SMALL_ROWS, "adamw_small")

    def unpack_small(buf):
        out = {}
        for k, n in enumerate(VEC_1024):
            out[n] = buf[DEPTH * k:DEPTH * (k + 1)]
        at = DEPTH * len(VEC_1024)
        for k, n in enumerate(VEC_512):
            out[n] = buf[at + k].reshape(DEPTH, CONV_CH)
        at += len(VEC_512)
        out["b_forget"] = buf[at, :DEPTH * FOX_HEADS].reshape(DEPTH, FOX_HEADS)
        out["conv_w"] = buf[at + 1:at + 5].reshape(-1)[:DEPTH * CONV_WIDTH * conv_cols].reshape(DEPTH, CONV_WIDTH, conv_cols)
        return out

    landed = {"w_in": [win0_g[0], win1_g[0]]}
    for i, n in enumerate(REST):
        landed[n] = [rest_g0[i], rest_g1[i]]
    big_out = {n: _adamw(landed[n], p[n], p["m_" + n], p["v_" + n], ADAMW_TILE[n], f"adamw_{n}") for n in BIG}

    result = [loss, grad_x[None]]
    for k in range(4):
        smalls = unpack_small(small_out[k][0])
        result += [big_out[n][k] if n in big_out else smalls[n] for n in names]
    return tuple(result)
```

```python
import functools

import jax
import jax.numpy as jnp
from jax import lax
from jax.experimental import pallas as pl
from jax.experimental.pallas import tpu as pltpu

F32 = jnp.float32
BF16 = jnp.bfloat16

N_DEV = 8
DEPTH = 2
D_MODEL = 1024
CONV_CH = 512
CONV_WIDTH = 31
CONV_PAD = 32
FOX_HEADS = 8
FOX_HEAD_DIM = 64
FOX_WIDTH = 512
N_PAIRS = 4
MEM_HEADS = 4
MEM_HEAD_DIM = 128
MEM_INNER = 512
D_FF = 4096
IN_MAIN = 2560
IN_COLS = 2568
IN_CAT = IN_MAIN + 128
LANES = 128
EPS = 1e-6
NEG_INF = -1e30

ADAM_LR = 0.001
ADAM_B1 = 0.9
ADAM_B2 = 0.999
ADAM_EPS = 1e-08
ADAM_WD = 0.01
ADAM_STEP = 10

NT_DIMS = (((1,), (1,)), ((), ()))
TN_DIMS = (((0,), (0,)), ((), ()))

BIG = ("w_in", "w_out", "w_mq", "w_mk", "w_mv", "w_mo", "w_up", "w_down")
ADAMW_TILE = {"w_in": 256, "w_out": 128, "w_mq": 128, "w_mk": 128, "w_mv": 128, "w_mo": 512, "w_up": 256, "w_down": 128}

VEC_1024 = ("norm_mix_pre", "norm_mix_post", "norm_mem_pre", "norm_mem_post", "norm_memkv", "norm_mlp_pre", "norm_mlp_post")
VEC_512 = ("conv_b", "conv_ln_g", "conv_ln_b")
VEC = VEC_1024 + VEC_512 + ("b_forget",)
SMALL_ROWS = 32


def _sigmoid(x):
    return 1.0 / (1.0 + jnp.exp(-x))


def _rms(x, g):
    r = lax.rsqrt(jnp.mean(x * x, axis=-1, keepdims=True) + EPS)
    return x * r * g


def _rms_bwd(x, g, dh):
    r = lax.rsqrt(jnp.mean(x * x, axis=-1, keepdims=True) + EPS)
    gh = dh * g
    c = jnp.mean(gh * x, axis=-1, keepdims=True)
    dx = r * gh - x * (r * r * r * c)
    dg = jnp.sum(dh * (x * r), axis=0, keepdims=True)
    return dx, dg


def _full(shape):
    nd = len(shape)
    return pl.BlockSpec(shape, lambda *_: (0,) * nd)


def _params(*sem):
    return pltpu.CompilerParams(dimension_semantics=sem)


def _exchange_copies(src_refs, out_refs, same, send_sems, recv_sems, local_sems, with_recvs):
    x, y, c = lax.axis_index("x"), lax.axis_index("y"), lax.axis_index("c")
    me = 4 * x + 2 * y + c
    local, sends, recvs = [], [], []
    for a, (s_ref, o_ref) in enumerate(zip(src_refs, out_refs)):
        def mine(idx, s_ref=s_ref, whole=same[a]):
            return s_ref if whole else s_ref.at[idx]

        local.append(pltpu.make_async_copy(mine(me), o_ref.at[me], local_sems.at[a]))
        for k in range(1, N_DEV):
            px = 1 - x if k & 4 else x
            py = 1 - y if k & 2 else y
            pc = 1 - c if k & 1 else c
            peer = 4 * px + 2 * py + pc
            sem = a * (N_DEV - 1) + k - 1
            common = dict(send_sem=send_sems.at[sem], recv_sem=recv_sems.at[sem], device_id=(px, py, pc),
                          device_id_type=pl.DeviceIdType.MESH)
            sends.append(pltpu.make_async_remote_copy(src_ref=mine(peer), dst_ref=o_ref.at[me], **common))
            if with_recvs:
                recvs.append(pltpu.make_async_remote_copy(src_ref=mine(peer), dst_ref=o_ref.at[peer], **common))
    return local, sends, recvs


def _gather_copies(src_refs, out_refs, send_sems, recv_sems, local_sems, phase):
    x, y, c = lax.axis_index("x"), lax.axis_index("y"), lax.axis_index("c")
    sibling = (x, y, 1 - c)
    chips = [(1 - x, y), (x, 1 - y), (1 - x, 1 - y)]

    def idx(px, py, pc):
        return 4 * px + 2 * py + pc

    local, first, arrive, passed, final = [], [], [], [], []
    for a, (s_ref, o_ref) in enumerate(zip(src_refs, out_refs)):
        def cp(k, src, block, to, a=a, o_ref=o_ref):
            sem = a * (N_DEV - 1) + k
            return pltpu.make_async_remote_copy(src_ref=src, dst_ref=o_ref.at[block], send_sem=send_sems.at[sem],
                                                recv_sem=recv_sems.at[sem], device_id=to, device_id_type=pl.DeviceIdType.MESH)

        me = idx(x, y, c)
        if phase != 1:
            local.append(pltpu.make_async_copy(s_ref, o_ref.at[me], local_sems.at[a]))
            first.append(cp(0, s_ref, me, sibling))
        if phase == 2:
            final.append(cp(0, s_ref, idx(x, y, 1 - c), sibling))
        for j, chip in enumerate(chips):
            theirs = idx(*chip, c)
            if phase != 1:
                first.append(cp(1 + j, s_ref, me, (*chip, c)))
            if phase == 1:
                arrive.append(cp(1 + j, s_ref, theirs, (*chip, c)))
            if phase != 0:
                passed.append(cp(4 + j, o_ref.at[theirs], theirs, sibling))
            if phase == 2:
                final.append(cp(4 + j, s_ref, idx(*chip, 1 - c), sibling))
    return local, first, arrive, passed, final


def _pcall(body, *, name, grid, in_specs, out_specs, out_shape, args, scratch_shapes=(), sem=(), host=None):
    if host is None:
        return pl.pallas_call(body, name=name, grid=grid, in_specs=in_specs, out_specs=out_specs, out_shape=out_shape,
                              scratch_shapes=list(scratch_shapes), compiler_params=_params(*sem))(*args)
    srcs, same = host
    n_in, n_out, n_scr, n_h = len(in_specs), len(out_specs), len(scratch_shapes), len(srcs)
    hbm = pl.BlockSpec(memory_space=pltpu.HBM)
    lands = [jax.ShapeDtypeStruct((N_DEV,) + (s.shape if whole else s.shape[1:]), s.dtype) for s, whole in zip(srcs, same)]

    def wrapped(*refs):
        ins, src_refs = refs[:n_in], refs[n_in:n_in + n_h]
        outs = refs[n_in + n_h:n_in + n_h + n_out]
        land_refs = refs[n_in + n_h + n_out:n_in + 2 * n_h + n_out]
        scr = refs[n_in + 2 * n_h + n_out:n_in + 2 * n_h + n_out + n_scr]
        sems = refs[n_in + 2 * n_h + n_out + n_scr:]
        ids = [pl.program_id(d) for d in range(len(grid))]
        first = functools.reduce(jnp.logical_and, [i == 0 for i in ids])
        last = functools.reduce(jnp.logical_and, [i == n - 1 for i, n in zip(ids, grid)])
        later = functools.reduce(jnp.logical_and, [ids[0] == (3 * grid[0]) // 4] + [i == 0 for i in ids[1:]])

        if all(same):
            @pl.when(first)
            def _():
                local, sends, _, _, _ = _gather_copies(src_refs, land_refs, *sems, 0)
                for cp in local + sends:
                    cp.start()

            body(*ins, *outs, *scr)

            @pl.when(later)
            def _():
                _, _, arrive, passed, _ = _gather_copies(src_refs, land_refs, *sems, 1)
                for cp in arrive:
                    cp.wait_recv()
                for cp in passed:
                    cp.start()

            @pl.when(last)
            def _():
                local, sends, _, passed, final = _gather_copies(src_refs, land_refs, *sems, 2)
                for cp in final:
                    cp.wait_recv()
                for cp in sends + passed:
                    cp.wait_send()
                for cp in local:
                    cp.wait()
        else:
            @pl.when(first)
            def _():
                local, sends, _ = _exchange_copies(src_refs, land_refs, same, *sems, False)
                for cp in local + sends:
                    cp.start()

            body(*ins, *outs, *scr)

            @pl.when(last)
            def _():
                local, sends, recvs = _exchange_copies(src_refs, land_refs, same, *sems, True)
                for cp in recvs:
                    cp.wait_recv()
                for cp in sends:
                    cp.wait_send()
                for cp in local:
                    cp.wait()

    n_sem = n_h * (N_DEV - 1)
    res = pl.pallas_call(
        wrapped, name=name, grid=grid, in_specs=list(in_specs) + [hbm] * n_h, out_specs=list(out_specs) + [hbm] * n_h,
        out_shape=list(out_shape) + lands,
        scratch_shapes=list(scratch_shapes) + [pltpu.SemaphoreType.DMA((n_sem,)), pltpu.SemaphoreType.DMA((n_sem,)),
                                               pltpu.SemaphoreType.DMA((n_h,))],
        compiler_params=_params(*(("arbitrary",) * len(grid))),
    )(*args, *srcs)
    return list(res[:n_out]), list(res[n_out:])


def _exchange(srcs, same, name):
    def body():
        pass

    return _pcall(body, name=name, grid=(1,), in_specs=[], out_specs=[], out_shape=[], args=[], host=(srcs, same))[1]


def _rms_matmul(x, g, w, segs, tile, name, host=None):
    s_len, d = x.shape
    n = w.shape[1]
    chunk = 512

    def body(x_ref, g_ref, w_ref, h_ref, *outs):
        h = _rms(x_ref[...], g_ref[...]).astype(BF16)
        h_ref[...] = h
        oi = 0
        for c0, c1, fns in segs:
            for a in range(c0, c1, chunk):
                b = min(a + chunk, c1)
                z = jnp.dot(h, w_ref[:, a:b], preferred_element_type=F32)
                for k, (dt, fn) in enumerate(fns):
                    outs[oi + k][:, a - c0:b - c0] = fn(z).astype(dt)
            oi += len(fns)

    out_shape = [jax.ShapeDtypeStruct((s_len, d), BF16)]
    out_specs = [pl.BlockSpec((tile, d), lambda i: (i, 0))]
    for c0, c1, fns in segs:
        for dt, _ in fns:
            out_shape.append(jax.ShapeDtypeStruct((s_len, c1 - c0), dt))
            out_specs.append(pl.BlockSpec((tile, c1 - c0), lambda i: (i, 0)))
    return _pcall(
        body, name=name, grid=(s_len // tile,),
        in_specs=[pl.BlockSpec((tile, d), lambda i: (i, 0)), _full((1, d)), _full((d, n))],
        out_specs=out_specs, out_shape=out_shape, args=[x, g, w], sem=("parallel",), host=host)


def _matmul_resnorm(a, w, x, g, tile, name):
    s_len, k = a.shape
    d = w.shape[1]

    def body(a_ref, w_ref, x_ref, g_ref, y_ref, xo_ref):
        y = jnp.dot(a_ref[...], w_ref[...], preferred_element_type=F32)
        y_ref[...] = y
        xo_ref[...] = x_ref[...] + _rms(y, g_ref[...])

    row = lambda i: (i, 0)
    return pl.pallas_call(
        body, name=name, grid=(s_len // tile,),
        in_specs=[pl.BlockSpec((tile, k), row), _full((k, d)), pl.BlockSpec((tile, d), row), _full((1, d))],
        out_specs=[pl.BlockSpec((tile, d), row), pl.BlockSpec((tile, d), row)],
        out_shape=[jax.ShapeDtypeStruct((s_len, d), F32), jax.ShapeDtypeStruct((s_len, d), F32)],
        compiler_params=_params("parallel"),
    )(a, w, x, g)


def _resnorm_bwd_mm(dx, y, g, w, tile, name, out_dtype, pre=None):
    s_len, d = dx.shape
    k = w.shape[0]
    chunk = 512

    def body(*refs):
        if pre is None:
            dx_ref, y_ref, g_ref, w_ref, dy_ref, da_ref, dg_ref = refs
        else:
            dx_ref, y_ref, g_ref, w_ref, pre_ref, dy_ref, da_ref, dg_ref = refs
        dy, dg = _rms_bwd(y_ref[...], g_ref[...], dx_ref[...])
        dyb = dy.astype(BF16)
        dy_ref[...] = dyb

        @pl.when(pl.program_id(0) == 0)
        def _():
            dg_ref[...] = jnp.zeros_like(dg_ref)

        dg_ref[...] += dg
        for a in range(0, k, chunk):
            b = min(a + chunk, k)
            da = lax.dot_general(dyb, w_ref[a:b, :], NT_DIMS, preferred_element_type=F32)
            if pre is not None:
                da = da * (2.0 * jnp.maximum(pre_ref[:, a:b].astype(F32), 0.0))
            da_ref[:, a:b] = da.astype(out_dtype)

    row = lambda i: (i, 0)
    in_specs = [pl.BlockSpec((tile, d), row), pl.BlockSpec((tile, d), row), _full((1, d)), _full((k, d))]
    args = [dx, y, g, w]
    if pre is not None:
        in_specs.append(pl.BlockSpec((tile, k), row))
        args.append(pre)
    return pl.pallas_call(
        body, name=name, grid=(s_len // tile,), in_specs=in_specs,
        out_specs=[pl.BlockSpec((tile, d), row), pl.BlockSpec((tile, k), row), _full((1, d))],
        out_shape=[jax.ShapeDtypeStruct((s_len, d), BF16), jax.ShapeDtypeStruct((s_len, k), out_dtype),
                   jax.ShapeDtypeStruct((1, d), F32)],
        compiler_params=_params("arbitrary"),
    )(*args)


def _mm_prenorm_bwd(dz, w, x, g, dres, tile, name, host=None):
    pieces = list(dz) if isinstance(dz, (list, tuple)) else [dz]
    n_p = len(pieces)
    widths = [p.shape[1] for p in pieces]
    s_len, n = pieces[0].shape[0], sum(widths)
    d = w.shape[0]

    def body(*refs):
        dz_refs, rest = refs[:n_p], refs[n_p:]
        if dres is None:
            w_ref, x_ref, g_ref, dx_ref, dg_ref = rest
        else:
            w_ref, x_ref, g_ref, dres_ref, dx_ref, dg_ref = rest
        dh, off = None, 0
        for dz_ref, width in zip(dz_refs, widths):
            part = lax.dot_general(dz_ref[...], w_ref[:, off:off + width], NT_DIMS, preferred_element_type=F32)
            dh = part if dh is None else dh + part
            off += width
        dx, dg = _rms_bwd(x_ref[...], g_ref[...], dh)
        if dres is not None:
            dx = dx + dres_ref[...]
        dx_ref[...] = dx

        @pl.when(pl.program_id(0) == 0)
        def _():
            dg_ref[...] = jnp.zeros_like(dg_ref)

        dg_ref[...] += dg

    row = lambda i: (i, 0)
    in_specs = [pl.BlockSpec((tile, width), row) for width in widths]
    in_specs += [_full((d, n)), pl.BlockSpec((tile, d), row), _full((1, d))]
    args = pieces + [w, x, g]
    if dres is not None:
        in_specs.append(pl.BlockSpec((tile, d), row))
        args.append(dres)
    return _pcall(
        body, name=name, grid=(s_len // tile,), in_specs=in_specs,
        out_specs=[pl.BlockSpec((tile, d), row), _full((1, d))],
        out_shape=[jax.ShapeDtypeStruct((s_len, d), F32), jax.ShapeDtypeStruct((1, d), F32)],
        args=args, sem=("arbitrary",), host=host)


def _matmul_tn(a, bs, tm, tk, name):
    s_len, m = a.shape
    widths = [b.shape[1] for b in bs]
    n = sum(widths)

    def body(a_ref, *refs):
        b_refs, o_ref = refs[:-1], refs[-1]

        @pl.when(pl.program_id(1) == 0)
        def _():
            o_ref[...] = jnp.zeros_like(o_ref)

        a_tile = a_ref[...]
        off = 0
        for b_ref, width in zip(b_refs, widths):
            o_ref[:, off:off + width] += lax.dot_general(a_tile, b_ref[...], TN_DIMS, preferred_element_type=F32)
            off += width

    return pl.pallas_call(
        body, name=name, grid=(m // tm, s_len // tk),
        in_specs=[pl.BlockSpec((tk, tm), lambda i, k: (k, i))]
        + [pl.BlockSpec((tk, width), lambda i, k: (k, 0)) for width in widths],
        out_specs=pl.BlockSpec((tm, n), lambda i, k: (i, 0)),
        out_shape=jax.ShapeDtypeStruct((m, n), F32),
        compiler_params=_params("parallel", "arbitrary"),
    )(a, *bs)


def _matmul_tn_shards(a, b, axis, tk, name):
    s_len, m = a.shape
    n = b.shape[1]
    r, c = (m // N_DEV, n) if axis == 0 else (m, n // N_DEV)
    n_k = s_len // tk
    tm = max(r, min(m, (1 << 20) // n)) if axis == 0 else min(m, (1 << 20) // n)

    def body(a_ref, b_ref, o_ref, acc):
        k = pl.program_id(1)

        @pl.when(k == 0)
        def _():
            acc[...] = jnp.zeros_like(acc)

        acc[...] += lax.dot_general(a_ref[...], b_ref[...], TN_DIMS, preferred_element_type=F32)

        @pl.when(k == n_k - 1)
        def _():
            if axis == 0:
                o_ref[...] = acc[...].reshape(tm // r, r, c).astype(BF16)
            else:
                for j in range(N_DEV):
                    o_ref[j] = acc[:, j * c:(j + 1) * c].astype(BF16)

    if axis == 0:
        out_spec = pl.BlockSpec((tm // r, r, c), lambda i, k: (i, 0, 0))
    else:
        out_spec = pl.BlockSpec((N_DEV, tm, c), lambda i, k: (0, i, 0))
    return pl.pallas_call(
        body, name=name, grid=(m // tm, n_k),
        in_specs=[pl.BlockSpec((tk, tm), lambda i, k: (k, i)), pl.BlockSpec((tk, n), lambda i, k: (k, 0))],
        out_specs=out_spec, out_shape=jax.ShapeDtypeStruct((N_DEV, r, c), BF16),
        scratch_shapes=[pltpu.VMEM((tm, n), F32)],
        compiler_params=_params("parallel", "arbitrary"),
    )(a, b)


def _cumsum_bwd(rs, cs, fl, b, tile, name):
    s_len = fl.shape[0]
    n_t = s_len // tile

    def body(rs_ref, cs_ref, fl_ref, b_ref, dfl_ref, db_ref, carry):
        @pl.when(pl.program_id(0) == 0)
        def _():
            carry[...] = jnp.zeros_like(carry)
            db_ref[...] = jnp.zeros_like(db_ref)

        r = lax.broadcasted_iota(jnp.int32, (tile, tile), 0)
        c = lax.broadcasted_iota(jnp.int32, (tile, tile), 1)
        tri = (c >= r).astype(F32)
        lane = lax.broadcasted_iota(jnp.int32, (1, LANES), 1)
        dc = jnp.zeros((tile, LANES), F32)
        for p in range(N_PAIRS):
            blk = rs_ref[:, LANES * p:LANES * (p + 1)] - cs_ref[:, LANES * p:LANES * (p + 1)]
            dc = jnp.where(lane == 2 * p, blk, dc)
            dc = jnp.where(lane == 2 * p + 1, pltpu.roll(blk, FOX_HEAD_DIM, axis=1), dc)
        dl = jnp.dot(tri, dc, precision=lax.Precision.HIGHEST, preferred_element_type=F32) + carry[...]
        carry[...] = dl[0:1, :]
        dfl = dl * _sigmoid(-(fl_ref[...] + b_ref[...]))
        dfl_ref[...] = dfl.astype(BF16)
        db_ref[...] += jnp.sum(dfl, axis=0, keepdims=True)

    rev = lambda i: (n_t - 1 - i, 0)
    return pl.pallas_call(
        body, name=name, grid=(n_t,),
        in_specs=[pl.BlockSpec((tile, FOX_WIDTH), rev), pl.BlockSpec((tile, FOX_WIDTH), rev), pl.BlockSpec((tile, LANES), rev),
                  _full((1, LANES))],
        out_specs=[pl.BlockSpec((tile, LANES), rev), _full((1, LANES))],
        out_shape=[jax.ShapeDtypeStruct((s_len, LANES), BF16), jax.ShapeDtypeStruct((1, LANES), F32)],
        scratch_shapes=[pltpu.VMEM((1, LANES), F32)],
        compiler_params=_params("arbitrary"),
    )(rs, cs, fl, b)


SUBLANES = 8
CONV_ROWS = 64


def _phase_copies(src, dst, rows):
    for p in range(SUBLANES):
        dst[p] = src[pl.ds(p, rows), :]


def _phase_rows(extp_ref, off, r0, rows):
    p = off % SUBLANES
    return extp_ref[p, pl.ds(pl.multiple_of(r0 + (off - p), SUBLANES), rows), :]


def _conv_taps(w_ref, extp_ref, base, r0, rows, reverse):
    acc = None
    for k in range(CONV_WIDTH):
        off = base + ((CONV_WIDTH - 1 - k) if reverse else k)
        term = w_ref[k:k + 1, :] * _phase_rows(extp_ref, off, r0, rows)
        acc = term if acc is None else acc + term
    return acc


def _fold_rows(x):
    out = x[0:SUBLANES]
    for i in range(1, x.shape[0] // SUBLANES):
        out = out + x[i * SUBLANES:(i + 1) * SUBLANES]
    return out


def _conv_fwd(ag, w, cb, lg, lb, tile, name):
    s_len = ag.shape[0]
    c = CONV_CH
    rb = tile

    def body(ag_ref, w_ref, cb_ref, lg_ref, lb_ref, u_ref, ext, extp):
        @pl.when(pl.program_id(0) == 0)
        def _():
            ext[0:CONV_PAD, :] = jnp.zeros((CONV_PAD, c), F32)
            ext[tile + CONV_PAD:tile + CONV_PAD + SUBLANES, :] = jnp.zeros((SUBLANES, c), F32)

        ext[CONV_PAD:CONV_PAD + tile, :] = ag_ref[:, 0:c] * _sigmoid(ag_ref[:, c:2 * c])
        _phase_copies(ext, extp, tile + CONV_PAD)

        def block(b, carry):
            r0 = pl.multiple_of(b * rb, rb)
            u1 = _conv_taps(w_ref, extp, CONV_PAD - (CONV_WIDTH - 1), r0, rb, False) + cb_ref[...]
            mu = jnp.mean(u1, axis=-1, keepdims=True)
            xc = u1 - mu
            y = xc * lax.rsqrt(jnp.mean(xc * xc, axis=-1, keepdims=True) + EPS) * lg_ref[...] + lb_ref[...]
            u_ref[pl.ds(r0, rb), :] = (y * _sigmoid(y)).astype(BF16)
            return carry

        lax.fori_loop(0, tile // rb, block, 0)
        ext[0:CONV_PAD, :] = ext[tile:tile + CONV_PAD, :]

    return pl.pallas_call(
        body, name=name, grid=(s_len // tile,),
        in_specs=[pl.BlockSpec((tile, 2 * c), lambda i: (i, 0)), _full((CONV_PAD, c)), _full((1, c)), _full((1, c)),
                  _full((1, c))],
        out_specs=pl.BlockSpec((tile, c), lambda i: (i, 0)),
        out_shape=jax.ShapeDtypeStruct((s_len, c), BF16),
        scratch_shapes=[pltpu.VMEM((tile + CONV_PAD + SUBLANES, c), F32), pltpu.VMEM((SUBLANES, tile + CONV_PAD, c), F32)],
        compiler_params=_params("arbitrary"),
    )(ag, w, cb, lg, lb)


def _conv_bwd(ag, dcat, w, cb, lg, lb, tile, name):
    s_len = ag.shape[0]
    c = CONV_CH
    n_t = s_len // tile
    per = tile // CONV_PAD
    rb = min(CONV_ROWS, tile)

    def body(ag_ref, halo_ref, du_ref, w_ref, cb_ref, lg_ref, lb_ref, dag_ref, dw_ref, dv_ref, ext, ext2, extp, dwacc):
        i = pl.program_id(0)
        t = n_t - 1 - i

        @pl.when(i == 0)
        def _():
            ext2[tile:tile + CONV_PAD + SUBLANES, :] = jnp.zeros((CONV_PAD + SUBLANES, c), F32)
            ext[tile + CONV_PAD:tile + CONV_PAD + SUBLANES, :] = jnp.zeros((SUBLANES, c), F32)
            dwacc[...] = jnp.zeros_like(dwacc)
            dv_ref[...] = jnp.zeros_like(dv_ref)

        halo = halo_ref[:, 0:c] * _sigmoid(halo_ref[:, c:2 * c])
        ext[0:CONV_PAD, :] = jnp.where(t > 0, halo, 0.0)
        ext[CONV_PAD:CONV_PAD + tile, :] = ag_ref[:, 0:c] * _sigmoid(ag_ref[:, c:2 * c])
        _phase_copies(ext, extp, tile + CONV_PAD)
        base = CONV_PAD - (CONV_WIDTH - 1)

        def block1(b, carry):
            r0 = pl.multiple_of(b * rb, rb)
            u1 = _conv_taps(w_ref, extp, base, r0, rb, False) + cb_ref[...]
            mu = jnp.mean(u1, axis=-1, keepdims=True)
            xc = u1 - mu
            rs = lax.rsqrt(jnp.mean(xc * xc, axis=-1, keepdims=True) + EPS)
            xhat = xc * rs
            y = xhat * lg_ref[...] + lb_ref[...]
            sy = _sigmoid(y)
            dy = du_ref[pl.ds(r0, rb), :] * (sy * (1.0 + y * (1.0 - sy)))
            dxh = dy * lg_ref[...]
            du1 = rs * (dxh - jnp.mean(dxh, axis=-1, keepdims=True) - xhat * jnp.mean(dxh * xhat, axis=-1, keepdims=True))
            dv_ref[0:1, :] += jnp.sum(du1, axis=0, keepdims=True)
            dv_ref[1:2, :] += jnp.sum(dy * xhat, axis=0, keepdims=True)
            dv_ref[2:3, :] += jnp.sum(dy, axis=0, keepdims=True)
            for k in range(CONV_WIDTH):
                dwacc[k] += _fold_rows(du1 * _phase_rows(extp, base + k, r0, rb))
            ext2[pl.ds(r0, rb), :] = du1
            return carry

        lax.fori_loop(0, tile // rb, block1, 0)
        _phase_copies(ext2, extp, tile + CONV_PAD)

        def block2(b, carry):
            r0 = pl.multiple_of(b * rb, rb)
            du0 = _conv_taps(w_ref, extp, 0, r0, rb, True)
            a = ag_ref[pl.ds(r0, rb), 0:c]
            sg = _sigmoid(ag_ref[pl.ds(r0, rb), c:2 * c])
            dag_ref[pl.ds(r0, rb), 0:c] = (du0 * sg).astype(BF16)
            dag_ref[pl.ds(r0, rb), c:2 * c] = (du0 * a * sg * (1.0 - sg)).astype(BF16)
            return carry

        lax.fori_loop(0, tile // rb, block2, 0)
        ext2[tile:tile + CONV_PAD, :] = ext2[0:CONV_PAD, :]

        @pl.when(i == n_t - 1)
        def _():
            for k in range(CONV_WIDTH):
                dw_ref[k:k + 1, :] = jnp.sum(dwacc[k], axis=0, keepdims=True)
            dw_ref[CONV_WIDTH:CONV_PAD, :] = jnp.zeros((CONV_PAD - CONV_WIDTH, c), F32)

    rev = lambda i: (n_t - 1 - i, 0)
    return pl.pallas_call(
        body, name=name, grid=(n_t,),
        in_specs=[pl.BlockSpec((tile, 2 * c), rev),
                  pl.BlockSpec((CONV_PAD, 2 * c), lambda i: (jnp.maximum((n_t - 1 - i) * per - 1, 0), 0)),
                  pl.BlockSpec((tile, c), rev), _full((CONV_PAD, c)), _full((1, c)), _full((1, c)), _full((1, c))],
        out_specs=[pl.BlockSpec((tile, 2 * c), rev), _full((CONV_PAD, c)), _full((8, c))],
        out_shape=[jax.ShapeDtypeStruct((s_len, 2 * c), BF16), jax.ShapeDtypeStruct((CONV_PAD, c), F32),
                   jax.ShapeDtypeStruct((8, c), F32)],
        scratch_shapes=[pltpu.VMEM((tile + CONV_PAD + SUBLANES, c), F32), pltpu.VMEM((tile + CONV_PAD + SUBLANES, c), F32),
                        pltpu.VMEM((SUBLANES, tile + CONV_PAD, c), F32), pltpu.VMEM((CONV_PAD, SUBLANES, c), F32)],
        compiler_params=_params("arbitrary"),
    )(ag, ag, dcat, w, cb, lg, lb)


Q_SCALE = FOX_HEAD_DIM ** -0.5


def _head_col(x, lane, h):
    return jnp.sum(jnp.where(lane == h, x, 0.0), axis=1, keepdims=True)


def _split3(x):
    hi = x.astype(BF16).astype(F32)
    r = x - hi
    mid = r.astype(BF16).astype(F32)
    lo = (r - mid).astype(BF16).astype(F32)
    return hi, mid, lo


def _in_lanes(lane, lo, n):
    return (lane >= lo) & (lane < lo + n)


def _put3(lane, lo, parts, rest):
    return jnp.where(lane == lo, parts[0], jnp.where(lane == lo + 1, parts[1], jnp.where(lane == lo + 2, parts[2], rest)))


def _spare_lane(h):
    return FOX_HEAD_DIM if h % 2 == 0 else 0


def _shift_div(i, num, den):
    return i * (num // den) if num >= den else lax.shift_right_logical(i, (den // num).bit_length() - 1)


def _foxa_prep(qkv, fl, b, tile, name):
    s_len = qkv.shape[0]

    def body(q_ref, k_ref, v_ref, fl_ref, b_ref, qa_ref, ka_ref, va_ref, carry):
        @pl.when(pl.program_id(0) == 0)
        def _():
            carry[...] = jnp.zeros_like(carry)

        xx = fl_ref[...] + b_ref[...]
        lf = jnp.minimum(xx, 0.0) - jnp.log1p(jnp.exp(-jnp.abs(xx)))
        tri = (lax.broadcasted_iota(jnp.int32, (tile, tile), 1) <= lax.broadcasted_iota(jnp.int32, (tile, tile), 0))
        cum_t = jnp.dot(tri.astype(F32), lf, precision=lax.Precision.HIGHEST, preferred_element_type=F32) + carry[...]
        carry[...] = cum_t[tile - 1:tile, :]
        lane = lax.broadcasted_iota(jnp.int32, (1, LANES), 1)
        for h in range(FOX_HEADS):
            e = _spare_lane(h)
            head = ~_in_lanes(lane, e, FOX_HEAD_DIM)
            blk = slice(LANES * (h // 2), LANES * (h // 2) + LANES)
            out = slice(LANES * h, LANES * h + LANES)
            c3 = _split3(_head_col(cum_t, lane, h))
            ex_q = _put3(lane, e, c3, jnp.where(_in_lanes(lane, e + 3, 3), 1.0, 0.0))
            qa_ref[:, out] = jnp.where(head, q_ref[:, blk].astype(F32) * Q_SCALE, ex_q).astype(BF16)
            ones = jnp.where(_in_lanes(lane, e, 3) | _in_lanes(lane, e + 6, 3), 1.0, 0.0)
            ex_k = _put3(lane, e + 3, [-c for c in c3], ones)
            ka_ref[:, out] = jnp.where(head, k_ref[:, blk].astype(F32), ex_k).astype(BF16)
            ex_v = jnp.where(_in_lanes(lane, e, 3), 1.0, 0.0)
            va_ref[:, out] = jnp.where(head, v_ref[:, blk].astype(F32), ex_v).astype(BF16)

    col = lambda c: pl.BlockSpec((tile, FOX_WIDTH), lambda i: (i, c))
    wide = pl.BlockSpec((tile, 2 * FOX_WIDTH), lambda i: (i, 0))
    return pl.pallas_call(
        body, name=name, grid=(s_len // tile,),
        in_specs=[col(0), col(1), col(2), pl.BlockSpec((tile, LANES), lambda i: (i, 0)), _full((1, LANES))],
        out_specs=[wide, wide, wide], out_shape=[jax.ShapeDtypeStruct((s_len, 2 * FOX_WIDTH), BF16)] * 3,
        scratch_shapes=[pltpu.VMEM((1, LANES), F32)],
        compiler_params=_params("arbitrary"),
    )(qkv, qkv, qkv, fl, b)


def _foxa_fwd(qa, ka, va, tq, tk, name, host=None):
    s_len = qa.shape[0]

    def body(q_ref, k_ref, v_ref, o_ref, lse_ref, ob_ref):
        i = pl.program_id(1)
        lane = lax.broadcasted_iota(jnp.int32, (1, LANES), 1)
        cols = [slice(LANES * hh, LANES * hh + LANES) for hh in range(2)]
        qh = [q_ref[:, c] for c in cols]

        def scores(j):
            off = pl.multiple_of(j * tk, tk)
            return [lax.dot_general(qh[hh], k_ref[pl.ds(off, tk), cols[hh]], NT_DIMS, preferred_element_type=F32)
                    for hh in range(2)]

        def update(j, s, m, acc, mask):
            off = pl.multiple_of(j * tk, tk)
            m_out, acc_out = [], []
            for hh in range(2):
                sh = s[hh] if mask is None else jnp.where(mask, s[hh], NEG_INF)
                m_new = jnp.maximum(m[hh], jnp.max(sh, axis=1, keepdims=True))
                pr = jnp.exp(sh - m_new).astype(BF16)
                acc_out.append(jnp.exp(m[hh] - m_new) * acc[hh]
                               + jnp.dot(pr, v_ref[pl.ds(off, tk), cols[hh]], preferred_element_type=F32))
                m_out.append(m_new)
            return m_out, acc_out

        def step(j, carry):
            s_next = scores(j + 1)
            m, acc = update(j, carry[0:2], carry[2:4], carry[4:6], None)
            return (*s_next, *m, *acc)

        n_full = _shift_div(i, tq, tk)
        n_part = max(tq // tk, 1)
        qi = lax.broadcasted_iota(jnp.int32, (tq, tk), 0) + i * tq
        ki = lax.broadcasted_iota(jnp.int32, (tq, tk), 1)
        carry = (*scores(0), *([jnp.full((tq, 1), NEG_INF, F32)] * 2), *([jnp.zeros((tq, LANES), F32)] * 2))
        carry = lax.fori_loop(0, n_full, step, carry)
        s, m, acc = list(carry[0:2]), list(carry[2:4]), list(carry[4:6])
        for jj in range(n_part):
            s_next = scores(n_full + jj + 1) if jj < n_part - 1 else None
            m, acc = update(n_full + jj, s, m, acc, ki + (n_full + jj) * tk <= qi)
            s = s_next
        res = []
        for hh in range(2):
            l = acc[hh][:, _spare_lane(hh):_spare_lane(hh) + 1]
            res.append((acc[hh] / l, m[hh] + jnp.log(l)))
        low = lane < FOX_HEAD_DIM
        o_pair = jnp.where(low, res[0][0], res[1][0])
        o_ref[...] = o_pair
        ob_ref[...] = o_pair.astype(BF16)
        lse_ref[...] = jnp.where(low, res[0][1], res[1][1])

    pair = pl.BlockSpec((s_len, 2 * LANES), lambda p, i: (0, p))
    out = pl.BlockSpec((tq, LANES), lambda p, i: (i, p))
    return _pcall(
        body, name=name, grid=(N_PAIRS, s_len // tq),
        in_specs=[pl.BlockSpec((tq, 2 * LANES), lambda p, i: (i, p)), pair, pair],
        out_specs=[out, out, out],
        out_shape=[jax.ShapeDtypeStruct((s_len, FOX_WIDTH), F32)] * 2 + [jax.ShapeDtypeStruct((s_len, FOX_WIDTH), BF16)],
        args=[qa, ka, va], sem=("parallel", "parallel"), host=host)


def _foxa_dq(qa, lse, o, dcat, ka, va, tq, tk, name, host=None):
    s_len = qa.shape[0]

    def body(qa_ref, lse_ref, o_ref, dcat_ref, k_ref, v_ref, dq_ref, rs_ref, q_ref, do_ref):
        i = pl.program_id(1)
        lane = lax.broadcasted_iota(jnp.int32, (1, LANES), 1)
        cols = [slice(LANES * hh, LANES * hh + LANES) for hh in range(2)]
        d_o = dcat_ref[...]
        prod = d_o * o_ref[...]
        for hh in range(2):
            e = _spare_lane(hh)
            head = ~_in_lanes(lane, e, FOX_HEAD_DIM)
            delta = jnp.sum(jnp.where(head, prod, 0.0), axis=1, keepdims=True)
            do_ref[:, cols[hh]] = _put3(lane, e, _split3(-delta), jnp.where(head, d_o, 0.0)).astype(BF16)
            l3 = _split3(-lse_ref[:, FOX_HEAD_DIM - e:FOX_HEAD_DIM - e + 1])
            q_ref[:, cols[hh]] = _put3(lane, e + 6, l3, qa_ref[:, cols[hh]].astype(F32)).astype(BF16)
        qh = [q_ref[:, c] for c in cols]
        doh = [do_ref[:, c] for c in cols]

        def update(j, acc, mask):
            off = pl.multiple_of(j * tk, tk)
            out = []
            for hh in range(2):
                kt = k_ref[pl.ds(off, tk), cols[hh]]
                pr = jnp.exp(lax.dot_general(qh[hh], kt, NT_DIMS, preferred_element_type=F32))
                if mask is not None:
                    pr = jnp.where(mask, pr, 0.0)
                ds = pr * lax.dot_general(doh[hh], v_ref[pl.ds(off, tk), cols[hh]], NT_DIMS, preferred_element_type=F32)
                out.append(acc[hh] + jnp.dot(ds.astype(BF16), kt, preferred_element_type=F32))
            return tuple(out)

        n_full = _shift_div(i, tq, tk)
        qi = lax.broadcasted_iota(jnp.int32, (tq, tk), 0) + i * tq
        ki = lax.broadcasted_iota(jnp.int32, (tq, tk), 1)
        acc = lax.fori_loop(0, n_full, lambda j, a: update(j, a, None), (jnp.zeros((tq, LANES), F32),) * 2)
        for jj in range(max(tq // tk, 1)):
            acc = update(n_full + jj, acc, ki + (n_full + jj) * tk <= qi)
        low = lane < FOX_HEAD_DIM
        dq_ref[...] = (jnp.where(low, acc[0], acc[1]) * Q_SCALE).astype(BF16)
        rs_ref[...] = jnp.where(low, acc[0][:, _spare_lane(0):_spare_lane(0) + 1], acc[1][:, _spare_lane(1):_spare_lane(1) + 1])

    pair = pl.BlockSpec((s_len, 2 * LANES), lambda p, i: (0, p))
    tile2 = pl.BlockSpec((tq, 2 * LANES), lambda p, i: (i, p))
    out = pl.BlockSpec((tq, LANES), lambda p, i: (i, p))
    wide = jax.ShapeDtypeStruct((s_len, 2 * FOX_WIDTH), BF16)
    return _pcall(
        body, name=name, grid=(N_PAIRS, s_len // tq),
        in_specs=[tile2, out, out, pl.BlockSpec((tq, LANES), lambda p, i: (i, N_PAIRS + p)), pair, pair],
        out_specs=[out, out, tile2, tile2],
        out_shape=[jax.ShapeDtypeStruct((s_len, FOX_WIDTH), BF16), jax.ShapeDtypeStruct((s_len, FOX_WIDTH), F32), wide, wide],
        args=[qa, lse, o, dcat, ka, va], sem=("parallel", "parallel"), host=host)


def _foxa_dkv(qb, ka, va, doa, tq, tk, name, host=None):
    s_len = qb.shape[0]
    n_q = s_len // tq

    def body(k_ref, v_ref, q_ref, do_ref, dk_ref, dv_ref, cs_ref):
        j = pl.program_id(1)
        lane = lax.broadcasted_iota(jnp.int32, (1, LANES), 1)
        cols = [slice(LANES * hh, LANES * hh + LANES) for hh in range(2)]
        kh = [k_ref[:, c] for c in cols]
        vh = [v_ref[:, c] for c in cols]

        def update(i, acc, mask):
            off = pl.multiple_of(i * tq, tq)
            out = []
            for hh in range(2):
                qt = q_ref[pl.ds(off, tq), cols[hh]]
                dot = do_ref[pl.ds(off, tq), cols[hh]]
                pt = jnp.exp(lax.dot_general(kh[hh], qt, NT_DIMS, preferred_element_type=F32))
                if mask is not None:
                    pt = jnp.where(mask, pt, 0.0)
                dv = acc[2 * hh + 1] + jnp.dot(pt.astype(BF16), dot, preferred_element_type=F32)
                dst = pt * lax.dot_general(vh[hh], dot, NT_DIMS, preferred_element_type=F32)
                out += [acc[2 * hh] + jnp.dot(dst.astype(BF16), qt, preferred_element_type=F32), dv]
            return tuple(out)

        i0 = _shift_div(j, tk, tq)
        n_part = max(tk // tq, 1)
        ki = lax.broadcasted_iota(jnp.int32, (tk, tq), 0) + j * tk
        qi = lax.broadcasted_iota(jnp.int32, (tk, tq), 1)
        acc = (jnp.zeros((tk, LANES), F32),) * 4
        for ii in range(n_part):
            acc = update(i0 + ii, acc, ki <= qi + (i0 + ii) * tq)
        acc = lax.fori_loop(i0 + n_part, n_q, lambda i, a: update(i, a, None), acc)
        low = lane < FOX_HEAD_DIM
        dk_ref[...] = jnp.where(low, acc[0], acc[2]).astype(BF16)
        dv_ref[...] = jnp.where(low, acc[1], acc[3]).astype(BF16)
        cs_ref[...] = jnp.where(low, acc[0][:, _spare_lane(0) + 3:_spare_lane(0) + 4],
                                acc[2][:, _spare_lane(1) + 3:_spare_lane(1) + 4])

    pair = pl.BlockSpec((s_len, 2 * LANES), lambda p, j: (0, p))
    tile2 = pl.BlockSpec((tk, 2 * LANES), lambda p, j: (j, p))
    out = pl.BlockSpec((tk, LANES), lambda p, j: (j, p))
    return _pcall(
        body, name=name, grid=(N_PAIRS, s_len // tk), in_specs=[tile2, tile2, pair, pair], out_specs=[out, out, out],
        out_shape=[jax.ShapeDtypeStruct((s_len, FOX_WIDTH), BF16), jax.ShapeDtypeStruct((s_len, FOX_WIDTH), BF16),
                   jax.ShapeDtypeStruct((s_len, FOX_WIDTH), F32)],
        args=[ka, va, qb, doa], sem=("parallel", "parallel"), host=host)


def _mem_scores_t(q, kv, h):
    lo = h * MEM_HEAD_DIM
    st = lax.dot_general(kv[:, lo:lo + MEM_HEAD_DIM], q[:, lo:lo + MEM_HEAD_DIM], NT_DIMS,
                         preferred_element_type=F32) * (MEM_HEAD_DIM ** -0.5)
    e = jnp.exp(st - jnp.max(st, axis=0, keepdims=True))
    return e / jnp.sum(e, axis=0, keepdims=True)


def _memattn_fwd(q, kv, tile, name):
    s_len = q.shape[0]
    n_mem = kv.shape[0]

    def body(q_ref, kv_ref, o_ref):
        q = q_ref[...]
        kv = kv_ref[...]
        for h in range(MEM_HEADS):
            lo = h * MEM_HEAD_DIM
            pt = _mem_scores_t(q, kv, h).astype(BF16)
            vh = kv[:, MEM_INNER + lo:MEM_INNER + lo + MEM_HEAD_DIM]
            o_ref[:, lo:lo + MEM_HEAD_DIM] = lax.dot_general(pt, vh, TN_DIMS, preferred_element_type=F32).astype(BF16)

    return pl.pallas_call(
        body, name=name, grid=(s_len // tile,),
        in_specs=[pl.BlockSpec((tile, MEM_INNER), lambda i: (i, 0)), _full((n_mem, 2 * MEM_INNER))],
        out_specs=pl.BlockSpec((tile, MEM_INNER), lambda i: (i, 0)),
        out_shape=jax.ShapeDtypeStruct((s_len, MEM_INNER), BF16),
        compiler_params=_params("parallel"),
    )(q, kv)


def _memattn_bwd(q, kv, do, tile, name):
    s_len = q.shape[0]
    n_mem = kv.shape[0]
    scale = MEM_HEAD_DIM ** -0.5

    def body(q_ref, kv_ref, do_ref, dq_ref, dkv_ref):
        @pl.when(pl.program_id(0) == 0)
        def _():
            dkv_ref[...] = jnp.zeros_like(dkv_ref)

        q = q_ref[...]
        kv = kv_ref[...]
        do = do_ref[...]
        for h in range(MEM_HEADS):
            lo = h * MEM_HEAD_DIM
            qh = q[:, lo:lo + MEM_HEAD_DIM]
            kh = kv[:, lo:lo + MEM_HEAD_DIM]
            vh = kv[:, MEM_INNER + lo:MEM_INNER + lo + MEM_HEAD_DIM]
            doh = do[:, lo:lo + MEM_HEAD_DIM]
            pt = _mem_scores_t(q, kv, h)
            dkv_ref[:, MEM_INNER + lo:MEM_INNER + lo + MEM_HEAD_DIM] += jnp.dot(
                pt.astype(BF16), doh, preferred_element_type=F32)
            dpt = lax.dot_general(vh, doh, NT_DIMS, preferred_element_type=F32)
            dst = (pt * (dpt - jnp.sum(pt * dpt, axis=0, keepdims=True)) * scale).astype(BF16)
            dkv_ref[:, lo:lo + MEM_HEAD_DIM] += jnp.dot(dst, qh, preferred_element_type=F32)
            dq_ref[:, lo:lo + MEM_HEAD_DIM] = lax.dot_general(dst, kh, TN_DIMS, preferred_element_type=F32).astype(BF16)

    return pl.pallas_call(
        body, name=name, grid=(s_len // tile,),
        in_specs=[pl.BlockSpec((tile, MEM_INNER), lambda i: (i, 0)), _full((n_mem, 2 * MEM_INNER)),
                  pl.BlockSpec((tile, MEM_INNER), lambda i: (i, 0))],
        out_specs=[pl.BlockSpec((tile, MEM_INNER), lambda i: (i, 0)), _full((n_mem, 2 * MEM_INNER))],
        out_shape=[jax.ShapeDtypeStruct((s_len, MEM_INNER), BF16), jax.ShapeDtypeStruct((n_mem, 2 * MEM_INNER), F32)],
        compiler_params=_params("arbitrary"),
    )(q, kv, do)


def _loss_head(y, target, tile, name):
    s_len, d = y.shape

    def body(y_ref, t_ref, dy_ref, l_ref):
        @pl.when(pl.program_id(0) == 0)
        def _():
            l_ref[...] = jnp.zeros_like(l_ref)

        err = y_ref[...] - t_ref[...]
        dy_ref[...] = err * (1.0 / d)
        l_ref[...] += jnp.sum(err * err, axis=0, keepdims=True) * (0.5 / d)

    row = lambda i: (i, 0)
    return pl.pallas_call(
        body, name=name, grid=(s_len // tile,),
        in_specs=[pl.BlockSpec((tile, d), row), pl.BlockSpec((tile, d), row)],
        out_specs=[pl.BlockSpec((tile, d), row), _full((1, d))],
        out_shape=[jax.ShapeDtypeStruct((s_len, d), F32), jax.ShapeDtypeStruct((1, d), F32)],
        compiler_params=_params("arbitrary"),
    )(y, target)


def _attn_tile(s_len):
    return min(256, s_len // 2)


REST = ("w_out", "w_mq", "w_mk", "w_mv", "w_mo", "w_up", "w_down")
REST_DQ = ("w_up", "w_down")
REST_DKV = ("w_out", "w_mq", "w_mk", "w_mv", "w_mo")
SHARD_AXIS = {"w_in": 1, "w_out": 0, "w_mq": 0, "w_mk": 0, "w_mv": 0, "w_mo": 1, "w_up": 1, "w_down": 0}


def _full_from_shards(sh, axis):
    n, r, c = sh.shape
    if axis == 0:
        return sh.reshape(n * r, c)
    return sh.transpose(1, 0, 2).reshape(r, n * c)


def _rest_weights(lands):
    w = {n: _full_from_shards(sh, SHARD_AXIS[n]) for n, sh in zip(REST, lands)}
    w["w_mkv"] = jnp.concatenate([w.pop("w_mk"), w.pop("w_mv")], axis=1)
    return w


def _w_in_cat(land):
    return jnp.pad(_full_from_shards(land, 1), ((0, 0), (0, IN_CAT - IN_COLS)))


def _layer_fwd(x0, mem, w, l, rest_src=None, next_src=None):
    s_len = x0.shape[0]
    tile = min(512, s_len)
    tile_ff = min(256, s_len)
    ta = _attn_tile(s_len)
    ident = lambda z: z
    sv = {"x0": x0}

    h1, ag, qkv, fl = _rms_matmul(
        x0, w["norm_mix_pre"], w["w_in_cat"],
        [(0, 2 * CONV_CH, [(F32, ident)]), (2 * CONV_CH, IN_MAIN, [(BF16, ident)]), (IN_MAIN, IN_CAT, [(F32, ident)])],
        tile, f"mix_in_{l}")
    u3 = _conv_fwd(ag, w["conv_w"], w["conv_b"], w["conv_ln_g"], w["conv_ln_b"], tile, f"conv_fwd_{l}")
    qa, ka, va = _foxa_prep(qkv, fl, w["b_forget"], tile, f"fox_prep_{l}")
    next_land = None
    if rest_src is None:
        o, lse, o_bf = _foxa_fwd(qa, ka, va, ta, 2 * ta, f"fox_fwd_{l}")
    else:
        srcs = list(rest_src) + list(next_src or [])
        (o, lse, o_bf), lands = _foxa_fwd(qa, ka, va, ta, 2 * ta, f"fox_fwd_{l}", host=(srcs, [True] * len(srcs)))
        w = {**w, **_rest_weights(lands[:len(REST)])}
        next_land = lands[len(REST):]
    cat = jnp.concatenate([u3, o_bf], axis=1)
    y1, x1 = _matmul_resnorm(cat, w["w_out"], x0, w["norm_mix_post"], tile, f"mix_out_{l}")
    sv.update(h1=h1, ag=ag, fl=fl, qa=qa, ka=ka, va=va, o=o, lse=lse, cat=cat, y1=y1, x1=x1)

    h2, qm = _rms_matmul(x1, w["norm_mem_pre"], w["w_mq"], [(0, MEM_INNER, [(BF16, ident)])], tile, f"mem_q_{l}")
    mem_n, kv = _rms_matmul(mem, w["norm_memkv"], w["w_mkv"], [(0, 2 * MEM_INNER, [(BF16, ident)])],
                            mem.shape[0], f"mem_kv_{l}")
    om = _memattn_fwd(qm, kv, tile, f"mem_attn_fwd_{l}")
    y2, x2 = _matmul_resnorm(om, w["w_mo"], x1, w["norm_mem_post"], tile, f"mem_out_{l}")
    sv.update(h2=h2, qm=qm, mem_n=mem_n, kv=kv, om=om, y2=y2, x2=x2)

    relu2 = lambda z: jnp.square(jnp.maximum(z, 0.0))
    h3, pre, hid = _rms_matmul(x2, w["norm_mlp_pre"], w["w_up"], [(0, D_FF, [(BF16, ident), (BF16, relu2)])], tile_ff,
                               f"mlp_up_{l}")
    y3, x3 = _matmul_resnorm(hid, w["w_down"], x2, w["norm_mlp_post"], tile_ff, f"mlp_down_{l}")
    sv.update(h3=h3, pre=pre, hid=hid, y3=y3)
    return x3, sv, w, next_land


def _layer_bwd(dx3, mem, w, sv, l, scatter_rest=False, dkv_src=None, scatter_w_in=False):
    s_len = dx3.shape[0]
    tile = min(512, s_len)
    tile_ff = min(256, s_len)
    ta = _attn_tile(s_len)
    tk = min(512, s_len)
    n_mem = mem.shape[0]
    g = {}

    dy3, dpre, g["norm_mlp_post"] = _resnorm_bwd_mm(dx3, sv["y3"], w["norm_mlp_post"], w["w_down"], tile_ff,
                                                    f"mlp_down_bwd_{l}", BF16, pre=sv["pre"])
    g["w_down"] = _matmul_tn_shards(sv["hid"], dy3, 0, tk, f"dw_down_{l}")
    dx2, g["norm_mlp_pre"] = _mm_prenorm_bwd(dpre, w["w_up"], sv["x2"], w["norm_mlp_pre"], dx3, tile_ff,
                                             f"mlp_up_bwd_{l}")
    g["w_up"] = _matmul_tn_shards(sv["h3"], dpre, 1, tk, f"dw_up_{l}")

    dy2, dom, g["norm_mem_post"] = _resnorm_bwd_mm(dx2, sv["y2"], w["norm_mem_post"], w["w_mo"], tile,
                                                   f"mem_out_bwd_{l}", BF16)
    g["w_mo"] = _matmul_tn_shards(sv["om"], dy2, 1, tk, f"dw_mo_{l}")
    dqm, dkv = _memattn_bwd(sv["qm"], sv["kv"], dom, tile, f"mem_attn_bwd_{l}")
    dkvb = dkv.astype(BF16)
    g["w_mq"] = _matmul_tn_shards(sv["h2"], dqm, 0, tk, f"dw_mq_{l}")
    dx1, g["norm_mem_pre"] = _mm_prenorm_bwd(dqm, w["w_mq"], sv["x1"], w["norm_mem_pre"], dx2, tile, f"mem_q_bwd_{l}")
    _, g["norm_memkv"] = _mm_prenorm_bwd(dkvb, w["w_mkv"], mem, w["norm_memkv"], None, n_mem, f"mem_kv_bwd_{l}")
    g["w_mk"] = _matmul_tn_shards(sv["mem_n"], dkvb[:, :MEM_INNER], 0, n_mem, f"dw_mk_{l}")
    g["w_mv"] = _matmul_tn_shards(sv["mem_n"], dkvb[:, MEM_INNER:], 0, n_mem, f"dw_mv_{l}")

    dy1, dcat, g["norm_mix_post"] = _resnorm_bwd_mm(dx1, sv["y1"], w["norm_mix_post"], w["w_out"], tile,
                                                    f"mix_out_bwd_{l}", F32)
    g["w_out"] = _matmul_tn_shards(sv["cat"], dy1, 0, tk, f"dw_out_{l}")
    dq_args = (sv["qa"], sv["lse"], sv["o"], dcat, sv["ka"], sv["va"], ta, 2 * ta, f"fox_dq_{l}")
    rest_land, dkv_land = None, None
    if scatter_rest:
        with_dq = [g[n] for n in REST_DQ]
        (dq, rs, qb, doa), land_dq = _foxa_dq(*dq_args, host=(with_dq, [False] * len(with_dq)))
        with_dkv = [g[n] for n in REST_DKV] + list(dkv_src or [])
        (dk, dv, cs), land_dkv = _foxa_dkv(qb, sv["ka"], sv["va"], doa, 2 * ta, ta, f"fox_dkv_{l}",
                                           host=(with_dkv, [False] * len(with_dkv)))
        by_name = dict(zip(REST_DQ + REST_DKV, land_dq + land_dkv))
        rest_land = [by_name[n] for n in REST]
        dkv_land = land_dkv[len(REST_DKV):]
    else:
        dq, rs, qb, doa = _foxa_dq(*dq_args)
        dk, dv, cs = _foxa_dkv(qb, sv["ka"], sv["va"], doa, 2 * ta, ta, f"fox_dkv_{l}")
    dfl, db = _cumsum_bwd(rs, cs, sv["fl"], w["b_forget"], tile, f"cumsum_bwd_{l}")
    g["b_forget"] = db[:, :FOX_HEADS]
    dag, dconv_w, dconv_v = _conv_bwd(sv["ag"], dcat, w["conv_w"], w["conv_b"], w["conv_ln_g"], w["conv_ln_b"], tile,
                                      f"conv_bwd_{l}")
    g["conv_w"] = dconv_w[:CONV_WIDTH]
    g["conv_b"], g["conv_ln_g"], g["conv_ln_b"] = dconv_v[0:1], dconv_v[1:2], dconv_v[2:3]
    dz = [dag, dq, dk, dv, dfl]
    dw_in = _matmul_tn(sv["h1"], dz, 512, tk, f"dw_in_{l}")[:, :IN_COLS]
    g["w_in"] = dw_in.reshape(D_MODEL, N_DEV, IN_COLS // N_DEV).transpose(1, 0, 2).astype(BF16)
    res = _mm_prenorm_bwd(dz, w["w_in_cat"], sv["x0"], w["norm_mix_pre"], dx1, tile, f"mix_in_bwd_{l}",
                          host=([g["w_in"]], [False]) if scatter_w_in else None)
    (dx0, g["norm_mix_pre"]), w_in_land = res if scatter_w_in else (res, None)
    return dx0, g, rest_land, dkv_land, w_in_land


def _sum_blocks(a, name):
    n, rows, cols = a.shape

    def body(a_ref, o_ref):
        acc = a_ref[0]
        for j in range(1, n):
            acc = acc + a_ref[j]
        o_ref[...] = acc

    return pl.pallas_call(
        body, name=name, in_specs=[_full((n, rows, cols))], out_specs=_full((rows, cols)),
        out_shape=jax.ShapeDtypeStruct((rows, cols), F32), grid=(1,),
    )(a)


def _adamw(gparts, w, m, v, tile, name):
    n_l, rows, cols = w.shape
    n = gparts[0].shape[0]
    n_t = rows // tile
    c1 = 1.0 - ADAM_B1
    c2 = 1.0 - ADAM_B2
    bc1 = 1.0 - ADAM_B1 ** ADAM_STEP
    bc2 = 1.0 - ADAM_B2 ** ADAM_STEP

    def body(*refs):
        gp_refs, (w_ref, m_ref, v_ref, g_ref, d_ref, mo_ref, vo_ref) = refs[:n_l], refs[n_l:]
        layer = pl.program_id(0)
        g = None
        for l, gp_ref in enumerate(gp_refs):
            gl = gp_ref[0].astype(F32)
            for j in range(1, n):
                gl = gl + gp_ref[j].astype(F32)
            g = gl if g is None else jnp.where(layer == l, gl, g)
        g_ref[...] = g
        m_new = ADAM_B1 * m_ref[...] + c1 * g
        v_new = ADAM_B2 * v_ref[...] + c2 * (g * g)
        mo_ref[...] = m_new
        vo_ref[...] = v_new
        d_ref[...] = -ADAM_LR * ((m_new / bc1) / (jnp.sqrt(v_new / bc2) + ADAM_EPS) + ADAM_WD * w_ref[...])

    def gp_spec(l):
        return pl.BlockSpec((n, tile, cols), lambda L, i: (0, jnp.where(L < l, 0, jnp.where(L > l, n_t - 1, i)), 0))

    spec = pl.BlockSpec((None, tile, cols), lambda L, i: (L, i, 0))
    return pl.pallas_call(
        body, name=name, grid=(n_l, n_t),
        in_specs=[gp_spec(l) for l in range(n_l)] + [spec, spec, spec],
        out_specs=[spec] * 4, out_shape=[jax.ShapeDtypeStruct((n_l, rows, cols), F32)] * 4,
        compiler_params=_params("arbitrary", "arbitrary"),
    )(*gparts, w, m, v)


def _pack_rows(parts, total_rows):
    flat = [p.reshape(-1, D_MODEL) for p in parts]
    used = sum(f.shape[0] for f in flat)
    if total_rows > used:
        flat.append(jnp.zeros((total_rows - used, D_MODEL), flat[0].dtype))
    return jnp.concatenate(flat, axis=0)


def kernel(x, mem, norm_mix_pre, norm_mix_post, w_in, b_forget, conv_w, conv_b, conv_ln_g, conv_ln_b, w_out, norm_mem_pre, norm_mem_post, norm_memkv, w_mq, w_mk, w_mv, w_mo, norm_mlp_pre, norm_mlp_post, w_up, w_down, loss_target, m_norm_mix_pre, m_norm_mix_post, m_w_in, m_b_forget, m_conv_w, m_conv_b, m_conv_ln_g, m_conv_ln_b, m_w_out, m_norm_mem_pre, m_norm_mem_post, m_norm_memkv, m_w_mq, m_w_mk, m_w_mv, m_w_mo, m_norm_mlp_pre, m_norm_mlp_post, m_w_up, m_w_down, v_norm_mix_pre, v_norm_mix_post, v_w_in, v_b_forget, v_conv_w, v_conv_b, v_conv_ln_g, v_conv_ln_b, v_w_out, v_norm_mem_pre, v_norm_mem_post, v_norm_memkv, v_w_mq, v_w_mk, v_w_mv, v_w_mo, v_norm_mlp_pre, v_norm_mlp_post, v_w_up, v_w_down):
    p = dict(locals())
    names = ("norm_mix_pre", "norm_mix_post", "w_in", "b_forget", "conv_w", "conv_b", "conv_ln_g", "conv_ln_b", "w_out",
             "norm_mem_pre", "norm_mem_post", "norm_memkv", "w_mq", "w_mk", "w_mv", "w_mo", "norm_mlp_pre",
             "norm_mlp_post", "w_up", "w_down")
    me = 4 * lax.axis_index("x") + 2 * lax.axis_index("y") + lax.axis_index("c")
    conv_cols = conv_w.shape[2]
    s_len = x.shape[1]
    bf = {n: p[n].astype(BF16) for n in BIG}

    conv_pack = _pack_rows([jnp.pad(conv_w, ((0, 0), (0, CONV_PAD - CONV_WIDTH), (0, 0)))], 8)
    win_land, conv_land = _exchange([bf["w_in"][0], conv_pack], [True, True], "gather_first")
    conv_rows = DEPTH * CONV_PAD * conv_cols // D_MODEL
    conv_full = conv_land[:, :conv_rows].reshape(N_DEV, DEPTH, CONV_PAD, conv_cols)
    conv_full = conv_full.transpose(1, 2, 0, 3).reshape(DEPTH, CONV_PAD, N_DEV * conv_cols)
    b_forget_pad = jnp.pad(b_forget, ((0, 0), (0, LANES - FOX_HEADS)))

    def first_weights(l, land):
        w = {"w_in_cat": _w_in_cat(land), "conv_w": conv_full[l], "b_forget": b_forget_pad[l:l + 1]}
        for n in VEC[:-1]:
            w[n] = p[n][l:l + 1]
        return w

    h, sv0, w0, win1_land = _layer_fwd(x[0], mem[0], first_weights(0, win_land), 0,
                                       rest_src=[bf[n][0] for n in REST], next_src=[bf["w_in"][1]])
    h, sv1, w1, _ = _layer_fwd(h, mem[0], first_weights(1, win1_land[0]), 1, rest_src=[bf[n][1] for n in REST])
    dh, loss_row = _loss_head(h, loss_target[0], min(512, s_len), "loss_head")
    dh, g1, rest_g1, _, _ = _layer_bwd(dh, mem[0], w1, sv1, 1, scatter_rest=True)
    grad_x, g0, rest_g0, win1_g, win0_g = _layer_bwd(dh, mem[0], w0, sv0, 0, scatter_rest=True, dkv_src=[g1["w_in"]],
                                                     scatter_w_in=True)

    def small_rows(get):
        rows = [jnp.concatenate([get(n) for n in VEC_1024], axis=0),
                jnp.concatenate([get(n) for n in VEC_512], axis=0).reshape(len(VEC_512), D_MODEL),
                jnp.pad(get("b_forget").reshape(1, -1), ((0, 0), (0, D_MODEL - DEPTH * FOX_HEADS)))]
        return jnp.concatenate(rows, axis=0)

    def tap_rows(conv):
        return jnp.pad(conv.reshape(1, -1), ((0, 0), (0, 4 * D_MODEL - conv.size))).reshape(4, D_MODEL)

    n_vec_rows = DEPTH * len(VEC_1024) + len(VEC_512) + 1
    part = small_rows(lambda n: jnp.concatenate([g0[n], g1[n]], axis=0))
    conv_part = jnp.stack([g0["conv_w"], g1["conv_w"]]).reshape(CONV_WIDTH, D_MODEL)
    n_part = 1 + n_vec_rows + CONV_WIDTH
    pad_rows = -n_part % 8
    small_land = _exchange([jnp.concatenate([loss_row, part, conv_part, jnp.zeros((pad_rows, D_MODEL), F32)], axis=0)],
                           [True], "gather_small")[0]
    total = _sum_blocks(small_land, "sum_small")
    loss = jnp.sum(total[0])
    conv_g = total[1 + n_vec_rows:n_part].reshape(DEPTH, CONV_WIDTH, CONV_CH)
    conv_g = lax.dynamic_slice_in_dim(conv_g, me * conv_cols, conv_cols, axis=2)
    fill = jnp.zeros((SMALL_ROWS - n_vec_rows - 4, D_MODEL), F32)

    def small_pack(vec_rows, conv):
        return jnp.concatenate([vec_rows, tap_rows(conv), fill], axis=0)

    small_out = _adamw([small_pack(total[1:1 + n_vec_rows], conv_g)[None]],
                       *[small_pack(small_rows(lambda n: p[pre + n]), p[pre + "conv_w"])[None] for pre in ("", "m_", "v_")],
                       SMALL_ROWS, "adamw_small")

    def unpack_small(buf):
        out = {}
        for k, n in enumerate(VEC_1024):
            out[n] = buf[DEPTH * k:DEPTH * (k + 1)]
        at = DEPTH * len(VEC_1024)
        for k, n in enumerate(VEC_512):
            out[n] = buf[at + k].reshape(DEPTH, CONV_CH)
        at += len(VEC_512)
        out["b_forget"] = buf[at, :DEPTH * FOX_HEADS].reshape(DEPTH, FOX_HEADS)
        out["conv_w"] = buf[at + 1:at + 5].reshape(-1)[:DEPTH * CONV_WIDTH * conv_cols].reshape(DEPTH, CONV_WIDTH, conv_cols)
        return out

    landed = {"w_in": [win0_g[0], win1_g[0]]}
    for i, n in enumerate(REST):
        landed[n] = [rest_g0[i], rest_g1[i]]
    big_out = {n: _adamw(landed[n], p[n], p["m_" + n], p["v_" + n], ADAMW_TILE[n], f"adamw_{n}") for n in BIG}

    result = [loss, grad_x[None]]
    for k in range(4):
        smalls = unpack_small(small_out[k][0])
        result += [big_out[n][k] if n in big_out else smalls[n] for n in names]
    return tuple(result)
```

```python
import functools

import jax
import jax.numpy as jnp
from jax import lax
from jax.experimental import pallas as pl
from jax.experimental.pallas import tpu as pltpu

F32 = jnp.float32
BF16 = jnp.bfloat16

N_DEV = 8
DEPTH = 2
D_MODEL = 1024
CONV_CH = 512
CONV_WIDTH = 31
CONV_PAD = 32
FOX_HEADS = 8
FOX_HEAD_DIM = 64
FOX_WIDTH = 512
N_PAIRS = 4
MEM_HEADS = 4
MEM_HEAD_DIM = 128
MEM_INNER = 512
D_FF = 4096
IN_MAIN = 2560
IN_COLS = 2568
IN_CAT = IN_MAIN + 128
LANES = 128
EPS = 1e-6
NEG_INF = -1e30

ADAM_LR = 0.001
ADAM_B1 = 0.9
ADAM_B2 = 0.999
ADAM_EPS = 1e-08
ADAM_WD = 0.01
ADAM_STEP = 10

NT_DIMS = (((1,), (1,)), ((), ()))
TN_DIMS = (((0,), (0,)), ((), ()))

BIG = ("w_in", "w_out", "w_mq", "w_mk", "w_mv", "w_mo", "w_up", "w_down")
ADAMW_TILE = {"w_in": 256, "w_out": 128, "w_mq": 128, "w_mk": 128, "w_mv": 128, "w_mo": 512, "w_up": 256, "w_down": 128}

VEC_1024 = ("norm_mix_pre", "norm_mix_post", "norm_mem_pre", "norm_mem_post", "norm_memkv", "norm_mlp_pre", "norm_mlp_post")
VEC_512 = ("conv_b", "conv_ln_g", "conv_ln_b")
VEC = VEC_1024 + VEC_512 + ("b_forget",)
SMALL_ROWS = 32


def _sigmoid(x):
    return 1.0 / (1.0 + jnp.exp(-x))


def _rms(x, g):
    r = lax.rsqrt(jnp.mean(x * x, axis=-1, keepdims=True) + EPS)
    return x * r * g


def _rms_bwd(x, g, dh):
    r = lax.rsqrt(jnp.mean(x * x, axis=-1, keepdims=True) + EPS)
    gh = dh * g
    c = jnp.mean(gh * x, axis=-1, keepdims=True)
    dx = r * gh - x * (r * r * r * c)
    dg = jnp.sum(dh * (x * r), axis=0, keepdims=True)
    return dx, dg


def _full(shape):
    nd = len(shape)
    return pl.BlockSpec(shape, lambda *_: (0,) * nd)


def _params(*sem):
    return pltpu.CompilerParams(dimension_semantics=sem)


def _exchange_copies(src_refs, out_refs, same, send_sems, recv_sems, local_sems, with_recvs):
    x, y, c = lax.axis_index("x"), lax.axis_index("y"), lax.axis_index("c")
    me = 4 * x + 2 * y + c
    local, sends, recvs = [], [], []
    for a, (s_ref, o_ref) in enumerate(zip(src_refs, out_refs)):
        def mine(idx, s_ref=s_ref, whole=same[a]):
            return s_ref if whole else s_ref.at[idx]

        local.append(pltpu.make_async_copy(mine(me), o_ref.at[me], local_sems.at[a]))
        for k in range(1, N_DEV):
            px = 1 - x if k & 4 else x
            py = 1 - y if k & 2 else y
            pc = 1 - c if k & 1 else c
            peer = 4 * px + 2 * py + pc
            sem = a * (N_DEV - 1) + k - 1
            common = dict(send_sem=send_sems.at[sem], recv_sem=recv_sems.at[sem], device_id=(px, py, pc),
                          device_id_type=pl.DeviceIdType.MESH)
            sends.append(pltpu.make_async_remote_copy(src_ref=mine(peer), dst_ref=o_ref.at[me], **common))
            if with_recvs:
                recvs.append(pltpu.make_async_remote_copy(src_ref=mine(peer), dst_ref=o_ref.at[peer], **common))
    return local, sends, recvs


def _gather_copies(src_refs, out_refs, send_sems, recv_sems, local_sems, phase):
    x, y, c = lax.axis_index("x"), lax.axis_index("y"), lax.axis_index("c")
    sibling = (x, y, 1 - c)
    chips = [(1 - x, y), (x, 1 - y), (1 - x, 1 - y)]

    def idx(px, py, pc):
        return 4 * px + 2 * py + pc

    local, first, arrive, passed, final = [], [], [], [], []
    for a, (s_ref, o_ref) in enumerate(zip(src_refs, out_refs)):
        def cp(k, src, block, to, a=a, o_ref=o_ref):
            sem = a * (N_DEV - 1) + k
            return pltpu.make_async_remote_copy(src_ref=src, dst_ref=o_ref.at[block], send_sem=send_sems.at[sem],
                                                recv_sem=recv_sems.at[sem], device_id=to, device_id_type=pl.DeviceIdType.MESH)

        me = idx(x, y, c)
        if phase != 1:
            local.append(pltpu.make_async_copy(s_ref, o_ref.at[me], local_sems.at[a]))
            first.append(cp(0, s_ref, me, sibling))
        if phase == 2:
            final.append(cp(0, s_ref, idx(x, y, 1 - c), sibling))
        for j, chip in enumerate(chips):
            theirs = idx(*chip, c)
            if phase != 1:
                first.append(cp(1 + j, s_ref, me, (*chip, c)))
            if phase == 1:
                arrive.append(cp(1 + j, s_ref, theirs, (*chip, c)))
            if phase != 0:
                passed.append(cp(4 + j, o_ref.at[theirs], theirs, sibling))
            if phase == 2:
                final.append(cp(4 + j, s_ref, idx(*chip, 1 - c), sibling))
    return local, first, arrive, passed, final


def _pcall(body, *, name, grid, in_specs, out_specs, out_shape, args, scratch_shapes=(), sem=(), host=None):
    if host is None:
        return pl.pallas_call(body, name=name, grid=grid, in_specs=in_specs, out_specs=out_specs, out_shape=out_shape,
                              scratch_shapes=list(scratch_shapes), compiler_params=_params(*sem))(*args)
    srcs, same = host
    n_in, n_out, n_scr, n_h = len(in_specs), len(out_specs), len(scratch_shapes), len(srcs)
    hbm = pl.BlockSpec(memory_space=pltpu.HBM)
    lands = [jax.ShapeDtypeStruct((N_DEV,) + (s.shape if whole else s.shape[1:]), s.dtype) for s, whole in zip(srcs, same)]

    def wrapped(*refs):
        ins, src_refs = refs[:n_in], refs[n_in:n_in + n_h]
        outs = refs[n_in + n_h:n_in + n_h + n_out]
        land_refs = refs[n_in + n_h + n_out:n_in + 2 * n_h + n_out]
        scr = refs[n_in + 2 * n_h + n_out:n_in + 2 * n_h + n_out + n_scr]
        sems = refs[n_in + 2 * n_h + n_out + n_scr:]
        ids = [pl.program_id(d) for d in range(len(grid))]
        first = functools.reduce(jnp.logical_and, [i == 0 for i in ids])
        last = functools.reduce(jnp.logical_and, [i == n - 1 for i, n in zip(ids, grid)])
        later = functools.reduce(jnp.logical_and, [ids[0] == (3 * grid[0]) // 4] + [i == 0 for i in ids[1:]])

        if all(same):
            @pl.when(first)
            def _():
                local, sends, _, _, _ = _gather_copies(src_refs, land_refs, *sems, 0)
                for cp in local + sends:
                    cp.start()

            body(*ins, *outs, *scr)

            @pl.when(later)
            def _():
                _, _, arrive, passed, _ = _gather_copies(src_refs, land_refs, *sems, 1)
                for cp in arrive:
                    cp.wait_recv()
                for cp in passed:
                    cp.start()

            @pl.when(last)
            def _():
                local, sends, _, passed, final = _gather_copies(src_refs, land_refs, *sems, 2)
                for cp in final:
                    cp.wait_recv()
                for cp in sends + passed:
                    cp.wait_send()
                for cp in local:
                    cp.wait()
        else:
            @pl.when(first)
            def _():
                local, sends, _ = _exchange_copies(src_refs, land_refs, same, *sems, False)
                for cp in local + sends:
                    cp.start()

            body(*ins, *outs, *scr)

            @pl.when(last)
            def _():
                local, sends, recvs = _exchange_copies(src_refs, land_refs, same, *sems, True)
                for cp in recvs:
                    cp.wait_recv()
                for cp in sends:
                    cp.wait_send()
                for cp in local:
                    cp.wait()

    n_sem = n_h * (N_DEV - 1)
    res = pl.pallas_call(
        wrapped, name=name, grid=grid, in_specs=list(in_specs) + [hbm] * n_h, out_specs=list(out_specs) + [hbm] * n_h,
        out_shape=list(out_shape) + lands,
        scratch_shapes=list(scratch_shapes) + [pltpu.SemaphoreType.DMA((n_sem,)), pltpu.SemaphoreType.DMA((n_sem,)),
                                               pltpu.SemaphoreType.DMA((n_h,))],
        compiler_params=_params(*(("arbitrary",) * len(grid))),
    )(*args, *srcs)
    return list(res[:n_out]), list(res[n_out:])


def _exchange(srcs, same, name):
    def body():
        pass

    return _pcall(body, name=name, grid=(1,), in_specs=[], out_specs=[], out_shape=[], args=[], host=(srcs, same))[1]


def _rms_matmul(x, g, w, segs, tile, name, host=None, w_t=False):
    s_len, d = x.shape
    chunk = 512

    def body(x_ref, g_ref, w_ref, h_ref, *outs):
        h = _rms(x_ref[...], g_ref[...]).astype(BF16)
        h_ref[...] = h
        oi = 0
        for c0, c1, fns in segs:
            for a in range(c0, c1, chunk):
                b = min(a + chunk, c1)
                if w_t:
                    z = lax.dot_general(h, w_ref[a:b, :], NT_DIMS, preferred_element_type=F32)
                else:
                    z = jnp.dot(h, w_ref[:, a:b], preferred_element_type=F32)
                for k, (dt, fn) in enumerate(fns):
                    outs[oi + k][:, a - c0:b - c0] = fn(z).astype(dt)
            oi += len(fns)

    out_shape = [jax.ShapeDtypeStruct((s_len, d), BF16)]
    out_specs = [pl.BlockSpec((tile, d), lambda i: (i, 0))]
    for c0, c1, fns in segs:
        for dt, _ in fns:
            out_shape.append(jax.ShapeDtypeStruct((s_len, c1 - c0), dt))
            out_specs.append(pl.BlockSpec((tile, c1 - c0), lambda i: (i, 0)))
    return _pcall(
        body, name=name, grid=(s_len // tile,),
        in_specs=[pl.BlockSpec((tile, d), lambda i: (i, 0)), _full((1, d)), _full(w.shape)],
        out_specs=out_specs, out_shape=out_shape, args=[x, g, w], sem=("parallel",), host=host)


def _matmul_resnorm(a, w, x, g, tile, name):
    s_len, k = a.shape
    d = w.shape[1]

    def body(a_ref, w_ref, x_ref, g_ref, y_ref, xo_ref):
        y = jnp.dot(a_ref[...], w_ref[...], preferred_element_type=F32)
        y_ref[...] = y
        xo_ref[...] = x_ref[...] + _rms(y, g_ref[...])

    row = lambda i: (i, 0)
    return pl.pallas_call(
        body, name=name, grid=(s_len // tile,),
        in_specs=[pl.BlockSpec((tile, k), row), _full((k, d)), pl.BlockSpec((tile, d), row), _full((1, d))],
        out_specs=[pl.BlockSpec((tile, d), row), pl.BlockSpec((tile, d), row)],
        out_shape=[jax.ShapeDtypeStruct((s_len, d), F32), jax.ShapeDtypeStruct((s_len, d), F32)],
        compiler_params=_params("parallel"),
    )(a, w, x, g)


def _resnorm_bwd_mm(dx, y, g, w, tile, name, out_dtype, pre=None):
    s_len, d = dx.shape
    k = w.shape[0]
    chunk = 512

    def body(*refs):
        if pre is None:
            dx_ref, y_ref, g_ref, w_ref, dy_ref, da_ref, dg_ref = refs
        else:
            dx_ref, y_ref, g_ref, w_ref, pre_ref, dy_ref, da_ref, dg_ref = refs
        dy, dg = _rms_bwd(y_ref[...], g_ref[...], dx_ref[...])
        dyb = dy.astype(BF16)
        dy_ref[...] = dyb

        @pl.when(pl.program_id(0) == 0)
        def _():
            dg_ref[...] = jnp.zeros_like(dg_ref)

        dg_ref[...] += dg
        for a in range(0, k, chunk):
            b = min(a + chunk, k)
            da = lax.dot_general(dyb, w_ref[a:b, :], NT_DIMS, preferred_element_type=F32)
            if pre is not None:
                da = da * (2.0 * jnp.maximum(pre_ref[:, a:b].astype(F32), 0.0))
            da_ref[:, a:b] = da.astype(out_dtype)

    row = lambda i: (i, 0)
    in_specs = [pl.BlockSpec((tile, d), row), pl.BlockSpec((tile, d), row), _full((1, d)), _full((k, d))]
    args = [dx, y, g, w]
    if pre is not None:
        in_specs.append(pl.BlockSpec((tile, k), row))
        args.append(pre)
    return pl.pallas_call(
        body, name=name, grid=(s_len // tile,), in_specs=in_specs,
        out_specs=[pl.BlockSpec((tile, d), row), pl.BlockSpec((tile, k), row), _full((1, d))],
        out_shape=[jax.ShapeDtypeStruct((s_len, d), BF16), jax.ShapeDtypeStruct((s_len, k), out_dtype),
                   jax.ShapeDtypeStruct((1, d), F32)],
        compiler_params=_params("arbitrary"),
    )(*args)


def _mm_prenorm_bwd(dz, w, x, g, dres, tile, name, host=None, w_t=False):
    pieces = list(dz) if isinstance(dz, (list, tuple)) else [dz]
    n_p = len(pieces)
    widths = [p.shape[1] for p in pieces]
    s_len = pieces[0].shape[0]
    d = w.shape[1] if w_t else w.shape[0]

    def body(*refs):
        dz_refs, rest = refs[:n_p], refs[n_p:]
        if dres is None:
            w_ref, x_ref, g_ref, dx_ref, dg_ref = rest
        else:
            w_ref, x_ref, g_ref, dres_ref, dx_ref, dg_ref = rest
        dh, off = None, 0
        for dz_ref, width in zip(dz_refs, widths):
            if w_t:
                part = jnp.dot(dz_ref[...], w_ref[off:off + width, :], preferred_element_type=F32)
            else:
                part = lax.dot_general(dz_ref[...], w_ref[:, off:off + width], NT_DIMS, preferred_element_type=F32)
            dh = part if dh is None else dh + part
            off += width
        dx, dg = _rms_bwd(x_ref[...], g_ref[...], dh)
        if dres is not None:
            dx = dx + dres_ref[...]
        dx_ref[...] = dx

        @pl.when(pl.program_id(0) == 0)
        def _():
            dg_ref[...] = jnp.zeros_like(dg_ref)

        dg_ref[...] += dg

    row = lambda i: (i, 0)
    in_specs = [pl.BlockSpec((tile, width), row) for width in widths]
    in_specs += [_full(w.shape), pl.BlockSpec((tile, d), row), _full((1, d))]
    args = pieces + [w, x, g]
    if dres is not None:
        in_specs.append(pl.BlockSpec((tile, d), row))
        args.append(dres)
    return _pcall(
        body, name=name, grid=(s_len // tile,), in_specs=in_specs,
        out_specs=[pl.BlockSpec((tile, d), row), _full((1, d))],
        out_shape=[jax.ShapeDtypeStruct((s_len, d), F32), jax.ShapeDtypeStruct((1, d), F32)],
        args=args, sem=("arbitrary",), host=host)


def _matmul_tn_rows(pieces, b, tk, name):
    s_len, n = b.shape
    widths = [p.shape[1] for p in pieces]

    def body(*refs):
        a_refs, b_ref, o_ref = refs[:-2], refs[-2], refs[-1]

        @pl.when(pl.program_id(0) == 0)
        def _():
            o_ref[...] = jnp.zeros_like(o_ref)

        b_tile = b_ref[...]
        off = 0
        for a_ref, width in zip(a_refs, widths):
            o_ref[off:off + width, :] += lax.dot_general(a_ref[...], b_tile, TN_DIMS, preferred_element_type=F32)
            off += width

    return pl.pallas_call(
        body, name=name, grid=(s_len // tk,),
        in_specs=[pl.BlockSpec((tk, width), lambda k: (k, 0)) for width in widths] + [pl.BlockSpec((tk, n), lambda k: (k, 0))],
        out_specs=_full((sum(widths), n)),
        out_shape=jax.ShapeDtypeStruct((sum(widths), n), F32),
        compiler_params=_params("arbitrary"),
    )(*pieces, b)


def _matmul_tn_shards(a, b, axis, tk, name):
    s_len, m = a.shape
    n = b.shape[1]
    r, c = (m // N_DEV, n) if axis == 0 else (m, n // N_DEV)
    n_k = s_len // tk
    tm = max(r, min(m, (1 << 20) // n)) if axis == 0 else min(m, (1 << 20) // n)

    def body(a_ref, b_ref, o_ref, acc):
        k = pl.program_id(1)

        @pl.when(k == 0)
        def _():
            acc[...] = jnp.zeros_like(acc)

        acc[...] += lax.dot_general(a_ref[...], b_ref[...], TN_DIMS, preferred_element_type=F32)

        @pl.when(k == n_k - 1)
        def _():
            if axis == 0:
                o_ref[...] = acc[...].reshape(tm // r, r, c).astype(BF16)
            else:
                for j in range(N_DEV):
                    o_ref[j] = acc[:, j * c:(j + 1) * c].astype(BF16)

    if axis == 0:
        out_spec = pl.BlockSpec((tm // r, r, c), lambda i, k: (i, 0, 0))
    else:
        out_spec = pl.BlockSpec((N_DEV, tm, c), lambda i, k: (0, i, 0))
    return pl.pallas_call(
        body, name=name, grid=(m // tm, n_k),
        in_specs=[pl.BlockSpec((tk, tm), lambda i, k: (k, i)), pl.BlockSpec((tk, n), lambda i, k: (k, 0))],
        out_specs=out_spec, out_shape=jax.ShapeDtypeStruct((N_DEV, r, c), BF16),
        scratch_shapes=[pltpu.VMEM((tm, n), F32)],
        compiler_params=_params("parallel", "arbitrary"),
    )(a, b)


def _cumsum_bwd(rs, cs, fl, b, tile, name):
    s_len = fl.shape[0]
    n_t = s_len // tile

    def body(rs_ref, cs_ref, fl_ref, b_ref, dfl_ref, db_ref, carry):
        @pl.when(pl.program_id(0) == 0)
        def _():
            carry[...] = jnp.zeros_like(carry)
            db_ref[...] = jnp.zeros_like(db_ref)

        r = lax.broadcasted_iota(jnp.int32, (tile, tile), 0)
        c = lax.broadcasted_iota(jnp.int32, (tile, tile), 1)
        tri = (c >= r).astype(F32)
        lane = lax.broadcasted_iota(jnp.int32, (1, LANES), 1)
        dc = jnp.zeros((tile, LANES), F32)
        for p in range(N_PAIRS):
            blk = rs_ref[:, LANES * p:LANES * (p + 1)] - cs_ref[:, LANES * p:LANES * (p + 1)]
            dc = jnp.where(lane == 2 * p, blk, dc)
            dc = jnp.where(lane == 2 * p + 1, pltpu.roll(blk, FOX_HEAD_DIM, axis=1), dc)
        dl = jnp.dot(tri, dc, precision=lax.Precision.HIGHEST, preferred_element_type=F32) + carry[...]
        carry[...] = dl[0:1, :]
        dfl = dl * _sigmoid(-(fl_ref[...] + b_ref[...]))
        dfl_ref[...] = dfl.astype(BF16)
        db_ref[...] += jnp.sum(dfl, axis=0, keepdims=True)

    rev = lambda i: (n_t - 1 - i, 0)
    return pl.pallas_call(
        body, name=name, grid=(n_t,),
        in_specs=[pl.BlockSpec((tile, FOX_WIDTH), rev), pl.BlockSpec((tile, FOX_WIDTH), rev), pl.BlockSpec((tile, LANES), rev),
                  _full((1, LANES))],
        out_specs=[pl.BlockSpec((tile, LANES), rev), _full((1, LANES))],
        out_shape=[jax.ShapeDtypeStruct((s_len, LANES), BF16), jax.ShapeDtypeStruct((1, LANES), F32)],
        scratch_shapes=[pltpu.VMEM((1, LANES), F32)],
        compiler_params=_params("arbitrary"),
    )(rs, cs, fl, b)


SUBLANES = 8
CONV_ROWS = 64


def _phase_copies(src, dst, rows):
    for p in range(SUBLANES):
        dst[p] = src[pl.ds(p, rows), :]


def _phase_rows(extp_ref, off, r0, rows):
    p = off % SUBLANES
    return extp_ref[p, pl.ds(pl.multiple_of(r0 + (off - p), SUBLANES), rows), :]


def _conv_taps(w_ref, extp_ref, base, r0, rows, reverse):
    acc = None
    for k in range(CONV_WIDTH):
        off = base + ((CONV_WIDTH - 1 - k) if reverse else k)
        term = w_ref[k:k + 1, :] * _phase_rows(extp_ref, off, r0, rows)
        acc = term if acc is None else acc + term
    return acc


def _fold_rows(x):
    out = x[0:SUBLANES]
    for i in range(1, x.shape[0] // SUBLANES):
        out = out + x[i * SUBLANES:(i + 1) * SUBLANES]
    return out


def _conv_fwd(ag, w, cb, lg, lb, tile, name):
    s_len = ag.shape[0]
    c = CONV_CH
    rb = tile

    def body(ag_ref, w_ref, cb_ref, lg_ref, lb_ref, u_ref, ext, extp):
        @pl.when(pl.program_id(0) == 0)
        def _():
            ext[0:CONV_PAD, :] = jnp.zeros((CONV_PAD, c), F32)
            ext[tile + CONV_PAD:tile + CONV_PAD + SUBLANES, :] = jnp.zeros((SUBLANES, c), F32)

        ext[CONV_PAD:CONV_PAD + tile, :] = ag_ref[:, 0:c] * _sigmoid(ag_ref[:, c:2 * c])
        _phase_copies(ext, extp, tile + CONV_PAD)

        def block(b, carry):
            r0 = pl.multiple_of(b * rb, rb)
            u1 = _conv_taps(w_ref, extp, CONV_PAD - (CONV_WIDTH - 1), r0, rb, False) + cb_ref[...]
            mu = jnp.mean(u1, axis=-1, keepdims=True)
            xc = u1 - mu
            y = xc * lax.rsqrt(jnp.mean(xc * xc, axis=-1, keepdims=True) + EPS) * lg_ref[...] + lb_ref[...]
            u_ref[pl.ds(r0, rb), :] = (y * _sigmoid(y)).astype(BF16)
            return carry

        lax.fori_loop(0, tile // rb, block, 0)
        ext[0:CONV_PAD, :] = ext[tile:tile + CONV_PAD, :]

    return pl.pallas_call(
        body, name=name, grid=(s_len // tile,),
        in_specs=[pl.BlockSpec((tile, 2 * c), lambda i: (i, 0)), _full((CONV_PAD, c)), _full((1, c)), _full((1, c)),
                  _full((1, c))],
        out_specs=pl.BlockSpec((tile, c), lambda i: (i, 0)),
        out_shape=jax.ShapeDtypeStruct((s_len, c), BF16),
        scratch_shapes=[pltpu.VMEM((tile + CONV_PAD + SUBLANES, c), F32), pltpu.VMEM((SUBLANES, tile + CONV_PAD, c), F32)],
        compiler_params=_params("arbitrary"),
    )(ag, w, cb, lg, lb)


def _conv_bwd(ag, dcat, w, cb, lg, lb, tile, name):
    s_len = ag.shape[0]
    c = CONV_CH
    n_t = s_len // tile
    per = tile // CONV_PAD
    rb = min(CONV_ROWS, tile)

    def body(ag_ref, halo_ref, du_ref, w_ref, cb_ref, lg_ref, lb_ref, dag_ref, dw_ref, dv_ref, ext, ext2, extp, dwacc):
        i = pl.program_id(0)
        t = n_t - 1 - i

        @pl.when(i == 0)
        def _():
            ext2[tile:tile + CONV_PAD + SUBLANES, :] = jnp.zeros((CONV_PAD + SUBLANES, c), F32)
            ext[tile + CONV_PAD:tile + CONV_PAD + SUBLANES, :] = jnp.zeros((SUBLANES, c), F32)
            dwacc[...] = jnp.zeros_like(dwacc)
            dv_ref[...] = jnp.zeros_like(dv_ref)

        halo = halo_ref[:, 0:c] * _sigmoid(halo_ref[:, c:2 * c])
        ext[0:CONV_PAD, :] = jnp.where(t > 0, halo, 0.0)
        ext[CONV_PAD:CONV_PAD + tile, :] = ag_ref[:, 0:c] * _sigmoid(ag_ref[:, c:2 * c])
        _phase_copies(ext, extp, tile + CONV_PAD)
        base = CONV_PAD - (CONV_WIDTH - 1)

        def block1(b, carry):
            r0 = pl.multiple_of(b * rb, rb)
            u1 = _conv_taps(w_ref, extp, base, r0, rb, False) + cb_ref[...]
            mu = jnp.mean(u1, axis=-1, keepdims=True)
            xc = u1 - mu
            rs = lax.rsqrt(jnp.mean(xc * xc, axis=-1, keepdims=True) + EPS)
            xhat = xc * rs
            y = xhat * lg_ref[...] + lb_ref[...]
            sy = _sigmoid(y)
            dy = du_ref[pl.ds(r0, rb), :] * (sy * (1.0 + y * (1.0 - sy)))
            dxh = dy * lg_ref[...]
            du1 = rs * (dxh - jnp.mean(dxh, axis=-1, keepdims=True) - xhat * jnp.mean(dxh * xhat, axis=-1, keepdims=True))
            dv_ref[0:1, :] += jnp.sum(du1, axis=0, keepdims=True)
            dv_ref[1:2, :] += jnp.sum(dy * xhat, axis=0, keepdims=True)
            dv_ref[2:3, :] += jnp.sum(dy, axis=0, keepdims=True)
            for k in range(CONV_WIDTH):
                dwacc[k] += _fold_rows(du1 * _phase_rows(extp, base + k, r0, rb))
            ext2[pl.ds(r0, rb), :] = du1
            return carry

        lax.fori_loop(0, tile // rb, block1, 0)
        _phase_copies(ext2, extp, tile + CONV_PAD)

        def block2(b, carry):
            r0 = pl.multiple_of(b * rb, rb)
            du0 = _conv_taps(w_ref, extp, 0, r0, rb, True)
            a = ag_ref[pl.ds(r0, rb), 0:c]
            sg = _sigmoid(ag_ref[pl.ds(r0, rb), c:2 * c])
            dag_ref[pl.ds(r0, rb), 0:c] = (du0 * sg).astype(BF16)
            dag_ref[pl.ds(r0, rb), c:2 * c] = (du0 * a * sg * (1.0 - sg)).astype(BF16)
            return carry

        lax.fori_loop(0, tile // rb, block2, 0)
        ext2[tile:tile + CONV_PAD, :] = ext2[0:CONV_PAD, :]

        @pl.when(i == n_t - 1)
        def _():
            for k in range(CONV_WIDTH):
                dw_ref[k:k + 1, :] = jnp.sum(dwacc[k], axis=0, keepdims=True)
            dw_ref[CONV_WIDTH:CONV_PAD, :] = jnp.zeros((CONV_PAD - CONV_WIDTH, c), F32)

    rev = lambda i: (n_t - 1 - i, 0)
    return pl.pallas_call(
        body, name=name, grid=(n_t,),
        in_specs=[pl.BlockSpec((tile, 2 * c), rev),
                  pl.BlockSpec((CONV_PAD, 2 * c), lambda i: (jnp.maximum((n_t - 1 - i) * per - 1, 0), 0)),
                  pl.BlockSpec((tile, c), rev), _full((CONV_PAD, c)), _full((1, c)), _full((1, c)), _full((1, c))],
        out_specs=[pl.BlockSpec((tile, 2 * c), rev), _full((CONV_PAD, c)), _full((8, c))],
        out_shape=[jax.ShapeDtypeStruct((s_len, 2 * c), BF16), jax.ShapeDtypeStruct((CONV_PAD, c), F32),
                   jax.ShapeDtypeStruct((8, c), F32)],
        scratch_shapes=[pltpu.VMEM((tile + CONV_PAD + SUBLANES, c), F32), pltpu.VMEM((tile + CONV_PAD + SUBLANES, c), F32),
                        pltpu.VMEM((SUBLANES, tile + CONV_PAD, c), F32), pltpu.VMEM((CONV_PAD, SUBLANES, c), F32)],
        compiler_params=_params("arbitrary"),
    )(ag, ag, dcat, w, cb, lg, lb)


Q_SCALE = FOX_HEAD_DIM ** -0.5


def _head_col(x, lane, h):
    return jnp.sum(jnp.where(lane == h, x, 0.0), axis=1, keepdims=True)


def _split3(x):
    hi = x.astype(BF16).astype(F32)
    r = x - hi
    mid = r.astype(BF16).astype(F32)
    lo = (r - mid).astype(BF16).astype(F32)
    return hi, mid, lo


def _in_lanes(lane, lo, n):
    return (lane >= lo) & (lane < lo + n)


def _put3(lane, lo, parts, rest):
    return jnp.where(lane == lo, parts[0], jnp.where(lane == lo + 1, parts[1], jnp.where(lane == lo + 2, parts[2], rest)))


def _spare_lane(h):
    return FOX_HEAD_DIM if h % 2 == 0 else 0


def _shift_div(i, num, den):
    return i * (num // den) if num >= den else lax.shift_right_logical(i, (den // num).bit_length() - 1)


def _foxa_prep(qkv, fl, b, tile, name):
    s_len = qkv.shape[0]

    def body(q_ref, k_ref, v_ref, fl_ref, b_ref, qa_ref, ka_ref, va_ref, carry):
        @pl.when(pl.program_id(0) == 0)
        def _():
            carry[...] = jnp.zeros_like(carry)

        xx = fl_ref[...] + b_ref[...]
        lf = jnp.minimum(xx, 0.0) - jnp.log1p(jnp.exp(-jnp.abs(xx)))
        tri = (lax.broadcasted_iota(jnp.int32, (tile, tile), 1) <= lax.broadcasted_iota(jnp.int32, (tile, tile), 0))
        cum_t = jnp.dot(tri.astype(F32), lf, precision=lax.Precision.HIGHEST, preferred_element_type=F32) + carry[...]
        carry[...] = cum_t[tile - 1:tile, :]
        lane = lax.broadcasted_iota(jnp.int32, (1, LANES), 1)
        for h in range(FOX_HEADS):
            e = _spare_lane(h)
            head = ~_in_lanes(lane, e, FOX_HEAD_DIM)
            blk = slice(LANES * (h // 2), LANES * (h // 2) + LANES)
            out = slice(LANES * h, LANES * h + LANES)
            c3 = _split3(_head_col(cum_t, lane, h))
            ex_q = _put3(lane, e, c3, jnp.where(_in_lanes(lane, e + 3, 3), 1.0, 0.0))
            qa_ref[:, out] = jnp.where(head, q_ref[:, blk].astype(F32) * Q_SCALE, ex_q).astype(BF16)
            ones = jnp.where(_in_lanes(lane, e, 3) | _in_lanes(lane, e + 6, 3), 1.0, 0.0)
            ex_k = _put3(lane, e + 3, [-c for c in c3], ones)
            ka_ref[:, out] = jnp.where(head, k_ref[:, blk].astype(F32), ex_k).astype(BF16)
            ex_v = jnp.where(_in_lanes(lane, e, 3), 1.0, 0.0)
            va_ref[:, out] = jnp.where(head, v_ref[:, blk].astype(F32), ex_v).astype(BF16)

    col = lambda c: pl.BlockSpec((tile, FOX_WIDTH), lambda i: (i, c))
    wide = pl.BlockSpec((tile, 2 * FOX_WIDTH), lambda i: (i, 0))
    return pl.pallas_call(
        body, name=name, grid=(s_len // tile,),
        in_specs=[col(0), col(1), col(2), pl.BlockSpec((tile, LANES), lambda i: (i, 0)), _full((1, LANES))],
        out_specs=[wide, wide, wide], out_shape=[jax.ShapeDtypeStruct((s_len, 2 * FOX_WIDTH), BF16)] * 3,
        scratch_shapes=[pltpu.VMEM((1, LANES), F32)],
        compiler_params=_params("arbitrary"),
    )(qkv, qkv, qkv, fl, b)


def _foxa_fwd(qa, ka, va, tq, tk, name, host=None):
    s_len = qa.shape[0]

    def body(q_ref, k_ref, v_ref, o_ref, lse_ref, ob_ref):
        i = pl.program_id(1)
        lane = lax.broadcasted_iota(jnp.int32, (1, LANES), 1)
        cols = [slice(LANES * hh, LANES * hh + LANES) for hh in range(2)]
        qh = [q_ref[:, c] for c in cols]

        def scores(j):
            off = pl.multiple_of(j * tk, tk)
            return [lax.dot_general(qh[hh], k_ref[pl.ds(off, tk), cols[hh]], NT_DIMS, preferred_element_type=F32)
                    for hh in range(2)]

        def update(j, s, m, acc, mask):
            off = pl.multiple_of(j * tk, tk)
            m_out, acc_out = [], []
            for hh in range(2):
                sh = s[hh] if mask is None else jnp.where(mask, s[hh], NEG_INF)
                m_new = jnp.maximum(m[hh], jnp.max(sh, axis=1, keepdims=True))
                pr = jnp.exp(sh - m_new).astype(BF16)
                acc_out.append(jnp.exp(m[hh] - m_new) * acc[hh]
                               + jnp.dot(pr, v_ref[pl.ds(off, tk), cols[hh]], preferred_element_type=F32))
                m_out.append(m_new)
            return m_out, acc_out

        def step(j, carry):
            s_next = scores(j + 1)
            m, acc = update(j, carry[0:2], carry[2:4], carry[4:6], None)
            return (*s_next, *m, *acc)

        n_full = _shift_div(i, tq, tk)
        n_part = max(tq // tk, 1)
        qi = lax.broadcasted_iota(jnp.int32, (tq, tk), 0) + i * tq
        ki = lax.broadcasted_iota(jnp.int32, (tq, tk), 1)
        carry = (*scores(0), *([jnp.full((tq, 1), NEG_INF, F32)] * 2), *([jnp.zeros((tq, LANES), F32)] * 2))
        carry = lax.fori_loop(0, n_full, step, carry)
        s, m, acc = list(carry[0:2]), list(carry[2:4]), list(carry[4:6])
        for jj in range(n_part):
            s_next = scores(n_full + jj + 1) if jj < n_part - 1 else None
            m, acc = update(n_full + jj, s, m, acc, ki + (n_full + jj) * tk <= qi)
            s = s_next
        res = []
        for hh in range(2):
            l = acc[hh][:, _spare_lane(hh):_spare_lane(hh) + 1]
            res.append((acc[hh] / l, m[hh] + jnp.log(l)))
        low = lane < FOX_HEAD_DIM
        o_pair = jnp.where(low, res[0][0], res[1][0])
        o_ref[...] = o_pair
        ob_ref[...] = o_pair.astype(BF16)
        lse_ref[...] = jnp.where(low, res[0][1], res[1][1])

    pair = pl.BlockSpec((s_len, 2 * LANES), lambda p, i: (0, p))
    out = pl.BlockSpec((tq, LANES), lambda p, i: (i, p))
    return _pcall(
        body, name=name, grid=(N_PAIRS, s_len // tq),
        in_specs=[pl.BlockSpec((tq, 2 * LANES), lambda p, i: (i, p)), pair, pair],
        out_specs=[out, out, out],
        out_shape=[jax.ShapeDtypeStruct((s_len, FOX_WIDTH), F32)] * 2 + [jax.ShapeDtypeStruct((s_len, FOX_WIDTH), BF16)],
        args=[qa, ka, va], sem=("parallel", "parallel"), host=host)


def _foxa_dq(qa, lse, o, dcat, ka, va, tq, tk, name, host=None):
    s_len = qa.shape[0]

    def body(qa_ref, lse_ref, o_ref, dcat_ref, k_ref, v_ref, dq_ref, rs_ref, q_ref, do_ref):
        i = pl.program_id(1)
        lane = lax.broadcasted_iota(jnp.int32, (1, LANES), 1)
        cols = [slice(LANES * hh, LANES * hh + LANES) for hh in range(2)]
        d_o = dcat_ref[...]
        prod = d_o * o_ref[...]
        for hh in range(2):
            e = _spare_lane(hh)
            head = ~_in_lanes(lane, e, FOX_HEAD_DIM)
            delta = jnp.sum(jnp.where(head, prod, 0.0), axis=1, keepdims=True)
            do_ref[:, cols[hh]] = _put3(lane, e, _split3(-delta), jnp.where(head, d_o, 0.0)).astype(BF16)
            l3 = _split3(-lse_ref[:, FOX_HEAD_DIM - e:FOX_HEAD_DIM - e + 1])
            q_ref[:, cols[hh]] = _put3(lane, e + 6, l3, qa_ref[:, cols[hh]].astype(F32)).astype(BF16)
        qh = [q_ref[:, c] for c in cols]
        doh = [do_ref[:, c] for c in cols]

        def update(j, acc, mask):
            off = pl.multiple_of(j * tk, tk)
            out = []
            for hh in range(2):
                kt = k_ref[pl.ds(off, tk), cols[hh]]
                pr = jnp.exp(lax.dot_general(qh[hh], kt, NT_DIMS, preferred_element_type=F32))
                if mask is not None:
                    pr = jnp.where(mask, pr, 0.0)
                ds = pr * lax.dot_general(doh[hh], v_ref[pl.ds(off, tk), cols[hh]], NT_DIMS, preferred_element_type=F32)
                out.append(acc[hh] + jnp.dot(ds.astype(BF16), kt, preferred_element_type=F32))
            return tuple(out)

        n_full = _shift_div(i, tq, tk)
        qi = lax.broadcasted_iota(jnp.int32, (tq, tk), 0) + i * tq
        ki = lax.broadcasted_iota(jnp.int32, (tq, tk), 1)
        acc = lax.fori_loop(0, n_full, lambda j, a: update(j, a, None), (jnp.zeros((tq, LANES), F32),) * 2)
        for jj in range(max(tq // tk, 1)):
            acc = update(n_full + jj, acc, ki + (n_full + jj) * tk <= qi)
        low = lane < FOX_HEAD_DIM
        dq_ref[...] = (jnp.where(low, acc[0], acc[1]) * Q_SCALE).astype(BF16)
        rs_ref[...] = jnp.where(low, acc[0][:, _spare_lane(0):_spare_lane(0) + 1], acc[1][:, _spare_lane(1):_spare_lane(1) + 1])

    pair = pl.BlockSpec((s_len, 2 * LANES), lambda p, i: (0, p))
    tile2 = pl.BlockSpec((tq, 2 * LANES), lambda p, i: (i, p))
    out = pl.BlockSpec((tq, LANES), lambda p, i: (i, p))
    wide = jax.ShapeDtypeStruct((s_len, 2 * FOX_WIDTH), BF16)
    return _pcall(
        body, name=name, grid=(N_PAIRS, s_len // tq),
        in_specs=[tile2, out, out, pl.BlockSpec((tq, LANES), lambda p, i: (i, N_PAIRS + p)), pair, pair],
        out_specs=[out, out, tile2, tile2],
        out_shape=[jax.ShapeDtypeStruct((s_len, FOX_WIDTH), BF16), jax.ShapeDtypeStruct((s_len, FOX_WIDTH), F32), wide, wide],
        args=[qa, lse, o, dcat, ka, va], sem=("parallel", "parallel"), host=host)


def _foxa_dkv(qb, ka, va, doa, tq, tk, name, host=None):
    s_len = qb.shape[0]
    n_q = s_len // tq

    def body(k_ref, v_ref, q_ref, do_ref, dk_ref, dv_ref, cs_ref):
        j = pl.program_id(1)
        lane = lax.broadcasted_iota(jnp.int32, (1, LANES), 1)
        cols = [slice(LANES * hh, LANES * hh + LANES) for hh in range(2)]
        kh = [k_ref[:, c] for c in cols]
        vh = [v_ref[:, c] for c in cols]

        def update(i, acc, mask):
            off = pl.multiple_of(i * tq, tq)
            out = []
            for hh in range(2):
                qt = q_ref[pl.ds(off, tq), cols[hh]]
                dot = do_ref[pl.ds(off, tq), cols[hh]]
                pt = jnp.exp(lax.dot_general(kh[hh], qt, NT_DIMS, preferred_element_type=F32))
                if mask is not None:
                    pt = jnp.where(mask, pt, 0.0)
                dv = acc[2 * hh + 1] + jnp.dot(pt.astype(BF16), dot, preferred_element_type=F32)
                dst = pt * lax.dot_general(vh[hh], dot, NT_DIMS, preferred_element_type=F32)
                out += [acc[2 * hh] + jnp.dot(dst.astype(BF16), qt, preferred_element_type=F32), dv]
            return tuple(out)

        i0 = _shift_div(j, tk, tq)
        n_part = max(tk // tq, 1)
        ki = lax.broadcasted_iota(jnp.int32, (tk, tq), 0) + j * tk
        qi = lax.broadcasted_iota(jnp.int32, (tk, tq), 1)
        acc = (jnp.zeros((tk, LANES), F32),) * 4
        for ii in range(n_part):
            acc = update(i0 + ii, acc, ki <= qi + (i0 + ii) * tq)
        acc = lax.fori_loop(i0 + n_part, n_q, lambda i, a: update(i, a, None), acc)
        low = lane < FOX_HEAD_DIM
        dk_ref[...] = jnp.where(low, acc[0], acc[2]).astype(BF16)
        dv_ref[...] = jnp.where(low, acc[1], acc[3]).astype(BF16)
        cs_ref[...] = jnp.where(low, acc[0][:, _spare_lane(0) + 3:_spare_lane(0) + 4],
                                acc[2][:, _spare_lane(1) + 3:_spare_lane(1) + 4])

    pair = pl.BlockSpec((s_len, 2 * LANES), lambda p, j: (0, p))
    tile2 = pl.BlockSpec((tk, 2 * LANES), lambda p, j: (j, p))
    out = pl.BlockSpec((tk, LANES), lambda p, j: (j, p))
    return _pcall(
        body, name=name, grid=(N_PAIRS, s_len // tk), in_specs=[tile2, tile2, pair, pair], out_specs=[out, out, out],
        out_shape=[jax.ShapeDtypeStruct((s_len, FOX_WIDTH), BF16), jax.ShapeDtypeStruct((s_len, FOX_WIDTH), BF16),
                   jax.ShapeDtypeStruct((s_len, FOX_WIDTH), F32)],
        args=[ka, va, qb, doa], sem=("parallel", "parallel"), host=host)


def _mem_scores_t(q, kv, h):
    lo = h * MEM_HEAD_DIM
    st = lax.dot_general(kv[:, lo:lo + MEM_HEAD_DIM], q[:, lo:lo + MEM_HEAD_DIM], NT_DIMS,
                         preferred_element_type=F32) * (MEM_HEAD_DIM ** -0.5)
    e = jnp.exp(st - jnp.max(st, axis=0, keepdims=True))
    return e / jnp.sum(e, axis=0, keepdims=True)


def _memattn_fwd(q, kv, tile, name):
    s_len = q.shape[0]
    n_mem = kv.shape[0]

    def body(q_ref, kv_ref, o_ref):
        q = q_ref[...]
        kv = kv_ref[...]
        for h in range(MEM_HEADS):
            lo = h * MEM_HEAD_DIM
            pt = _mem_scores_t(q, kv, h).astype(BF16)
            vh = kv[:, MEM_INNER + lo:MEM_INNER + lo + MEM_HEAD_DIM]
            o_ref[:, lo:lo + MEM_HEAD_DIM] = lax.dot_general(pt, vh, TN_DIMS, preferred_element_type=F32).astype(BF16)

    return pl.pallas_call(
        body, name=name, grid=(s_len // tile,),
        in_specs=[pl.BlockSpec((tile, MEM_INNER), lambda i: (i, 0)), _full((n_mem, 2 * MEM_INNER))],
        out_specs=pl.BlockSpec((tile, MEM_INNER), lambda i: (i, 0)),
        out_shape=jax.ShapeDtypeStruct((s_len, MEM_INNER), BF16),
        compiler_params=_params("parallel"),
    )(q, kv)


def _memattn_bwd(q, kv, do, tile, name):
    s_len = q.shape[0]
    n_mem = kv.shape[0]
    scale = MEM_HEAD_DIM ** -0.5

    def body(q_ref, kv_ref, do_ref, dq_ref, dkv_ref):
        @pl.when(pl.program_id(0) == 0)
        def _():
            dkv_ref[...] = jnp.zeros_like(dkv_ref)

        q = q_ref[...]
        kv = kv_ref[...]
        do = do_ref[...]
        for h in range(MEM_HEADS):
            lo = h * MEM_HEAD_DIM
            qh = q[:, lo:lo + MEM_HEAD_DIM]
            kh = kv[:, lo:lo + MEM_HEAD_DIM]
            vh = kv[:, MEM_INNER + lo:MEM_INNER + lo + MEM_HEAD_DIM]
            doh = do[:, lo:lo + MEM_HEAD_DIM]
            pt = _mem_scores_t(q, kv, h)
            dkv_ref[:, MEM_INNER + lo:MEM_INNER + lo + MEM_HEAD_DIM] += jnp.dot(
                pt.astype(BF16), doh, preferred_element_type=F32)
            dpt = lax.dot_general(vh, doh, NT_DIMS, preferred_element_type=F32)
            dst = (pt * (dpt - jnp.sum(pt * dpt, axis=0, keepdims=True)) * scale).astype(BF16)
            dkv_ref[:, lo:lo + MEM_HEAD_DIM] += jnp.dot(dst, qh, preferred_element_type=F32)
            dq_ref[:, lo:lo + MEM_HEAD_DIM] = lax.dot_general(dst, kh, TN_DIMS, preferred_element_type=F32).astype(BF16)

    return pl.pallas_call(
        body, name=name, grid=(s_len // tile,),
        in_specs=[pl.BlockSpec((tile, MEM_INNER), lambda i: (i, 0)), _full((n_mem, 2 * MEM_INNER)),
                  pl.BlockSpec((tile, MEM_INNER), lambda i: (i, 0))],
        out_specs=[pl.BlockSpec((tile, MEM_INNER), lambda i: (i, 0)), _full((n_mem, 2 * MEM_INNER))],
        out_shape=[jax.ShapeDtypeStruct((s_len, MEM_INNER), BF16), jax.ShapeDtypeStruct((n_mem, 2 * MEM_INNER), F32)],
        compiler_params=_params("arbitrary"),
    )(q, kv, do)


def _loss_head(y, target, tile, name):
    s_len, d = y.shape

    def body(y_ref, t_ref, dy_ref, l_ref):
        @pl.when(pl.program_id(0) == 0)
        def _():
            l_ref[...] = jnp.zeros_like(l_ref)

        err = y_ref[...] - t_ref[...]
        dy_ref[...] = err * (1.0 / d)
        l_ref[...] += jnp.sum(err * err, axis=0, keepdims=True) * (0.5 / d)

    row = lambda i: (i, 0)
    return pl.pallas_call(
        body, name=name, grid=(s_len // tile,),
        in_specs=[pl.BlockSpec((tile, d), row), pl.BlockSpec((tile, d), row)],
        out_specs=[pl.BlockSpec((tile, d), row), _full((1, d))],
        out_shape=[jax.ShapeDtypeStruct((s_len, d), F32), jax.ShapeDtypeStruct((1, d), F32)],
        compiler_params=_params("arbitrary"),
    )(y, target)


def _attn_tile(s_len):
    return min(256, s_len // 2)


REST = ("w_out", "w_mq", "w_mk", "w_mv", "w_mo", "w_up", "w_down")
REST_DQ = ("w_up", "w_down")
REST_DKV = ("w_out", "w_mq", "w_mk", "w_mv", "w_mo")
SHARD_AXIS = {"w_in": 1, "w_out": 0, "w_mq": 0, "w_mk": 0, "w_mv": 0, "w_mo": 1, "w_up": 1, "w_down": 0}


def _full_from_shards(sh, axis):
    n, r, c = sh.shape
    if axis == 0:
        return sh.reshape(n * r, c)
    return sh.transpose(1, 0, 2).reshape(r, n * c)


def _rest_weights(lands):
    w = {n: _full_from_shards(sh, SHARD_AXIS[n]) for n, sh in zip(REST, lands)}
    w["w_mkv"] = jnp.concatenate([w.pop("w_mk"), w.pop("w_mv")], axis=1)
    return w


def _w_in_cat(land):
    return jnp.pad(land.reshape(-1, land.shape[2]), ((0, IN_CAT - IN_COLS), (0, 0)))


def _layer_fwd(x0, mem, w, l, rest_src=None, next_src=None):
    s_len = x0.shape[0]
    tile = min(512, s_len)
    tile_ff = min(256, s_len)
    ta = _attn_tile(s_len)
    ident = lambda z: z
    sv = {"x0": x0}

    h1, ag, qkv, fl = _rms_matmul(
        x0, w["norm_mix_pre"], w["w_in_cat"],
        [(0, 2 * CONV_CH, [(F32, ident)]), (2 * CONV_CH, IN_MAIN, [(BF16, ident)]), (IN_MAIN, IN_CAT, [(F32, ident)])],
        tile, f"mix_in_{l}", w_t=True)
    u3 = _conv_fwd(ag, w["conv_w"], w["conv_b"], w["conv_ln_g"], w["conv_ln_b"], tile, f"conv_fwd_{l}")
    qa, ka, va = _foxa_prep(qkv, fl, w["b_forget"], tile, f"fox_prep_{l}")
    next_land = None
    if rest_src is None:
        o, lse, o_bf = _foxa_fwd(qa, ka, va, ta, 2 * ta, f"fox_fwd_{l}")
    else:
        srcs = list(rest_src) + list(next_src or [])
        (o, lse, o_bf), lands = _foxa_fwd(qa, ka, va, ta, 2 * ta, f"fox_fwd_{l}", host=(srcs, [True] * len(srcs)))
        w = {**w, **_rest_weights(lands[:len(REST)])}
        next_land = lands[len(REST):]
    cat = jnp.concatenate([u3, o_bf], axis=1)
    y1, x1 = _matmul_resnorm(cat, w["w_out"], x0, w["norm_mix_post"], tile, f"mix_out_{l}")
    sv.update(h1=h1, ag=ag, fl=fl, qa=qa, ka=ka, va=va, o=o, lse=lse, cat=cat, y1=y1, x1=x1)

    h2, qm = _rms_matmul(x1, w["norm_mem_pre"], w["w_mq"], [(0, MEM_INNER, [(BF16, ident)])], tile, f"mem_q_{l}")
    mem_n, kv = _rms_matmul(mem, w["norm_memkv"], w["w_mkv"], [(0, 2 * MEM_INNER, [(BF16, ident)])],
                            mem.shape[0], f"mem_kv_{l}")
    om = _memattn_fwd(qm, kv, tile, f"mem_attn_fwd_{l}")
    y2, x2 = _matmul_resnorm(om, w["w_mo"], x1, w["norm_mem_post"], tile, f"mem_out_{l}")
    sv.update(h2=h2, qm=qm, mem_n=mem_n, kv=kv, om=om, y2=y2, x2=x2)

    relu2 = lambda z: jnp.square(jnp.maximum(z, 0.0))
    h3, pre, hid = _rms_matmul(x2, w["norm_mlp_pre"], w["w_up"], [(0, D_FF, [(BF16, ident), (BF16, relu2)])], tile_ff,
                               f"mlp_up_{l}")
    y3, x3 = _matmul_resnorm(hid, w["w_down"], x2, w["norm_mlp_post"], tile_ff, f"mlp_down_{l}")
    sv.update(h3=h3, pre=pre, hid=hid, y3=y3)
    return x3, sv, w, next_land


def _layer_bwd(dx3, mem, w, sv, l, scatter_rest=False, dkv_src=None, scatter_w_in=False):
    s_len = dx3.shape[0]
    tile = min(512, s_len)
    tile_ff = min(256, s_len)
    ta = _attn_tile(s_len)
    tk = min(512, s_len)
    n_mem = mem.shape[0]
    g = {}

    dy3, dpre, g["norm_mlp_post"] = _resnorm_bwd_mm(dx3, sv["y3"], w["norm_mlp_post"], w["w_down"], tile_ff,
                                                    f"mlp_down_bwd_{l}", BF16, pre=sv["pre"])
    g["w_down"] = _matmul_tn_shards(sv["hid"], dy3, 0, tk, f"dw_down_{l}")
    dx2, g["norm_mlp_pre"] = _mm_prenorm_bwd(dpre, w["w_up"], sv["x2"], w["norm_mlp_pre"], dx3, tile_ff,
                                             f"mlp_up_bwd_{l}")
    g["w_up"] = _matmul_tn_shards(sv["h3"], dpre, 1, tk, f"dw_up_{l}")

    dy2, dom, g["norm_mem_post"] = _resnorm_bwd_mm(dx2, sv["y2"], w["norm_mem_post"], w["w_mo"], tile,
                                                   f"mem_out_bwd_{l}", BF16)
    g["w_mo"] = _matmul_tn_shards(sv["om"], dy2, 1, tk, f"dw_mo_{l}")
    dqm, dkv = _memattn_bwd(sv["qm"], sv["kv"], dom, tile, f"mem_attn_bwd_{l}")
    dkvb = dkv.astype(BF16)
    g["w_mq"] = _matmul_tn_shards(sv["h2"], dqm, 0, tk, f"dw_mq_{l}")
    dx1, g["norm_mem_pre"] = _mm_prenorm_bwd(dqm, w["w_mq"], sv["x1"], w["norm_mem_pre"], dx2, tile, f"mem_q_bwd_{l}")
    _, g["norm_memkv"] = _mm_prenorm_bwd(dkvb, w["w_mkv"], mem, w["norm_memkv"], None, n_mem, f"mem_kv_bwd_{l}")
    g["w_mk"] = _matmul_tn_shards(sv["mem_n"], dkvb[:, :MEM_INNER], 0, n_mem, f"dw_mk_{l}")
    g["w_mv"] = _matmul_tn_shards(sv["mem_n"], dkvb[:, MEM_INNER:], 0, n_mem, f"dw_mv_{l}")

    dy1, dcat, g["norm_mix_post"] = _resnorm_bwd_mm(dx1, sv["y1"], w["norm_mix_post"], w["w_out"], tile,
                                                    f"mix_out_bwd_{l}", F32)
    g["w_out"] = _matmul_tn_shards(sv["cat"], dy1, 0, tk, f"dw_out_{l}")
    dq_args = (sv["qa"], sv["lse"], sv["o"], dcat, sv["ka"], sv["va"], ta, 2 * ta, f"fox_dq_{l}")
    rest_land, dkv_land = None, None
    if scatter_rest:
        with_dq = [g[n] for n in REST_DQ]
        (dq, rs, qb, doa), land_dq = _foxa_dq(*dq_args, host=(with_dq, [False] * len(with_dq)))
        with_dkv = [g[n] for n in REST_DKV] + list(dkv_src or [])
        (dk, dv, cs), land_dkv = _foxa_dkv(qb, sv["ka"], sv["va"], doa, 2 * ta, ta, f"fox_dkv_{l}",
                                           host=(with_dkv, [False] * len(with_dkv)))
        by_name = dict(zip(REST_DQ + REST_DKV, land_dq + land_dkv))
        rest_land = [by_name[n] for n in REST]
        dkv_land = land_dkv[len(REST_DKV):]
    else:
        dq, rs, qb, doa = _foxa_dq(*dq_args)
        dk, dv, cs = _foxa_dkv(qb, sv["ka"], sv["va"], doa, 2 * ta, ta, f"fox_dkv_{l}")
    dfl, db = _cumsum_bwd(rs, cs, sv["fl"], w["b_forget"], tile, f"cumsum_bwd_{l}")
    g["b_forget"] = db[:, :FOX_HEADS]
    dag, dconv_w, dconv_v = _conv_bwd(sv["ag"], dcat, w["conv_w"], w["conv_b"], w["conv_ln_g"], w["conv_ln_b"], tile,
                                      f"conv_bwd_{l}")
    g["conv_w"] = dconv_w[:CONV_WIDTH]
    g["conv_b"], g["conv_ln_g"], g["conv_ln_b"] = dconv_v[0:1], dconv_v[1:2], dconv_v[2:3]
    dz = [dag, dq, dk, dv, dfl]
    dw_in_t = _matmul_tn_rows(dz, sv["h1"], tk, f"dw_in_{l}")[:IN_COLS]
    g["w_in"] = dw_in_t.reshape(N_DEV, IN_COLS // N_DEV, D_MODEL).astype(BF16)
    res = _mm_prenorm_bwd(dz, w["w_in_cat"], sv["x0"], w["norm_mix_pre"], dx1, tile, f"mix_in_bwd_{l}",
                          host=([g["w_in"]], [False]) if scatter_w_in else None, w_t=True)
    (dx0, g["norm_mix_pre"]), w_in_land = res if scatter_w_in else (res, None)
    return dx0, g, rest_land, dkv_land, w_in_land


def _sum_blocks(a, name):
    n, rows, cols = a.shape

    def body(a_ref, o_ref):
        acc = a_ref[0]
        for j in range(1, n):
            acc = acc + a_ref[j]
        o_ref[...] = acc

    return pl.pallas_call(
        body, name=name, in_specs=[_full((n, rows, cols))], out_specs=_full((rows, cols)),
        out_shape=jax.ShapeDtypeStruct((rows, cols), F32), grid=(1,),
    )(a)


def _adamw(gparts, w, m, v, tile, name):
    n_l, rows, cols = w.shape
    n = gparts[0].shape[0]
    n_t = rows // tile
    c1 = 1.0 - ADAM_B1
    c2 = 1.0 - ADAM_B2
    bc1 = 1.0 - ADAM_B1 ** ADAM_STEP
    bc2 = 1.0 - ADAM_B2 ** ADAM_STEP

    def body(*refs):
        gp_refs, (w_ref, m_ref, v_ref, g_ref, d_ref, mo_ref, vo_ref) = refs[:n_l], refs[n_l:]
        layer = pl.program_id(0)
        g = None
        for l, gp_ref in enumerate(gp_refs):
            gl = gp_ref[0].astype(F32)
            for j in range(1, n):
                gl = gl + gp_ref[j].astype(F32)
            g = gl if g is None else jnp.where(layer == l, gl, g)
        g_ref[...] = g
        m_new = ADAM_B1 * m_ref[...] + c1 * g
        v_new = ADAM_B2 * v_ref[...] + c2 * (g * g)
        mo_ref[...] = m_new
        vo_ref[...] = v_new
        d_ref[...] = -ADAM_LR * ((m_new / bc1) / (jnp.sqrt(v_new / bc2) + ADAM_EPS) + ADAM_WD * w_ref[...])

    def gp_spec(l):
        return pl.BlockSpec((n, tile, cols), lambda L, i: (0, jnp.where(L < l, 0, jnp.where(L > l, n_t - 1, i)), 0))

    spec = pl.BlockSpec((None, tile, cols), lambda L, i: (L, i, 0))
    return pl.pallas_call(
        body, name=name, grid=(n_l, n_t),
        in_specs=[gp_spec(l) for l in range(n_l)] + [spec, spec, spec],
        out_specs=[spec] * 4, out_shape=[jax.ShapeDtypeStruct((n_l, rows, cols), F32)] * 4,
        compiler_params=_params("arbitrary", "arbitrary"),
    )(*gparts, w, m, v)


def _adamw_cols(gparts, w, m, v, name):
    n, rows, cols = gparts[0].shape
    n_l = len(gparts)
    c1 = 1.0 - ADAM_B1
    c2 = 1.0 - ADAM_B2
    bc1 = 1.0 - ADAM_B1 ** ADAM_STEP
    bc2 = 1.0 - ADAM_B2 ** ADAM_STEP

    def body(*refs):
        gp_refs, (w_ref, m_ref, v_ref, g_ref, d_ref, mo_ref, vo_ref) = refs[:n_l], refs[n_l:]
        layer = pl.program_id(0)
        g = None
        for l, gp_ref in enumerate(gp_refs):
            gl = gp_ref[0].astype(F32)
            for j in range(1, n):
                gl = gl + gp_ref[j].astype(F32)
            g = gl if g is None else jnp.where(layer == l, gl, g)
        g_ref[...] = g
        m_new = ADAM_B1 * m_ref[...] + c1 * g
        v_new = ADAM_B2 * v_ref[...] + c2 * (g * g)
        mo_ref[...] = m_new
        vo_ref[...] = v_new
        d_ref[...] = -ADAM_LR * ((m_new / bc1) / (jnp.sqrt(v_new / bc2) + ADAM_EPS) + ADAM_WD * w_ref[...])

    spec = pl.BlockSpec((rows, cols), lambda L: (0, L))
    return pl.pallas_call(
        body, name=name, grid=(n_l,),
        in_specs=[_full((n, rows, cols))] * n_l + [spec, spec, spec],
        out_specs=[spec] * 4, out_shape=[jax.ShapeDtypeStruct((rows, n_l * cols), F32)] * 4,
        compiler_params=_params("arbitrary"),
    )(*gparts, w, m, v)


def _pack_rows(parts, total_rows):
    flat = [p.reshape(-1, D_MODEL) for p in parts]
    used = sum(f.shape[0] for f in flat)
    if total_rows > used:
        flat.append(jnp.zeros((total_rows - used, D_MODEL), flat[0].dtype))
    return jnp.concatenate(flat, axis=0)


def kernel(x, mem, norm_mix_pre, norm_mix_post, w_in, b_forget, conv_w, conv_b, conv_ln_g, conv_ln_b, w_out, norm_mem_pre, norm_mem_post, norm_memkv, w_mq, w_mk, w_mv, w_mo, norm_mlp_pre, norm_mlp_post, w_up, w_down, loss_target, m_norm_mix_pre, m_norm_mix_post, m_w_in, m_b_forget, m_conv_w, m_conv_b, m_conv_ln_g, m_conv_ln_b, m_w_out, m_norm_mem_pre, m_norm_mem_post, m_norm_memkv, m_w_mq, m_w_mk, m_w_mv, m_w_mo, m_norm_mlp_pre, m_norm_mlp_post, m_w_up, m_w_down, v_norm_mix_pre, v_norm_mix_post, v_w_in, v_b_forget, v_conv_w, v_conv_b, v_conv_ln_g, v_conv_ln_b, v_w_out, v_norm_mem_pre, v_norm_mem_post, v_norm_memkv, v_w_mq, v_w_mk, v_w_mv, v_w_mo, v_norm_mlp_pre, v_norm_mlp_post, v_w_up, v_w_down):
    p = dict(locals())
    names = ("norm_mix_pre", "norm_mix_post", "w_in", "b_forget", "conv_w", "conv_b", "conv_ln_g", "conv_ln_b", "w_out",
             "norm_mem_pre", "norm_mem_post", "norm_memkv", "w_mq", "w_mk", "w_mv", "w_mo", "norm_mlp_pre",
             "norm_mlp_post", "w_up", "w_down")
    me = 4 * lax.axis_index("x") + 2 * lax.axis_index("y") + lax.axis_index("c")
    conv_cols = conv_w.shape[2]
    s_len = x.shape[1]
    bf = {n: p[n].astype(BF16) for n in REST}
    in_cols = w_in.shape[2]
    w_in_t = {pre: jnp.transpose(p[pre + "w_in"], (2, 0, 1)).reshape(in_cols, DEPTH * D_MODEL) for pre in ("", "m_", "v_")}
    bf_in = w_in_t[""].astype(BF16)
    bf["w_in"] = [bf_in[:, l * D_MODEL:(l + 1) * D_MODEL] for l in range(DEPTH)]

    conv_pack = _pack_rows([jnp.pad(conv_w, ((0, 0), (0, CONV_PAD - CONV_WIDTH), (0, 0)))], 8)
    win_land, conv_land = _exchange([bf["w_in"][0], conv_pack], [True, True], "gather_first")
    conv_rows = DEPTH * CONV_PAD * conv_cols // D_MODEL
    conv_full = conv_land[:, :conv_rows].reshape(N_DEV, DEPTH, CONV_PAD, conv_cols)
    conv_full = conv_full.transpose(1, 2, 0, 3).reshape(DEPTH, CONV_PAD, N_DEV * conv_cols)
    b_forget_pad = jnp.pad(b_forget, ((0, 0), (0, LANES - FOX_HEADS)))

    def first_weights(l, land):
        w = {"w_in_cat": _w_in_cat(land), "conv_w": conv_full[l], "b_forget": b_forget_pad[l:l + 1]}
        for n in VEC[:-1]:
            w[n] = p[n][l:l + 1]
        return w

    h, sv0, w0, win1_land = _layer_fwd(x[0], mem[0], first_weights(0, win_land), 0,
                                       rest_src=[bf[n][0] for n in REST], next_src=[bf["w_in"][1]])
    h, sv1, w1, _ = _layer_fwd(h, mem[0], first_weights(1, win1_land[0]), 1, rest_src=[bf[n][1] for n in REST])
    dh, loss_row = _loss_head(h, loss_target[0], min(512, s_len), "loss_head")
    dh, g1, rest_g1, _, _ = _layer_bwd(dh, mem[0], w1, sv1, 1, scatter_rest=True)
    grad_x, g0, rest_g0, win1_g, win0_g = _layer_bwd(dh, mem[0], w0, sv0, 0, scatter_rest=True, dkv_src=[g1["w_in"]],
                                                     scatter_w_in=True)

    def small_rows(get):
        rows = [jnp.concatenate([get(n) for n in VEC_1024], axis=0),
                jnp.concatenate([get(n) for n in VEC_512], axis=0).reshape(len(VEC_512), D_MODEL),
                jnp.pad(get("b_forget").reshape(1, -1), ((0, 0), (0, D_MODEL - DEPTH * FOX_HEADS)))]
        return jnp.concatenate(rows, axis=0)

    def tap_rows(conv):
        return jnp.pad(conv.reshape(1, -1), ((0, 0), (0, 4 * D_MODEL - conv.size))).reshape(4, D_MODEL)

    n_vec_rows = DEPTH * len(VEC_1024) + len(VEC_512) + 1
    part = small_rows(lambda n: jnp.concatenate([g0[n], g1[n]], axis=0))
    conv_part = jnp.stack([g0["conv_w"], g1["conv_w"]]).reshape(CONV_WIDTH, D_MODEL)
    n_part = 1 + n_vec_rows + CONV_WIDTH
    pad_rows = -n_part % 8
    small_land = _exchange([jnp.concatenate([loss_row, part, conv_part, jnp.zeros((pad_rows, D_MODEL), F32)], axis=0)],
                           [True], "gather_small")[0]
    total = _sum_blocks(small_land, "sum_small")
    loss = jnp.sum(total[0])
    conv_g = total[1 + n_vec_rows:n_part].reshape(DEPTH, CONV_WIDTH, CONV_CH)
    conv_g = lax.dynamic_slice_in_dim(conv_g, me * conv_cols, conv_cols, axis=2)
    fill = jnp.zeros((SMALL_ROWS - n_vec_rows - 4, D_MODEL), F32)

    def small_pack(vec_rows, conv):
        return jnp.concatenate([vec_rows, tap_rows(conv), fill], axis=0)

    small_out = _adamw([small_pack(total[1:1 + n_vec_rows], conv_g)[None]],
                       *[small_pack(small_rows(lambda n: p[pre + n]), p[pre + "conv_w"])[None] for pre in ("", "m_", "v_")],
                       SMALL_ROWS, "adamw_small")

    def unpack_small(buf):
        out = {}
        for k, n in enumerate(VEC_1024):
            out[n] = buf[DEPTH * k:DEPTH * (k + 1)]
        at = DEPTH * len(VEC_1024)
        for k, n in enumerate(VEC_512):
            out[n] = buf[at + k].reshape(DEPTH, CONV_CH)
        at += len(VEC_512)
        out["b_forget"] = buf[at, :DEPTH * FOX_HEADS].reshape(DEPTH, FOX_HEADS)
        out["conv_w"] = buf[at + 1:at + 5].reshape(-1)[:DEPTH * CONV_WIDTH * conv_cols].reshape(DEPTH, CONV_WIDTH, conv_cols)
        return out

    big_out = {n: _adamw([rest_g0[i], rest_g1[i]], p[n], p["m_" + n], p["v_" + n], ADAMW_TILE[n], f"adamw_{n}")
               for i, n in enumerate(REST)}
    in_out = _adamw_cols([win0_g[0], win1_g[0]], w_in_t[""], w_in_t["m_"], w_in_t["v_"], "adamw_w_in")
    big_out["w_in"] = [a.reshape(in_cols, DEPTH, D_MODEL).transpose(1, 2, 0) for a in in_out]

    result = [loss, grad_x[None]]
    for k in range(4):
        smalls = unpack_small(small_out[k][0])
        result += [big_out[n][k] if n in big_out else smalls[n] for n in names]
    return tuple(result)
```

```python
import functools

import jax
import jax.numpy as jnp
from jax import lax
from jax.experimental import pallas as pl
from jax.experimental.pallas import tpu as pltpu

F32 = jnp.float32
BF16 = jnp.bfloat16

N_DEV = 8
DEPTH = 2
D_MODEL = 1024
CONV_CH = 512
CONV_WIDTH = 31
CONV_PAD = 32
FOX_HEADS = 8
FOX_HEAD_DIM = 64
FOX_WIDTH = 512
N_PAIRS = 4
MEM_HEADS = 4
MEM_HEAD_DIM = 128
MEM_INNER = 512
D_FF = 4096
IN_MAIN = 2560
IN_COLS = 2568
IN_CAT = IN_MAIN + 128
LANES = 128
EPS = 1e-6
NEG_INF = -1e30

ADAM_LR = 0.001
ADAM_B1 = 0.9
ADAM_B2 = 0.999
ADAM_EPS = 1e-08
ADAM_WD = 0.01
ADAM_STEP = 10

NT_DIMS = (((1,), (1,)), ((), ()))
TN_DIMS = (((0,), (0,)), ((), ()))

BIG = ("w_in", "w_out", "w_mq", "w_mk", "w_mv", "w_mo", "w_up", "w_down")
ADAMW_TILE = {"w_out": 128, "w_mq": 128, "w_mk": 128, "w_mv": 128, "w_mo": 512, "w_up": 256, "w_down": 128}

VEC_1024 = ("norm_mix_pre", "norm_mix_post", "norm_mem_pre", "norm_mem_post", "norm_memkv", "norm_mlp_pre", "norm_mlp_post")
VEC_512 = ("conv_b", "conv_ln_g", "conv_ln_b")
VEC = VEC_1024 + VEC_512 + ("b_forget",)
SMALL_ROWS = 32


def _sigmoid(x):
    return 1.0 / (1.0 + jnp.exp(-x))


def _rms(x, g):
    r = lax.rsqrt(jnp.mean(x * x, axis=-1, keepdims=True) + EPS)
    return x * r * g


def _rms_bwd(x, g, dh):
    r = lax.rsqrt(jnp.mean(x * x, axis=-1, keepdims=True) + EPS)
    gh = dh * g
    c = jnp.mean(gh * x, axis=-1, keepdims=True)
    dx = r * gh - x * (r * r * r * c)
    dg = jnp.sum(dh * (x * r), axis=0, keepdims=True)
    return dx, dg


def _full(shape):
    nd = len(shape)
    return pl.BlockSpec(shape, lambda *_: (0,) * nd)


def _params(*sem):
    return pltpu.CompilerParams(dimension_semantics=sem)


def _exchange_copies(src_refs, out_refs, same, send_sems, recv_sems, local_sems, with_recvs):
    x, y, c = lax.axis_index("x"), lax.axis_index("y"), lax.axis_index("c")
    me = 4 * x + 2 * y + c
    local, sends, recvs = [], [], []
    for a, (s_ref, o_ref) in enumerate(zip(src_refs, out_refs)):
        def mine(idx, s_ref=s_ref, whole=same[a]):
            return s_ref if whole else s_ref.at[idx]

        local.append(pltpu.make_async_copy(mine(me), o_ref.at[me], local_sems.at[a]))
        for k in range(1, N_DEV):
            px = 1 - x if k & 4 else x
            py = 1 - y if k & 2 else y
            pc = 1 - c if k & 1 else c
            peer = 4 * px + 2 * py + pc
            sem = a * (N_DEV - 1) + k - 1
            common = dict(send_sem=send_sems.at[sem], recv_sem=recv_sems.at[sem], device_id=(px, py, pc),
                          device_id_type=pl.DeviceIdType.MESH)
            sends.append(pltpu.make_async_remote_copy(src_ref=mine(peer), dst_ref=o_ref.at[me], **common))
            if with_recvs:
                recvs.append(pltpu.make_async_remote_copy(src_ref=mine(peer), dst_ref=o_ref.at[peer], **common))
    return local, sends, recvs


def _gather_copies(src_refs, out_refs, send_sems, recv_sems, local_sems, phase):
    x, y, c = lax.axis_index("x"), lax.axis_index("y"), lax.axis_index("c")
    sibling = (x, y, 1 - c)
    chips = [(1 - x, y), (x, 1 - y), (1 - x, 1 - y)]

    def idx(px, py, pc):
        return 4 * px + 2 * py + pc

    local, first, arrive, passed, final = [], [], [], [], []
    for a, (s_ref, o_ref) in enumerate(zip(src_refs, out_refs)):
        def cp(k, src, block, to, a=a, o_ref=o_ref):
            sem = a * (N_DEV - 1) + k
            return pltpu.make_async_remote_copy(src_ref=src, dst_ref=o_ref.at[block], send_sem=send_sems.at[sem],
                                                recv_sem=recv_sems.at[sem], device_id=to, device_id_type=pl.DeviceIdType.MESH)

        me = idx(x, y, c)
        if phase != 1:
            local.append(pltpu.make_async_copy(s_ref, o_ref.at[me], local_sems.at[a]))
            first.append(cp(0, s_ref, me, sibling))
        if phase == 2:
            final.append(cp(0, s_ref, idx(x, y, 1 - c), sibling))
        for j, chip in enumerate(chips):
            theirs = idx(*chip, c)
            if phase != 1:
                first.append(cp(1 + j, s_ref, me, (*chip, c)))
            if phase == 1:
                arrive.append(cp(1 + j, s_ref, theirs, (*chip, c)))
            if phase != 0:
                passed.append(cp(4 + j, o_ref.at[theirs], theirs, sibling))
            if phase == 2:
                final.append(cp(4 + j, s_ref, idx(*chip, 1 - c), sibling))
    return local, first, arrive, passed, final


def _pcall(body, *, name, grid, in_specs, out_specs, out_shape, args, scratch_shapes=(), sem=(), host=None):
    if host is None:
        return pl.pallas_call(body, name=name, grid=grid, in_specs=in_specs, out_specs=out_specs, out_shape=out_shape,
                              scratch_shapes=list(scratch_shapes), compiler_params=_params(*sem))(*args)
    srcs, same = host
    n_in, n_out, n_scr, n_h = len(in_specs), len(out_specs), len(scratch_shapes), len(srcs)
    hbm = pl.BlockSpec(memory_space=pltpu.HBM)
    lands = [jax.ShapeDtypeStruct((N_DEV,) + (s.shape if whole else s.shape[1:]), s.dtype) for s, whole in zip(srcs, same)]

    def wrapped(*refs):
        ins, src_refs = refs[:n_in], refs[n_in:n_in + n_h]
        outs = refs[n_in + n_h:n_in + n_h + n_out]
        land_refs = refs[n_in + n_h + n_out:n_in + 2 * n_h + n_out]
        scr = refs[n_in + 2 * n_h + n_out:n_in + 2 * n_h + n_out + n_scr]
        sems = refs[n_in + 2 * n_h + n_out + n_scr:]
        ids = [pl.program_id(d) for d in range(len(grid))]
        first = functools.reduce(jnp.logical_and, [i == 0 for i in ids])
        last = functools.reduce(jnp.logical_and, [i == n - 1 for i, n in zip(ids, grid)])
        later = functools.reduce(jnp.logical_and, [ids[0] == (3 * grid[0]) // 4] + [i == 0 for i in ids[1:]])

        if all(same):
            @pl.when(first)
            def _():
                local, sends, _, _, _ = _gather_copies(src_refs, land_refs, *sems, 0)
                for cp in local + sends:
                    cp.start()

            body(*ins, *outs, *scr)

            @pl.when(later)
            def _():
                _, _, arrive, passed, _ = _gather_copies(src_refs, land_refs, *sems, 1)
                for cp in arrive:
                    cp.wait_recv()
                for cp in passed:
                    cp.start()

            @pl.when(last)
            def _():
                local, sends, _, passed, final = _gather_copies(src_refs, land_refs, *sems, 2)
                for cp in final:
                    cp.wait_recv()
                for cp in sends + passed:
                    cp.wait_send()
                for cp in local:
                    cp.wait()
        else:
            @pl.when(first)
            def _():
                local, sends, _ = _exchange_copies(src_refs, land_refs, same, *sems, False)
                for cp in local + sends:
                    cp.start()

            body(*ins, *outs, *scr)

            @pl.when(last)
            def _():
                local, sends, recvs = _exchange_copies(src_refs, land_refs, same, *sems, True)
                for cp in recvs:
                    cp.wait_recv()
                for cp in sends:
                    cp.wait_send()
                for cp in local:
                    cp.wait()

    n_sem = n_h * (N_DEV - 1)
    res = pl.pallas_call(
        wrapped, name=name, grid=grid, in_specs=list(in_specs) + [hbm] * n_h, out_specs=list(out_specs) + [hbm] * n_h,
        out_shape=list(out_shape) + lands,
        scratch_shapes=list(scratch_shapes) + [pltpu.SemaphoreType.DMA((n_sem,)), pltpu.SemaphoreType.DMA((n_sem,)),
                                               pltpu.SemaphoreType.DMA((n_h,))],
        compiler_params=_params(*(("arbitrary",) * len(grid))),
    )(*args, *srcs)
    return list(res[:n_out]), list(res[n_out:])


def _exchange(srcs, same, name):
    def body():
        pass

    return _pcall(body, name=name, grid=(1,), in_specs=[], out_specs=[], out_shape=[], args=[], host=(srcs, same))[1]


def _rms_matmul(x, g, w, segs, tile, name, host=None, w_t=False):
    s_len, d = x.shape
    chunk = 512

    def body(x_ref, g_ref, w_ref, h_ref, *outs):
        h = _rms(x_ref[...], g_ref[...]).astype(BF16)
        h_ref[...] = h
        oi = 0
        for c0, c1, fns in segs:
            for a in range(c0, c1, chunk):
                b = min(a + chunk, c1)
                if w_t:
                    z = lax.dot_general(h, w_ref[a:b, :], NT_DIMS, preferred_element_type=F32)
                else:
                    z = jnp.dot(h, w_ref[:, a:b], preferred_element_type=F32)
                for k, (dt, fn) in enumerate(fns):
                    outs[oi + k][:, a - c0:b - c0] = fn(z).astype(dt)
            oi += len(fns)

    out_shape = [jax.ShapeDtypeStruct((s_len, d), BF16)]
    out_specs = [pl.BlockSpec((tile, d), lambda i: (i, 0))]
    for c0, c1, fns in segs:
        for dt, _ in fns:
            out_shape.append(jax.ShapeDtypeStruct((s_len, c1 - c0), dt))
            out_specs.append(pl.BlockSpec((tile, c1 - c0), lambda i: (i, 0)))
    return _pcall(
        body, name=name, grid=(s_len // tile,),
        in_specs=[pl.BlockSpec((tile, d), lambda i: (i, 0)), _full((1, d)), _full(w.shape)],
        out_specs=out_specs, out_shape=out_shape, args=[x, g, w], sem=("parallel",), host=host)


def _matmul_resnorm(a, w, x, g, tile, name):
    s_len, k = a.shape
    d = w.shape[1]

    def body(a_ref, w_ref, x_ref, g_ref, y_ref, xo_ref):
        y = jnp.dot(a_ref[...], w_ref[...], preferred_element_type=F32)
        y_ref[...] = y
        xo_ref[...] = x_ref[...] + _rms(y, g_ref[...])

    row = lambda i: (i, 0)
    return pl.pallas_call(
        body, name=name, grid=(s_len // tile,),
        in_specs=[pl.BlockSpec((tile, k), row), _full((k, d)), pl.BlockSpec((tile, d), row), _full((1, d))],
        out_specs=[pl.BlockSpec((tile, d), row), pl.BlockSpec((tile, d), row)],
        out_shape=[jax.ShapeDtypeStruct((s_len, d), F32), jax.ShapeDtypeStruct((s_len, d), F32)],
        compiler_params=_params("parallel"),
    )(a, w, x, g)


def _resnorm_bwd_mm(dx, y, g, w, tile, name, out_dtype, pre=None):
    s_len, d = dx.shape
    k = w.shape[0]
    chunk = 512

    def body(*refs):
        if pre is None:
            dx_ref, y_ref, g_ref, w_ref, dy_ref, da_ref, dg_ref = refs
        else:
            dx_ref, y_ref, g_ref, w_ref, pre_ref, dy_ref, da_ref, dg_ref = refs
        dy, dg = _rms_bwd(y_ref[...], g_ref[...], dx_ref[...])
        dyb = dy.astype(BF16)
        dy_ref[...] = dyb

        @pl.when(pl.program_id(0) == 0)
        def _():
            dg_ref[...] = jnp.zeros_like(dg_ref)

        dg_ref[...] += dg
        for a in range(0, k, chunk):
            b = min(a + chunk, k)
            da = lax.dot_general(dyb, w_ref[a:b, :], NT_DIMS, preferred_element_type=F32)
            if pre is not None:
                da = da * (2.0 * jnp.maximum(pre_ref[:, a:b].astype(F32), 0.0))
            da_ref[:, a:b] = da.astype(out_dtype)

    row = lambda i: (i, 0)
    in_specs = [pl.BlockSpec((tile, d), row), pl.BlockSpec((tile, d), row), _full((1, d)), _full((k, d))]
    args = [dx, y, g, w]
    if pre is not None:
        in_specs.append(pl.BlockSpec((tile, k), row))
        args.append(pre)
    return pl.pallas_call(
        body, name=name, grid=(s_len // tile,), in_specs=in_specs,
        out_specs=[pl.BlockSpec((tile, d), row), pl.BlockSpec((tile, k), row), _full((1, d))],
        out_shape=[jax.ShapeDtypeStruct((s_len, d), BF16), jax.ShapeDtypeStruct((s_len, k), out_dtype),
                   jax.ShapeDtypeStruct((1, d), F32)],
        compiler_params=_params("arbitrary"),
    )(*args)


def _mm_prenorm_bwd(dz, w, x, g, dres, tile, name, host=None, w_t=False):
    pieces = list(dz) if isinstance(dz, (list, tuple)) else [dz]
    n_p = len(pieces)
    widths = [p.shape[1] for p in pieces]
    s_len = pieces[0].shape[0]
    d = w.shape[1] if w_t else w.shape[0]

    def body(*refs):
        dz_refs, rest = refs[:n_p], refs[n_p:]
        if dres is None:
            w_ref, x_ref, g_ref, dx_ref, dg_ref = rest
        else:
            w_ref, x_ref, g_ref, dres_ref, dx_ref, dg_ref = rest
        dh, off = None, 0
        for dz_ref, width in zip(dz_refs, widths):
            if w_t:
                part = jnp.dot(dz_ref[...], w_ref[off:off + width, :], preferred_element_type=F32)
            else:
                part = lax.dot_general(dz_ref[...], w_ref[:, off:off + width], NT_DIMS, preferred_element_type=F32)
            dh = part if dh is None else dh + part
            off += width
        dx, dg = _rms_bwd(x_ref[...], g_ref[...], dh)
        if dres is not None:
            dx = dx + dres_ref[...]
        dx_ref[...] = dx

        @pl.when(pl.program_id(0) == 0)
        def _():
            dg_ref[...] = jnp.zeros_like(dg_ref)

        dg_ref[...] += dg

    row = lambda i: (i, 0)
    in_specs = [pl.BlockSpec((tile, width), row) for width in widths]
    in_specs += [_full(w.shape), pl.BlockSpec((tile, d), row), _full((1, d))]
    args = pieces + [w, x, g]
    if dres is not None:
        in_specs.append(pl.BlockSpec((tile, d), row))
        args.append(dres)
    return _pcall(
        body, name=name, grid=(s_len // tile,), in_specs=in_specs,
        out_specs=[pl.BlockSpec((tile, d), row), _full((1, d))],
        out_shape=[jax.ShapeDtypeStruct((s_len, d), F32), jax.ShapeDtypeStruct((1, d), F32)],
        args=args, sem=("arbitrary",), host=host)


def _matmul_tn_rows(pieces, b, tk, name):
    s_len, n = b.shape
    widths = [p.shape[1] for p in pieces]

    def body(*refs):
        a_refs, b_ref, o_ref = refs[:-2], refs[-2], refs[-1]

        @pl.when(pl.program_id(0) == 0)
        def _():
            o_ref[...] = jnp.zeros_like(o_ref)

        b_tile = b_ref[...]
        off = 0
        for a_ref, width in zip(a_refs, widths):
            o_ref[off:off + width, :] += lax.dot_general(a_ref[...], b_tile, TN_DIMS, preferred_element_type=F32)
            off += width

    return pl.pallas_call(
        body, name=name, grid=(s_len // tk,),
        in_specs=[pl.BlockSpec((tk, width), lambda k: (k, 0)) for width in widths] + [pl.BlockSpec((tk, n), lambda k: (k, 0))],
        out_specs=_full((sum(widths), n)),
        out_shape=jax.ShapeDtypeStruct((sum(widths), n), F32),
        compiler_params=_params("arbitrary"),
    )(*pieces, b)


def _matmul_tn_shards(a, b, axis, tk, name):
    s_len, m = a.shape
    n = b.shape[1]
    r, c = (m // N_DEV, n) if axis == 0 else (m, n // N_DEV)
    n_k = s_len // tk
    tm = max(r, min(m, (1 << 20) // n)) if axis == 0 else min(m, (1 << 20) // n)

    def body(a_ref, b_ref, o_ref, acc):
        k = pl.program_id(1)

        @pl.when(k == 0)
        def _():
            acc[...] = jnp.zeros_like(acc)

        acc[...] += lax.dot_general(a_ref[...], b_ref[...], TN_DIMS, preferred_element_type=F32)

        @pl.when(k == n_k - 1)
        def _():
            if axis == 0:
                o_ref[...] = acc[...].reshape(tm // r, r, c).astype(BF16)
            else:
                for j in range(N_DEV):
                    o_ref[j] = acc[:, j * c:(j + 1) * c].astype(BF16)

    if axis == 0:
        out_spec = pl.BlockSpec((tm // r, r, c), lambda i, k: (i, 0, 0))
    else:
        out_spec = pl.BlockSpec((N_DEV, tm, c), lambda i, k: (0, i, 0))
    return pl.pallas_call(
        body, name=name, grid=(m // tm, n_k),
        in_specs=[pl.BlockSpec((tk, tm), lambda i, k: (k, i)), pl.BlockSpec((tk, n), lambda i, k: (k, 0))],
        out_specs=out_spec, out_shape=jax.ShapeDtypeStruct((N_DEV, r, c), BF16),
        scratch_shapes=[pltpu.VMEM((tm, n), F32)],
        compiler_params=_params("parallel", "arbitrary"),
    )(a, b)


def _cumsum_bwd(rs, cs, fl, b, tile, name):
    s_len = fl.shape[0]
    n_t = s_len // tile

    def body(rs_ref, cs_ref, fl_ref, b_ref, dfl_ref, db_ref, carry):
        @pl.when(pl.program_id(0) == 0)
        def _():
            carry[...] = jnp.zeros_like(carry)
            db_ref[...] = jnp.zeros_like(db_ref)

        r = lax.broadcasted_iota(jnp.int32, (tile, tile), 0)
        c = lax.broadcasted_iota(jnp.int32, (tile, tile), 1)
        tri = (c >= r).astype(F32)
        lane = lax.broadcasted_iota(jnp.int32, (1, LANES), 1)
        dc = jnp.zeros((tile, LANES), F32)
        for p in range(N_PAIRS):
            blk = rs_ref[:, LANES * p:LANES * (p + 1)] - cs_ref[:, LANES * p:LANES * (p + 1)]
            dc = jnp.where(lane == 2 * p, blk, dc)
            dc = jnp.where(lane == 2 * p + 1, pltpu.roll(blk, FOX_HEAD_DIM, axis=1), dc)
        dl = jnp.dot(tri, dc, precision=lax.Precision.HIGHEST, preferred_element_type=F32) + carry[...]
        carry[...] = dl[0:1, :]
        dfl = dl * _sigmoid(-(fl_ref[...] + b_ref[...]))
        dfl_ref[...] = dfl.astype(BF16)
        db_ref[...] += jnp.sum(dfl, axis=0, keepdims=True)

    rev = lambda i: (n_t - 1 - i, 0)
    return pl.pallas_call(
        body, name=name, grid=(n_t,),
        in_specs=[pl.BlockSpec((tile, FOX_WIDTH), rev), pl.BlockSpec((tile, FOX_WIDTH), rev), pl.BlockSpec((tile, LANES), rev),
                  _full((1, LANES))],
        out_specs=[pl.BlockSpec((tile, LANES), rev), _full((1, LANES))],
        out_shape=[jax.ShapeDtypeStruct((s_len, LANES), BF16), jax.ShapeDtypeStruct((1, LANES), F32)],
        scratch_shapes=[pltpu.VMEM((1, LANES), F32)],
        compiler_params=_params("arbitrary"),
    )(rs, cs, fl, b)


SUBLANES = 8
CONV_ROWS = 64


def _phase_copies(src, dst, rows):
    for p in range(SUBLANES):
        dst[p] = src[pl.ds(p, rows), :]


def _phase_rows(extp_ref, off, r0, rows):
    p = off % SUBLANES
    return extp_ref[p, pl.ds(pl.multiple_of(r0 + (off - p), SUBLANES), rows), :]


def _conv_taps(w_ref, extp_ref, base, r0, rows, reverse):
    acc = None
    for k in range(CONV_WIDTH):
        off = base + ((CONV_WIDTH - 1 - k) if reverse else k)
        term = w_ref[k:k + 1, :] * _phase_rows(extp_ref, off, r0, rows)
        acc = term if acc is None else acc + term
    return acc


def _fold_rows(x):
    out = x[0:SUBLANES]
    for i in range(1, x.shape[0] // SUBLANES):
        out = out + x[i * SUBLANES:(i + 1) * SUBLANES]
    return out


def _conv_fwd(ag, w, cb, lg, lb, tile, name):
    s_len = ag.shape[0]
    c = CONV_CH
    rb = tile

    def body(ag_ref, w_ref, cb_ref, lg_ref, lb_ref, u_ref, ext, extp):
        @pl.when(pl.program_id(0) == 0)
        def _():
            ext[0:CONV_PAD, :] = jnp.zeros((CONV_PAD, c), F32)
            ext[tile + CONV_PAD:tile + CONV_PAD + SUBLANES, :] = jnp.zeros((SUBLANES, c), F32)

        ext[CONV_PAD:CONV_PAD + tile, :] = ag_ref[:, 0:c] * _sigmoid(ag_ref[:, c:2 * c])
        _phase_copies(ext, extp, tile + CONV_PAD)

        def block(b, carry):
            r0 = pl.multiple_of(b * rb, rb)
            u1 = _conv_taps(w_ref, extp, CONV_PAD - (CONV_WIDTH - 1), r0, rb, False) + cb_ref[...]
            mu = jnp.mean(u1, axis=-1, keepdims=True)
            xc = u1 - mu
            y = xc * lax.rsqrt(jnp.mean(xc * xc, axis=-1, keepdims=True) + EPS) * lg_ref[...] + lb_ref[...]
            u_ref[pl.ds(r0, rb), :] = (y * _sigmoid(y)).astype(BF16)
            return carry

        lax.fori_loop(0, tile // rb, block, 0)
        ext[0:CONV_PAD, :] = ext[tile:tile + CONV_PAD, :]

    return pl.pallas_call(
        body, name=name, grid=(s_len // tile,),
        in_specs=[pl.BlockSpec((tile, 2 * c), lambda i: (i, 0)), _full((CONV_PAD, c)), _full((1, c)), _full((1, c)),
                  _full((1, c))],
        out_specs=pl.BlockSpec((tile, c), lambda i: (i, 0)),
        out_shape=jax.ShapeDtypeStruct((s_len, c), BF16),
        scratch_shapes=[pltpu.VMEM((tile + CONV_PAD + SUBLANES, c), F32), pltpu.VMEM((SUBLANES, tile + CONV_PAD, c), F32)],
        compiler_params=_params("arbitrary"),
    )(ag, w, cb, lg, lb)


def _conv_bwd(ag, dcat, w, cb, lg, lb, tile, name):
    s_len = ag.shape[0]
    c = CONV_CH
    n_t = s_len // tile
    per = tile // CONV_PAD
    rb = min(CONV_ROWS, tile)

    def body(ag_ref, halo_ref, du_ref, w_ref, cb_ref, lg_ref, lb_ref, dag_ref, dw_ref, dv_ref, ext, ext2, extp, dwacc):
        i = pl.program_id(0)
        t = n_t - 1 - i

        @pl.when(i == 0)
        def _():
            ext2[tile:tile + CONV_PAD + SUBLANES, :] = jnp.zeros((CONV_PAD + SUBLANES, c), F32)
            ext[tile + CONV_PAD:tile + CONV_PAD + SUBLANES, :] = jnp.zeros((SUBLANES, c), F32)
            dwacc[...] = jnp.zeros_like(dwacc)
            dv_ref[...] = jnp.zeros_like(dv_ref)

        halo = halo_ref[:, 0:c] * _sigmoid(halo_ref[:, c:2 * c])
        ext[0:CONV_PAD, :] = jnp.where(t > 0, halo, 0.0)
        ext[CONV_PAD:CONV_PAD + tile, :] = ag_ref[:, 0:c] * _sigmoid(ag_ref[:, c:2 * c])
        _phase_copies(ext, extp, tile + CONV_PAD)
        base = CONV_PAD - (CONV_WIDTH - 1)

        def block1(b, carry):
            r0 = pl.multiple_of(b * rb, rb)
            u1 = _conv_taps(w_ref, extp, base, r0, rb, False) + cb_ref[...]
            mu = jnp.mean(u1, axis=-1, keepdims=True)
            xc = u1 - mu
            rs = lax.rsqrt(jnp.mean(xc * xc, axis=-1, keepdims=True) + EPS)
            xhat = xc * rs
            y = xhat * lg_ref[...] + lb_ref[...]
            sy = _sigmoid(y)
            dy = du_ref[pl.ds(r0, rb), :] * (sy * (1.0 + y * (1.0 - sy)))
            dxh = dy * lg_ref[...]
            du1 = rs * (dxh - jnp.mean(dxh, axis=-1, keepdims=True) - xhat * jnp.mean(dxh * xhat, axis=-1, keepdims=True))
            dv_ref[0:1, :] += jnp.sum(du1, axis=0, keepdims=True)
            dv_ref[1:2, :] += jnp.sum(dy * xhat, axis=0, keepdims=True)
            dv_ref[2:3, :] += jnp.sum(dy, axis=0, keepdims=True)
            for k in range(CONV_WIDTH):
                dwacc[k] += _fold_rows(du1 * _phase_rows(extp, base + k, r0, rb))
            ext2[pl.ds(r0, rb), :] = du1
            return carry

        lax.fori_loop(0, tile // rb, block1, 0)
        _phase_copies(ext2, extp, tile + CONV_PAD)

        def block2(b, carry):
            r0 = pl.multiple_of(b * rb, rb)
            du0 = _conv_taps(w_ref, extp, 0, r0, rb, True)
            a = ag_ref[pl.ds(r0, rb), 0:c]
            sg = _sigmoid(ag_ref[pl.ds(r0, rb), c:2 * c])
            dag_ref[pl.ds(r0, rb), 0:c] = (du0 * sg).astype(BF16)
            dag_ref[pl.ds(r0, rb), c:2 * c] = (du0 * a * sg * (1.0 - sg)).astype(BF16)
            return carry

        lax.fori_loop(0, tile // rb, block2, 0)
        ext2[tile:tile + CONV_PAD, :] = ext2[0:CONV_PAD, :]

        @pl.when(i == n_t - 1)
        def _():
            for k in range(CONV_WIDTH):
                dw_ref[k:k + 1, :] = jnp.sum(dwacc[k], axis=0, keepdims=True)
            dw_ref[CONV_WIDTH:CONV_PAD, :] = jnp.zeros((CONV_PAD - CONV_WIDTH, c), F32)

    rev = lambda i: (n_t - 1 - i, 0)
    return pl.pallas_call(
        body, name=name, grid=(n_t,),
        in_specs=[pl.BlockSpec((tile, 2 * c), rev),
                  pl.BlockSpec((CONV_PAD, 2 * c), lambda i: (jnp.maximum((n_t - 1 - i) * per - 1, 0), 0)),
                  pl.BlockSpec((tile, c), rev), _full((CONV_PAD, c)), _full((1, c)), _full((1, c)), _full((1, c))],
        out_specs=[pl.BlockSpec((tile, 2 * c), rev), _full((CONV_PAD, c)), _full((8, c))],
        out_shape=[jax.ShapeDtypeStruct((s_len, 2 * c), BF16), jax.ShapeDtypeStruct((CONV_PAD, c), F32),
                   jax.ShapeDtypeStruct((8, c), F32)],
        scratch_shapes=[pltpu.VMEM((tile + CONV_PAD + SUBLANES, c), F32), pltpu.VMEM((tile + CONV_PAD + SUBLANES, c), F32),
                        pltpu.VMEM((SUBLANES, tile + CONV_PAD, c), F32), pltpu.VMEM((CONV_PAD, SUBLANES, c), F32)],
        compiler_params=_params("arbitrary"),
    )(ag, ag, dcat, w, cb, lg, lb)


Q_SCALE = FOX_HEAD_DIM ** -0.5


def _head_col(x, lane, h):
    return jnp.sum(jnp.where(lane == h, x, 0.0), axis=1, keepdims=True)


def _split3(x):
    hi = x.astype(BF16).astype(F32)
    r = x - hi
    mid = r.astype(BF16).astype(F32)
    lo = (r - mid).astype(BF16).astype(F32)
    return hi, mid, lo


def _in_lanes(lane, lo, n):
    return (lane >= lo) & (lane < lo + n)


def _put3(lane, lo, parts, rest):
    return jnp.where(lane == lo, parts[0], jnp.where(lane == lo + 1, parts[1], jnp.where(lane == lo + 2, parts[2], rest)))


def _spare_lane(h):
    return FOX_HEAD_DIM if h % 2 == 0 else 0


def _shift_div(i, num, den):
    return i * (num // den) if num >= den else lax.shift_right_logical(i, (den // num).bit_length() - 1)


def _foxa_prep(qkv, fl, b, tile, name):
    s_len = qkv.shape[0]

    def body(q_ref, k_ref, v_ref, fl_ref, b_ref, qa_ref, ka_ref, va_ref, carry):
        @pl.when(pl.program_id(0) == 0)
        def _():
            carry[...] = jnp.zeros_like(carry)

        xx = fl_ref[...] + b_ref[...]
        lf = jnp.minimum(xx, 0.0) - jnp.log1p(jnp.exp(-jnp.abs(xx)))
        tri = (lax.broadcasted_iota(jnp.int32, (tile, tile), 1) <= lax.broadcasted_iota(jnp.int32, (tile, tile), 0))
        cum_t = jnp.dot(tri.astype(F32), lf, precision=lax.Precision.HIGHEST, preferred_element_type=F32) + carry[...]
        carry[...] = cum_t[tile - 1:tile, :]
        lane = lax.broadcasted_iota(jnp.int32, (1, LANES), 1)
        for h in range(FOX_HEADS):
            e = _spare_lane(h)
            head = ~_in_lanes(lane, e, FOX_HEAD_DIM)
            blk = slice(LANES * (h // 2), LANES * (h // 2) + LANES)
            out = slice(LANES * h, LANES * h + LANES)
            c3 = _split3(_head_col(cum_t, lane, h))
            ex_q = _put3(lane, e, c3, jnp.where(_in_lanes(lane, e + 3, 3), 1.0, 0.0))
            qa_ref[:, out] = jnp.where(head, q_ref[:, blk].astype(F32) * Q_SCALE, ex_q).astype(BF16)
            ones = jnp.where(_in_lanes(lane, e, 3) | _in_lanes(lane, e + 6, 3), 1.0, 0.0)
            ex_k = _put3(lane, e + 3, [-c for c in c3], ones)
            ka_ref[:, out] = jnp.where(head, k_ref[:, blk].astype(F32), ex_k).astype(BF16)
            ex_v = jnp.where(_in_lanes(lane, e, 3), 1.0, 0.0)
            va_ref[:, out] = jnp.where(head, v_ref[:, blk].astype(F32), ex_v).astype(BF16)

    col = lambda c: pl.BlockSpec((tile, FOX_WIDTH), lambda i: (i, c))
    wide = pl.BlockSpec((tile, 2 * FOX_WIDTH), lambda i: (i, 0))
    return pl.pallas_call(
        body, name=name, grid=(s_len // tile,),
        in_specs=[col(0), col(1), col(2), pl.BlockSpec((tile, LANES), lambda i: (i, 0)), _full((1, LANES))],
        out_specs=[wide, wide, wide], out_shape=[jax.ShapeDtypeStruct((s_len, 2 * FOX_WIDTH), BF16)] * 3,
        scratch_shapes=[pltpu.VMEM((1, LANES), F32)],
        compiler_params=_params("arbitrary"),
    )(qkv, qkv, qkv, fl, b)


def _foxa_fwd(qa, ka, va, tq, tk, name, host=None):
    s_len = qa.shape[0]

    def body(q_ref, k_ref, v_ref, o_ref, lse_ref, ob_ref):
        i = pl.program_id(1)
        lane = lax.broadcasted_iota(jnp.int32, (1, LANES), 1)
        cols = [slice(LANES * hh, LANES * hh + LANES) for hh in range(2)]
        qh = [q_ref[:, c] for c in cols]

        def scores(j):
            off = pl.multiple_of(j * tk, tk)
            return [lax.dot_general(qh[hh], k_ref[pl.ds(off, tk), cols[hh]], NT_DIMS, preferred_element_type=F32)
                    for hh in range(2)]

        def update(j, s, m, acc, mask):
            off = pl.multiple_of(j * tk, tk)
            m_out, acc_out = [], []
            for hh in range(2):
                sh = s[hh] if mask is None else jnp.where(mask, s[hh], NEG_INF)
                m_new = jnp.maximum(m[hh], jnp.max(sh, axis=1, keepdims=True))
                pr = jnp.exp(sh - m_new).astype(BF16)
                acc_out.append(jnp.exp(m[hh] - m_new) * acc[hh]
                               + jnp.dot(pr, v_ref[pl.ds(off, tk), cols[hh]], preferred_element_type=F32))
                m_out.append(m_new)
            return m_out, acc_out

        def step(j, carry):
            s_next = scores(j + 1)
            m, acc = update(j, carry[0:2], carry[2:4], carry[4:6], None)
            return (*s_next, *m, *acc)

        n_full = _shift_div(i, tq, tk)
        n_part = max(tq // tk, 1)
        qi = lax.broadcasted_iota(jnp.int32, (tq, tk), 0) + i * tq
        ki = lax.broadcasted_iota(jnp.int32, (tq, tk), 1)
        carry = (*scores(0), *([jnp.full((tq, 1), NEG_INF, F32)] * 2), *([jnp.zeros((tq, LANES), F32)] * 2))
        carry = lax.fori_loop(0, n_full, step, carry)
        s, m, acc = list(carry[0:2]), list(carry[2:4]), list(carry[4:6])
        for jj in range(n_part):
            s_next = scores(n_full + jj + 1) if jj < n_part - 1 else None
            m, acc = update(n_full + jj, s, m, acc, ki + (n_full + jj) * tk <= qi)
            s = s_next
        res = []
        for hh in range(2):
            l = acc[hh][:, _spare_lane(hh):_spare_lane(hh) + 1]
            res.append((acc[hh] / l, m[hh] + jnp.log(l)))
        low = lane < FOX_HEAD_DIM
        o_pair = jnp.where(low, res[0][0], res[1][0])
        o_ref[...] = o_pair
        ob_ref[...] = o_pair.astype(BF16)
        lse_ref[...] = jnp.where(low, res[0][1], res[1][1])

    pair = pl.BlockSpec((s_len, 2 * LANES), lambda p, i: (0, p))
    out = pl.BlockSpec((tq, LANES), lambda p, i: (i, p))
    return _pcall(
        body, name=name, grid=(N_PAIRS, s_len // tq),
        in_specs=[pl.BlockSpec((tq, 2 * LANES), lambda p, i: (i, p)), pair, pair],
        out_specs=[out, out, out],
        out_shape=[jax.ShapeDtypeStruct((s_len, FOX_WIDTH), F32)] * 2 + [jax.ShapeDtypeStruct((s_len, FOX_WIDTH), BF16)],
        args=[qa, ka, va], sem=("parallel", "parallel"), host=host)


def _foxa_dq(qa, lse, o, dcat, ka, va, tq, tk, name, host=None):
    s_len = qa.shape[0]

    def body(qa_ref, lse_ref, o_ref, dcat_ref, k_ref, v_ref, dq_ref, rs_ref, q_ref, do_ref):
        i = pl.program_id(1)
        lane = lax.broadcasted_iota(jnp.int32, (1, LANES), 1)
        cols = [slice(LANES * hh, LANES * hh + LANES) for hh in range(2)]
        d_o = dcat_ref[...]
        prod = d_o * o_ref[...]
        for hh in range(2):
            e = _spare_lane(hh)
            head = ~_in_lanes(lane, e, FOX_HEAD_DIM)
            delta = jnp.sum(jnp.where(head, prod, 0.0), axis=1, keepdims=True)
            do_ref[:, cols[hh]] = _put3(lane, e, _split3(-delta), jnp.where(head, d_o, 0.0)).astype(BF16)
            l3 = _split3(-lse_ref[:, FOX_HEAD_DIM - e:FOX_HEAD_DIM - e + 1])
            q_ref[:, cols[hh]] = _put3(lane, e + 6, l3, qa_ref[:, cols[hh]].astype(F32)).astype(BF16)
        qh = [q_ref[:, c] for c in cols]
        doh = [do_ref[:, c] for c in cols]

        def update(j, acc, mask):
            off = pl.multiple_of(j * tk, tk)
            out = []
            for hh in range(2):
                kt = k_ref[pl.ds(off, tk), cols[hh]]
                pr = jnp.exp(lax.dot_general(qh[hh], kt, NT_DIMS, preferred_element_type=F32))
                if mask is not None:
                    pr = jnp.where(mask, pr, 0.0)
                ds = pr * lax.dot_general(doh[hh], v_ref[pl.ds(off, tk), cols[hh]], NT_DIMS, preferred_element_type=F32)
                out.append(acc[hh] + jnp.dot(ds.astype(BF16), kt, preferred_element_type=F32))
            return tuple(out)

        n_full = _shift_div(i, tq, tk)
        qi = lax.broadcasted_iota(jnp.int32, (tq, tk), 0) + i * tq
        ki = lax.broadcasted_iota(jnp.int32, (tq, tk), 1)
        acc = lax.fori_loop(0, n_full, lambda j, a: update(j, a, None), (jnp.zeros((tq, LANES), F32),) * 2)
        for jj in range(max(tq // tk, 1)):
            acc = update(n_full + jj, acc, ki + (n_full + jj) * tk <= qi)
        low = lane < FOX_HEAD_DIM
        dq_ref[...] = (jnp.where(low, acc[0], acc[1]) * Q_SCALE).astype(BF16)
        rs_ref[...] = jnp.where(low, acc[0][:, _spare_lane(0):_spare_lane(0) + 1], acc[1][:, _spare_lane(1):_spare_lane(1) + 1])

    pair = pl.BlockSpec((s_len, 2 * LANES), lambda p, i: (0, p))
    tile2 = pl.BlockSpec((tq, 2 * LANES), lambda p, i: (i, p))
    out = pl.BlockSpec((tq, LANES), lambda p, i: (i, p))
    wide = jax.ShapeDtypeStruct((s_len, 2 * FOX_WIDTH), BF16)
    return _pcall(
        body, name=name, grid=(N_PAIRS, s_len // tq),
        in_specs=[tile2, out, out, pl.BlockSpec((tq, LANES), lambda p, i: (i, N_PAIRS + p)), pair, pair],
        out_specs=[out, out, tile2, tile2],
        out_shape=[jax.ShapeDtypeStruct((s_len, FOX_WIDTH), BF16), jax.ShapeDtypeStruct((s_len, FOX_WIDTH), F32), wide, wide],
        args=[qa, lse, o, dcat, ka, va], sem=("parallel", "parallel"), host=host)


def _foxa_dkv(qb, ka, va, doa, tq, tk, name, host=None):
    s_len = qb.shape[0]
    n_q = s_len // tq

    def body(k_ref, v_ref, q_ref, do_ref, dk_ref, dv_ref, cs_ref):
        j = pl.program_id(1)
        lane = lax.broadcasted_iota(jnp.int32, (1, LANES), 1)
        cols = [slice(LANES * hh, LANES * hh + LANES) for hh in range(2)]
        kh = [k_ref[:, c] for c in cols]
        vh = [v_ref[:, c] for c in cols]

        def update(i, acc, mask):
            off = pl.multiple_of(i * tq, tq)
            out = []
            for hh in range(2):
                qt = q_ref[pl.ds(off, tq), cols[hh]]
                dot = do_ref[pl.ds(off, tq), cols[hh]]
                pt = jnp.exp(lax.dot_general(kh[hh], qt, NT_DIMS, preferred_element_type=F32))
                if mask is not None:
                    pt = jnp.where(mask, pt, 0.0)
                dv = acc[2 * hh + 1] + jnp.dot(pt.astype(BF16), dot, preferred_element_type=F32)
                dst = pt * lax.dot_general(vh[hh], dot, NT_DIMS, preferred_element_type=F32)
                out += [acc[2 * hh] + jnp.dot(dst.astype(BF16), qt, preferred_element_type=F32), dv]
            return tuple(out)

        i0 = _shift_div(j, tk, tq)
        n_part = max(tk // tq, 1)
        ki = lax.broadcasted_iota(jnp.int32, (tk, tq), 0) + j * tk
        qi = lax.broadcasted_iota(jnp.int32, (tk, tq), 1)
        acc = (jnp.zeros((tk, LANES), F32),) * 4
        for ii in range(n_part):
            acc = update(i0 + ii, acc, ki <= qi + (i0 + ii) * tq)
        acc = lax.fori_loop(i0 + n_part, n_q, lambda i, a: update(i, a, None), acc)
        low = lane < FOX_HEAD_DIM
        dk_ref[...] = jnp.where(low, acc[0], acc[2]).astype(BF16)
        dv_ref[...] = jnp.where(low, acc[1], acc[3]).astype(BF16)
        cs_ref[...] = jnp.where(low, acc[0][:, _spare_lane(0) + 3:_spare_lane(0) + 4],
                                acc[2][:, _spare_lane(1) + 3:_spare_lane(1) + 4])

    pair = pl.BlockSpec((s_len, 2 * LANES), lambda p, j: (0, p))
    tile2 = pl.BlockSpec((tk, 2 * LANES), lambda p, j: (j, p))
    out = pl.BlockSpec((tk, LANES), lambda p, j: (j, p))
    return _pcall(
        body, name=name, grid=(N_PAIRS, s_len // tk), in_specs=[tile2, tile2, pair, pair], out_specs=[out, out, out],
        out_shape=[jax.ShapeDtypeStruct((s_len, FOX_WIDTH), BF16), jax.ShapeDtypeStruct((s_len, FOX_WIDTH), BF16),
                   jax.ShapeDtypeStruct((s_len, FOX_WIDTH), F32)],
        args=[ka, va, qb, doa], sem=("parallel", "parallel"), host=host)


def _mem_scores_t(q, kv, h):
    lo = h * MEM_HEAD_DIM
    st = lax.dot_general(kv[:, lo:lo + MEM_HEAD_DIM], q[:, lo:lo + MEM_HEAD_DIM], NT_DIMS,
                         preferred_element_type=F32) * (MEM_HEAD_DIM ** -0.5)
    e = jnp.exp(st - jnp.max(st, axis=0, keepdims=True))
    return e / jnp.sum(e, axis=0, keepdims=True)


def _memattn_fwd(q, kv, tile, name):
    s_len = q.shape[0]
    n_mem = kv.shape[0]

    def body(q_ref, kv_ref, o_ref):
        q = q_ref[...]
        kv = kv_ref[...]
        for h in range(MEM_HEADS):
            lo = h * MEM_HEAD_DIM
            pt = _mem_scores_t(q, kv, h).astype(BF16)
            vh = kv[:, MEM_INNER + lo:MEM_INNER + lo + MEM_HEAD_DIM]
            o_ref[:, lo:lo + MEM_HEAD_DIM] = lax.dot_general(pt, vh, TN_DIMS, preferred_element_type=F32).astype(BF16)

    return pl.pallas_call(
        body, name=name, grid=(s_len // tile,),
        in_specs=[pl.BlockSpec((tile, MEM_INNER), lambda i: (i, 0)), _full((n_mem, 2 * MEM_INNER))],
        out_specs=pl.BlockSpec((tile, MEM_INNER), lambda i: (i, 0)),
        out_shape=jax.ShapeDtypeStruct((s_len, MEM_INNER), BF16),
        compiler_params=_params("parallel"),
    )(q, kv)


def _memattn_bwd(q, kv, do, tile, name):
    s_len = q.shape[0]
    n_mem = kv.shape[0]
    scale = MEM_HEAD_DIM ** -0.5

    def body(q_ref, kv_ref, do_ref, dq_ref, dkv_ref):
        @pl.when(pl.program_id(0) == 0)
        def _():
            dkv_ref[...] = jnp.zeros_like(dkv_ref)

        q = q_ref[...]
        kv = kv_ref[...]
        do = do_ref[...]
        for h in range(MEM_HEADS):
            lo = h * MEM_HEAD_DIM
            qh = q[:, lo:lo + MEM_HEAD_DIM]
            kh = kv[:, lo:lo + MEM_HEAD_DIM]
            vh = kv[:, MEM_INNER + lo:MEM_INNER + lo + MEM_HEAD_DIM]
            doh = do[:, lo:lo + MEM_HEAD_DIM]
            pt = _mem_scores_t(q, kv, h)
            dkv_ref[:, MEM_INNER + lo:MEM_INNER + lo + MEM_HEAD_DIM] += jnp.dot(
                pt.astype(BF16), doh, preferred_element_type=F32)
            dpt = lax.dot_general(vh, doh, NT_DIMS, preferred_element_type=F32)
            dst = (pt * (dpt - jnp.sum(pt * dpt, axis=0, keepdims=True)) * scale).astype(BF16)
            dkv_ref[:, lo:lo + MEM_HEAD_DIM] += jnp.dot(dst, qh, preferred_element_type=F32)
            dq_ref[:, lo:lo + MEM_HEAD_DIM] = lax.dot_general(dst, kh, TN_DIMS, preferred_element_type=F32).astype(BF16)

    return pl.pallas_call(
        body, name=name, grid=(s_len // tile,),
        in_specs=[pl.BlockSpec((tile, MEM_INNER), lambda i: (i, 0)), _full((n_mem, 2 * MEM_INNER)),
                  pl.BlockSpec((tile, MEM_INNER), lambda i: (i, 0))],
        out_specs=[pl.BlockSpec((tile, MEM_INNER), lambda i: (i, 0)), _full((n_mem, 2 * MEM_INNER))],
        out_shape=[jax.ShapeDtypeStruct((s_len, MEM_INNER), BF16), jax.ShapeDtypeStruct((n_mem, 2 * MEM_INNER), F32)],
        compiler_params=_params("arbitrary"),
    )(q, kv, do)


def _loss_head(y, target, tile, name):
    s_len, d = y.shape

    def body(y_ref, t_ref, dy_ref, l_ref):
        @pl.when(pl.program_id(0) == 0)
        def _():
            l_ref[...] = jnp.zeros_like(l_ref)

        err = y_ref[...] - t_ref[...]
        dy_ref[...] = err * (1.0 / d)
        l_ref[...] += jnp.sum(err * err, axis=0, keepdims=True) * (0.5 / d)

    row = lambda i: (i, 0)
    return pl.pallas_call(
        body, name=name, grid=(s_len // tile,),
        in_specs=[pl.BlockSpec((tile, d), row), pl.BlockSpec((tile, d), row)],
        out_specs=[pl.BlockSpec((tile, d), row), _full((1, d))],
        out_shape=[jax.ShapeDtypeStruct((s_len, d), F32), jax.ShapeDtypeStruct((1, d), F32)],
        compiler_params=_params("arbitrary"),
    )(y, target)


def _attn_tile(s_len):
    return min(256, s_len // 2)


REST = ("w_out", "w_mq", "w_mk", "w_mv", "w_mo", "w_up", "w_down")
REST_DQ = ("w_up", "w_down")
REST_DKV = ("w_out", "w_mq", "w_mk", "w_mv", "w_mo")
SHARD_AXIS = {"w_in": 1, "w_out": 0, "w_mq": 0, "w_mk": 0, "w_mv": 0, "w_mo": 1, "w_up": 1, "w_down": 0}


def _full_from_shards(sh, axis):
    n, r, c = sh.shape
    if axis == 0:
        return sh.reshape(n * r, c)
    return sh.transpose(1, 0, 2).reshape(r, n * c)


def _rest_weights(lands):
    w = {n: _full_from_shards(sh, SHARD_AXIS[n]) for n, sh in zip(REST, lands)}
    w["w_mkv"] = jnp.concatenate([w.pop("w_mk"), w.pop("w_mv")], axis=1)
    return w


def _w_in_cat(land):
    return jnp.pad(land.reshape(-1, land.shape[2]), ((0, IN_CAT - IN_COLS), (0, 0)))


def _layer_fwd(x0, mem, w, l, rest_src=None, next_src=None):
    s_len = x0.shape[0]
    tile = min(512, s_len)
    tile_ff = min(512, s_len)
    ta = _attn_tile(s_len)
    ident = lambda z: z
    sv = {"x0": x0}

    h1, ag, qkv, fl = _rms_matmul(
        x0, w["norm_mix_pre"], w["w_in_cat"],
        [(0, 2 * CONV_CH, [(F32, ident)]), (2 * CONV_CH, IN_MAIN, [(BF16, ident)]), (IN_MAIN, IN_CAT, [(F32, ident)])],
        tile, f"mix_in_{l}", w_t=True)
    u3 = _conv_fwd(ag, w["conv_w"], w["conv_b"], w["conv_ln_g"], w["conv_ln_b"], tile, f"conv_fwd_{l}")
    qa, ka, va = _foxa_prep(qkv, fl, w["b_forget"], tile, f"fox_prep_{l}")
    next_land = None
    if rest_src is None:
        o, lse, o_bf = _foxa_fwd(qa, ka, va, ta, 2 * ta, f"fox_fwd_{l}")
    else:
        srcs = list(rest_src) + list(next_src or [])
        (o, lse, o_bf), lands = _foxa_fwd(qa, ka, va, ta, 2 * ta, f"fox_fwd_{l}", host=(srcs, [True] * len(srcs)))
        w = {**w, **_rest_weights(lands[:len(REST)])}
        next_land = lands[len(REST):]
    cat = jnp.concatenate([u3, o_bf], axis=1)
    y1, x1 = _matmul_resnorm(cat, w["w_out"], x0, w["norm_mix_post"], tile, f"mix_out_{l}")
    sv.update(h1=h1, ag=ag, fl=fl, qa=qa, ka=ka, va=va, o=o, lse=lse, cat=cat, y1=y1, x1=x1)

    h2, qm = _rms_matmul(x1, w["norm_mem_pre"], w["w_mq"], [(0, MEM_INNER, [(BF16, ident)])], tile, f"mem_q_{l}")
    mem_n, kv = _rms_matmul(mem, w["norm_memkv"], w["w_mkv"], [(0, 2 * MEM_INNER, [(BF16, ident)])],
                            mem.shape[0], f"mem_kv_{l}")
    om = _memattn_fwd(qm, kv, tile, f"mem_attn_fwd_{l}")
    y2, x2 = _matmul_resnorm(om, w["w_mo"], x1, w["norm_mem_post"], tile, f"mem_out_{l}")
    sv.update(h2=h2, qm=qm, mem_n=mem_n, kv=kv, om=om, y2=y2, x2=x2)

    relu2 = lambda z: jnp.square(jnp.maximum(z, 0.0))
    h3, pre, hid = _rms_matmul(x2, w["norm_mlp_pre"], w["w_up"], [(0, D_FF, [(BF16, ident), (BF16, relu2)])], tile_ff,
                               f"mlp_up_{l}")
    y3, x3 = _matmul_resnorm(hid, w["w_down"], x2, w["norm_mlp_post"], tile_ff, f"mlp_down_{l}")
    sv.update(h3=h3, pre=pre, hid=hid, y3=y3)
    return x3, sv, w, next_land


def _layer_bwd(dx3, mem, w, sv, l, scatter_rest=False, dkv_src=None, scatter_w_in=False):
    s_len = dx3.shape[0]
    tile = min(512, s_len)
    tile_ff = min(512, s_len)
    ta = _attn_tile(s_len)
    tk = min(512, s_len)
    n_mem = mem.shape[0]
    g = {}

    dy3, dpre, g["norm_mlp_post"] = _resnorm_bwd_mm(dx3, sv["y3"], w["norm_mlp_post"], w["w_down"], tile_ff,
                                                    f"mlp_down_bwd_{l}", BF16, pre=sv["pre"])
    g["w_down"] = _matmul_tn_shards(sv["hid"], dy3, 0, tk, f"dw_down_{l}")
    dx2, g["norm_mlp_pre"] = _mm_prenorm_bwd(dpre, w["w_up"], sv["x2"], w["norm_mlp_pre"], dx3, tile_ff,
                                             f"mlp_up_bwd_{l}")
    g["w_up"] = _matmul_tn_shards(sv["h3"], dpre, 1, tk, f"dw_up_{l}")

    dy2, dom, g["norm_mem_post"] = _resnorm_bwd_mm(dx2, sv["y2"], w["norm_mem_post"], w["w_mo"], tile,
                                                   f"mem_out_bwd_{l}", BF16)
    g["w_mo"] = _matmul_tn_shards(sv["om"], dy2, 1, tk, f"dw_mo_{l}")
    dqm, dkv = _memattn_bwd(sv["qm"], sv["kv"], dom, tile, f"mem_attn_bwd_{l}")
    dkvb = dkv.astype(BF16)
    g["w_mq"] = _matmul_tn_shards(sv["h2"], dqm, 0, tk, f"dw_mq_{l}")
    dx1, g["norm_mem_pre"] = _mm_prenorm_bwd(dqm, w["w_mq"], sv["x1"], w["norm_mem_pre"], dx2, tile, f"mem_q_bwd_{l}")
    _, g["norm_memkv"] = _mm_prenorm_bwd(dkvb, w["w_mkv"], mem, w["norm_memkv"], None, n_mem, f"mem_kv_bwd_{l}")
    g["w_mk"] = _matmul_tn_shards(sv["mem_n"], dkvb[:, :MEM_INNER], 0, n_mem, f"dw_mk_{l}")
    g["w_mv"] = _matmul_tn_shards(sv["mem_n"], dkvb[:, MEM_INNER:], 0, n_mem, f"dw_mv_{l}")

    dy1, dcat, g["norm_mix_post"] = _resnorm_bwd_mm(dx1, sv["y1"], w["norm_mix_post"], w["w_out"], tile,
                                                    f"mix_out_bwd_{l}", F32)
    g["w_out"] = _matmul_tn_shards(sv["cat"], dy1, 0, tk, f"dw_out_{l}")
    dq_args = (sv["qa"], sv["lse"], sv["o"], dcat, sv["ka"], sv["va"], ta, 2 * ta, f"fox_dq_{l}")
    rest_land, dkv_land = None, None
    if scatter_rest:
        with_dq = [g[n] for n in REST_DQ]
        (dq, rs, qb, doa), land_dq = _foxa_dq(*dq_args, host=(with_dq, [False] * len(with_dq)))
        with_dkv = [g[n] for n in REST_DKV] + list(dkv_src or [])
        (dk, dv, cs), land_dkv = _foxa_dkv(qb, sv["ka"], sv["va"], doa, 2 * ta, ta, f"fox_dkv_{l}",
                                           host=(with_dkv, [False] * len(with_dkv)))
        by_name = dict(zip(REST_DQ + REST_DKV, land_dq + land_dkv))
        rest_land = [by_name[n] for n in REST]
        dkv_land = land_dkv[len(REST_DKV):]
    else:
        dq, rs, qb, doa = _foxa_dq(*dq_args)
        dk, dv, cs = _foxa_dkv(qb, sv["ka"], sv["va"], doa, 2 * ta, ta, f"fox_dkv_{l}")
    dfl, db = _cumsum_bwd(rs, cs, sv["fl"], w["b_forget"], tile, f"cumsum_bwd_{l}")
    g["b_forget"] = db[:, :FOX_HEADS]
    dag, dconv_w, dconv_v = _conv_bwd(sv["ag"], dcat, w["conv_w"], w["conv_b"], w["conv_ln_g"], w["conv_ln_b"], tile,
                                      f"conv_bwd_{l}")
    g["conv_w"] = dconv_w[:CONV_WIDTH]
    g["conv_b"], g["conv_ln_g"], g["conv_ln_b"] = dconv_v[0:1], dconv_v[1:2], dconv_v[2:3]
    dz = [dag, dq, dk, dv, dfl]
    dw_in_t = _matmul_tn_rows(dz, sv["h1"], tk, f"dw_in_{l}")[:IN_COLS]
    g["w_in"] = dw_in_t.reshape(N_DEV, IN_COLS // N_DEV, D_MODEL).astype(BF16)
    res = _mm_prenorm_bwd(dz, w["w_in_cat"], sv["x0"], w["norm_mix_pre"], dx1, tile, f"mix_in_bwd_{l}",
                          host=([g["w_in"]], [False]) if scatter_w_in else None, w_t=True)
    (dx0, g["norm_mix_pre"]), w_in_land = res if scatter_w_in else (res, None)
    return dx0, g, rest_land, dkv_land, w_in_land


def _sum_blocks(a, name):
    n, rows, cols = a.shape

    def body(a_ref, o_ref):
        acc = a_ref[0]
        for j in range(1, n):
            acc = acc + a_ref[j]
        o_ref[...] = acc

    return pl.pallas_call(
        body, name=name, in_specs=[_full((n, rows, cols))], out_specs=_full((rows, cols)),
        out_shape=jax.ShapeDtypeStruct((rows, cols), F32), grid=(1,),
    )(a)


def _adamw(gparts, w, m, v, tile, name):
    n_l, rows, cols = w.shape
    n = gparts[0].shape[0]
    n_t = rows // tile
    c1 = 1.0 - ADAM_B1
    c2 = 1.0 - ADAM_B2
    bc1 = 1.0 - ADAM_B1 ** ADAM_STEP
    bc2 = 1.0 - ADAM_B2 ** ADAM_STEP

    def body(*refs):
        gp_refs, (w_ref, m_ref, v_ref, g_ref, d_ref, mo_ref, vo_ref) = refs[:n_l], refs[n_l:]
        layer = pl.program_id(0)
        g = None
        for l, gp_ref in enumerate(gp_refs):
            gl = gp_ref[0].astype(F32)
            for j in range(1, n):
                gl = gl + gp_ref[j].astype(F32)
            g = gl if g is None else jnp.where(layer == l, gl, g)
        g_ref[...] = g
        m_new = ADAM_B1 * m_ref[...] + c1 * g
        v_new = ADAM_B2 * v_ref[...] + c2 * (g * g)
        mo_ref[...] = m_new
        vo_ref[...] = v_new
        d_ref[...] = -ADAM_LR * ((m_new / bc1) / (jnp.sqrt(v_new / bc2) + ADAM_EPS) + ADAM_WD * w_ref[...])

    def gp_spec(l):
        return pl.BlockSpec((n, tile, cols), lambda L, i: (0, jnp.where(L < l, 0, jnp.where(L > l, n_t - 1, i)), 0))

    spec = pl.BlockSpec((None, tile, cols), lambda L, i: (L, i, 0))
    return pl.pallas_call(
        body, name=name, grid=(n_l, n_t),
        in_specs=[gp_spec(l) for l in range(n_l)] + [spec, spec, spec],
        out_specs=[spec] * 4, out_shape=[jax.ShapeDtypeStruct((n_l, rows, cols), F32)] * 4,
        compiler_params=_params("arbitrary", "arbitrary"),
    )(*gparts, w, m, v)


def _adamw_cols(gparts, w, m, v, name):
    n, rows, cols = gparts[0].shape
    n_l = len(gparts)
    c1 = 1.0 - ADAM_B1
    c2 = 1.0 - ADAM_B2
    bc1 = 1.0 - ADAM_B1 ** ADAM_STEP
    bc2 = 1.0 - ADAM_B2 ** ADAM_STEP

    def body(*refs):
        gp_refs, (w_ref, m_ref, v_ref, g_ref, d_ref, mo_ref, vo_ref) = refs[:n_l], refs[n_l:]
        layer = pl.program_id(0)
        g = None
        for l, gp_ref in enumerate(gp_refs):
            gl = gp_ref[0].astype(F32)
            for j in range(1, n):
                gl = gl + gp_ref[j].astype(F32)
            g = gl if g is None else jnp.where(layer == l, gl, g)
        g_ref[...] = g
        m_new = ADAM_B1 * m_ref[...] + c1 * g
        v_new = ADAM_B2 * v_ref[...] + c2 * (g * g)
        mo_ref[...] = m_new
        vo_ref[...] = v_new
        d_ref[...] = -ADAM_LR * ((m_new / bc1) / (jnp.sqrt(v_new / bc2) + ADAM_EPS) + ADAM_WD * w_ref[...])

    spec = pl.BlockSpec((rows, cols), lambda L: (0, L))
    return pl.pallas_call(
        body, name=name, grid=(n_l,),
        in_specs=[_full((n, rows, cols))] * n_l + [spec, spec, spec],
        out_specs=[spec] * 4, out_shape=[jax.ShapeDtypeStruct((rows, n_l * cols), F32)] * 4,
        compiler_params=_params("arbitrary"),
    )(*gparts, w, m, v)


def _pack_rows(parts, total_rows):
    flat = [p.reshape(-1, D_MODEL) for p in parts]
    used = sum(f.shape[0] for f in flat)
    if total_rows > used:
        flat.append(jnp.zeros((total_rows - used, D_MODEL), flat[0].dtype))
    return jnp.concatenate(flat, axis=0)


def kernel(x, mem, norm_mix_pre, norm_mix_post, w_in, b_forget, conv_w, conv_b, conv_ln_g, conv_ln_b, w_out, norm_mem_pre, norm_mem_post, norm_memkv, w_mq, w_mk, w_mv, w_mo, norm_mlp_pre, norm_mlp_post, w_up, w_down, loss_target, m_norm_mix_pre, m_norm_mix_post, m_w_in, m_b_forget, m_conv_w, m_conv_b, m_conv_ln_g, m_conv_ln_b, m_w_out, m_norm_mem_pre, m_norm_mem_post, m_norm_memkv, m_w_mq, m_w_mk, m_w_mv, m_w_mo, m_norm_mlp_pre, m_norm_mlp_post, m_w_up, m_w_down, v_norm_mix_pre, v_norm_mix_post, v_w_in, v_b_forget, v_conv_w, v_conv_b, v_conv_ln_g, v_conv_ln_b, v_w_out, v_norm_mem_pre, v_norm_mem_post, v_norm_memkv, v_w_mq, v_w_mk, v_w_mv, v_w_mo, v_norm_mlp_pre, v_norm_mlp_post, v_w_up, v_w_down):
    p = dict(locals())
    names = ("norm_mix_pre", "norm_mix_post", "w_in", "b_forget", "conv_w", "conv_b", "conv_ln_g", "conv_ln_b", "w_out",
             "norm_mem_pre", "norm_mem_post", "norm_memkv", "w_mq", "w_mk", "w_mv", "w_mo", "norm_mlp_pre",
             "norm_mlp_post", "w_up", "w_down")
    me = 4 * lax.axis_index("x") + 2 * lax.axis_index("y") + lax.axis_index("c")
    conv_cols = conv_w.shape[2]
    s_len = x.shape[1]
    bf = {n: p[n].astype(BF16) for n in REST}
    in_cols = w_in.shape[2]
    w_in_t = {pre: jnp.transpose(p[pre + "w_in"], (2, 0, 1)).reshape(in_cols, DEPTH * D_MODEL) for pre in ("", "m_", "v_")}
    bf_in = w_in_t[""].astype(BF16)
    bf["w_in"] = [bf_in[:, l * D_MODEL:(l + 1) * D_MODEL] for l in range(DEPTH)]

    conv_pack = _pack_rows([jnp.pad(conv_w, ((0, 0), (0, CONV_PAD - CONV_WIDTH), (0, 0)))], 8)
    win_land, conv_land = _exchange([bf["w_in"][0], conv_pack], [True, True], "gather_first")
    conv_rows = DEPTH * CONV_PAD * conv_cols // D_MODEL
    conv_full = conv_land[:, :conv_rows].reshape(N_DEV, DEPTH, CONV_PAD, conv_cols)
    conv_full = conv_full.transpose(1, 2, 0, 3).reshape(DEPTH, CONV_PAD, N_DEV * conv_cols)
    b_forget_pad = jnp.pad(b_forget, ((0, 0), (0, LANES - FOX_HEADS)))

    def first_weights(l, land):
        w = {"w_in_cat": _w_in_cat(land), "conv_w": conv_full[l], "b_forget": b_forget_pad[l:l + 1]}
        for n in VEC[:-1]:
            w[n] = p[n][l:l + 1]
        return w

    h, sv0, w0, win1_land = _layer_fwd(x[0], mem[0], first_weights(0, win_land), 0,
                                       rest_src=[bf[n][0] for n in REST], next_src=[bf["w_in"][1]])
    h, sv1, w1, _ = _layer_fwd(h, mem[0], first_weights(1, win1_land[0]), 1, rest_src=[bf[n][1] for n in REST])
    dh, loss_row = _loss_head(h, loss_target[0], min(512, s_len), "loss_head")
    dh, g1, rest_g1, _, _ = _layer_bwd(dh, mem[0], w1, sv1, 1, scatter_rest=True)
    grad_x, g0, rest_g0, win1_g, win0_g = _layer_bwd(dh, mem[0], w0, sv0, 0, scatter_rest=True, dkv_src=[g1["w_in"]],
                                                     scatter_w_in=True)

    def small_rows(get):
        rows = [jnp.concatenate([get(n) for n in VEC_1024], axis=0),
                jnp.concatenate([get(n) for n in VEC_512], axis=0).reshape(len(VEC_512), D_MODEL),
                jnp.pad(get("b_forget").reshape(1, -1), ((0, 0), (0, D_MODEL - DEPTH * FOX_HEADS)))]
        return jnp.concatenate(rows, axis=0)

    def tap_rows(conv):
        return jnp.pad(conv.reshape(1, -1), ((0, 0), (0, 4 * D_MODEL - conv.size))).reshape(4, D_MODEL)

    n_vec_rows = DEPTH * len(VEC_1024) + len(VEC_512) + 1
    part = small_rows(lambda n: jnp.concatenate([g0[n], g1[n]], axis=0))
    conv_part = jnp.stack([g0["conv_w"], g1["conv_w"]]).reshape(CONV_WIDTH, D_MODEL)
    n_part = 1 + n_vec_rows + CONV_WIDTH
    pad_rows = -n_part % 8
    small_land = _exchange([jnp.concatenate([loss_row, part, conv_part, jnp.zeros((pad_rows, D_MODEL), F32)], axis=0)],
                           [True], "gather_small")[0]
    total = _sum_blocks(small_land, "sum_small")
    loss = jnp.sum(total[0])
    conv_g = total[1 + n_vec_rows:n_part].reshape(DEPTH, CONV_WIDTH, CONV_CH)
    conv_g = lax.dynamic_slice_in_dim(conv_g, me * conv_cols, conv_cols, axis=2)
    fill = jnp.zeros((SMALL_ROWS - n_vec_rows - 4, D_MODEL), F32)

    def small_pack(vec_rows, conv):
        return jnp.concatenate([vec_rows, tap_rows(conv), fill], axis=0)

    small_out = _adamw([small_pack(total[1:1 + n_vec_rows], conv_g)[None]],
                       *[small_pack(small_rows(lambda n: p[pre + n]), p[pre + "conv_w"])[None] for pre in ("", "m_", "v_")],
                       SMALL_ROWS, "adamw_small")

    def unpack_small(buf):
        out = {}
        for k, n in enumerate(VEC_1024):
            out[n] = buf[DEPTH * k:DEPTH * (k + 1)]
        at = DEPTH * len(VEC_1024)
        for k, n in enumerate(VEC_512):
            out[n] = buf[at + k].reshape(DEPTH, CONV_CH)
        at += len(VEC_512)
        out["b_forget"] = buf[at, :DEPTH * FOX_HEADS].reshape(DEPTH, FOX_HEADS)
        out["conv_w"] = buf[at + 1:at + 5].reshape(-1)[:DEPTH * CONV_WIDTH * conv_cols].reshape(DEPTH, CONV_WIDTH, conv_cols)
        return out

    big_out = {n: _adamw([rest_g0[i], rest_g1[i]], p[n], p["m_" + n], p["v_" + n], ADAMW_TILE[n], f"adamw_{n}")
               for i, n in enumerate(REST)}
    in_out = _adamw_cols([win0_g[0], win1_g[0]], w_in_t[""], w_in_t["m_"], w_in_t["v_"], "adamw_w_in")
    big_out["w_in"] = [a.reshape(in_cols, DEPTH, D_MODEL).transpose(1, 2, 0) for a in in_out]

    result = [loss, grad_x[None]]
    for k in range(4):
        smalls = unpack_small(small_out[k][0])
        result += [big_out[n][k] if n in big_out else smalls[n] for n in names]
    return tuple(result)
```

```python
import functools

import jax
import jax.numpy as jnp
from jax import lax
from jax.experimental import pallas as pl
from jax.experimental.pallas import tpu as pltpu

F32 = jnp.float32
BF16 = jnp.bfloat16

N_DEV = 8
DEPTH = 2
D_MODEL = 1024
CONV_CH = 512
CONV_WIDTH = 31
CONV_PAD = 32
FOX_HEADS = 8
FOX_HEAD_DIM = 64
FOX_WIDTH = 512
N_PAIRS = 4
MEM_HEADS = 4
MEM_HEAD_DIM = 128
MEM_INNER = 512
D_FF = 4096
IN_MAIN = 2560
IN_COLS = 2568
IN_CAT = IN_MAIN + 128
LANES = 128
EPS = 1e-6
NEG_INF = -1e30

ADAM_LR = 0.001
ADAM_B1 = 0.9
ADAM_B2 = 0.999
ADAM_EPS = 1e-08
ADAM_WD = 0.01
ADAM_STEP = 10

NT_DIMS = (((1,), (1,)), ((), ()))
TN_DIMS = (((0,), (0,)), ((), ()))

BIG = ("w_in", "w_out", "w_mq", "w_mk", "w_mv", "w_mo", "w_up", "w_down")
ADAMW_TILE = {"w_out": 128, "w_mq": 128, "w_mk": 128, "w_mv": 128, "w_mo": 512, "w_up": 256, "w_down": 128}

VEC_1024 = ("norm_mix_pre", "norm_mix_post", "norm_mem_pre", "norm_mem_post", "norm_memkv", "norm_mlp_pre", "norm_mlp_post")
VEC_512 = ("conv_b", "conv_ln_g", "conv_ln_b")
VEC = VEC_1024 + VEC_512 + ("b_forget",)
SMALL_ROWS = 32


def _sigmoid(x):
    return 1.0 / (1.0 + jnp.exp(-x))


def _rms(x, g):
    r = lax.rsqrt(jnp.mean(x * x, axis=-1, keepdims=True) + EPS)
    return x * r * g


def _rms_bwd(x, g, dh):
    r = lax.rsqrt(jnp.mean(x * x, axis=-1, keepdims=True) + EPS)
    gh = dh * g
    c = jnp.mean(gh * x, axis=-1, keepdims=True)
    dx = r * gh - x * (r * r * r * c)
    dg = jnp.sum(dh * (x * r), axis=0, keepdims=True)
    return dx, dg


def _full(shape):
    nd = len(shape)
    return pl.BlockSpec(shape, lambda *_: (0,) * nd)


def _params(*sem):
    return pltpu.CompilerParams(dimension_semantics=sem)


def _exchange_copies(src_refs, out_refs, same, send_sems, recv_sems, local_sems, with_recvs):
    x, y, c = lax.axis_index("x"), lax.axis_index("y"), lax.axis_index("c")
    me = 4 * x + 2 * y + c
    local, sends, recvs = [], [], []
    for a, (s_ref, o_ref) in enumerate(zip(src_refs, out_refs)):
        def mine(idx, s_ref=s_ref, whole=same[a]):
            return s_ref if whole else s_ref.at[idx]

        local.append(pltpu.make_async_copy(mine(me), o_ref.at[me], local_sems.at[a]))
        for k in range(1, N_DEV):
            px = 1 - x if k & 4 else x
            py = 1 - y if k & 2 else y
            pc = 1 - c if k & 1 else c
            peer = 4 * px + 2 * py + pc
            sem = a * (N_DEV - 1) + k - 1
            common = dict(send_sem=send_sems.at[sem], recv_sem=recv_sems.at[sem], device_id=(px, py, pc),
                          device_id_type=pl.DeviceIdType.MESH)
            sends.append(pltpu.make_async_remote_copy(src_ref=mine(peer), dst_ref=o_ref.at[me], **common))
            if with_recvs:
                recvs.append(pltpu.make_async_remote_copy(src_ref=mine(peer), dst_ref=o_ref.at[peer], **common))
    return local, sends, recvs


def _gather_copies(src_refs, out_refs, send_sems, recv_sems, local_sems, phase):
    x, y, c = lax.axis_index("x"), lax.axis_index("y"), lax.axis_index("c")
    sibling = (x, y, 1 - c)
    chips = [(1 - x, y), (x, 1 - y), (1 - x, 1 - y)]

    def idx(px, py, pc):
        return 4 * px + 2 * py + pc

    local, first, arrive, passed, final = [], [], [], [], []
    for a, (s_ref, o_ref) in enumerate(zip(src_refs, out_refs)):
        def cp(k, src, block, to, a=a, o_ref=o_ref):
            sem = a * (N_DEV - 1) + k
            return pltpu.make_async_remote_copy(src_ref=src, dst_ref=o_ref.at[block], send_sem=send_sems.at[sem],
                                                recv_sem=recv_sems.at[sem], device_id=to, device_id_type=pl.DeviceIdType.MESH)

        me = idx(x, y, c)
        if phase != 1:
            local.append(pltpu.make_async_copy(s_ref, o_ref.at[me], local_sems.at[a]))
            first.append(cp(0, s_ref, me, sibling))
        if phase == 2:
            final.append(cp(0, s_ref, idx(x, y, 1 - c), sibling))
        for j, chip in enumerate(chips):
            theirs = idx(*chip, c)
            if phase != 1:
                first.append(cp(1 + j, s_ref, me, (*chip, c)))
            if phase == 1:
                arrive.append(cp(1 + j, s_ref, theirs, (*chip, c)))
            if phase != 0:
                passed.append(cp(4 + j, o_ref.at[theirs], theirs, sibling))
            if phase == 2:
                final.append(cp(4 + j, s_ref, idx(*chip, 1 - c), sibling))
    return local, first, arrive, passed, final


def _pcall(body, *, name, grid, in_specs, out_specs, out_shape, args, scratch_shapes=(), sem=(), host=None):
    if host is None:
        return pl.pallas_call(body, name=name, grid=grid, in_specs=in_specs, out_specs=out_specs, out_shape=out_shape,
                              scratch_shapes=list(scratch_shapes), compiler_params=_params(*sem))(*args)
    srcs, same = host
    n_in, n_out, n_scr, n_h = len(in_specs), len(out_specs), len(scratch_shapes), len(srcs)
    hbm = pl.BlockSpec(memory_space=pltpu.HBM)
    lands = [jax.ShapeDtypeStruct((N_DEV,) + (s.shape if whole else s.shape[1:]), s.dtype) for s, whole in zip(srcs, same)]

    def wrapped(*refs):
        ins, src_refs = refs[:n_in], refs[n_in:n_in + n_h]
        outs = refs[n_in + n_h:n_in + n_h + n_out]
        land_refs = refs[n_in + n_h + n_out:n_in + 2 * n_h + n_out]
        scr = refs[n_in + 2 * n_h + n_out:n_in + 2 * n_h + n_out + n_scr]
        sems = refs[n_in + 2 * n_h + n_out + n_scr:]
        ids = [pl.program_id(d) for d in range(len(grid))]
        first = functools.reduce(jnp.logical_and, [i == 0 for i in ids])
        last = functools.reduce(jnp.logical_and, [i == n - 1 for i, n in zip(ids, grid)])
        later = functools.reduce(jnp.logical_and, [ids[0] == (3 * grid[0]) // 4] + [i == 0 for i in ids[1:]])

        if all(same):
            @pl.when(first)
            def _():
                local, sends, _, _, _ = _gather_copies(src_refs, land_refs, *sems, 0)
                for cp in local + sends:
                    cp.start()

            body(*ins, *outs, *scr)

            @pl.when(later)
            def _():
                _, _, arrive, passed, _ = _gather_copies(src_refs, land_refs, *sems, 1)
                for cp in arrive:
                    cp.wait_recv()
                for cp in passed:
                    cp.start()

            @pl.when(last)
            def _():
                local, sends, _, passed, final = _gather_copies(src_refs, land_refs, *sems, 2)
                for cp in final:
                    cp.wait_recv()
                for cp in sends + passed:
                    cp.wait_send()
                for cp in local:
                    cp.wait()
        else:
            @pl.when(first)
            def _():
                local, sends, _ = _exchange_copies(src_refs, land_refs, same, *sems, False)
                for cp in local + sends:
                    cp.start()

            body(*ins, *outs, *scr)

            @pl.when(last)
            def _():
                local, sends, recvs = _exchange_copies(src_refs, land_refs, same, *sems, True)
                for cp in recvs:
                    cp.wait_recv()
                for cp in sends:
                    cp.wait_send()
                for cp in local:
                    cp.wait()

    n_sem = n_h * (N_DEV - 1)
    res = pl.pallas_call(
        wrapped, name=name, grid=grid, in_specs=list(in_specs) + [hbm] * n_h, out_specs=list(out_specs) + [hbm] * n_h,
        out_shape=list(out_shape) + lands,
        scratch_shapes=list(scratch_shapes) + [pltpu.SemaphoreType.DMA((n_sem,)), pltpu.SemaphoreType.DMA((n_sem,)),
                                               pltpu.SemaphoreType.DMA((n_h,))],
        compiler_params=_params(*(("arbitrary",) * len(grid))),
    )(*args, *srcs)
    return list(res[:n_out]), list(res[n_out:])


def _exchange(srcs, same, name):
    def body():
        pass

    return _pcall(body, name=name, grid=(1,), in_specs=[], out_specs=[], out_shape=[], args=[], host=(srcs, same))[1]


def _rms_matmul(x, g, w, segs, tile, name, host=None, w_t=False):
    s_len, d = x.shape
    chunk = 512

    def body(x_ref, g_ref, w_ref, h_ref, *outs):
        h = _rms(x_ref[...], g_ref[...]).astype(BF16)
        h_ref[...] = h
        oi = 0
        for c0, c1, fns in segs:
            for a in range(c0, c1, chunk):
                b = min(a + chunk, c1)
                if w_t:
                    z = lax.dot_general(h, w_ref[a:b, :], NT_DIMS, preferred_element_type=F32)
                else:
                    z = jnp.dot(h, w_ref[:, a:b], preferred_element_type=F32)
                for k, (dt, fn) in enumerate(fns):
                    outs[oi + k][:, a - c0:b - c0] = fn(z).astype(dt)
            oi += len(fns)

    out_shape = [jax.ShapeDtypeStruct((s_len, d), BF16)]
    out_specs = [pl.BlockSpec((tile, d), lambda i: (i, 0))]
    for c0, c1, fns in segs:
        for dt, _ in fns:
            out_shape.append(jax.ShapeDtypeStruct((s_len, c1 - c0), dt))
            out_specs.append(pl.BlockSpec((tile, c1 - c0), lambda i: (i, 0)))
    return _pcall(
        body, name=name, grid=(s_len // tile,),
        in_specs=[pl.BlockSpec((tile, d), lambda i: (i, 0)), _full((1, d)), _full(w.shape)],
        out_specs=out_specs, out_shape=out_shape, args=[x, g, w], sem=("parallel",), host=host)


def _matmul_resnorm(a, w, x, g, tile, name):
    s_len, k = a.shape
    d = w.shape[1]

    def body(a_ref, w_ref, x_ref, g_ref, y_ref, xo_ref):
        y = jnp.dot(a_ref[...], w_ref[...], preferred_element_type=F32)
        y_ref[...] = y
        xo_ref[...] = x_ref[...] + _rms(y, g_ref[...])

    row = lambda i: (i, 0)
    return pl.pallas_call(
        body, name=name, grid=(s_len // tile,),
        in_specs=[pl.BlockSpec((tile, k), row), _full((k, d)), pl.BlockSpec((tile, d), row), _full((1, d))],
        out_specs=[pl.BlockSpec((tile, d), row), pl.BlockSpec((tile, d), row)],
        out_shape=[jax.ShapeDtypeStruct((s_len, d), F32), jax.ShapeDtypeStruct((s_len, d), F32)],
        compiler_params=_params("parallel"),
    )(a, w, x, g)


def _resnorm_bwd_mm(dx, y, g, w, tile, name, out_dtype, pre=None):
    s_len, d = dx.shape
    k = w.shape[0]
    chunk = 512

    def body(*refs):
        if pre is None:
            dx_ref, y_ref, g_ref, w_ref, dy_ref, da_ref, dg_ref = refs
        else:
            dx_ref, y_ref, g_ref, w_ref, pre_ref, dy_ref, da_ref, dg_ref = refs
        dy, dg = _rms_bwd(y_ref[...], g_ref[...], dx_ref[...])
        dyb = dy.astype(BF16)
        dy_ref[...] = dyb

        @pl.when(pl.program_id(0) == 0)
        def _():
            dg_ref[...] = jnp.zeros_like(dg_ref)

        dg_ref[...] += dg
        for a in range(0, k, chunk):
            b = min(a + chunk, k)
            da = lax.dot_general(dyb, w_ref[a:b, :], NT_DIMS, preferred_element_type=F32)
            if pre is not None:
                da = da * (2.0 * jnp.maximum(pre_ref[:, a:b].astype(F32), 0.0))
            da_ref[:, a:b] = da.astype(out_dtype)

    row = lambda i: (i, 0)
    in_specs = [pl.BlockSpec((tile, d), row), pl.BlockSpec((tile, d), row), _full((1, d)), _full((k, d))]
    args = [dx, y, g, w]
    if pre is not None:
        in_specs.append(pl.BlockSpec((tile, k), row))
        args.append(pre)
    return pl.pallas_call(
        body, name=name, grid=(s_len // tile,), in_specs=in_specs,
        out_specs=[pl.BlockSpec((tile, d), row), pl.BlockSpec((tile, k), row), _full((1, d))],
        out_shape=[jax.ShapeDtypeStruct((s_len, d), BF16), jax.ShapeDtypeStruct((s_len, k), out_dtype),
                   jax.ShapeDtypeStruct((1, d), F32)],
        compiler_params=_params("arbitrary"),
    )(*args)


def _mm_prenorm_bwd(dz, w, x, g, dres, tile, name, host=None, w_t=False):
    pieces = list(dz) if isinstance(dz, (list, tuple)) else [dz]
    n_p = len(pieces)
    widths = [p.shape[1] for p in pieces]
    s_len = pieces[0].shape[0]
    d = w.shape[1] if w_t else w.shape[0]

    def body(*refs):
        dz_refs, rest = refs[:n_p], refs[n_p:]
        if dres is None:
            w_ref, x_ref, g_ref, dx_ref, dg_ref = rest
        else:
            w_ref, x_ref, g_ref, dres_ref, dx_ref, dg_ref = rest
        dh, off = None, 0
        for dz_ref, width in zip(dz_refs, widths):
            if w_t:
                part = jnp.dot(dz_ref[...], w_ref[off:off + width, :], preferred_element_type=F32)
            else:
                part = lax.dot_general(dz_ref[...], w_ref[:, off:off + width], NT_DIMS, preferred_element_type=F32)
            dh = part if dh is None else dh + part
            off += width
        dx, dg = _rms_bwd(x_ref[...], g_ref[...], dh)
        if dres is not None:
            dx = dx + dres_ref[...]
        dx_ref[...] = dx

        @pl.when(pl.program_id(0) == 0)
        def _():
            dg_ref[...] = jnp.zeros_like(dg_ref)

        dg_ref[...] += dg

    row = lambda i: (i, 0)
    in_specs = [pl.BlockSpec((tile, width), row) for width in widths]
    in_specs += [_full(w.shape), pl.BlockSpec((tile, d), row), _full((1, d))]
    args = pieces + [w, x, g]
    if dres is not None:
        in_specs.append(pl.BlockSpec((tile, d), row))
        args.append(dres)
    return _pcall(
        body, name=name, grid=(s_len // tile,), in_specs=in_specs,
        out_specs=[pl.BlockSpec((tile, d), row), _full((1, d))],
        out_shape=[jax.ShapeDtypeStruct((s_len, d), F32), jax.ShapeDtypeStruct((1, d), F32)],
        args=args, sem=("arbitrary",), host=host)


def _matmul_tn_rows(pieces, b, tk, name):
    s_len, n = b.shape
    widths = [p.shape[1] for p in pieces]

    def body(*refs):
        a_refs, b_ref, o_ref = refs[:-2], refs[-2], refs[-1]

        @pl.when(pl.program_id(0) == 0)
        def _():
            o_ref[...] = jnp.zeros_like(o_ref)

        b_tile = b_ref[...]
        off = 0
        for a_ref, width in zip(a_refs, widths):
            o_ref[off:off + width, :] += lax.dot_general(a_ref[...], b_tile, TN_DIMS, preferred_element_type=F32)
            off += width

    return pl.pallas_call(
        body, name=name, grid=(s_len // tk,),
        in_specs=[pl.BlockSpec((tk, width), lambda k: (k, 0)) for width in widths] + [pl.BlockSpec((tk, n), lambda k: (k, 0))],
        out_specs=_full((sum(widths), n)),
        out_shape=jax.ShapeDtypeStruct((sum(widths), n), F32),
        compiler_params=_params("arbitrary"),
    )(*pieces, b)


def _matmul_tn_shards(a, b, axis, tk, name):
    s_len, m = a.shape
    n = b.shape[1]
    r, c = (m // N_DEV, n) if axis == 0 else (m, n // N_DEV)
    n_k = s_len // tk
    tm = max(r, min(m, (1 << 20) // n)) if axis == 0 else min(m, (1 << 20) // n)

    def body(a_ref, b_ref, o_ref, acc):
        k = pl.program_id(1)

        @pl.when(k == 0)
        def _():
            acc[...] = jnp.zeros_like(acc)

        acc[...] += lax.dot_general(a_ref[...], b_ref[...], TN_DIMS, preferred_element_type=F32)

        @pl.when(k == n_k - 1)
        def _():
            if axis == 0:
                o_ref[...] = acc[...].reshape(tm // r, r, c).astype(BF16)
            else:
                for j in range(N_DEV):
                    o_ref[j] = acc[:, j * c:(j + 1) * c].astype(BF16)

    if axis == 0:
        out_spec = pl.BlockSpec((tm // r, r, c), lambda i, k: (i, 0, 0))
    else:
        out_spec = pl.BlockSpec((N_DEV, tm, c), lambda i, k: (0, i, 0))
    return pl.pallas_call(
        body, name=name, grid=(m // tm, n_k),
        in_specs=[pl.BlockSpec((tk, tm), lambda i, k: (k, i)), pl.BlockSpec((tk, n), lambda i, k: (k, 0))],
        out_specs=out_spec, out_shape=jax.ShapeDtypeStruct((N_DEV, r, c), BF16),
        scratch_shapes=[pltpu.VMEM((tm, n), F32)],
        compiler_params=_params("parallel", "arbitrary"),
    )(a, b)


def _cumsum_bwd(rs, cs, fl, b, tile, name):
    s_len = fl.shape[0]
    n_t = s_len // tile

    def body(rs_ref, cs_ref, fl_ref, b_ref, dfl_ref, db_ref, carry):
        @pl.when(pl.program_id(0) == 0)
        def _():
            carry[...] = jnp.zeros_like(carry)
            db_ref[...] = jnp.zeros_like(db_ref)

        r = lax.broadcasted_iota(jnp.int32, (tile, tile), 0)
        c = lax.broadcasted_iota(jnp.int32, (tile, tile), 1)
        tri = (c >= r).astype(F32)
        lane = lax.broadcasted_iota(jnp.int32, (1, LANES), 1)
        dc = jnp.zeros((tile, LANES), F32)
        for p in range(N_PAIRS):
            blk = rs_ref[:, LANES * p:LANES * (p + 1)] - cs_ref[:, LANES * p:LANES * (p + 1)]
            dc = jnp.where(lane == 2 * p, blk, dc)
            dc = jnp.where(lane == 2 * p + 1, pltpu.roll(blk, FOX_HEAD_DIM, axis=1), dc)
        dl = jnp.dot(tri, dc, precision=lax.Precision.HIGHEST, preferred_element_type=F32) + carry[...]
        carry[...] = dl[0:1, :]
        dfl = dl * _sigmoid(-(fl_ref[...] + b_ref[...]))
        dfl_ref[...] = dfl.astype(BF16)
        db_ref[...] += jnp.sum(dfl, axis=0, keepdims=True)

    rev = lambda i: (n_t - 1 - i, 0)
    return pl.pallas_call(
        body, name=name, grid=(n_t,),
        in_specs=[pl.BlockSpec((tile, FOX_WIDTH), rev), pl.BlockSpec((tile, FOX_WIDTH), rev), pl.BlockSpec((tile, LANES), rev),
                  _full((1, LANES))],
        out_specs=[pl.BlockSpec((tile, LANES), rev), _full((1, LANES))],
        out_shape=[jax.ShapeDtypeStruct((s_len, LANES), BF16), jax.ShapeDtypeStruct((1, LANES), F32)],
        scratch_shapes=[pltpu.VMEM((1, LANES), F32)],
        compiler_params=_params("arbitrary"),
    )(rs, cs, fl, b)


SUBLANES = 8
CONV_ROWS = 64


def _phase_copies(src, dst, rows):
    for p in range(SUBLANES):
        dst[p] = src[pl.ds(p, rows), :]


def _phase_rows(extp_ref, off, r0, rows):
    p = off % SUBLANES
    return extp_ref[p, pl.ds(pl.multiple_of(r0 + (off - p), SUBLANES), rows), :]


def _conv_taps(w_ref, extp_ref, base, r0, rows, reverse):
    acc = None
    for k in range(CONV_WIDTH):
        off = base + ((CONV_WIDTH - 1 - k) if reverse else k)
        term = w_ref[k:k + 1, :] * _phase_rows(extp_ref, off, r0, rows)
        acc = term if acc is None else acc + term
    return acc


def _fold_rows(x):
    out = x[0:SUBLANES]
    for i in range(1, x.shape[0] // SUBLANES):
        out = out + x[i * SUBLANES:(i + 1) * SUBLANES]
    return out


def _conv_fwd(ag, w, cb, lg, lb, tile, name):
    s_len = ag.shape[0]
    c = CONV_CH
    rb = tile

    def body(ag_ref, w_ref, cb_ref, lg_ref, lb_ref, u_ref, u1_ref, ext, extp):
        @pl.when(pl.program_id(0) == 0)
        def _():
            ext[0:CONV_PAD, :] = jnp.zeros((CONV_PAD, c), F32)
            ext[tile + CONV_PAD:tile + CONV_PAD + SUBLANES, :] = jnp.zeros((SUBLANES, c), F32)

        ext[CONV_PAD:CONV_PAD + tile, :] = ag_ref[:, 0:c] * _sigmoid(ag_ref[:, c:2 * c])
        _phase_copies(ext, extp, tile + CONV_PAD)

        def block(b, carry):
            r0 = pl.multiple_of(b * rb, rb)
            u1 = _conv_taps(w_ref, extp, CONV_PAD - (CONV_WIDTH - 1), r0, rb, False) + cb_ref[...]
            u1_ref[pl.ds(r0, rb), :] = u1
            mu = jnp.mean(u1, axis=-1, keepdims=True)
            xc = u1 - mu
            y = xc * lax.rsqrt(jnp.mean(xc * xc, axis=-1, keepdims=True) + EPS) * lg_ref[...] + lb_ref[...]
            u_ref[pl.ds(r0, rb), :] = (y * _sigmoid(y)).astype(BF16)
            return carry

        lax.fori_loop(0, tile // rb, block, 0)
        ext[0:CONV_PAD, :] = ext[tile:tile + CONV_PAD, :]

    return pl.pallas_call(
        body, name=name, grid=(s_len // tile,),
        in_specs=[pl.BlockSpec((tile, 2 * c), lambda i: (i, 0)), _full((CONV_PAD, c)), _full((1, c)), _full((1, c)),
                  _full((1, c))],
        out_specs=[pl.BlockSpec((tile, c), lambda i: (i, 0))] * 2,
        out_shape=[jax.ShapeDtypeStruct((s_len, c), BF16), jax.ShapeDtypeStruct((s_len, c), F32)],
        scratch_shapes=[pltpu.VMEM((tile + CONV_PAD + SUBLANES, c), F32), pltpu.VMEM((SUBLANES, tile + CONV_PAD, c), F32)],
        compiler_params=_params("arbitrary"),
    )(ag, w, cb, lg, lb)


def _conv_bwd(ag, u1, dcat, w, lg, lb, tile, name):
    s_len = ag.shape[0]
    c = CONV_CH
    n_t = s_len // tile
    per = tile // CONV_PAD
    rb = min(CONV_ROWS, tile)

    def body(ag_ref, halo_ref, u1_ref, du_ref, w_ref, lg_ref, lb_ref, dag_ref, dw_ref, dv_ref, ext, ext2, extp, dwacc):
        i = pl.program_id(0)
        t = n_t - 1 - i

        @pl.when(i == 0)
        def _():
            ext2[tile:tile + CONV_PAD + SUBLANES, :] = jnp.zeros((CONV_PAD + SUBLANES, c), F32)
            ext[tile + CONV_PAD:tile + CONV_PAD + SUBLANES, :] = jnp.zeros((SUBLANES, c), F32)
            dwacc[...] = jnp.zeros_like(dwacc)
            dv_ref[...] = jnp.zeros_like(dv_ref)

        halo = halo_ref[:, 0:c] * _sigmoid(halo_ref[:, c:2 * c])
        ext[0:CONV_PAD, :] = jnp.where(t > 0, halo, 0.0)
        ext[CONV_PAD:CONV_PAD + tile, :] = ag_ref[:, 0:c] * _sigmoid(ag_ref[:, c:2 * c])
        _phase_copies(ext, extp, tile + CONV_PAD)
        base = CONV_PAD - (CONV_WIDTH - 1)

        def block1(b, carry):
            r0 = pl.multiple_of(b * rb, rb)
            u1 = u1_ref[pl.ds(r0, rb), :]
            mu = jnp.mean(u1, axis=-1, keepdims=True)
            xc = u1 - mu
            rs = lax.rsqrt(jnp.mean(xc * xc, axis=-1, keepdims=True) + EPS)
            xhat = xc * rs
            y = xhat * lg_ref[...] + lb_ref[...]
            sy = _sigmoid(y)
            dy = du_ref[pl.ds(r0, rb), :] * (sy * (1.0 + y * (1.0 - sy)))
            dxh = dy * lg_ref[...]
            du1 = rs * (dxh - jnp.mean(dxh, axis=-1, keepdims=True) - xhat * jnp.mean(dxh * xhat, axis=-1, keepdims=True))
            dv_ref[0:1, :] += jnp.sum(du1, axis=0, keepdims=True)
            dv_ref[1:2, :] += jnp.sum(dy * xhat, axis=0, keepdims=True)
            dv_ref[2:3, :] += jnp.sum(dy, axis=0, keepdims=True)
            for k in range(CONV_WIDTH):
                dwacc[k] += _fold_rows(du1 * _phase_rows(extp, base + k, r0, rb))
            ext2[pl.ds(r0, rb), :] = du1
            return carry

        lax.fori_loop(0, tile // rb, block1, 0)
        _phase_copies(ext2, extp, tile + CONV_PAD)

        def block2(b, carry):
            r0 = pl.multiple_of(b * rb, rb)
            du0 = _conv_taps(w_ref, extp, 0, r0, rb, True)
            a = ag_ref[pl.ds(r0, rb), 0:c]
            sg = _sigmoid(ag_ref[pl.ds(r0, rb), c:2 * c])
            dag_ref[pl.ds(r0, rb), 0:c] = (du0 * sg).astype(BF16)
            dag_ref[pl.ds(r0, rb), c:2 * c] = (du0 * a * sg * (1.0 - sg)).astype(BF16)
            return carry

        lax.fori_loop(0, tile // rb, block2, 0)
        ext2[tile:tile + CONV_PAD, :] = ext2[0:CONV_PAD, :]

        @pl.when(i == n_t - 1)
        def _():
            for k in range(CONV_WIDTH):
                dw_ref[k:k + 1, :] = jnp.sum(dwacc[k], axis=0, keepdims=True)
            dw_ref[CONV_WIDTH:CONV_PAD, :] = jnp.zeros((CONV_PAD - CONV_WIDTH, c), F32)

    rev = lambda i: (n_t - 1 - i, 0)
    return pl.pallas_call(
        body, name=name, grid=(n_t,),
        in_specs=[pl.BlockSpec((tile, 2 * c), rev),
                  pl.BlockSpec((CONV_PAD, 2 * c), lambda i: (jnp.maximum((n_t - 1 - i) * per - 1, 0), 0)),
                  pl.BlockSpec((tile, c), rev), pl.BlockSpec((tile, c), rev), _full((CONV_PAD, c)), _full((1, c)), _full((1, c))],
        out_specs=[pl.BlockSpec((tile, 2 * c), rev), _full((CONV_PAD, c)), _full((8, c))],
        out_shape=[jax.ShapeDtypeStruct((s_len, 2 * c), BF16), jax.ShapeDtypeStruct((CONV_PAD, c), F32),
                   jax.ShapeDtypeStruct((8, c), F32)],
        scratch_shapes=[pltpu.VMEM((tile + CONV_PAD + SUBLANES, c), F32), pltpu.VMEM((tile + CONV_PAD + SUBLANES, c), F32),
                        pltpu.VMEM((SUBLANES, tile + CONV_PAD, c), F32), pltpu.VMEM((CONV_PAD, SUBLANES, c), F32)],
        compiler_params=_params("arbitrary"),
    )(ag, ag, u1, dcat, w, lg, lb)


Q_SCALE = FOX_HEAD_DIM ** -0.5


def _head_col(x, lane, h):
    return jnp.sum(jnp.where(lane == h, x, 0.0), axis=1, keepdims=True)


def _split3(x):
    hi = x.astype(BF16).astype(F32)
    r = x - hi
    mid = r.astype(BF16).astype(F32)
    lo = (r - mid).astype(BF16).astype(F32)
    return hi, mid, lo


def _in_lanes(lane, lo, n):
    return (lane >= lo) & (lane < lo + n)


def _put3(lane, lo, parts, rest):
    return jnp.where(lane == lo, parts[0], jnp.where(lane == lo + 1, parts[1], jnp.where(lane == lo + 2, parts[2], rest)))


def _spare_lane(h):
    return FOX_HEAD_DIM if h % 2 == 0 else 0


def _shift_div(i, num, den):
    return i * (num // den) if num >= den else lax.shift_right_logical(i, (den // num).bit_length() - 1)


def _foxa_prep(qkv, fl, b, tile, name):
    s_len = qkv.shape[0]

    def body(q_ref, k_ref, v_ref, fl_ref, b_ref, qa_ref, ka_ref, va_ref, carry):
        @pl.when(pl.program_id(0) == 0)
        def _():
            carry[...] = jnp.zeros_like(carry)

        xx = fl_ref[...] + b_ref[...]
        lf = jnp.minimum(xx, 0.0) - jnp.log1p(jnp.exp(-jnp.abs(xx)))
        tri = (lax.broadcasted_iota(jnp.int32, (tile, tile), 1) <= lax.broadcasted_iota(jnp.int32, (tile, tile), 0))
        cum_t = jnp.dot(tri.astype(F32), lf, precision=lax.Precision.HIGHEST, preferred_element_type=F32) + carry[...]
        carry[...] = cum_t[tile - 1:tile, :]
        lane = lax.broadcasted_iota(jnp.int32, (1, LANES), 1)
        for h in range(FOX_HEADS):
            e = _spare_lane(h)
            head = ~_in_lanes(lane, e, FOX_HEAD_DIM)
            blk = slice(LANES * (h // 2), LANES * (h // 2) + LANES)
            out = slice(LANES * h, LANES * h + LANES)
            c3 = _split3(_head_col(cum_t, lane, h))
            ex_q = _put3(lane, e, c3, jnp.where(_in_lanes(lane, e + 3, 3), 1.0, 0.0))
            qa_ref[:, out] = jnp.where(head, q_ref[:, blk].astype(F32) * Q_SCALE, ex_q).astype(BF16)
            ones = jnp.where(_in_lanes(lane, e, 3) | _in_lanes(lane, e + 6, 3), 1.0, 0.0)
            ex_k = _put3(lane, e + 3, [-c for c in c3], ones)
            ka_ref[:, out] = jnp.where(head, k_ref[:, blk].astype(F32), ex_k).astype(BF16)
            ex_v = jnp.where(_in_lanes(lane, e, 3), 1.0, 0.0)
            va_ref[:, out] = jnp.where(head, v_ref[:, blk].astype(F32), ex_v).astype(BF16)

    col = lambda c: pl.BlockSpec((tile, FOX_WIDTH), lambda i: (i, c))
    wide = pl.BlockSpec((tile, 2 * FOX_WIDTH), lambda i: (i, 0))
    return pl.pallas_call(
        body, name=name, grid=(s_len // tile,),
        in_specs=[col(0), col(1), col(2), pl.BlockSpec((tile, LANES), lambda i: (i, 0)), _full((1, LANES))],
        out_specs=[wide, wide, wide], out_shape=[jax.ShapeDtypeStruct((s_len, 2 * FOX_WIDTH), BF16)] * 3,
        scratch_shapes=[pltpu.VMEM((1, LANES), F32)],
        compiler_params=_params("arbitrary"),
    )(qkv, qkv, qkv, fl, b)


def _foxa_fwd(qa, ka, va, tq, tk, name, host=None):
    s_len = qa.shape[0]

    def body(q_ref, k_ref, v_ref, o_ref, lse_ref, ob_ref):
        i = pl.program_id(1)
        lane = lax.broadcasted_iota(jnp.int32, (1, LANES), 1)
        cols = [slice(LANES * hh, LANES * hh + LANES) for hh in range(2)]
        qh = [q_ref[:, c] for c in cols]

        def scores(j):
            off = pl.multiple_of(j * tk, tk)
            return [lax.dot_general(qh[hh], k_ref[pl.ds(off, tk), cols[hh]], NT_DIMS, preferred_element_type=F32)
                    for hh in range(2)]

        def update(j, s, m, acc, mask):
            off = pl.multiple_of(j * tk, tk)
            m_out, acc_out = [], []
            for hh in range(2):
                sh = s[hh] if mask is None else jnp.where(mask, s[hh], NEG_INF)
                m_new = jnp.maximum(m[hh], jnp.max(sh, axis=1, keepdims=True))
                pr = jnp.exp(sh - m_new).astype(BF16)
                acc_out.append(jnp.exp(m[hh] - m_new) * acc[hh]
                               + jnp.dot(pr, v_ref[pl.ds(off, tk), cols[hh]], preferred_element_type=F32))
                m_out.append(m_new)
            return m_out, acc_out

        def step(j, carry):
            s_next = scores(j + 1)
            m, acc = update(j, carry[0:2], carry[2:4], carry[4:6], None)
            return (*s_next, *m, *acc)

        n_full = _shift_div(i, tq, tk)
        n_part = max(tq // tk, 1)
        qi = lax.broadcasted_iota(jnp.int32, (tq, tk), 0) + i * tq
        ki = lax.broadcasted_iota(jnp.int32, (tq, tk), 1)
        carry = (*scores(0), *([jnp.full((tq, 1), NEG_INF, F32)] * 2), *([jnp.zeros((tq, LANES), F32)] * 2))
        carry = lax.fori_loop(0, n_full, step, carry)
        s, m, acc = list(carry[0:2]), list(carry[2:4]), list(carry[4:6])
        for jj in range(n_part):
            s_next = scores(n_full + jj + 1) if jj < n_part - 1 else None
            m, acc = update(n_full + jj, s, m, acc, ki + (n_full + jj) * tk <= qi)
            s = s_next
        res = []
        for hh in range(2):
            l = acc[hh][:, _spare_lane(hh):_spare_lane(hh) + 1]
            res.append((acc[hh] / l, m[hh] + jnp.log(l)))
        low = lane < FOX_HEAD_DIM
        o_pair = jnp.where(low, res[0][0], res[1][0])
        o_ref[...] = o_pair
        ob_ref[...] = o_pair.astype(BF16)
        lse_ref[...] = jnp.where(low, res[0][1], res[1][1])

    pair = pl.BlockSpec((s_len, 2 * LANES), lambda p, i: (0, p))
    out = pl.BlockSpec((tq, LANES), lambda p, i: (i, p))
    return _pcall(
        body, name=name, grid=(N_PAIRS, s_len // tq),
        in_specs=[pl.BlockSpec((tq, 2 * LANES), lambda p, i: (i, p)), pair, pair],
        out_specs=[out, out, out],
        out_shape=[jax.ShapeDtypeStruct((s_len, FOX_WIDTH), F32)] * 2 + [jax.ShapeDtypeStruct((s_len, FOX_WIDTH), BF16)],
        args=[qa, ka, va], sem=("parallel", "parallel"), host=host)


def _foxa_dq(qa, lse, o, dcat, ka, va, tq, tk, name, host=None):
    s_len = qa.shape[0]

    def body(qa_ref, lse_ref, o_ref, dcat_ref, k_ref, v_ref, dq_ref, rs_ref, q_ref, do_ref):
        i = pl.program_id(1)
        lane = lax.broadcasted_iota(jnp.int32, (1, LANES), 1)
        cols = [slice(LANES * hh, LANES * hh + LANES) for hh in range(2)]
        d_o = dcat_ref[...]
        prod = d_o * o_ref[...]
        for hh in range(2):
            e = _spare_lane(hh)
            head = ~_in_lanes(lane, e, FOX_HEAD_DIM)
            delta = jnp.sum(jnp.where(head, prod, 0.0), axis=1, keepdims=True)
            do_ref[:, cols[hh]] = _put3(lane, e, _split3(-delta), jnp.where(head, d_o, 0.0)).astype(BF16)
            l3 = _split3(-lse_ref[:, FOX_HEAD_DIM - e:FOX_HEAD_DIM - e + 1])
            q_ref[:, cols[hh]] = _put3(lane, e + 6, l3, qa_ref[:, cols[hh]].astype(F32)).astype(BF16)
        qh = [q_ref[:, c] for c in cols]
        doh = [do_ref[:, c] for c in cols]

        def update(j, acc, mask):
            off = pl.multiple_of(j * tk, tk)
            out = []
            for hh in range(2):
                kt = k_ref[pl.ds(off, tk), cols[hh]]
                pr = jnp.exp(lax.dot_general(qh[hh], kt, NT_DIMS, preferred_element_type=F32))
                if mask is not None:
                    pr = jnp.where(mask, pr, 0.0)
                ds = pr * lax.dot_general(doh[hh], v_ref[pl.ds(off, tk), cols[hh]], NT_DIMS, preferred_element_type=F32)
                out.append(acc[hh] + jnp.dot(ds.astype(BF16), kt, preferred_element_type=F32))
            return tuple(out)

        n_full = _shift_div(i, tq, tk)
        qi = lax.broadcasted_iota(jnp.int32, (tq, tk), 0) + i * tq
        ki = lax.broadcasted_iota(jnp.int32, (tq, tk), 1)
        acc = lax.fori_loop(0, n_full, lambda j, a: update(j, a, None), (jnp.zeros((tq, LANES), F32),) * 2)
        for jj in range(max(tq // tk, 1)):
            acc = update(n_full + jj, acc, ki + (n_full + jj) * tk <= qi)
        low = lane < FOX_HEAD_DIM
        dq_ref[...] = (jnp.where(low, acc[0], acc[1]) * Q_SCALE).astype(BF16)
        rs_ref[...] = jnp.where(low, acc[0][:, _spare_lane(0):_spare_lane(0) + 1], acc[1][:, _spare_lane(1):_spare_lane(1) + 1])

    pair = pl.BlockSpec((s_len, 2 * LANES), lambda p, i: (0, p))
    tile2 = pl.BlockSpec((tq, 2 * LANES), lambda p, i: (i, p))
    out = pl.BlockSpec((tq, LANES), lambda p, i: (i, p))
    wide = jax.ShapeDtypeStruct((s_len, 2 * FOX_WIDTH), BF16)
    return _pcall(
        body, name=name, grid=(N_PAIRS, s_len // tq),
        in_specs=[tile2, out, out, pl.BlockSpec((tq, LANES), lambda p, i: (i, N_PAIRS + p)), pair, pair],
        out_specs=[out, out, tile2, tile2],
        out_shape=[jax.ShapeDtypeStruct((s_len, FOX_WIDTH), BF16), jax.ShapeDtypeStruct((s_len, FOX_WIDTH), F32), wide, wide],
        args=[qa, lse, o, dcat, ka, va], sem=("parallel", "parallel"), host=host)


def _foxa_dkv(qb, ka, va, doa, tq, tk, name, host=None):
    s_len = qb.shape[0]
    n_q = s_len // tq

    def body(k_ref, v_ref, q_ref, do_ref, dk_ref, dv_ref, cs_ref):
        j = pl.program_id(1)
        lane = lax.broadcasted_iota(jnp.int32, (1, LANES), 1)
        cols = [slice(LANES * hh, LANES * hh + LANES) for hh in range(2)]
        kh = [k_ref[:, c] for c in cols]
        vh = [v_ref[:, c] for c in cols]

        def update(i, acc, mask):
            off = pl.multiple_of(i * tq, tq)
            out = []
            for hh in range(2):
                qt = q_ref[pl.ds(off, tq), cols[hh]]
                dot = do_ref[pl.ds(off, tq), cols[hh]]
                pt = jnp.exp(lax.dot_general(kh[hh], qt, NT_DIMS, preferred_element_type=F32))
                if mask is not None:
                    pt = jnp.where(mask, pt, 0.0)
                dv = acc[2 * hh + 1] + jnp.dot(pt.astype(BF16), dot, preferred_element_type=F32)
                dst = pt * lax.dot_general(vh[hh], dot, NT_DIMS, preferred_element_type=F32)
                out += [acc[2 * hh] + jnp.dot(dst.astype(BF16), qt, preferred_element_type=F32), dv]
            return tuple(out)

        i0 = _shift_div(j, tk, tq)
        n_part = max(tk // tq, 1)
        ki = lax.broadcasted_iota(jnp.int32, (tk, tq), 0) + j * tk
        qi = lax.broadcasted_iota(jnp.int32, (tk, tq), 1)
        acc = (jnp.zeros((tk, LANES), F32),) * 4
        for ii in range(n_part):
            acc = update(i0 + ii, acc, ki <= qi + (i0 + ii) * tq)
        acc = lax.fori_loop(i0 + n_part, n_q, lambda i, a: update(i, a, None), acc)
        low = lane < FOX_HEAD_DIM
        dk_ref[...] = jnp.where(low, acc[0], acc[2]).astype(BF16)
        dv_ref[...] = jnp.where(low, acc[1], acc[3]).astype(BF16)
        cs_ref[...] = jnp.where(low, acc[0][:, _spare_lane(0) + 3:_spare_lane(0) + 4],
                                acc[2][:, _spare_lane(1) + 3:_spare_lane(1) + 4])

    pair = pl.BlockSpec((s_len, 2 * LANES), lambda p, j: (0, p))
    tile2 = pl.BlockSpec((tk, 2 * LANES), lambda p, j: (j, p))
    out = pl.BlockSpec((tk, LANES), lambda p, j: (j, p))
    return _pcall(
        body, name=name, grid=(N_PAIRS, s_len // tk), in_specs=[tile2, tile2, pair, pair], out_specs=[out, out, out],
        out_shape=[jax.ShapeDtypeStruct((s_len, FOX_WIDTH), BF16), jax.ShapeDtypeStruct((s_len, FOX_WIDTH), BF16),
                   jax.ShapeDtypeStruct((s_len, FOX_WIDTH), F32)],
        args=[ka, va, qb, doa], sem=("parallel", "parallel"), host=host)


def _mem_scores_t(q, kv, h):
    lo = h * MEM_HEAD_DIM
    st = lax.dot_general(kv[:, lo:lo + MEM_HEAD_DIM], q[:, lo:lo + MEM_HEAD_DIM], NT_DIMS,
                         preferred_element_type=F32) * (MEM_HEAD_DIM ** -0.5)
    e = jnp.exp(st - jnp.max(st, axis=0, keepdims=True))
    return e / jnp.sum(e, axis=0, keepdims=True)


def _memattn_fwd(q, kv, tile, name):
    s_len = q.shape[0]
    n_mem = kv.shape[0]

    def body(q_ref, kv_ref, o_ref):
        q = q_ref[...]
        kv = kv_ref[...]
        for h in range(MEM_HEADS):
            lo = h * MEM_HEAD_DIM
            pt = _mem_scores_t(q, kv, h).astype(BF16)
            vh = kv[:, MEM_INNER + lo:MEM_INNER + lo + MEM_HEAD_DIM]
            o_ref[:, lo:lo + MEM_HEAD_DIM] = lax.dot_general(pt, vh, TN_DIMS, preferred_element_type=F32).astype(BF16)

    return pl.pallas_call(
        body, name=name, grid=(s_len // tile,),
        in_specs=[pl.BlockSpec((tile, MEM_INNER), lambda i: (i, 0)), _full((n_mem, 2 * MEM_INNER))],
        out_specs=pl.BlockSpec((tile, MEM_INNER), lambda i: (i, 0)),
        out_shape=jax.ShapeDtypeStruct((s_len, MEM_INNER), BF16),
        compiler_params=_params("parallel"),
    )(q, kv)


def _memattn_bwd(q, kv, do, tile, name):
    s_len = q.shape[0]
    n_mem = kv.shape[0]
    scale = MEM_HEAD_DIM ** -0.5

    def body(q_ref, kv_ref, do_ref, dq_ref, dkv_ref):
        @pl.when(pl.program_id(0) == 0)
        def _():
            dkv_ref[...] = jnp.zeros_like(dkv_ref)

        q = q_ref[...]
        kv = kv_ref[...]
        do = do_ref[...]
        for h in range(MEM_HEADS):
            lo = h * MEM_HEAD_DIM
            qh = q[:, lo:lo + MEM_HEAD_DIM]
            kh = kv[:, lo:lo + MEM_HEAD_DIM]
            vh = kv[:, MEM_INNER + lo:MEM_INNER + lo + MEM_HEAD_DIM]
            doh = do[:, lo:lo + MEM_HEAD_DIM]
            pt = _mem_scores_t(q, kv, h)
            dkv_ref[:, MEM_INNER + lo:MEM_INNER + lo + MEM_HEAD_DIM] += jnp.dot(
                pt.astype(BF16), doh, preferred_element_type=F32)
            dpt = lax.dot_general(vh, doh, NT_DIMS, preferred_element_type=F32)
            dst = (pt * (dpt - jnp.sum(pt * dpt, axis=0, keepdims=True)) * scale).astype(BF16)
            dkv_ref[:, lo:lo + MEM_HEAD_DIM] += jnp.dot(dst, qh, preferred_element_type=F32)
            dq_ref[:, lo:lo + MEM_HEAD_DIM] = lax.dot_general(dst, kh, TN_DIMS, preferred_element_type=F32).astype(BF16)

    return pl.pallas_call(
        body, name=name, grid=(s_len // tile,),
        in_specs=[pl.BlockSpec((tile, MEM_INNER), lambda i: (i, 0)), _full((n_mem, 2 * MEM_INNER)),
                  pl.BlockSpec((tile, MEM_INNER), lambda i: (i, 0))],
        out_specs=[pl.BlockSpec((tile, MEM_INNER), lambda i: (i, 0)), _full((n_mem, 2 * MEM_INNER))],
        out_shape=[jax.ShapeDtypeStruct((s_len, MEM_INNER), BF16), jax.ShapeDtypeStruct((n_mem, 2 * MEM_INNER), F32)],
        compiler_params=_params("arbitrary"),
    )(q, kv, do)


def _loss_head(y, target, tile, name):
    s_len, d = y.shape

    def body(y_ref, t_ref, dy_ref, l_ref):
        @pl.when(pl.program_id(0) == 0)
        def _():
            l_ref[...] = jnp.zeros_like(l_ref)

        err = y_ref[...] - t_ref[...]
        dy_ref[...] = err * (1.0 / d)
        l_ref[...] += jnp.sum(err * err, axis=0, keepdims=True) * (0.5 / d)

    row = lambda i: (i, 0)
    return pl.pallas_call(
        body, name=name, grid=(s_len // tile,),
        in_specs=[pl.BlockSpec((tile, d), row), pl.BlockSpec((tile, d), row)],
        out_specs=[pl.BlockSpec((tile, d), row), _full((1, d))],
        out_shape=[jax.ShapeDtypeStruct((s_len, d), F32), jax.ShapeDtypeStruct((1, d), F32)],
        compiler_params=_params("arbitrary"),
    )(y, target)


def _attn_tile(s_len):
    return min(256, s_len // 2)


REST = ("w_out", "w_mq", "w_mk", "w_mv", "w_mo", "w_up", "w_down")
REST_DQ = ("w_up", "w_down")
REST_DKV = ("w_out", "w_mq", "w_mk", "w_mv", "w_mo")
SHARD_AXIS = {"w_in": 1, "w_out": 0, "w_mq": 0, "w_mk": 0, "w_mv": 0, "w_mo": 1, "w_up": 1, "w_down": 0}


def _full_from_shards(sh, axis):
    n, r, c = sh.shape
    if axis == 0:
        return sh.reshape(n * r, c)
    return sh.transpose(1, 0, 2).reshape(r, n * c)


def _rest_weights(lands):
    w = {n: _full_from_shards(sh, SHARD_AXIS[n]) for n, sh in zip(REST, lands)}
    w["w_mkv"] = jnp.concatenate([w.pop("w_mk"), w.pop("w_mv")], axis=1)
    return w


def _w_in_cat(land):
    return jnp.pad(land.reshape(-1, land.shape[2]), ((0, IN_CAT - IN_COLS), (0, 0)))


def _layer_fwd(x0, mem, w, l, rest_src=None, next_src=None):
    s_len = x0.shape[0]
    tile = min(512, s_len)
    tile_ff = min(512, s_len)
    ta = _attn_tile(s_len)
    ident = lambda z: z
    sv = {"x0": x0}

    h1, ag, qkv, fl = _rms_matmul(
        x0, w["norm_mix_pre"], w["w_in_cat"],
        [(0, 2 * CONV_CH, [(F32, ident)]), (2 * CONV_CH, IN_MAIN, [(BF16, ident)]), (IN_MAIN, IN_CAT, [(F32, ident)])],
        tile, f"mix_in_{l}", w_t=True)
    u3, u1 = _conv_fwd(ag, w["conv_w"], w["conv_b"], w["conv_ln_g"], w["conv_ln_b"], tile, f"conv_fwd_{l}")
    qa, ka, va = _foxa_prep(qkv, fl, w["b_forget"], tile, f"fox_prep_{l}")
    next_land = None
    if rest_src is None:
        o, lse, o_bf = _foxa_fwd(qa, ka, va, ta, 2 * ta, f"fox_fwd_{l}")
    else:
        srcs = list(rest_src) + list(next_src or [])
        (o, lse, o_bf), lands = _foxa_fwd(qa, ka, va, ta, 2 * ta, f"fox_fwd_{l}", host=(srcs, [True] * len(srcs)))
        w = {**w, **_rest_weights(lands[:len(REST)])}
        next_land = lands[len(REST):]
    cat = jnp.concatenate([u3, o_bf], axis=1)
    y1, x1 = _matmul_resnorm(cat, w["w_out"], x0, w["norm_mix_post"], tile, f"mix_out_{l}")
    sv.update(h1=h1, ag=ag, u1=u1, fl=fl, qa=qa, ka=ka, va=va, o=o, lse=lse, cat=cat, y1=y1, x1=x1)

    h2, qm = _rms_matmul(x1, w["norm_mem_pre"], w["w_mq"], [(0, MEM_INNER, [(BF16, ident)])], tile, f"mem_q_{l}")
    mem_n, kv = _rms_matmul(mem, w["norm_memkv"], w["w_mkv"], [(0, 2 * MEM_INNER, [(BF16, ident)])],
                            mem.shape[0], f"mem_kv_{l}")
    om = _memattn_fwd(qm, kv, tile, f"mem_attn_fwd_{l}")
    y2, x2 = _matmul_resnorm(om, w["w_mo"], x1, w["norm_mem_post"], tile, f"mem_out_{l}")
    sv.update(h2=h2, qm=qm, mem_n=mem_n, kv=kv, om=om, y2=y2, x2=x2)

    relu2 = lambda z: jnp.square(jnp.maximum(z, 0.0))
    h3, pre, hid = _rms_matmul(x2, w["norm_mlp_pre"], w["w_up"], [(0, D_FF, [(BF16, ident), (BF16, relu2)])], tile_ff,
                               f"mlp_up_{l}")
    y3, x3 = _matmul_resnorm(hid, w["w_down"], x2, w["norm_mlp_post"], tile_ff, f"mlp_down_{l}")
    sv.update(h3=h3, pre=pre, hid=hid, y3=y3)
    return x3, sv, w, next_land


def _layer_bwd(dx3, mem, w, sv, l, scatter_rest=False, dkv_src=None, scatter_w_in=False):
    s_len = dx3.shape[0]
    tile = min(512, s_len)
    tile_ff = min(512, s_len)
    ta = _attn_tile(s_len)
    tk = min(512, s_len)
    n_mem = mem.shape[0]
    g = {}

    dy3, dpre, g["norm_mlp_post"] = _resnorm_bwd_mm(dx3, sv["y3"], w["norm_mlp_post"], w["w_down"], tile_ff,
                                                    f"mlp_down_bwd_{l}", BF16, pre=sv["pre"])
    g["w_down"] = _matmul_tn_shards(sv["hid"], dy3, 0, tk, f"dw_down_{l}")
    dx2, g["norm_mlp_pre"] = _mm_prenorm_bwd(dpre, w["w_up"], sv["x2"], w["norm_mlp_pre"], dx3, tile_ff,
                                             f"mlp_up_bwd_{l}")
    g["w_up"] = _matmul_tn_shards(sv["h3"], dpre, 1, tk, f"dw_up_{l}")

    dy2, dom, g["norm_mem_post"] = _resnorm_bwd_mm(dx2, sv["y2"], w["norm_mem_post"], w["w_mo"], tile,
                                                   f"mem_out_bwd_{l}", BF16)
    g["w_mo"] = _matmul_tn_shards(sv["om"], dy2, 1, tk, f"dw_mo_{l}")
    dqm, dkv = _memattn_bwd(sv["qm"], sv["kv"], dom, tile, f"mem_attn_bwd_{l}")
    dkvb = dkv.astype(BF16)
    g["w_mq"] = _matmul_tn_shards(sv["h2"], dqm, 0, tk, f"dw_mq_{l}")
    dx1, g["norm_mem_pre"] = _mm_prenorm_bwd(dqm, w["w_mq"], sv["x1"], w["norm_mem_pre"], dx2, tile, f"mem_q_bwd_{l}")
    _, g["norm_memkv"] = _mm_prenorm_bwd(dkvb, w["w_mkv"], mem, w["norm_memkv"], None, n_mem, f"mem_kv_bwd_{l}")
    g["w_mk"] = _matmul_tn_shards(sv["mem_n"], dkvb[:, :MEM_INNER], 0, n_mem, f"dw_mk_{l}")
    g["w_mv"] = _matmul_tn_shards(sv["mem_n"], dkvb[:, MEM_INNER:], 0, n_mem, f"dw_mv_{l}")

    dy1, dcat, g["norm_mix_post"] = _resnorm_bwd_mm(dx1, sv["y1"], w["norm_mix_post"], w["w_out"], tile,
                                                    f"mix_out_bwd_{l}", F32)
    g["w_out"] = _matmul_tn_shards(sv["cat"], dy1, 0, tk, f"dw_out_{l}")
    dq_args = (sv["qa"], sv["lse"], sv["o"], dcat, sv["ka"], sv["va"], ta, 2 * ta, f"fox_dq_{l}")
    rest_land, dkv_land = None, None
    if scatter_rest:
        with_dq = [g[n] for n in REST_DQ]
        (dq, rs, qb, doa), land_dq = _foxa_dq(*dq_args, host=(with_dq, [False] * len(with_dq)))
        with_dkv = [g[n] for n in REST_DKV] + list(dkv_src or [])
        (dk, dv, cs), land_dkv = _foxa_dkv(qb, sv["ka"], sv["va"], doa, 2 * ta, ta, f"fox_dkv_{l}",
                                           host=(with_dkv, [False] * len(with_dkv)))
        by_name = dict(zip(REST_DQ + REST_DKV, land_dq + land_dkv))
        rest_land = [by_name[n] for n in REST]
        dkv_land = land_dkv[len(REST_DKV):]
    else:
        dq, rs, qb, doa = _foxa_dq(*dq_args)
        dk, dv, cs = _foxa_dkv(qb, sv["ka"], sv["va"], doa, 2 * ta, ta, f"fox_dkv_{l}")
    dfl, db = _cumsum_bwd(rs, cs, sv["fl"], w["b_forget"], tile, f"cumsum_bwd_{l}")
    g["b_forget"] = db[:, :FOX_HEADS]
    dag, dconv_w, dconv_v = _conv_bwd(sv["ag"], sv["u1"], dcat, w["conv_w"], w["conv_ln_g"], w["conv_ln_b"], tile,
                                      f"conv_bwd_{l}")
    g["conv_w"] = dconv_w[:CONV_WIDTH]
    g["conv_b"], g["conv_ln_g"], g["conv_ln_b"] = dconv_v[0:1], dconv_v[1:2], dconv_v[2:3]
    dz = [dag, dq, dk, dv, dfl]
    dw_in_t = _matmul_tn_rows(dz, sv["h1"], tk, f"dw_in_{l}")[:IN_COLS]
    g["w_in"] = dw_in_t.reshape(N_DEV, IN_COLS // N_DEV, D_MODEL).astype(BF16)
    res = _mm_prenorm_bwd(dz, w["w_in_cat"], sv["x0"], w["norm_mix_pre"], dx1, tile, f"mix_in_bwd_{l}",
                          host=([g["w_in"]], [False]) if scatter_w_in else None, w_t=True)
    (dx0, g["norm_mix_pre"]), w_in_land = res if scatter_w_in else (res, None)
    return dx0, g, rest_land, dkv_land, w_in_land


def _sum_blocks(a, name):
    n, rows, cols = a.shape

    def body(a_ref, o_ref):
        acc = a_ref[0]
        for j in range(1, n):
            acc = acc + a_ref[j]
        o_ref[...] = acc

    return pl.pallas_call(
        body, name=name, in_specs=[_full((n, rows, cols))], out_specs=_full((rows, cols)),
        out_shape=jax.ShapeDtypeStruct((rows, cols), F32), grid=(1,),
    )(a)


def _adamw(gparts, w, m, v, tile, name):
    n_l, rows, cols = w.shape
    n = gparts[0].shape[0]
    n_t = rows // tile
    c1 = 1.0 - ADAM_B1
    c2 = 1.0 - ADAM_B2
    bc1 = 1.0 - ADAM_B1 ** ADAM_STEP
    bc2 = 1.0 - ADAM_B2 ** ADAM_STEP

    def body(*refs):
        gp_refs, (w_ref, m_ref, v_ref, g_ref, d_ref, mo_ref, vo_ref) = refs[:n_l], refs[n_l:]
        layer = pl.program_id(0)
        g = None
        for l, gp_ref in enumerate(gp_refs):
            gl = gp_ref[0].astype(F32)
            for j in range(1, n):
                gl = gl + gp_ref[j].astype(F32)
            g = gl if g is None else jnp.where(layer == l, gl, g)
        g_ref[...] = g
        m_new = ADAM_B1 * m_ref[...] + c1 * g
        v_new = ADAM_B2 * v_ref[...] + c2 * (g * g)
        mo_ref[...] = m_new
        vo_ref[...] = v_new
        d_ref[...] = -ADAM_LR * ((m_new / bc1) / (jnp.sqrt(v_new / bc2) + ADAM_EPS) + ADAM_WD * w_ref[...])

    def gp_spec(l):
        return pl.BlockSpec((n, tile, cols), lambda L, i: (0, jnp.where(L < l, 0, jnp.where(L > l, n_t - 1, i)), 0))

    spec = pl.BlockSpec((None, tile, cols), lambda L, i: (L, i, 0))
    return pl.pallas_call(
        body, name=name, grid=(n_l, n_t),
        in_specs=[gp_spec(l) for l in range(n_l)] + [spec, spec, spec],
        out_specs=[spec] * 4, out_shape=[jax.ShapeDtypeStruct((n_l, rows, cols), F32)] * 4,
        compiler_params=_params("arbitrary", "arbitrary"),
    )(*gparts, w, m, v)


def _adamw_cols(gparts, w, m, v, name):
    n, rows, cols = gparts[0].shape
    n_l = len(gparts)
    c1 = 1.0 - ADAM_B1
    c2 = 1.0 - ADAM_B2
    bc1 = 1.0 - ADAM_B1 ** ADAM_STEP
    bc2 = 1.0 - ADAM_B2 ** ADAM_STEP

    def body(*refs):
        gp_refs, (w_ref, m_ref, v_ref, g_ref, d_ref, mo_ref, vo_ref) = refs[:n_l], refs[n_l:]
        layer = pl.program_id(0)
        g = None
        for l, gp_ref in enumerate(gp_refs):
            gl = gp_ref[0].astype(F32)
            for j in range(1, n):
                gl = gl + gp_ref[j].astype(F32)
            g = gl if g is None else jnp.where(layer == l, gl, g)
        g_ref[...] = g
        m_new = ADAM_B1 * m_ref[...] + c1 * g
        v_new = ADAM_B2 * v_ref[...] + c2 * (g * g)
        mo_ref[...] = m_new
        vo_ref[...] = v_new
        d_ref[...] = -ADAM_LR * ((m_new / bc1) / (jnp.sqrt(v_new / bc2) + ADAM_EPS) + ADAM_WD * w_ref[...])

    spec = pl.BlockSpec((rows, cols), lambda L: (0, L))
    return pl.pallas_call(
        body, name=name, grid=(n_l,),
        in_specs=[_full((n, rows, cols))] * n_l + [spec, spec, spec],
        out_specs=[spec] * 4, out_shape=[jax.ShapeDtypeStruct((rows, n_l * cols), F32)] * 4,
        compiler_params=_params("arbitrary"),
    )(*gparts, w, m, v)


def _pack_rows(parts, total_rows):
    flat = [p.reshape(-1, D_MODEL) for p in parts]
    used = sum(f.shape[0] for f in flat)
    if total_rows > used:
        flat.append(jnp.zeros((total_rows - used, D_MODEL), flat[0].dtype))
    return jnp.concatenate(flat, axis=0)


def kernel(x, mem, norm_mix_pre, norm_mix_post, w_in, b_forget, conv_w, conv_b, conv_ln_g, conv_ln_b, w_out, norm_mem_pre, norm_mem_post, norm_memkv, w_mq, w_mk, w_mv, w_mo, norm_mlp_pre, norm_mlp_post, w_up, w_down, loss_target, m_norm_mix_pre, m_norm_mix_post, m_w_in, m_b_forget, m_conv_w, m_conv_b, m_conv_ln_g, m_conv_ln_b, m_w_out, m_norm_mem_pre, m_norm_mem_post, m_norm_memkv, m_w_mq, m_w_mk, m_w_mv, m_w_mo, m_norm_mlp_pre, m_norm_mlp_post, m_w_up, m_w_down, v_norm_mix_pre, v_norm_mix_post, v_w_in, v_b_forget, v_conv_w, v_conv_b, v_conv_ln_g, v_conv_ln_b, v_w_out, v_norm_mem_pre, v_norm_mem_post, v_norm_memkv, v_w_mq, v_w_mk, v_w_mv, v_w_mo, v_norm_mlp_pre, v_norm_mlp_post, v_w_up, v_w_down):
    p = dict(locals())
    names = ("norm_mix_pre", "norm_mix_post", "w_in", "b_forget", "conv_w", "conv_b", "conv_ln_g", "conv_ln_b", "w_out",
             "norm_mem_pre", "norm_mem_post", "norm_memkv", "w_mq", "w_mk", "w_mv", "w_mo", "norm_mlp_pre",
             "norm_mlp_post", "w_up", "w_down")
    me = 4 * lax.axis_index("x") + 2 * lax.axis_index("y") + lax.axis_index("c")
    conv_cols = conv_w.shape[2]
    s_len = x.shape[1]
    bf = {n: p[n].astype(BF16) for n in REST}
    in_cols = w_in.shape[2]
    w_in_t = {pre: jnp.transpose(p[pre + "w_in"], (2, 0, 1)).reshape(in_cols, DEPTH * D_MODEL) for pre in ("", "m_", "v_")}
    bf_in = w_in_t[""].astype(BF16)
    bf["w_in"] = [bf_in[:, l * D_MODEL:(l + 1) * D_MODEL] for l in range(DEPTH)]

    conv_pack = _pack_rows([jnp.pad(conv_w, ((0, 0), (0, CONV_PAD - CONV_WIDTH), (0, 0)))], 8)
    win_land, conv_land = _exchange([bf["w_in"][0], conv_pack], [True, True], "gather_first")
    conv_rows = DEPTH * CONV_PAD * conv_cols // D_MODEL
    conv_full = conv_land[:, :conv_rows].reshape(N_DEV, DEPTH, CONV_PAD, conv_cols)
    conv_full = conv_full.transpose(1, 2, 0, 3).reshape(DEPTH, CONV_PAD, N_DEV * conv_cols)
    b_forget_pad = jnp.pad(b_forget, ((0, 0), (0, LANES - FOX_HEADS)))

    def first_weights(l, land):
        w = {"w_in_cat": _w_in_cat(land), "conv_w": conv_full[l], "b_forget": b_forget_pad[l:l + 1]}
        for n in VEC[:-1]:
            w[n] = p[n][l:l + 1]
        return w

    h, sv0, w0, win1_land = _layer_fwd(x[0], mem[0], first_weights(0, win_land), 0,
                                       rest_src=[bf[n][0] for n in REST], next_src=[bf["w_in"][1]])
    h, sv1, w1, _ = _layer_fwd(h, mem[0], first_weights(1, win1_land[0]), 1, rest_src=[bf[n][1] for n in REST])
    dh, loss_row = _loss_head(h, loss_target[0], min(512, s_len), "loss_head")
    dh, g1, rest_g1, _, _ = _layer_bwd(dh, mem[0], w1, sv1, 1, scatter_rest=True)
    grad_x, g0, rest_g0, win1_g, win0_g = _layer_bwd(dh, mem[0], w0, sv0, 0, scatter_rest=True, dkv_src=[g1["w_in"]],
                                                     scatter_w_in=True)

    def small_rows(get):
        rows = [jnp.concatenate([get(n) for n in VEC_1024], axis=0),
                jnp.concatenate([get(n) for n in VEC_512], axis=0).reshape(len(VEC_512), D_MODEL),
                jnp.pad(get("b_forget").reshape(1, -1), ((0, 0), (0, D_MODEL - DEPTH * FOX_HEADS)))]
        return jnp.concatenate(rows, axis=0)

    def tap_rows(conv):
        return jnp.pad(conv.reshape(1, -1), ((0, 0), (0, 4 * D_MODEL - conv.size))).reshape(4, D_MODEL)

    n_vec_rows = DEPTH * len(VEC_1024) + len(VEC_512) + 1
    part = small_rows(lambda n: jnp.concatenate([g0[n], g1[n]], axis=0))
    conv_part = jnp.stack([g0["conv_w"], g1["conv_w"]]).reshape(CONV_WIDTH, D_MODEL)
    n_part = 1 + n_vec_rows + CONV_WIDTH
    pad_rows = -n_part % 8
    small_land = _exchange([jnp.concatenate([loss_row, part, conv_part, jnp.zeros((pad_rows, D_MODEL), F32)], axis=0)],
                           [True], "gather_small")[0]
    total = _sum_blocks(small_land, "sum_small")
    loss = jnp.sum(total[0])
    conv_g = total[1 + n_vec_rows:n_part].reshape(DEPTH, CONV_WIDTH, CONV_CH)
    conv_g = lax.dynamic_slice_in_dim(conv_g, me * conv_cols, conv_cols, axis=2)
    fill = jnp.zeros((SMALL_ROWS - n_vec_rows - 4, D_MODEL), F32)

    def small_pack(vec_rows, conv):
        return jnp.concatenate([vec_rows, tap_rows(conv), fill], axis=0)

    small_out = _adamw([small_pack(total[1:1 + n_vec_rows], conv_g)[None]],
                       *[small_pack(small_rows(lambda n: p[pre + n]), p[pre + "conv_w"])[None] for pre in ("", "m_", "v_")],
                       SMALL_ROWS, "adamw_small")

    def unpack_small(buf):
        out = {}
        for k, n in enumerate(VEC_1024):
            out[n] = buf[DEPTH * k:DEPTH * (k + 1)]
        at = DEPTH * len(VEC_1024)
        for k, n in enumerate(VEC_512):
            out[n] = buf[at + k].reshape(DEPTH, CONV_CH)
        at += len(VEC_512)
        out["b_forget"] = buf[at, :DEPTH * FOX_HEADS].reshape(DEPTH, FOX_HEADS)
        out["conv_w"] = buf[at + 1:at + 5].reshape(-1)[:DEPTH * CONV_WIDTH * conv_cols].reshape(DEPTH, CONV_WIDTH, conv_cols)
        return out

    big_out = {n: _adamw([rest_g0[i], rest_g1[i]], p[n], p["m_" + n], p["v_" + n], ADAMW_TILE[n], f"adamw_{n}")
               for i, n in enumerate(REST)}
    in_out = _adamw_cols([win0_g[0], win1_g[0]], w_in_t[""], w_in_t["m_"], w_in_t["v_"], "adamw_w_in")
    big_out["w_in"] = [a.reshape(in_cols, DEPTH, D_MODEL).transpose(1, 2, 0) for a in in_out]

    result = [loss, grad_x[None]]
    for k in range(4):
        smalls = unpack_small(small_out[k][0])
        result += [big_out[n][k] if n in big_out else smalls[n] for n in names]
    return tuple(result)
```

```python
import functools

import jax
import jax.numpy as jnp
from jax import lax
from jax.experimental import pallas as pl
from jax.experimental.pallas import tpu as pltpu

F32 = jnp.float32
BF16 = jnp.bfloat16

N_DEV = 8
DEPTH = 2
D_MODEL = 1024
CONV_CH = 512
CONV_WIDTH = 31
CONV_PAD = 32
FOX_HEADS = 8
FOX_HEAD_DIM = 64
FOX_WIDTH = 512
N_PAIRS = 4
MEM_HEADS = 4
MEM_HEAD_DIM = 128
MEM_INNER = 512
D_FF = 4096
IN_MAIN = 2560
IN_COLS = 2568
IN_CAT = IN_MAIN + 128
LANES = 128
EPS = 1e-6
NEG_INF = -1e30

ADAM_LR = 0.001
ADAM_B1 = 0.9
ADAM_B2 = 0.999
ADAM_EPS = 1e-08
ADAM_WD = 0.01
ADAM_STEP = 10

NT_DIMS = (((1,), (1,)), ((), ()))
TN_DIMS = (((0,), (0,)), ((), ()))

BIG = ("w_in", "w_out", "w_mq", "w_mk", "w_mv", "w_mo", "w_up", "w_down")
ADAMW_TILE = {"w_out": 128, "w_mq": 128, "w_mk": 128, "w_mv": 128, "w_mo": 512, "w_up": 256, "w_down": 128}

VEC_1024 = ("norm_mix_pre", "norm_mix_post", "norm_mem_pre", "norm_mem_post", "norm_memkv", "norm_mlp_pre", "norm_mlp_post")
VEC_512 = ("conv_b", "conv_ln_g", "conv_ln_b")
VEC = VEC_1024 + VEC_512 + ("b_forget",)
SMALL_ROWS = 32


def _sigmoid(x):
    return 1.0 / (1.0 + jnp.exp(-x))


def _rms(x, g):
    r = lax.rsqrt(jnp.mean(x * x, axis=-1, keepdims=True) + EPS)
    return x * r * g


def _rms_bwd(x, g, dh):
    r = lax.rsqrt(jnp.mean(x * x, axis=-1, keepdims=True) + EPS)
    gh = dh * g
    c = jnp.mean(gh * x, axis=-1, keepdims=True)
    dx = r * gh - x * (r * r * r * c)
    dg = jnp.sum(dh * (x * r), axis=0, keepdims=True)
    return dx, dg


def _full(shape):
    nd = len(shape)
    return pl.BlockSpec(shape, lambda *_: (0,) * nd)


def _params(*sem):
    return pltpu.CompilerParams(dimension_semantics=sem)


def _exchange_copies(src_refs, out_refs, same, send_sems, recv_sems, local_sems, with_recvs):
    x, y, c = lax.axis_index("x"), lax.axis_index("y"), lax.axis_index("c")
    me = 4 * x + 2 * y + c
    local, sends, recvs = [], [], []
    for a, (s_ref, o_ref) in enumerate(zip(src_refs, out_refs)):
        def mine(idx, s_ref=s_ref, whole=same[a]):
            return s_ref if whole else s_ref.at[idx]

        local.append(pltpu.make_async_copy(mine(me), o_ref.at[me], local_sems.at[a]))
        for k in range(1, N_DEV):
            px = 1 - x if k & 4 else x
            py = 1 - y if k & 2 else y
            pc = 1 - c if k & 1 else c
            peer = 4 * px + 2 * py + pc
            sem = a * (N_DEV - 1) + k - 1
            common = dict(send_sem=send_sems.at[sem], recv_sem=recv_sems.at[sem], device_id=(px, py, pc),
                          device_id_type=pl.DeviceIdType.MESH)
            sends.append(pltpu.make_async_remote_copy(src_ref=mine(peer), dst_ref=o_ref.at[me], **common))
            if with_recvs:
                recvs.append(pltpu.make_async_remote_copy(src_ref=mine(peer), dst_ref=o_ref.at[peer], **common))
    return local, sends, recvs


def _gather_copies(src_refs, out_refs, send_sems, recv_sems, local_sems, phase):
    x, y, c = lax.axis_index("x"), lax.axis_index("y"), lax.axis_index("c")
    sibling = (x, y, 1 - c)
    chips = [(1 - x, y), (x, 1 - y), (1 - x, 1 - y)]

    def idx(px, py, pc):
        return 4 * px + 2 * py + pc

    local, first, arrive, passed, final = [], [], [], [], []
    for a, (s_ref, o_ref) in enumerate(zip(src_refs, out_refs)):
        def cp(k, src, block, to, a=a, o_ref=o_ref):
            sem = a * (N_DEV - 1) + k
            return pltpu.make_async_remote_copy(src_ref=src, dst_ref=o_ref.at[block], send_sem=send_sems.at[sem],
                                                recv_sem=recv_sems.at[sem], device_id=to, device_id_type=pl.DeviceIdType.MESH)

        me = idx(x, y, c)
        if phase != 1:
            local.append(pltpu.make_async_copy(s_ref, o_ref.at[me], local_sems.at[a]))
            first.append(cp(0, s_ref, me, sibling))
        if phase == 2:
            final.append(cp(0, s_ref, idx(x, y, 1 - c), sibling))
        for j, chip in enumerate(chips):
            theirs = idx(*chip, c)
            if phase != 1:
                first.append(cp(1 + j, s_ref, me, (*chip, c)))
            if phase == 1:
                arrive.append(cp(1 + j, s_ref, theirs, (*chip, c)))
            if phase != 0:
                passed.append(cp(4 + j, o_ref.at[theirs], theirs, sibling))
            if phase == 2:
                final.append(cp(4 + j, s_ref, idx(*chip, 1 - c), sibling))
    return local, first, arrive, passed, final


def _pcall(body, *, name, grid, in_specs, out_specs, out_shape, args, scratch_shapes=(), sem=(), host=None):
    if host is None:
        return pl.pallas_call(body, name=name, grid=grid, in_specs=in_specs, out_specs=out_specs, out_shape=out_shape,
                              scratch_shapes=list(scratch_shapes), compiler_params=_params(*sem))(*args)
    srcs, same = host
    n_in, n_out, n_scr, n_h = len(in_specs), len(out_specs), len(scratch_shapes), len(srcs)
    hbm = pl.BlockSpec(memory_space=pltpu.HBM)
    lands = [jax.ShapeDtypeStruct((N_DEV,) + (s.shape if whole else s.shape[1:]), s.dtype) for s, whole in zip(srcs, same)]

    def wrapped(*refs):
        ins, src_refs = refs[:n_in], refs[n_in:n_in + n_h]
        outs = refs[n_in + n_h:n_in + n_h + n_out]
        land_refs = refs[n_in + n_h + n_out:n_in + 2 * n_h + n_out]
        scr = refs[n_in + 2 * n_h + n_out:n_in + 2 * n_h + n_out + n_scr]
        sems = refs[n_in + 2 * n_h + n_out + n_scr:]
        ids = [pl.program_id(d) for d in range(len(grid))]
        first = functools.reduce(jnp.logical_and, [i == 0 for i in ids])
        last = functools.reduce(jnp.logical_and, [i == n - 1 for i, n in zip(ids, grid)])
        later = functools.reduce(jnp.logical_and, [ids[0] == (3 * grid[0]) // 4] + [i == 0 for i in ids[1:]])

        if all(same):
            @pl.when(first)
            def _():
                local, sends, _, _, _ = _gather_copies(src_refs, land_refs, *sems, 0)
                for cp in local + sends:
                    cp.start()

            body(*ins, *outs, *scr)

            @pl.when(later)
            def _():
                _, _, arrive, passed, _ = _gather_copies(src_refs, land_refs, *sems, 1)
                for cp in arrive:
                    cp.wait_recv()
                for cp in passed:
                    cp.start()

            @pl.when(last)
            def _():
                local, sends, _, passed, final = _gather_copies(src_refs, land_refs, *sems, 2)
                for cp in final:
                    cp.wait_recv()
                for cp in sends + passed:
                    cp.wait_send()
                for cp in local:
                    cp.wait()
        else:
            @pl.when(first)
            def _():
                local, sends, _ = _exchange_copies(src_refs, land_refs, same, *sems, False)
                for cp in local + sends:
                    cp.start()

            body(*ins, *outs, *scr)

            @pl.when(last)
            def _():
                local, sends, recvs = _exchange_copies(src_refs, land_refs, same, *sems, True)
                for cp in recvs:
                    cp.wait_recv()
                for cp in sends:
                    cp.wait_send()
                for cp in local:
                    cp.wait()

    n_sem = n_h * (N_DEV - 1)
    res = pl.pallas_call(
        wrapped, name=name, grid=grid, in_specs=list(in_specs) + [hbm] * n_h, out_specs=list(out_specs) + [hbm] * n_h,
        out_shape=list(out_shape) + lands,
        scratch_shapes=list(scratch_shapes) + [pltpu.SemaphoreType.DMA((n_sem,)), pltpu.SemaphoreType.DMA((n_sem,)),
                                               pltpu.SemaphoreType.DMA((n_h,))],
        compiler_params=_params(*(("arbitrary",) * len(grid))),
    )(*args, *srcs)
    return list(res[:n_out]), list(res[n_out:])


def _exchange(srcs, same, name):
    def body():
        pass

    return _pcall(body, name=name, grid=(1,), in_specs=[], out_specs=[], out_shape=[], args=[], host=(srcs, same))[1]


def _rms_matmul(x, g, w, segs, tile, name, host=None, w_t=False):
    s_len, d = x.shape
    chunk = 512

    def body(x_ref, g_ref, w_ref, h_ref, *outs):
        h = _rms(x_ref[...], g_ref[...]).astype(BF16)
        h_ref[...] = h
        oi = 0
        for c0, c1, fns in segs:
            for a in range(c0, c1, chunk):
                b = min(a + chunk, c1)
                if w_t:
                    z = lax.dot_general(h, w_ref[a:b, :], NT_DIMS, preferred_element_type=F32)
                else:
                    z = jnp.dot(h, w_ref[:, a:b], preferred_element_type=F32)
                for k, (dt, fn) in enumerate(fns):
                    outs[oi + k][:, a - c0:b - c0] = fn(z).astype(dt)
            oi += len(fns)

    out_shape = [jax.ShapeDtypeStruct((s_len, d), BF16)]
    out_specs = [pl.BlockSpec((tile, d), lambda i: (i, 0))]
    for c0, c1, fns in segs:
        for dt, _ in fns:
            out_shape.append(jax.ShapeDtypeStruct((s_len, c1 - c0), dt))
            out_specs.append(pl.BlockSpec((tile, c1 - c0), lambda i: (i, 0)))
    return _pcall(
        body, name=name, grid=(s_len // tile,),
        in_specs=[pl.BlockSpec((tile, d), lambda i: (i, 0)), _full((1, d)), _full(w.shape)],
        out_specs=out_specs, out_shape=out_shape, args=[x, g, w], sem=("parallel",), host=host)


def _matmul_resnorm(a, w, x, g, tile, name):
    s_len, k = a.shape
    d = w.shape[1]

    def body(a_ref, w_ref, x_ref, g_ref, y_ref, xo_ref):
        y = jnp.dot(a_ref[...], w_ref[...], preferred_element_type=F32)
        y_ref[...] = y
        xo_ref[...] = x_ref[...] + _rms(y, g_ref[...])

    row = lambda i: (i, 0)
    return pl.pallas_call(
        body, name=name, grid=(s_len // tile,),
        in_specs=[pl.BlockSpec((tile, k), row), _full((k, d)), pl.BlockSpec((tile, d), row), _full((1, d))],
        out_specs=[pl.BlockSpec((tile, d), row), pl.BlockSpec((tile, d), row)],
        out_shape=[jax.ShapeDtypeStruct((s_len, d), F32), jax.ShapeDtypeStruct((s_len, d), F32)],
        compiler_params=_params("parallel"),
    )(a, w, x, g)


def _resnorm_bwd_mm(dx, y, g, w, tile, name, out_dtype, pre=None):
    s_len, d = dx.shape
    k = w.shape[0]
    chunk = 512

    def body(*refs):
        if pre is None:
            dx_ref, y_ref, g_ref, w_ref, dy_ref, da_ref, dg_ref = refs
        else:
            dx_ref, y_ref, g_ref, w_ref, pre_ref, dy_ref, da_ref, dg_ref = refs
        dy, dg = _rms_bwd(y_ref[...], g_ref[...], dx_ref[...])
        dyb = dy.astype(BF16)
        dy_ref[...] = dyb

        @pl.when(pl.program_id(0) == 0)
        def _():
            dg_ref[...] = jnp.zeros_like(dg_ref)

        dg_ref[...] += dg
        for a in range(0, k, chunk):
            b = min(a + chunk, k)
            da = lax.dot_general(dyb, w_ref[a:b, :], NT_DIMS, preferred_element_type=F32)
            if pre is not None:
                da = da * (2.0 * jnp.maximum(pre_ref[:, a:b].astype(F32), 0.0))
            da_ref[:, a:b] = da.astype(out_dtype)

    row = lambda i: (i, 0)
    in_specs = [pl.BlockSpec((tile, d), row), pl.BlockSpec((tile, d), row), _full((1, d)), _full((k, d))]
    args = [dx, y, g, w]
    if pre is not None:
        in_specs.append(pl.BlockSpec((tile, k), row))
        args.append(pre)
    return pl.pallas_call(
        body, name=name, grid=(s_len // tile,), in_specs=in_specs,
        out_specs=[pl.BlockSpec((tile, d), row), pl.BlockSpec((tile, k), row), _full((1, d))],
        out_shape=[jax.ShapeDtypeStruct((s_len, d), BF16), jax.ShapeDtypeStruct((s_len, k), out_dtype),
                   jax.ShapeDtypeStruct((1, d), F32)],
        compiler_params=_params("arbitrary"),
    )(*args)


def _mm_prenorm_bwd(dz, w, x, g, dres, tile, name, host=None, w_t=False):
    pieces = list(dz) if isinstance(dz, (list, tuple)) else [dz]
    n_p = len(pieces)
    widths = [p.shape[1] for p in pieces]
    s_len = pieces[0].shape[0]
    d = w.shape[1] if w_t else w.shape[0]

    def body(*refs):
        dz_refs, rest = refs[:n_p], refs[n_p:]
        if dres is None:
            w_ref, x_ref, g_ref, dx_ref, dg_ref = rest
        else:
            w_ref, x_ref, g_ref, dres_ref, dx_ref, dg_ref = rest
        dh, off = None, 0
        for dz_ref, width in zip(dz_refs, widths):
            if w_t:
                part = jnp.dot(dz_ref[...], w_ref[off:off + width, :], preferred_element_type=F32)
            else:
                part = lax.dot_general(dz_ref[...], w_ref[:, off:off + width], NT_DIMS, preferred_element_type=F32)
            dh = part if dh is None else dh + part
            off += width
        dx, dg = _rms_bwd(x_ref[...], g_ref[...], dh)
        if dres is not None:
            dx = dx + dres_ref[...]
        dx_ref[...] = dx

        @pl.when(pl.program_id(0) == 0)
        def _():
            dg_ref[...] = jnp.zeros_like(dg_ref)

        dg_ref[...] += dg

    row = lambda i: (i, 0)
    in_specs = [pl.BlockSpec((tile, width), row) for width in widths]
    in_specs += [_full(w.shape), pl.BlockSpec((tile, d), row), _full((1, d))]
    args = pieces + [w, x, g]
    if dres is not None:
        in_specs.append(pl.BlockSpec((tile, d), row))
        args.append(dres)
    return _pcall(
        body, name=name, grid=(s_len // tile,), in_specs=in_specs,
        out_specs=[pl.BlockSpec((tile, d), row), _full((1, d))],
        out_shape=[jax.ShapeDtypeStruct((s_len, d), F32), jax.ShapeDtypeStruct((1, d), F32)],
        args=args, sem=("arbitrary",), host=host)


def _matmul_tn_rows(pieces, b, tk, name):
    s_len, n = b.shape
    widths = [p.shape[1] for p in pieces]

    def body(*refs):
        a_refs, b_ref, o_ref = refs[:-2], refs[-2], refs[-1]

        @pl.when(pl.program_id(0) == 0)
        def _():
            o_ref[...] = jnp.zeros_like(o_ref)

        b_tile = b_ref[...]
        off = 0
        for a_ref, width in zip(a_refs, widths):
            o_ref[off:off + width, :] += lax.dot_general(a_ref[...], b_tile, TN_DIMS, preferred_element_type=F32)
            off += width

    return pl.pallas_call(
        body, name=name, grid=(s_len // tk,),
        in_specs=[pl.BlockSpec((tk, width), lambda k: (k, 0)) for width in widths] + [pl.BlockSpec((tk, n), lambda k: (k, 0))],
        out_specs=_full((sum(widths), n)),
        out_shape=jax.ShapeDtypeStruct((sum(widths), n), F32),
        compiler_params=_params("arbitrary"),
    )(*pieces, b)


def _matmul_tn_shards(a, b, axis, tk, name):
    s_len, m = a.shape
    n = b.shape[1]
    r, c = (m // N_DEV, n) if axis == 0 else (m, n // N_DEV)
    n_k = s_len // tk
    tm = max(r, min(m, (1 << 20) // n)) if axis == 0 else min(m, (1 << 20) // n)

    def body(a_ref, b_ref, o_ref, acc):
        k = pl.program_id(1)

        @pl.when(k == 0)
        def _():
            acc[...] = jnp.zeros_like(acc)

        acc[...] += lax.dot_general(a_ref[...], b_ref[...], TN_DIMS, preferred_element_type=F32)

        @pl.when(k == n_k - 1)
        def _():
            if axis == 0:
                o_ref[...] = acc[...].reshape(tm // r, r, c).astype(BF16)
            else:
                for j in range(N_DEV):
                    o_ref[j] = acc[:, j * c:(j + 1) * c].astype(BF16)

    if axis == 0:
        out_spec = pl.BlockSpec((tm // r, r, c), lambda i, k: (i, 0, 0))
    else:
        out_spec = pl.BlockSpec((N_DEV, tm, c), lambda i, k: (0, i, 0))
    return pl.pallas_call(
        body, name=name, grid=(m // tm, n_k),
        in_specs=[pl.BlockSpec((tk, tm), lambda i, k: (k, i)), pl.BlockSpec((tk, n), lambda i, k: (k, 0))],
        out_specs=out_spec, out_shape=jax.ShapeDtypeStruct((N_DEV, r, c), BF16),
        scratch_shapes=[pltpu.VMEM((tm, n), F32)],
        compiler_params=_params("parallel", "arbitrary"),
    )(a, b)


def _cumsum_bwd(rs, cs, fl, b, tile, name):
    s_len = fl.shape[0]
    n_t = s_len // tile

    def body(rs_ref, cs_ref, fl_ref, b_ref, dfl_ref, db_ref, carry):
        @pl.when(pl.program_id(0) == 0)
        def _():
            carry[...] = jnp.zeros_like(carry)
            db_ref[...] = jnp.zeros_like(db_ref)

        r = lax.broadcasted_iota(jnp.int32, (tile, tile), 0)
        c = lax.broadcasted_iota(jnp.int32, (tile, tile), 1)
        tri = (c >= r).astype(F32)
        lane = lax.broadcasted_iota(jnp.int32, (1, LANES), 1)
        dc = jnp.zeros((tile, LANES), F32)
        for p in range(N_PAIRS):
            blk = rs_ref[:, LANES * p:LANES * (p + 1)] - cs_ref[:, LANES * p:LANES * (p + 1)]
            dc = jnp.where(lane == 2 * p, blk, dc)
            dc = jnp.where(lane == 2 * p + 1, pltpu.roll(blk, FOX_HEAD_DIM, axis=1), dc)
        dl = jnp.dot(tri, dc, precision=lax.Precision.HIGHEST, preferred_element_type=F32) + carry[...]
        carry[...] = dl[0:1, :]
        dfl = dl * _sigmoid(-(fl_ref[...] + b_ref[...]))
        dfl_ref[...] = dfl.astype(BF16)
        db_ref[...] += jnp.sum(dfl, axis=0, keepdims=True)

    rev = lambda i: (n_t - 1 - i, 0)
    return pl.pallas_call(
        body, name=name, grid=(n_t,),
        in_specs=[pl.BlockSpec((tile, FOX_WIDTH), rev), pl.BlockSpec((tile, FOX_WIDTH), rev), pl.BlockSpec((tile, LANES), rev),
                  _full((1, LANES))],
        out_specs=[pl.BlockSpec((tile, LANES), rev), _full((1, LANES))],
        out_shape=[jax.ShapeDtypeStruct((s_len, LANES), BF16), jax.ShapeDtypeStruct((1, LANES), F32)],
        scratch_shapes=[pltpu.VMEM((1, LANES), F32)],
        compiler_params=_params("arbitrary"),
    )(rs, cs, fl, b)


SUBLANES = 8
CONV_ROWS = 64


def _phase_copies(src, dst, rows):
    for p in range(SUBLANES):
        dst[p] = src[pl.ds(p, rows), :]


def _phase_rows(extp_ref, off, r0, rows):
    p = off % SUBLANES
    return extp_ref[p, pl.ds(pl.multiple_of(r0 + (off - p), SUBLANES), rows), :]


def _conv_taps(w_ref, extp_ref, base, r0, rows, reverse):
    acc = None
    for k in range(CONV_WIDTH):
        off = base + ((CONV_WIDTH - 1 - k) if reverse else k)
        term = w_ref[k:k + 1, :] * _phase_rows(extp_ref, off, r0, rows)
        acc = term if acc is None else acc + term
    return acc


def _fold_rows(x):
    out = x[0:SUBLANES]
    for i in range(1, x.shape[0] // SUBLANES):
        out = out + x[i * SUBLANES:(i + 1) * SUBLANES]
    return out


def _conv_fwd(ag, w, cb, lg, lb, tile, name):
    s_len = ag.shape[0]
    c = CONV_CH
    rb = tile

    def body(ag_ref, w_ref, cb_ref, lg_ref, lb_ref, u_ref, u1_ref, ext, extp):
        @pl.when(pl.program_id(0) == 0)
        def _():
            ext[0:CONV_PAD, :] = jnp.zeros((CONV_PAD, c), F32)
            ext[tile + CONV_PAD:tile + CONV_PAD + SUBLANES, :] = jnp.zeros((SUBLANES, c), F32)

        ext[CONV_PAD:CONV_PAD + tile, :] = ag_ref[:, 0:c] * _sigmoid(ag_ref[:, c:2 * c])
        _phase_copies(ext, extp, tile + CONV_PAD)

        def block(b, carry):
            r0 = pl.multiple_of(b * rb, rb)
            u1 = _conv_taps(w_ref, extp, CONV_PAD - (CONV_WIDTH - 1), r0, rb, False) + cb_ref[...]
            u1_ref[pl.ds(r0, rb), :] = u1
            mu = jnp.mean(u1, axis=-1, keepdims=True)
            xc = u1 - mu
            y = xc * lax.rsqrt(jnp.mean(xc * xc, axis=-1, keepdims=True) + EPS) * lg_ref[...] + lb_ref[...]
            u_ref[pl.ds(r0, rb), :] = (y * _sigmoid(y)).astype(BF16)
            return carry

        lax.fori_loop(0, tile // rb, block, 0)
        ext[0:CONV_PAD, :] = ext[tile:tile + CONV_PAD, :]

    return pl.pallas_call(
        body, name=name, grid=(s_len // tile,),
        in_specs=[pl.BlockSpec((tile, 2 * c), lambda i: (i, 0)), _full((CONV_PAD, c)), _full((1, c)), _full((1, c)),
                  _full((1, c))],
        out_specs=[pl.BlockSpec((tile, c), lambda i: (i, 0))] * 2,
        out_shape=[jax.ShapeDtypeStruct((s_len, c), BF16), jax.ShapeDtypeStruct((s_len, c), F32)],
        scratch_shapes=[pltpu.VMEM((tile + CONV_PAD + SUBLANES, c), F32), pltpu.VMEM((SUBLANES, tile + CONV_PAD, c), F32)],
        compiler_params=_params("arbitrary"),
    )(ag, w, cb, lg, lb)


def _conv_bwd(ag, u1, dcat, w, lg, lb, tile, name):
    s_len = ag.shape[0]
    c = CONV_CH
    n_t = s_len // tile
    per = tile // CONV_PAD
    rb = min(CONV_ROWS, tile)

    def body(ag_ref, halo_ref, u1_ref, du_ref, w_ref, lg_ref, lb_ref, dag_ref, dw_ref, dv_ref, ext, ext2, extp, dwacc):
        i = pl.program_id(0)
        t = n_t - 1 - i

        @pl.when(i == 0)
        def _():
            ext2[tile:tile + CONV_PAD + SUBLANES, :] = jnp.zeros((CONV_PAD + SUBLANES, c), F32)
            ext[tile + CONV_PAD:tile + CONV_PAD + SUBLANES, :] = jnp.zeros((SUBLANES, c), F32)
            dwacc[...] = jnp.zeros_like(dwacc)
            dv_ref[...] = jnp.zeros_like(dv_ref)

        halo = halo_ref[:, 0:c] * _sigmoid(halo_ref[:, c:2 * c])
        ext[0:CONV_PAD, :] = jnp.where(t > 0, halo, 0.0)
        ext[CONV_PAD:CONV_PAD + tile, :] = ag_ref[:, 0:c] * _sigmoid(ag_ref[:, c:2 * c])
        _phase_copies(ext, extp, tile + CONV_PAD)
        base = CONV_PAD - (CONV_WIDTH - 1)

        def block1(b, carry):
            r0 = pl.multiple_of(b * rb, rb)
            u1 = u1_ref[pl.ds(r0, rb), :]
            mu = jnp.mean(u1, axis=-1, keepdims=True)
            xc = u1 - mu
            rs = lax.rsqrt(jnp.mean(xc * xc, axis=-1, keepdims=True) + EPS)
            xhat = xc * rs
            y = xhat * lg_ref[...] + lb_ref[...]
            sy = _sigmoid(y)
            dy = du_ref[pl.ds(r0, rb), :] * (sy * (1.0 + y * (1.0 - sy)))
            dxh = dy * lg_ref[...]
            du1 = rs * (dxh - jnp.mean(dxh, axis=-1, keepdims=True) - xhat * jnp.mean(dxh * xhat, axis=-1, keepdims=True))
            dv_ref[0:1, :] += jnp.sum(du1, axis=0, keepdims=True)
            dv_ref[1:2, :] += jnp.sum(dy * xhat, axis=0, keepdims=True)
            dv_ref[2:3, :] += jnp.sum(dy, axis=0, keepdims=True)
            for k in range(CONV_WIDTH):
                dwacc[k] += _fold_rows(du1 * _phase_rows(extp, base + k, r0, rb))
            ext2[pl.ds(r0, rb), :] = du1
            return carry

        lax.fori_loop(0, tile // rb, block1, 0)
        _phase_copies(ext2, extp, tile + CONV_PAD)

        def block2(b, carry):
            r0 = pl.multiple_of(b * rb, rb)
            du0 = _conv_taps(w_ref, extp, 0, r0, rb, True)
            a = ag_ref[pl.ds(r0, rb), 0:c]
            sg = _sigmoid(ag_ref[pl.ds(r0, rb), c:2 * c])
            dag_ref[pl.ds(r0, rb), 0:c] = (du0 * sg).astype(BF16)
            dag_ref[pl.ds(r0, rb), c:2 * c] = (du0 * a * sg * (1.0 - sg)).astype(BF16)
            return carry

        lax.fori_loop(0, tile // rb, block2, 0)
        ext2[tile:tile + CONV_PAD, :] = ext2[0:CONV_PAD, :]

        @pl.when(i == n_t - 1)
        def _():
            for k in range(CONV_WIDTH):
                dw_ref[k:k + 1, :] = jnp.sum(dwacc[k], axis=0, keepdims=True)
            dw_ref[CONV_WIDTH:CONV_PAD, :] = jnp.zeros((CONV_PAD - CONV_WIDTH, c), F32)

    rev = lambda i: (n_t - 1 - i, 0)
    return pl.pallas_call(
        body, name=name, grid=(n_t,),
        in_specs=[pl.BlockSpec((tile, 2 * c), rev),
                  pl.BlockSpec((CONV_PAD, 2 * c), lambda i: (jnp.maximum((n_t - 1 - i) * per - 1, 0), 0)),
                  pl.BlockSpec((tile, c), rev), pl.BlockSpec((tile, c), rev), _full((CONV_PAD, c)), _full((1, c)), _full((1, c))],
        out_specs=[pl.BlockSpec((tile, 2 * c), rev), _full((CONV_PAD, c)), _full((8, c))],
        out_shape=[jax.ShapeDtypeStruct((s_len, 2 * c), BF16), jax.ShapeDtypeStruct((CONV_PAD, c), F32),
                   jax.ShapeDtypeStruct((8, c), F32)],
        scratch_shapes=[pltpu.VMEM((tile + CONV_PAD + SUBLANES, c), F32), pltpu.VMEM((tile + CONV_PAD + SUBLANES, c), F32),
                        pltpu.VMEM((SUBLANES, tile + CONV_PAD, c), F32), pltpu.VMEM((CONV_PAD, SUBLANES, c), F32)],
        compiler_params=_params("arbitrary"),
    )(ag, ag, u1, dcat, w, lg, lb)


Q_SCALE = FOX_HEAD_DIM ** -0.5


def _head_col(x, lane, h):
    return jnp.sum(jnp.where(lane == h, x, 0.0), axis=1, keepdims=True)


def _split3(x):
    hi = x.astype(BF16).astype(F32)
    r = x - hi
    mid = r.astype(BF16).astype(F32)
    lo = (r - mid).astype(BF16).astype(F32)
    return hi, mid, lo


def _in_lanes(lane, lo, n):
    return (lane >= lo) & (lane < lo + n)


def _put3(lane, lo, parts, rest):
    return jnp.where(lane == lo, parts[0], jnp.where(lane == lo + 1, parts[1], jnp.where(lane == lo + 2, parts[2], rest)))


def _spare_lane(h):
    return FOX_HEAD_DIM if h % 2 == 0 else 0


def _shift_div(i, num, den):
    return i * (num // den) if num >= den else lax.shift_right_logical(i, (den // num).bit_length() - 1)


def _foxa_prep(qkv, fl, b, tile, name):
    s_len = qkv.shape[0]

    def body(q_ref, k_ref, v_ref, fl_ref, b_ref, qa_ref, ka_ref, va_ref, carry):
        @pl.when(pl.program_id(0) == 0)
        def _():
            carry[...] = jnp.zeros_like(carry)

        xx = fl_ref[...] + b_ref[...]
        lf = jnp.minimum(xx, 0.0) - jnp.log1p(jnp.exp(-jnp.abs(xx)))
        tri = (lax.broadcasted_iota(jnp.int32, (tile, tile), 1) <= lax.broadcasted_iota(jnp.int32, (tile, tile), 0))
        cum_t = jnp.dot(tri.astype(F32), lf, precision=lax.Precision.HIGHEST, preferred_element_type=F32) + carry[...]
        carry[...] = cum_t[tile - 1:tile, :]
        lane = lax.broadcasted_iota(jnp.int32, (1, LANES), 1)
        for h in range(FOX_HEADS):
            e = _spare_lane(h)
            head = ~_in_lanes(lane, e, FOX_HEAD_DIM)
            blk = slice(LANES * (h // 2), LANES * (h // 2) + LANES)
            out = slice(LANES * h, LANES * h + LANES)
            c3 = _split3(_head_col(cum_t, lane, h))
            ex_q = _put3(lane, e, c3, jnp.where(_in_lanes(lane, e + 3, 3), 1.0, 0.0))
            qa_ref[:, out] = jnp.where(head, q_ref[:, blk].astype(F32) * Q_SCALE, ex_q).astype(BF16)
            ones = jnp.where(_in_lanes(lane, e, 3) | _in_lanes(lane, e + 6, 3), 1.0, 0.0)
            ex_k = _put3(lane, e + 3, [-c for c in c3], ones)
            ka_ref[:, out] = jnp.where(head, k_ref[:, blk].astype(F32), ex_k).astype(BF16)
            ex_v = jnp.where(_in_lanes(lane, e, 3), 1.0, 0.0)
            va_ref[:, out] = jnp.where(head, v_ref[:, blk].astype(F32), ex_v).astype(BF16)

    col = lambda c: pl.BlockSpec((tile, FOX_WIDTH), lambda i: (i, c))
    wide = pl.BlockSpec((tile, 2 * FOX_WIDTH), lambda i: (i, 0))
    return pl.pallas_call(
        body, name=name, grid=(s_len // tile,),
        in_specs=[col(0), col(1), col(2), pl.BlockSpec((tile, LANES), lambda i: (i, 0)), _full((1, LANES))],
        out_specs=[wide, wide, wide], out_shape=[jax.ShapeDtypeStruct((s_len, 2 * FOX_WIDTH), BF16)] * 3,
        scratch_shapes=[pltpu.VMEM((1, LANES), F32)],
        compiler_params=_params("arbitrary"),
    )(qkv, qkv, qkv, fl, b)


def _foxa_fwd(qa, ka, va, tq, tk, name, host=None):
    s_len = qa.shape[0]

    def body(q_ref, k_ref, v_ref, o_ref, lse_ref, ob_ref):
        i = pl.program_id(1)
        lane = lax.broadcasted_iota(jnp.int32, (1, LANES), 1)
        cols = [slice(LANES * hh, LANES * hh + LANES) for hh in range(2)]
        qh = [q_ref[:, c] for c in cols]

        def scores(j):
            off = pl.multiple_of(j * tk, tk)
            return [lax.dot_general(qh[hh], k_ref[pl.ds(off, tk), cols[hh]], NT_DIMS, preferred_element_type=F32)
                    for hh in range(2)]

        def update(j, s, m, acc, mask):
            off = pl.multiple_of(j * tk, tk)
            m_out, acc_out = [], []
            for hh in range(2):
                sh = s[hh] if mask is None else jnp.where(mask, s[hh], NEG_INF)
                m_new = jnp.maximum(m[hh], jnp.max(sh, axis=1, keepdims=True))
                pr = jnp.exp(sh - m_new).astype(BF16)
                acc_out.append(jnp.exp(m[hh] - m_new) * acc[hh]
                               + jnp.dot(pr, v_ref[pl.ds(off, tk), cols[hh]], preferred_element_type=F32))
                m_out.append(m_new)
            return m_out, acc_out

        def step(j, carry):
            s_next = scores(j + 1)
            m, acc = update(j, carry[0:2], carry[2:4], carry[4:6], None)
            return (*s_next, *m, *acc)

        n_full = _shift_div(i, tq, tk)
        n_part = max(tq // tk, 1)
        qi = lax.broadcasted_iota(jnp.int32, (tq, tk), 0) + i * tq
        ki = lax.broadcasted_iota(jnp.int32, (tq, tk), 1)
        carry = (*scores(0), *([jnp.full((tq, 1), NEG_INF, F32)] * 2), *([jnp.zeros((tq, LANES), F32)] * 2))
        carry = lax.fori_loop(0, n_full, step, carry)
        s, m, acc = list(carry[0:2]), list(carry[2:4]), list(carry[4:6])
        for jj in range(n_part):
            s_next = scores(n_full + jj + 1) if jj < n_part - 1 else None
            m, acc = update(n_full + jj, s, m, acc, ki + (n_full + jj) * tk <= qi)
            s = s_next
        res = []
        for hh in range(2):
            l = acc[hh][:, _spare_lane(hh):_spare_lane(hh) + 1]
            res.append((acc[hh] / l, m[hh] + jnp.log(l)))
        low = lane < FOX_HEAD_DIM
        o_pair = jnp.where(low, res[0][0], res[1][0])
        o_ref[...] = o_pair
        ob_ref[...] = o_pair.astype(BF16)
        lse_ref[...] = jnp.where(low, res[0][1], res[1][1])

    pair = pl.BlockSpec((s_len, 2 * LANES), lambda p, i: (0, p))
    out = pl.BlockSpec((tq, LANES), lambda p, i: (i, p))
    return _pcall(
        body, name=name, grid=(N_PAIRS, s_len // tq),
        in_specs=[pl.BlockSpec((tq, 2 * LANES), lambda p, i: (i, p)), pair, pair],
        out_specs=[out, out, out],
        out_shape=[jax.ShapeDtypeStruct((s_len, FOX_WIDTH), F32)] * 2 + [jax.ShapeDtypeStruct((s_len, FOX_WIDTH), BF16)],
        args=[qa, ka, va], sem=("parallel", "parallel"), host=host)


def _foxa_dq(qa, lse, o, dcat, ka, va, tq, tk, name, host=None):
    s_len = qa.shape[0]

    def body(qa_ref, lse_ref, o_ref, dcat_ref, k_ref, v_ref, dq_ref, rs_ref, q_ref, do_ref):
        i = pl.program_id(1)
        lane = lax.broadcasted_iota(jnp.int32, (1, LANES), 1)
        cols = [slice(LANES * hh, LANES * hh + LANES) for hh in range(2)]
        d_o = dcat_ref[...]
        prod = d_o * o_ref[...]
        for hh in range(2):
            e = _spare_lane(hh)
            head = ~_in_lanes(lane, e, FOX_HEAD_DIM)
            delta = jnp.sum(jnp.where(head, prod, 0.0), axis=1, keepdims=True)
            do_ref[:, cols[hh]] = _put3(lane, e, _split3(-delta), jnp.where(head, d_o, 0.0)).astype(BF16)
            l3 = _split3(-lse_ref[:, FOX_HEAD_DIM - e:FOX_HEAD_DIM - e + 1])
            q_ref[:, cols[hh]] = _put3(lane, e + 6, l3, qa_ref[:, cols[hh]].astype(F32)).astype(BF16)
        qh = [q_ref[:, c] for c in cols]
        doh = [do_ref[:, c] for c in cols]

        def update(j, acc, mask):
            off = pl.multiple_of(j * tk, tk)
            out = []
            for hh in range(2):
                kt = k_ref[pl.ds(off, tk), cols[hh]]
                pr = jnp.exp(lax.dot_general(qh[hh], kt, NT_DIMS, preferred_element_type=F32))
                if mask is not None:
                    pr = jnp.where(mask, pr, 0.0)
                ds = pr * lax.dot_general(doh[hh], v_ref[pl.ds(off, tk), cols[hh]], NT_DIMS, preferred_element_type=F32)
                out.append(acc[hh] + jnp.dot(ds.astype(BF16), kt, preferred_element_type=F32))
            return tuple(out)

        n_full = _shift_div(i, tq, tk)
        qi = lax.broadcasted_iota(jnp.int32, (tq, tk), 0) + i * tq
        ki = lax.broadcasted_iota(jnp.int32, (tq, tk), 1)
        acc = lax.fori_loop(0, n_full, lambda j, a: update(j, a, None), (jnp.zeros((tq, LANES), F32),) * 2)
        for jj in range(max(tq // tk, 1)):
            acc = update(n_full + jj, acc, ki + (n_full + jj) * tk <= qi)
        low = lane < FOX_HEAD_DIM
        dq_ref[...] = (jnp.where(low, acc[0], acc[1]) * Q_SCALE).astype(BF16)
        rs_ref[...] = jnp.where(low, acc[0][:, _spare_lane(0):_spare_lane(0) + 1], acc[1][:, _spare_lane(1):_spare_lane(1) + 1])

    pair = pl.BlockSpec((s_len, 2 * LANES), lambda p, i: (0, p))
    tile2 = pl.BlockSpec((tq, 2 * LANES), lambda p, i: (i, p))
    out = pl.BlockSpec((tq, LANES), lambda p, i: (i, p))
    wide = jax.ShapeDtypeStruct((s_len, 2 * FOX_WIDTH), BF16)
    return _pcall(
        body, name=name, grid=(N_PAIRS, s_len // tq),
        in_specs=[tile2, out, out, pl.BlockSpec((tq, LANES), lambda p, i: (i, N_PAIRS + p)), pair, pair],
        out_specs=[out, out, tile2, tile2],
        out_shape=[jax.ShapeDtypeStruct((s_len, FOX_WIDTH), BF16), jax.ShapeDtypeStruct((s_len, FOX_WIDTH), F32), wide, wide],
        args=[qa, lse, o, dcat, ka, va], sem=("parallel", "parallel"), host=host)


def _foxa_dkv(qb, ka, va, doa, tq, tk, name, host=None):
    s_len = qb.shape[0]
    n_q = s_len // tq

    def body(k_ref, v_ref, q_ref, do_ref, dk_ref, dv_ref, cs_ref):
        j = pl.program_id(1)
        lane = lax.broadcasted_iota(jnp.int32, (1, LANES), 1)
        cols = [slice(LANES * hh, LANES * hh + LANES) for hh in range(2)]
        kh = [k_ref[:, c] for c in cols]
        vh = [v_ref[:, c] for c in cols]

        def update(i, acc, mask):
            off = pl.multiple_of(i * tq, tq)
            out = []
            for hh in range(2):
                qt = q_ref[pl.ds(off, tq), cols[hh]]
                dot = do_ref[pl.ds(off, tq), cols[hh]]
                pt = jnp.exp(lax.dot_general(kh[hh], qt, NT_DIMS, preferred_element_type=F32))
                if mask is not None:
                    pt = jnp.where(mask, pt, 0.0)
                dv = acc[2 * hh + 1] + jnp.dot(pt.astype(BF16), dot, preferred_element_type=F32)
                dst = pt * lax.dot_general(vh[hh], dot, NT_DIMS, preferred_element_type=F32)
                out += [acc[2 * hh] + jnp.dot(dst.astype(BF16), qt, preferred_element_type=F32), dv]
            return tuple(out)

        i0 = _shift_div(j, tk, tq)
        n_part = max(tk // tq, 1)
        ki = lax.broadcasted_iota(jnp.int32, (tk, tq), 0) + j * tk
        qi = lax.broadcasted_iota(jnp.int32, (tk, tq), 1)
        acc = (jnp.zeros((tk, LANES), F32),) * 4
        for ii in range(n_part):
            acc = update(i0 + ii, acc, ki <= qi + (i0 + ii) * tq)
        acc = lax.fori_loop(i0 + n_part, n_q, lambda i, a: update(i, a, None), acc)
        low = lane < FOX_HEAD_DIM
        dk_ref[...] = jnp.where(low, acc[0], acc[2]).astype(BF16)
        dv_ref[...] = jnp.where(low, acc[1], acc[3]).astype(BF16)
        cs_ref[...] = jnp.where(low, acc[0][:, _spare_lane(0) + 3:_spare_lane(0) + 4],
                                acc[2][:, _spare_lane(1) + 3:_spare_lane(1) + 4])

    pair = pl.BlockSpec((s_len, 2 * LANES), lambda p, j: (0, p))
    tile2 = pl.BlockSpec((tk, 2 * LANES), lambda p, j: (j, p))
    out = pl.BlockSpec((tk, LANES), lambda p, j: (j, p))
    return _pcall(
        body, name=name, grid=(N_PAIRS, s_len // tk), in_specs=[tile2, tile2, pair, pair], out_specs=[out, out, out],
        out_shape=[jax.ShapeDtypeStruct((s_len, FOX_WIDTH), BF16), jax.ShapeDtypeStruct((s_len, FOX_WIDTH), BF16),
                   jax.ShapeDtypeStruct((s_len, FOX_WIDTH), F32)],
        args=[ka, va, qb, doa], sem=("parallel", "parallel"), host=host)


def _mem_scores_t(q, kv, h):
    lo = h * MEM_HEAD_DIM
    st = lax.dot_general(kv[:, lo:lo + MEM_HEAD_DIM], q[:, lo:lo + MEM_HEAD_DIM], NT_DIMS,
                         preferred_element_type=F32) * (MEM_HEAD_DIM ** -0.5)
    e = jnp.exp(st - jnp.max(st, axis=0, keepdims=True))
    return e / jnp.sum(e, axis=0, keepdims=True)


def _memattn_fwd(q, kv, tile, name):
    s_len = q.shape[0]
    n_mem = kv.shape[0]

    def body(q_ref, kv_ref, o_ref):
        q = q_ref[...]
        kv = kv_ref[...]
        for h in range(MEM_HEADS):
            lo = h * MEM_HEAD_DIM
            pt = _mem_scores_t(q, kv, h).astype(BF16)
            vh = kv[:, MEM_INNER + lo:MEM_INNER + lo + MEM_HEAD_DIM]
            o_ref[:, lo:lo + MEM_HEAD_DIM] = lax.dot_general(pt, vh, TN_DIMS, preferred_element_type=F32).astype(BF16)

    return pl.pallas_call(
        body, name=name, grid=(s_len // tile,),
        in_specs=[pl.BlockSpec((tile, MEM_INNER), lambda i: (i, 0)), _full((n_mem, 2 * MEM_INNER))],
        out_specs=pl.BlockSpec((tile, MEM_INNER), lambda i: (i, 0)),
        out_shape=jax.ShapeDtypeStruct((s_len, MEM_INNER), BF16),
        compiler_params=_params("parallel"),
    )(q, kv)


def _memattn_bwd(q, kv, do, tile, name):
    s_len = q.shape[0]
    n_mem = kv.shape[0]
    scale = MEM_HEAD_DIM ** -0.5

    def body(q_ref, kv_ref, do_ref, dq_ref, dkv_ref):
        @pl.when(pl.program_id(0) == 0)
        def _():
            dkv_ref[...] = jnp.zeros_like(dkv_ref)

        q = q_ref[...]
        kv = kv_ref[...]
        do = do_ref[...]
        for h in range(MEM_HEADS):
            lo = h * MEM_HEAD_DIM
            qh = q[:, lo:lo + MEM_HEAD_DIM]
            kh = kv[:, lo:lo + MEM_HEAD_DIM]
            vh = kv[:, MEM_INNER + lo:MEM_INNER + lo + MEM_HEAD_DIM]
            doh = do[:, lo:lo + MEM_HEAD_DIM]
            pt = _mem_scores_t(q, kv, h)
            dkv_ref[:, MEM_INNER + lo:MEM_INNER + lo + MEM_HEAD_DIM] += jnp.dot(
                pt.astype(BF16), doh, preferred_element_type=F32)
            dpt = lax.dot_general(vh, doh, NT_DIMS, preferred_element_type=F32)
            dst = (pt * (dpt - jnp.sum(pt * dpt, axis=0, keepdims=True)) * scale).astype(BF16)
            dkv_ref[:, lo:lo + MEM_HEAD_DIM] += jnp.dot(dst, qh, preferred_element_type=F32)
            dq_ref[:, lo:lo + MEM_HEAD_DIM] = lax.dot_general(dst, kh, TN_DIMS, preferred_element_type=F32).astype(BF16)

    return pl.pallas_call(
        body, name=name, grid=(s_len // tile,),
        in_specs=[pl.BlockSpec((tile, MEM_INNER), lambda i: (i, 0)), _full((n_mem, 2 * MEM_INNER)),
                  pl.BlockSpec((tile, MEM_INNER), lambda i: (i, 0))],
        out_specs=[pl.BlockSpec((tile, MEM_INNER), lambda i: (i, 0)), _full((n_mem, 2 * MEM_INNER))],
        out_shape=[jax.ShapeDtypeStruct((s_len, MEM_INNER), BF16), jax.ShapeDtypeStruct((n_mem, 2 * MEM_INNER), F32)],
        compiler_params=_params("arbitrary"),
    )(q, kv, do)


def _loss_head(y, target, tile, name):
    s_len, d = y.shape

    def body(y_ref, t_ref, dy_ref, l_ref):
        @pl.when(pl.program_id(0) == 0)
        def _():
            l_ref[...] = jnp.zeros_like(l_ref)

        err = y_ref[...] - t_ref[...]
        dy_ref[...] = err * (1.0 / d)
        l_ref[...] += jnp.sum(err * err, axis=0, keepdims=True) * (0.5 / d)

    row = lambda i: (i, 0)
    return pl.pallas_call(
        body, name=name, grid=(s_len // tile,),
        in_specs=[pl.BlockSpec((tile, d), row), pl.BlockSpec((tile, d), row)],
        out_specs=[pl.BlockSpec((tile, d), row), _full((1, d))],
        out_shape=[jax.ShapeDtypeStruct((s_len, d), F32), jax.ShapeDtypeStruct((1, d), F32)],
        compiler_params=_params("arbitrary"),
    )(y, target)


def _attn_tile(s_len):
    return min(256, s_len // 2)


REST = ("w_out", "w_mq", "w_mk", "w_mv", "w_mo", "w_up", "w_down")
REST_DQ = ("w_up", "w_down")
REST_DKV = ("w_out", "w_mq", "w_mk", "w_mv", "w_mo")
SHARD_AXIS = {"w_in": 1, "w_out": 0, "w_mq": 0, "w_mk": 0, "w_mv": 0, "w_mo": 1, "w_up": 1, "w_down": 0}


def _full_from_shards(sh, axis):
    n, r, c = sh.shape
    if axis == 0:
        return sh.reshape(n * r, c)
    return sh.transpose(1, 0, 2).reshape(r, n * c)


def _rest_weights(lands):
    w = {n: _full_from_shards(sh, SHARD_AXIS[n]) for n, sh in zip(REST, lands)}
    w["w_mkv"] = jnp.concatenate([w.pop("w_mk"), w.pop("w_mv")], axis=1)
    return w


def _w_in_cat(land):
    return jnp.pad(land.reshape(-1, land.shape[2]), ((0, IN_CAT - IN_COLS), (0, 0)))


def _layer_fwd(x0, mem, w, l, rest_src=None, next_src=None):
    s_len = x0.shape[0]
    tile = min(512, s_len)
    tile_ff = min(512, s_len)
    ta = _attn_tile(s_len)
    ident = lambda z: z
    sv = {"x0": x0}

    h1, ag, qkv, fl = _rms_matmul(
        x0, w["norm_mix_pre"], w["w_in_cat"],
        [(0, 2 * CONV_CH, [(F32, ident)]), (2 * CONV_CH, IN_MAIN, [(BF16, ident)]), (IN_MAIN, IN_CAT, [(F32, ident)])],
        tile, f"mix_in_{l}", w_t=True)
    u3, u1 = _conv_fwd(ag, w["conv_w"], w["conv_b"], w["conv_ln_g"], w["conv_ln_b"], tile, f"conv_fwd_{l}")
    qa, ka, va = _foxa_prep(qkv, fl, w["b_forget"], tile, f"fox_prep_{l}")
    next_land = None
    if rest_src is None:
        o, lse, o_bf = _foxa_fwd(qa, ka, va, ta, 2 * ta, f"fox_fwd_{l}")
    else:
        srcs = list(rest_src) + list(next_src or [])
        (o, lse, o_bf), lands = _foxa_fwd(qa, ka, va, ta, 2 * ta, f"fox_fwd_{l}", host=(srcs, [True] * len(srcs)))
        w = {**w, **_rest_weights(lands[:len(REST)])}
        next_land = lands[len(REST):]
    cat = jnp.concatenate([u3, o_bf], axis=1)
    y1, x1 = _matmul_resnorm(cat, w["w_out"], x0, w["norm_mix_post"], tile, f"mix_out_{l}")
    sv.update(h1=h1, ag=ag, u1=u1, fl=fl, qa=qa, ka=ka, va=va, o=o, lse=lse, cat=cat, y1=y1, x1=x1)

    h2, qm = _rms_matmul(x1, w["norm_mem_pre"], w["w_mq"], [(0, MEM_INNER, [(BF16, ident)])], tile, f"mem_q_{l}")
    mem_n, kv = _rms_matmul(mem, w["norm_memkv"], w["w_mkv"], [(0, 2 * MEM_INNER, [(BF16, ident)])],
                            mem.shape[0], f"mem_kv_{l}")
    om = _memattn_fwd(qm, kv, tile, f"mem_attn_fwd_{l}")
    y2, x2 = _matmul_resnorm(om, w["w_mo"], x1, w["norm_mem_post"], tile, f"mem_out_{l}")
    sv.update(h2=h2, qm=qm, mem_n=mem_n, kv=kv, om=om, y2=y2, x2=x2)

    relu2 = lambda z: jnp.square(jnp.maximum(z, 0.0))
    h3, pre, hid = _rms_matmul(x2, w["norm_mlp_pre"], w["w_up"], [(0, D_FF, [(BF16, ident), (BF16, relu2)])], tile_ff,
                               f"mlp_up_{l}")
    y3, x3 = _matmul_resnorm(hid, w["w_down"], x2, w["norm_mlp_post"], tile_ff, f"mlp_down_{l}")
    sv.update(h3=h3, pre=pre, hid=hid, y3=y3)
    return x3, sv, w, next_land


def _layer_bwd(dx3, mem, w, sv, l, scatter_rest=False, dkv_src=None, scatter_w_in=False):
    s_len = dx3.shape[0]
    tile = min(512, s_len)
    tile_ff = min(512, s_len)
    ta = _attn_tile(s_len)
    tk = min(2048, s_len)
    tk_in = min(1024, s_len)
    n_mem = mem.shape[0]
    g = {}

    dy3, dpre, g["norm_mlp_post"] = _resnorm_bwd_mm(dx3, sv["y3"], w["norm_mlp_post"], w["w_down"], tile_ff,
                                                    f"mlp_down_bwd_{l}", BF16, pre=sv["pre"])
    g["w_down"] = _matmul_tn_shards(sv["hid"], dy3, 0, tk, f"dw_down_{l}")
    dx2, g["norm_mlp_pre"] = _mm_prenorm_bwd(dpre, w["w_up"], sv["x2"], w["norm_mlp_pre"], dx3, tile_ff,
                                             f"mlp_up_bwd_{l}")
    g["w_up"] = _matmul_tn_shards(sv["h3"], dpre, 1, tk, f"dw_up_{l}")

    dy2, dom, g["norm_mem_post"] = _resnorm_bwd_mm(dx2, sv["y2"], w["norm_mem_post"], w["w_mo"], tile,
                                                   f"mem_out_bwd_{l}", BF16)
    g["w_mo"] = _matmul_tn_shards(sv["om"], dy2, 1, tk, f"dw_mo_{l}")
    dqm, dkv = _memattn_bwd(sv["qm"], sv["kv"], dom, tile, f"mem_attn_bwd_{l}")
    dkvb = dkv.astype(BF16)
    g["w_mq"] = _matmul_tn_shards(sv["h2"], dqm, 0, tk, f"dw_mq_{l}")
    dx1, g["norm_mem_pre"] = _mm_prenorm_bwd(dqm, w["w_mq"], sv["x1"], w["norm_mem_pre"], dx2, tile, f"mem_q_bwd_{l}")
    _, g["norm_memkv"] = _mm_prenorm_bwd(dkvb, w["w_mkv"], mem, w["norm_memkv"], None, n_mem, f"mem_kv_bwd_{l}")
    g["w_mk"] = _matmul_tn_shards(sv["mem_n"], dkvb[:, :MEM_INNER], 0, n_mem, f"dw_mk_{l}")
    g["w_mv"] = _matmul_tn_shards(sv["mem_n"], dkvb[:, MEM_INNER:], 0, n_mem, f"dw_mv_{l}")

    dy1, dcat, g["norm_mix_post"] = _resnorm_bwd_mm(dx1, sv["y1"], w["norm_mix_post"], w["w_out"], tile,
                                                    f"mix_out_bwd_{l}", F32)
    g["w_out"] = _matmul_tn_shards(sv["cat"], dy1, 0, tk, f"dw_out_{l}")
    dq_args = (sv["qa"], sv["lse"], sv["o"], dcat, sv["ka"], sv["va"], ta, 2 * ta, f"fox_dq_{l}")
    rest_land, dkv_land = None, None
    if scatter_rest:
        with_dq = [g[n] for n in REST_DQ]
        (dq, rs, qb, doa), land_dq = _foxa_dq(*dq_args, host=(with_dq, [False] * len(with_dq)))
        with_dkv = [g[n] for n in REST_DKV] + list(dkv_src or [])
        (dk, dv, cs), land_dkv = _foxa_dkv(qb, sv["ka"], sv["va"], doa, 2 * ta, ta, f"fox_dkv_{l}",
                                           host=(with_dkv, [False] * len(with_dkv)))
        by_name = dict(zip(REST_DQ + REST_DKV, land_dq + land_dkv))
        rest_land = [by_name[n] for n in REST]
        dkv_land = land_dkv[len(REST_DKV):]
    else:
        dq, rs, qb, doa = _foxa_dq(*dq_args)
        dk, dv, cs = _foxa_dkv(qb, sv["ka"], sv["va"], doa, 2 * ta, ta, f"fox_dkv_{l}")
    dfl, db = _cumsum_bwd(rs, cs, sv["fl"], w["b_forget"], tile, f"cumsum_bwd_{l}")
    g["b_forget"] = db[:, :FOX_HEADS]
    dag, dconv_w, dconv_v = _conv_bwd(sv["ag"], sv["u1"], dcat, w["conv_w"], w["conv_ln_g"], w["conv_ln_b"], tile,
                                      f"conv_bwd_{l}")
    g["conv_w"] = dconv_w[:CONV_WIDTH]
    g["conv_b"], g["conv_ln_g"], g["conv_ln_b"] = dconv_v[0:1], dconv_v[1:2], dconv_v[2:3]
    dz = [dag, dq, dk, dv, dfl]
    dw_in_t = _matmul_tn_rows(dz, sv["h1"], tk_in, f"dw_in_{l}")[:IN_COLS]
    g["w_in"] = dw_in_t.reshape(N_DEV, IN_COLS // N_DEV, D_MODEL).astype(BF16)
    res = _mm_prenorm_bwd(dz, w["w_in_cat"], sv["x0"], w["norm_mix_pre"], dx1, tile, f"mix_in_bwd_{l}",
                          host=([g["w_in"]], [False]) if scatter_w_in else None, w_t=True)
    (dx0, g["norm_mix_pre"]), w_in_land = res if scatter_w_in else (res, None)
    return dx0, g, rest_land, dkv_land, w_in_land


def _sum_blocks(a, name):
    n, rows, cols = a.shape

    def body(a_ref, o_ref):
        acc = a_ref[0]
        for j in range(1, n):
            acc = acc + a_ref[j]
        o_ref[...] = acc

    return pl.pallas_call(
        body, name=name, in_specs=[_full((n, rows, cols))], out_specs=_full((rows, cols)),
        out_shape=jax.ShapeDtypeStruct((rows, cols), F32), grid=(1,),
    )(a)


def _adamw(gparts, w, m, v, tile, name):
    n_l, rows, cols = w.shape
    n = gparts[0].shape[0]
    n_t = rows // tile
    c1 = 1.0 - ADAM_B1
    c2 = 1.0 - ADAM_B2
    bc1 = 1.0 - ADAM_B1 ** ADAM_STEP
    bc2 = 1.0 - ADAM_B2 ** ADAM_STEP

    def body(*refs):
        gp_refs, (w_ref, m_ref, v_ref, g_ref, d_ref, mo_ref, vo_ref) = refs[:n_l], refs[n_l:]
        layer = pl.program_id(0)
        g = None
        for l, gp_ref in enumerate(gp_refs):
            gl = gp_ref[0].astype(F32)
            for j in range(1, n):
                gl = gl + gp_ref[j].astype(F32)
            g = gl if g is None else jnp.where(layer == l, gl, g)
        g_ref[...] = g
        m_new = ADAM_B1 * m_ref[...] + c1 * g
        v_new = ADAM_B2 * v_ref[...] + c2 * (g * g)
        mo_ref[...] = m_new
        vo_ref[...] = v_new
        d_ref[...] = -ADAM_LR * ((m_new / bc1) / (jnp.sqrt(v_new / bc2) + ADAM_EPS) + ADAM_WD * w_ref[...])

    def gp_spec(l):
        return pl.BlockSpec((n, tile, cols), lambda L, i: (0, jnp.where(L < l, 0, jnp.where(L > l, n_t - 1, i)), 0))

    spec = pl.BlockSpec((None, tile, cols), lambda L, i: (L, i, 0))
    return pl.pallas_call(
        body, name=name, grid=(n_l, n_t),
        in_specs=[gp_spec(l) for l in range(n_l)] + [spec, spec, spec],
        out_specs=[spec] * 4, out_shape=[jax.ShapeDtypeStruct((n_l, rows, cols), F32)] * 4,
        compiler_params=_params("arbitrary", "arbitrary"),
    )(*gparts, w, m, v)


def _adamw_cols(gparts, w, m, v, name):
    n, rows, cols = gparts[0].shape
    n_l = len(gparts)
    c1 = 1.0 - ADAM_B1
    c2 = 1.0 - ADAM_B2
    bc1 = 1.0 - ADAM_B1 ** ADAM_STEP
    bc2 = 1.0 - ADAM_B2 ** ADAM_STEP

    def body(*refs):
        gp_refs, (w_ref, m_ref, v_ref, g_ref, d_ref, mo_ref, vo_ref) = refs[:n_l], refs[n_l:]
        layer = pl.program_id(0)
        g = None
        for l, gp_ref in enumerate(gp_refs):
            gl = gp_ref[0].astype(F32)
            for j in range(1, n):
                gl = gl + gp_ref[j].astype(F32)
            g = gl if g is None else jnp.where(layer == l, gl, g)
        g_ref[...] = g
        m_new = ADAM_B1 * m_ref[...] + c1 * g
        v_new = ADAM_B2 * v_ref[...] + c2 * (g * g)
        mo_ref[...] = m_new
        vo_ref[...] = v_new
        d_ref[...] = -ADAM_LR * ((m_new / bc1) / (jnp.sqrt(v_new / bc2) + ADAM_EPS) + ADAM_WD * w_ref[...])

    spec = pl.BlockSpec((rows, cols), lambda L: (0, L))
    return pl.pallas_call(
        body, name=name, grid=(n_l,),
        in_specs=[_full((n, rows, cols))] * n_l + [spec, spec, spec],
        out_specs=[spec] * 4, out_shape=[jax.ShapeDtypeStruct((rows, n_l * cols), F32)] * 4,
        compiler_params=_params("arbitrary"),
    )(*gparts, w, m, v)


def _pack_rows(parts, total_rows):
    flat = [p.reshape(-1, D_MODEL) for p in parts]
    used = sum(f.shape[0] for f in flat)
    if total_rows > used:
        flat.append(jnp.zeros((total_rows - used, D_MODEL), flat[0].dtype))
    return jnp.concatenate(flat, axis=0)


def kernel(x, mem, norm_mix_pre, norm_mix_post, w_in, b_forget, conv_w, conv_b, conv_ln_g, conv_ln_b, w_out, norm_mem_pre, norm_mem_post, norm_memkv, w_mq, w_mk, w_mv, w_mo, norm_mlp_pre, norm_mlp_post, w_up, w_down, loss_target, m_norm_mix_pre, m_norm_mix_post, m_w_in, m_b_forget, m_conv_w, m_conv_b, m_conv_ln_g, m_conv_ln_b, m_w_out, m_norm_mem_pre, m_norm_mem_post, m_norm_memkv, m_w_mq, m_w_mk, m_w_mv, m_w_mo, m_norm_mlp_pre, m_norm_mlp_post, m_w_up, m_w_down, v_norm_mix_pre, v_norm_mix_post, v_w_in, v_b_forget, v_conv_w, v_conv_b, v_conv_ln_g, v_conv_ln_b, v_w_out, v_norm_mem_pre, v_norm_mem_post, v_norm_memkv, v_w_mq, v_w_mk, v_w_mv, v_w_mo, v_norm_mlp_pre, v_norm_mlp_post, v_w_up, v_w_down):
    p = dict(locals())
    names = ("norm_mix_pre", "norm_mix_post", "w_in", "b_forget", "conv_w", "conv_b", "conv_ln_g", "conv_ln_b", "w_out",
             "norm_mem_pre", "norm_mem_post", "norm_memkv", "w_mq", "w_mk", "w_mv", "w_mo", "norm_mlp_pre",
             "norm_mlp_post", "w_up", "w_down")
    me = 4 * lax.axis_index("x") + 2 * lax.axis_index("y") + lax.axis_index("c")
    conv_cols = conv_w.shape[2]
    s_len = x.shape[1]
    bf = {n: p[n].astype(BF16) for n in REST}
    in_cols = w_in.shape[2]
    w_in_t = {pre: jnp.transpose(p[pre + "w_in"], (2, 0, 1)).reshape(in_cols, DEPTH * D_MODEL) for pre in ("", "m_", "v_")}
    bf_in = w_in_t[""].astype(BF16)
    bf["w_in"] = [bf_in[:, l * D_MODEL:(l + 1) * D_MODEL] for l in range(DEPTH)]

    conv_pack = _pack_rows([jnp.pad(conv_w, ((0, 0), (0, CONV_PAD - CONV_WIDTH), (0, 0)))], 8)
    win_land, conv_land = _exchange([bf["w_in"][0], conv_pack], [True, True], "gather_first")
    conv_rows = DEPTH * CONV_PAD * conv_cols // D_MODEL
    conv_full = conv_land[:, :conv_rows].reshape(N_DEV, DEPTH, CONV_PAD, conv_cols)
    conv_full = conv_full.transpose(1, 2, 0, 3).reshape(DEPTH, CONV_PAD, N_DEV * conv_cols)
    b_forget_pad = jnp.pad(b_forget, ((0, 0), (0, LANES - FOX_HEADS)))

    def first_weights(l, land):
        w = {"w_in_cat": _w_in_cat(land), "conv_w": conv_full[l], "b_forget": b_forget_pad[l:l + 1]}
        for n in VEC[:-1]:
            w[n] = p[n][l:l + 1]
        return w

    h, sv0, w0, win1_land = _layer_fwd(x[0], mem[0], first_weights(0, win_land), 0,
                                       rest_src=[bf[n][0] for n in REST], next_src=[bf["w_in"][1]])
    h, sv1, w1, _ = _layer_fwd(h, mem[0], first_weights(1, win1_land[0]), 1, rest_src=[bf[n][1] for n in REST])
    dh, loss_row = _loss_head(h, loss_target[0], min(512, s_len), "loss_head")
    dh, g1, rest_g1, _, _ = _layer_bwd(dh, mem[0], w1, sv1, 1, scatter_rest=True)
    grad_x, g0, rest_g0, win1_g, win0_g = _layer_bwd(dh, mem[0], w0, sv0, 0, scatter_rest=True, dkv_src=[g1["w_in"]],
                                                     scatter_w_in=True)

    def small_rows(get):
        rows = [jnp.concatenate([get(n) for n in VEC_1024], axis=0),
                jnp.concatenate([get(n) for n in VEC_512], axis=0).reshape(len(VEC_512), D_MODEL),
                jnp.pad(get("b_forget").reshape(1, -1), ((0, 0), (0, D_MODEL - DEPTH * FOX_HEADS)))]
        return jnp.concatenate(rows, axis=0)

    def tap_rows(conv):
        return jnp.pad(conv.reshape(1, -1), ((0, 0), (0, 4 * D_MODEL - conv.size))).reshape(4, D_MODEL)

    n_vec_rows = DEPTH * len(VEC_1024) + len(VEC_512) + 1
    part = small_rows(lambda n: jnp.concatenate([g0[n], g1[n]], axis=0))
    conv_part = jnp.stack([g0["conv_w"], g1["conv_w"]]).reshape(CONV_WIDTH, D_MODEL)
    n_part = 1 + n_vec_rows + CONV_WIDTH
    pad_rows = -n_part % 8
    small_land = _exchange([jnp.concatenate([loss_row, part, conv_part, jnp.zeros((pad_rows, D_MODEL), F32)], axis=0)],
                           [True], "gather_small")[0]
    total = _sum_blocks(small_land, "sum_small")
    loss = jnp.sum(total[0])
    conv_g = total[1 + n_vec_rows:n_part].reshape(DEPTH, CONV_WIDTH, CONV_CH)
    conv_g = lax.dynamic_slice_in_dim(conv_g, me * conv_cols, conv_cols, axis=2)
    fill = jnp.zeros((SMALL_ROWS - n_vec_rows - 4, D_MODEL), F32)

    def small_pack(vec_rows, conv):
        return jnp.concatenate([vec_rows, tap_rows(conv), fill], axis=0)

    small_out = _adamw([small_pack(total[1:1 + n_vec_rows], conv_g)[None]],
                       *[small_pack(small_rows(lambda n: p[pre + n]), p[pre + "conv_w"])[None] for pre in ("", "m_", "v_")],
                       SMALL_ROWS, "adamw_small")

    def unpack_small(buf):
        out = {}
        for k, n in enumerate(VEC_1024):
            out[n] = buf[DEPTH * k:DEPTH * (k + 1)]
        at = DEPTH * len(VEC_1024)
        for k, n in enumerate(VEC_512):
            out[n] = buf[at + k].reshape(DEPTH, CONV_CH)
        at += len(VEC_512)
        out["b_forget"] = buf[at, :DEPTH * FOX_HEADS].reshape(DEPTH, FOX_HEADS)
        out["conv_w"] = buf[at + 1:at + 5].reshape(-1)[:DEPTH * CONV_WIDTH * conv_cols].reshape(DEPTH, CONV_WIDTH, conv_cols)
        return out

    big_out = {n: _adamw([rest_g0[i], rest_g1[i]], p[n], p["m_" + n], p["v_" + n], ADAMW_TILE[n], f"adamw_{n}")
               for i, n in enumerate(REST)}
    in_out = _adamw_cols([win0_g[0], win1_g[0]], w_in_t[""], w_in_t["m_"], w_in_t["v_"], "adamw_w_in")
    big_out["w_in"] = [a.reshape(in_cols, DEPTH, D_MODEL).transpose(1, 2, 0) for a in in_out]

    result = [loss, grad_x[None]]
    for k in range(4):
        smalls = unpack_small(small_out[k][0])
        result += [big_out[n][k] if n in big_out else smalls[n] for n in names]
    return tuple(result)
```

```python
import functools

import jax
import jax.numpy as jnp
from jax import lax
from jax.experimental import pallas as pl
from jax.experimental.pallas import tpu as pltpu

F32 = jnp.float32
BF16 = jnp.bfloat16

N_DEV = 8
DEPTH = 2
D_MODEL = 1024
CONV_CH = 512
CONV_WIDTH = 31
CONV_PAD = 32
FOX_HEADS = 8
FOX_HEAD_DIM = 64
FOX_WIDTH = 512
N_PAIRS = 4
MEM_HEADS = 4
MEM_HEAD_DIM = 128
MEM_INNER = 512
D_FF = 4096
IN_MAIN = 2560
IN_COLS = 2568
IN_CAT = IN_MAIN + 128
LANES = 128
EPS = 1e-6
NEG_INF = -1e30

ADAM_LR = 0.001
ADAM_B1 = 0.9
ADAM_B2 = 0.999
ADAM_EPS = 1e-08
ADAM_WD = 0.01
ADAM_STEP = 10

NT_DIMS = (((1,), (1,)), ((), ()))
TN_DIMS = (((0,), (0,)), ((), ()))

BIG = ("w_in", "w_out", "w_mq", "w_mk", "w_mv", "w_mo", "w_up", "w_down")
ADAMW_TILE = {"w_out": 128, "w_mq": 128, "w_mk": 128, "w_mv": 128, "w_mo": 512, "w_up": 256, "w_down": 128}

VEC_1024 = ("norm_mix_pre", "norm_mix_post", "norm_mem_pre", "norm_mem_post", "norm_memkv", "norm_mlp_pre", "norm_mlp_post")
VEC_512 = ("conv_b", "conv_ln_g", "conv_ln_b")
VEC = VEC_1024 + VEC_512 + ("b_forget",)
SMALL_ROWS = 32


def _sigmoid(x):
    return 1.0 / (1.0 + jnp.exp(-x))


def _rms(x, g):
    r = lax.rsqrt(jnp.mean(x * x, axis=-1, keepdims=True) + EPS)
    return x * r * g


def _rms_bwd(x, g, dh):
    r = lax.rsqrt(jnp.mean(x * x, axis=-1, keepdims=True) + EPS)
    gh = dh * g
    c = jnp.mean(gh * x, axis=-1, keepdims=True)
    dx = r * gh - x * (r * r * r * c)
    dg = jnp.sum(dh * (x * r), axis=0, keepdims=True)
    return dx, dg


def _full(shape):
    nd = len(shape)
    return pl.BlockSpec(shape, lambda *_: (0,) * nd)


def _params(*sem):
    return pltpu.CompilerParams(dimension_semantics=sem)


def _exchange_copies(src_refs, out_refs, same, send_sems, recv_sems, local_sems, with_recvs):
    x, y, c = lax.axis_index("x"), lax.axis_index("y"), lax.axis_index("c")
    me = 4 * x + 2 * y + c
    local, sends, recvs = [], [], []
    for a, (s_ref, o_ref) in enumerate(zip(src_refs, out_refs)):
        def mine(idx, s_ref=s_ref, whole=same[a]):
            return s_ref if whole else s_ref.at[idx]

        local.append(pltpu.make_async_copy(mine(me), o_ref.at[me], local_sems.at[a]))
        for k in range(1, N_DEV):
            px = 1 - x if k & 4 else x
            py = 1 - y if k & 2 else y
            pc = 1 - c if k & 1 else c
            peer = 4 * px + 2 * py + pc
            sem = a * (N_DEV - 1) + k - 1
            common = dict(send_sem=send_sems.at[sem], recv_sem=recv_sems.at[sem], device_id=(px, py, pc),
                          device_id_type=pl.DeviceIdType.MESH)
            sends.append(pltpu.make_async_remote_copy(src_ref=mine(peer), dst_ref=o_ref.at[me], **common))
            if with_recvs:
                recvs.append(pltpu.make_async_remote_copy(src_ref=mine(peer), dst_ref=o_ref.at[peer], **common))
    return local, sends, recvs


def _gather_copies(src_refs, out_refs, send_sems, recv_sems, local_sems, phase):
    x, y, c = lax.axis_index("x"), lax.axis_index("y"), lax.axis_index("c")
    sibling = (x, y, 1 - c)
    chips = [(1 - x, y), (x, 1 - y), (1 - x, 1 - y)]

    def idx(px, py, pc):
        return 4 * px + 2 * py + pc

    local, first, arrive, passed, final = [], [], [], [], []
    for a, (s_ref, o_ref) in enumerate(zip(src_refs, out_refs)):
        def cp(k, src, block, to, a=a, o_ref=o_ref):
            sem = a * (N_DEV - 1) + k
            return pltpu.make_async_remote_copy(src_ref=src, dst_ref=o_ref.at[block], send_sem=send_sems.at[sem],
                                                recv_sem=recv_sems.at[sem], device_id=to, device_id_type=pl.DeviceIdType.MESH)

        me = idx(x, y, c)
        if phase != 1:
            local.append(pltpu.make_async_copy(s_ref, o_ref.at[me], local_sems.at[a]))
            first.append(cp(0, s_ref, me, sibling))
        if phase == 2:
            final.append(cp(0, s_ref, idx(x, y, 1 - c), sibling))
        for j, chip in enumerate(chips):
            theirs = idx(*chip, c)
            if phase != 1:
                first.append(cp(1 + j, s_ref, me, (*chip, c)))
            if phase == 1:
                arrive.append(cp(1 + j, s_ref, theirs, (*chip, c)))
            if phase != 0:
                passed.append(cp(4 + j, o_ref.at[theirs], theirs, sibling))
            if phase == 2:
                final.append(cp(4 + j, s_ref, idx(*chip, 1 - c), sibling))
    return local, first, arrive, passed, final


def _pcall(body, *, name, grid, in_specs, out_specs, out_shape, args, scratch_shapes=(), sem=(), host=None):
    if host is None:
        return pl.pallas_call(body, name=name, grid=grid, in_specs=in_specs, out_specs=out_specs, out_shape=out_shape,
                              scratch_shapes=list(scratch_shapes), compiler_params=_params(*sem))(*args)
    srcs, same = host
    n_in, n_out, n_scr, n_h = len(in_specs), len(out_specs), len(scratch_shapes), len(srcs)
    hbm = pl.BlockSpec(memory_space=pltpu.HBM)
    lands = [jax.ShapeDtypeStruct((N_DEV,) + (s.shape if whole else s.shape[1:]), s.dtype) for s, whole in zip(srcs, same)]

    def wrapped(*refs):
        ins, src_refs = refs[:n_in], refs[n_in:n_in + n_h]
        outs = refs[n_in + n_h:n_in + n_h + n_out]
        land_refs = refs[n_in + n_h + n_out:n_in + 2 * n_h + n_out]
        scr = refs[n_in + 2 * n_h + n_out:n_in + 2 * n_h + n_out + n_scr]
        sems = refs[n_in + 2 * n_h + n_out + n_scr:]
        ids = [pl.program_id(d) for d in range(len(grid))]
        first = functools.reduce(jnp.logical_and, [i == 0 for i in ids])
        last = functools.reduce(jnp.logical_and, [i == n - 1 for i, n in zip(ids, grid)])
        later = functools.reduce(jnp.logical_and, [ids[0] == (3 * grid[0]) // 4] + [i == 0 for i in ids[1:]])

        if all(same):
            @pl.when(first)
            def _():
                local, sends, _, _, _ = _gather_copies(src_refs, land_refs, *sems, 0)
                for cp in local + sends:
                    cp.start()

            body(*ins, *outs, *scr)

            @pl.when(later)
            def _():
                _, _, arrive, passed, _ = _gather_copies(src_refs, land_refs, *sems, 1)
                for cp in arrive:
                    cp.wait_recv()
                for cp in passed:
                    cp.start()

            @pl.when(last)
            def _():
                local, sends, _, passed, final = _gather_copies(src_refs, land_refs, *sems, 2)
                for cp in final:
                    cp.wait_recv()
                for cp in sends + passed:
                    cp.wait_send()
                for cp in local:
                    cp.wait()
        else:
            @pl.when(first)
            def _():
                local, sends, _ = _exchange_copies(src_refs, land_refs, same, *sems, False)
                for cp in local + sends:
                    cp.start()

            body(*ins, *outs, *scr)

            @pl.when(last)
            def _():
                local, sends, recvs = _exchange_copies(src_refs, land_refs, same, *sems, True)
                for cp in recvs:
                    cp.wait_recv()
                for cp in sends:
                    cp.wait_send()
                for cp in local:
                    cp.wait()

    n_sem = n_h * (N_DEV - 1)
    res = pl.pallas_call(
        wrapped, name=name, grid=grid, in_specs=list(in_specs) + [hbm] * n_h, out_specs=list(out_specs) + [hbm] * n_h,
        out_shape=list(out_shape) + lands,
        scratch_shapes=list(scratch_shapes) + [pltpu.SemaphoreType.DMA((n_sem,)), pltpu.SemaphoreType.DMA((n_sem,)),
                                               pltpu.SemaphoreType.DMA((n_h,))],
        compiler_params=_params(*(("arbitrary",) * len(grid))),
    )(*args, *srcs)
    return list(res[:n_out]), list(res[n_out:])


def _exchange(srcs, same, name):
    def body():
        pass

    return _pcall(body, name=name, grid=(1,), in_specs=[], out_specs=[], out_shape=[], args=[], host=(srcs, same))[1]


def _rms_matmul(x, g, w, segs, tile, name, host=None, w_t=False):
    s_len, d = x.shape
    chunk = 512

    def body(x_ref, g_ref, w_ref, h_ref, *outs):
        h = _rms(x_ref[...], g_ref[...]).astype(BF16)
        h_ref[...] = h
        oi = 0
        for c0, c1, fns in segs:
            for a in range(c0, c1, chunk):
                b = min(a + chunk, c1)
                if w_t:
                    z = lax.dot_general(h, w_ref[a:b, :], NT_DIMS, preferred_element_type=F32)
                else:
                    z = jnp.dot(h, w_ref[:, a:b], preferred_element_type=F32)
                for k, (dt, fn) in enumerate(fns):
                    outs[oi + k][:, a - c0:b - c0] = fn(z).astype(dt)
            oi += len(fns)

    out_shape = [jax.ShapeDtypeStruct((s_len, d), BF16)]
    out_specs = [pl.BlockSpec((tile, d), lambda i: (i, 0))]
    for c0, c1, fns in segs:
        for dt, _ in fns:
            out_shape.append(jax.ShapeDtypeStruct((s_len, c1 - c0), dt))
            out_specs.append(pl.BlockSpec((tile, c1 - c0), lambda i: (i, 0)))
    return _pcall(
        body, name=name, grid=(s_len // tile,),
        in_specs=[pl.BlockSpec((tile, d), lambda i: (i, 0)), _full((1, d)), _full(w.shape)],
        out_specs=out_specs, out_shape=out_shape, args=[x, g, w], sem=("parallel",), host=host)


def _matmul_resnorm(a, w, x, g, tile, name):
    s_len, k = a.shape
    d = w.shape[1]

    def body(a_ref, w_ref, x_ref, g_ref, y_ref, xo_ref):
        y = jnp.dot(a_ref[...], w_ref[...], preferred_element_type=F32)
        y_ref[...] = y
        xo_ref[...] = x_ref[...] + _rms(y, g_ref[...])

    row = lambda i: (i, 0)
    return pl.pallas_call(
        body, name=name, grid=(s_len // tile,),
        in_specs=[pl.BlockSpec((tile, k), row), _full((k, d)), pl.BlockSpec((tile, d), row), _full((1, d))],
        out_specs=[pl.BlockSpec((tile, d), row), pl.BlockSpec((tile, d), row)],
        out_shape=[jax.ShapeDtypeStruct((s_len, d), F32), jax.ShapeDtypeStruct((s_len, d), F32)],
        compiler_params=_params("parallel"),
    )(a, w, x, g)


def _resnorm_bwd_mm(dx, y, g, w, tile, name, out_dtype, pre=None):
    s_len, d = dx.shape
    k = w.shape[0]
    chunk = 512

    def body(*refs):
        if pre is None:
            dx_ref, y_ref, g_ref, w_ref, dy_ref, da_ref, dg_ref = refs
        else:
            dx_ref, y_ref, g_ref, w_ref, pre_ref, dy_ref, da_ref, dg_ref = refs
        dy, dg = _rms_bwd(y_ref[...], g_ref[...], dx_ref[...])
        dyb = dy.astype(BF16)
        dy_ref[...] = dyb

        @pl.when(pl.program_id(0) == 0)
        def _():
            dg_ref[...] = jnp.zeros_like(dg_ref)

        dg_ref[...] += dg
        for a in range(0, k, chunk):
            b = min(a + chunk, k)
            da = lax.dot_general(dyb, w_ref[a:b, :], NT_DIMS, preferred_element_type=F32)
            if pre is not None:
                da = da * (2.0 * jnp.maximum(pre_ref[:, a:b].astype(F32), 0.0))
            da_ref[:, a:b] = da.astype(out_dtype)

    row = lambda i: (i, 0)
    in_specs = [pl.BlockSpec((tile, d), row), pl.BlockSpec((tile, d), row), _full((1, d)), _full((k, d))]
    args = [dx, y, g, w]
    if pre is not None:
        in_specs.append(pl.BlockSpec((tile, k), row))
        args.append(pre)
    return pl.pallas_call(
        body, name=name, grid=(s_len // tile,), in_specs=in_specs,
        out_specs=[pl.BlockSpec((tile, d), row), pl.BlockSpec((tile, k), row), _full((1, d))],
        out_shape=[jax.ShapeDtypeStruct((s_len, d), BF16), jax.ShapeDtypeStruct((s_len, k), out_dtype),
                   jax.ShapeDtypeStruct((1, d), F32)],
        compiler_params=_params("arbitrary"),
    )(*args)


def _mm_prenorm_bwd(dz, w, x, g, dres, tile, name, host=None, w_t=False):
    pieces = list(dz) if isinstance(dz, (list, tuple)) else [dz]
    n_p = len(pieces)
    widths = [p.shape[1] for p in pieces]
    s_len = pieces[0].shape[0]
    d = w.shape[1] if w_t else w.shape[0]

    def body(*refs):
        dz_refs, rest = refs[:n_p], refs[n_p:]
        if dres is None:
            w_ref, x_ref, g_ref, dx_ref, dg_ref = rest
        else:
            w_ref, x_ref, g_ref, dres_ref, dx_ref, dg_ref = rest
        dh, off = None, 0
        for dz_ref, width in zip(dz_refs, widths):
            if w_t:
                part = jnp.dot(dz_ref[...], w_ref[off:off + width, :], preferred_element_type=F32)
            else:
                part = lax.dot_general(dz_ref[...], w_ref[:, off:off + width], NT_DIMS, preferred_element_type=F32)
            dh = part if dh is None else dh + part
            off += width
        dx, dg = _rms_bwd(x_ref[...], g_ref[...], dh)
        if dres is not None:
            dx = dx + dres_ref[...]
        dx_ref[...] = dx

        @pl.when(pl.program_id(0) == 0)
        def _():
            dg_ref[...] = jnp.zeros_like(dg_ref)

        dg_ref[...] += dg

    row = lambda i: (i, 0)
    in_specs = [pl.BlockSpec((tile, width), row) for width in widths]
    in_specs += [_full(w.shape), pl.BlockSpec((tile, d), row), _full((1, d))]
    args = pieces + [w, x, g]
    if dres is not None:
        in_specs.append(pl.BlockSpec((tile, d), row))
        args.append(dres)
    return _pcall(
        body, name=name, grid=(s_len // tile,), in_specs=in_specs,
        out_specs=[pl.BlockSpec((tile, d), row), _full((1, d))],
        out_shape=[jax.ShapeDtypeStruct((s_len, d), F32), jax.ShapeDtypeStruct((1, d), F32)],
        args=args, sem=("arbitrary",), host=host)


def _matmul_tn_rows(pieces, b, tk, name):
    s_len, n = b.shape
    widths = [p.shape[1] for p in pieces]

    def body(*refs):
        a_refs, b_ref, o_ref = refs[:-2], refs[-2], refs[-1]

        @pl.when(pl.program_id(0) == 0)
        def _():
            o_ref[...] = jnp.zeros_like(o_ref)

        b_tile = b_ref[...]
        off = 0
        for a_ref, width in zip(a_refs, widths):
            o_ref[off:off + width, :] += lax.dot_general(a_ref[...], b_tile, TN_DIMS, preferred_element_type=F32)
            off += width

    return pl.pallas_call(
        body, name=name, grid=(s_len // tk,),
        in_specs=[pl.BlockSpec((tk, width), lambda k: (k, 0)) for width in widths] + [pl.BlockSpec((tk, n), lambda k: (k, 0))],
        out_specs=_full((sum(widths), n)),
        out_shape=jax.ShapeDtypeStruct((sum(widths), n), F32),
        compiler_params=_params("arbitrary"),
    )(*pieces, b)


def _matmul_tn_shards(a, b, axis, tk, name):
    s_len, m = a.shape
    n = b.shape[1]
    r, c = (m // N_DEV, n) if axis == 0 else (m, n // N_DEV)
    n_k = s_len // tk
    tm = max(r, min(m, (1 << 20) // n)) if axis == 0 else min(m, (1 << 20) // n)

    def body(a_ref, b_ref, o_ref, acc):
        k = pl.program_id(1)

        @pl.when(k == 0)
        def _():
            acc[...] = jnp.zeros_like(acc)

        acc[...] += lax.dot_general(a_ref[...], b_ref[...], TN_DIMS, preferred_element_type=F32)

        @pl.when(k == n_k - 1)
        def _():
            if axis == 0:
                o_ref[...] = acc[...].reshape(tm // r, r, c).astype(BF16)
            else:
                for j in range(N_DEV):
                    o_ref[j] = acc[:, j * c:(j + 1) * c].astype(BF16)

    if axis == 0:
        out_spec = pl.BlockSpec((tm // r, r, c), lambda i, k: (i, 0, 0))
    else:
        out_spec = pl.BlockSpec((N_DEV, tm, c), lambda i, k: (0, i, 0))
    return pl.pallas_call(
        body, name=name, grid=(m // tm, n_k),
        in_specs=[pl.BlockSpec((tk, tm), lambda i, k: (k, i)), pl.BlockSpec((tk, n), lambda i, k: (k, 0))],
        out_specs=out_spec, out_shape=jax.ShapeDtypeStruct((N_DEV, r, c), BF16),
        scratch_shapes=[pltpu.VMEM((tm, n), F32)],
        compiler_params=_params("parallel", "arbitrary"),
    )(a, b)


def _cumsum_bwd(rs, cs, fl, b, tile, name):
    s_len = fl.shape[0]
    n_t = s_len // tile

    def body(rs_ref, cs_ref, fl_ref, b_ref, dfl_ref, db_ref, carry):
        @pl.when(pl.program_id(0) == 0)
        def _():
            carry[...] = jnp.zeros_like(carry)
            db_ref[...] = jnp.zeros_like(db_ref)

        r = lax.broadcasted_iota(jnp.int32, (tile, tile), 0)
        c = lax.broadcasted_iota(jnp.int32, (tile, tile), 1)
        tri = (c >= r).astype(F32)
        lane = lax.broadcasted_iota(jnp.int32, (1, LANES), 1)
        dc = jnp.zeros((tile, LANES), F32)
        for p in range(N_PAIRS):
            blk = rs_ref[:, LANES * p:LANES * (p + 1)] - cs_ref[:, LANES * p:LANES * (p + 1)]
            dc = jnp.where(lane == 2 * p, blk, dc)
            dc = jnp.where(lane == 2 * p + 1, pltpu.roll(blk, FOX_HEAD_DIM, axis=1), dc)
        dl = jnp.dot(tri, dc, precision=lax.Precision.HIGHEST, preferred_element_type=F32) + carry[...]
        carry[...] = dl[0:1, :]
        dfl = dl * _sigmoid(-(fl_ref[...] + b_ref[...]))
        dfl_ref[...] = dfl.astype(BF16)
        db_ref[...] += jnp.sum(dfl, axis=0, keepdims=True)

    rev = lambda i: (n_t - 1 - i, 0)
    return pl.pallas_call(
        body, name=name, grid=(n_t,),
        in_specs=[pl.BlockSpec((tile, FOX_WIDTH), rev), pl.BlockSpec((tile, FOX_WIDTH), rev), pl.BlockSpec((tile, LANES), rev),
                  _full((1, LANES))],
        out_specs=[pl.BlockSpec((tile, LANES), rev), _full((1, LANES))],
        out_shape=[jax.ShapeDtypeStruct((s_len, LANES), BF16), jax.ShapeDtypeStruct((1, LANES), F32)],
        scratch_shapes=[pltpu.VMEM((1, LANES), F32)],
        compiler_params=_params("arbitrary"),
    )(rs, cs, fl, b)


SUBLANES = 8
CONV_ROWS = 64


def _phase_copies(src, dst, rows):
    for p in range(SUBLANES):
        dst[p] = src[pl.ds(p, rows), :]


def _phase_rows(extp_ref, off, r0, rows):
    p = off % SUBLANES
    return extp_ref[p, pl.ds(pl.multiple_of(r0 + (off - p), SUBLANES), rows), :]


def _conv_taps(w_ref, extp_ref, base, r0, rows, reverse):
    acc = None
    for k in range(CONV_WIDTH):
        off = base + ((CONV_WIDTH - 1 - k) if reverse else k)
        term = w_ref[k:k + 1, :] * _phase_rows(extp_ref, off, r0, rows)
        acc = term if acc is None else acc + term
    return acc


def _fold_rows(x):
    out = x[0:SUBLANES]
    for i in range(1, x.shape[0] // SUBLANES):
        out = out + x[i * SUBLANES:(i + 1) * SUBLANES]
    return out


def _conv_fwd(ag, w, cb, lg, lb, tile, name):
    s_len = ag.shape[0]
    c = CONV_CH
    rb = tile

    def body(ag_ref, w_ref, cb_ref, lg_ref, lb_ref, u_ref, u1_ref, ext, extp):
        @pl.when(pl.program_id(0) == 0)
        def _():
            ext[0:CONV_PAD, :] = jnp.zeros((CONV_PAD, c), F32)
            ext[tile + CONV_PAD:tile + CONV_PAD + SUBLANES, :] = jnp.zeros((SUBLANES, c), F32)

        ext[CONV_PAD:CONV_PAD + tile, :] = ag_ref[:, 0:c] * _sigmoid(ag_ref[:, c:2 * c])
        _phase_copies(ext, extp, tile + CONV_PAD)

        def block(b, carry):
            r0 = pl.multiple_of(b * rb, rb)
            u1 = _conv_taps(w_ref, extp, CONV_PAD - (CONV_WIDTH - 1), r0, rb, False) + cb_ref[...]
            u1_ref[pl.ds(r0, rb), :] = u1
            mu = jnp.mean(u1, axis=-1, keepdims=True)
            xc = u1 - mu
            y = xc * lax.rsqrt(jnp.mean(xc * xc, axis=-1, keepdims=True) + EPS) * lg_ref[...] + lb_ref[...]
            u_ref[pl.ds(r0, rb), :] = (y * _sigmoid(y)).astype(BF16)
            return carry

        lax.fori_loop(0, tile // rb, block, 0)
        ext[0:CONV_PAD, :] = ext[tile:tile + CONV_PAD, :]

    return pl.pallas_call(
        body, name=name, grid=(s_len // tile,),
        in_specs=[pl.BlockSpec((tile, 2 * c), lambda i: (i, 0)), _full((CONV_PAD, c)), _full((1, c)), _full((1, c)),
                  _full((1, c))],
        out_specs=[pl.BlockSpec((tile, c), lambda i: (i, 0))] * 2,
        out_shape=[jax.ShapeDtypeStruct((s_len, c), BF16), jax.ShapeDtypeStruct((s_len, c), F32)],
        scratch_shapes=[pltpu.VMEM((tile + CONV_PAD + SUBLANES, c), F32), pltpu.VMEM((SUBLANES, tile + CONV_PAD, c), F32)],
        compiler_params=_params("arbitrary"),
    )(ag, w, cb, lg, lb)


def _conv_bwd(ag, u1, dcat, w, lg, lb, tile, name):
    s_len = ag.shape[0]
    c = CONV_CH
    n_t = s_len // tile
    per = tile // CONV_PAD
    rb = min(CONV_ROWS, tile)

    def body(ag_ref, halo_ref, u1_ref, du_ref, w_ref, lg_ref, lb_ref, dag_ref, dw_ref, dv_ref, ext, ext2, extp, dwacc):
        i = pl.program_id(0)
        t = n_t - 1 - i

        @pl.when(i == 0)
        def _():
            ext2[tile:tile + CONV_PAD + SUBLANES, :] = jnp.zeros((CONV_PAD + SUBLANES, c), F32)
            ext[tile + CONV_PAD:tile + CONV_PAD + SUBLANES, :] = jnp.zeros((SUBLANES, c), F32)
            dwacc[...] = jnp.zeros_like(dwacc)
            dv_ref[...] = jnp.zeros_like(dv_ref)

        halo = halo_ref[:, 0:c] * _sigmoid(halo_ref[:, c:2 * c])
        ext[0:CONV_PAD, :] = jnp.where(t > 0, halo, 0.0)
        ext[CONV_PAD:CONV_PAD + tile, :] = ag_ref[:, 0:c] * _sigmoid(ag_ref[:, c:2 * c])
        _phase_copies(ext, extp, tile + CONV_PAD)
        base = CONV_PAD - (CONV_WIDTH - 1)

        def block1(b, carry):
            r0 = pl.multiple_of(b * rb, rb)
            u1 = u1_ref[pl.ds(r0, rb), :]
            mu = jnp.mean(u1, axis=-1, keepdims=True)
            xc = u1 - mu
            rs = lax.rsqrt(jnp.mean(xc * xc, axis=-1, keepdims=True) + EPS)
            xhat = xc * rs
            y = xhat * lg_ref[...] + lb_ref[...]
            sy = _sigmoid(y)
            dy = du_ref[pl.ds(r0, rb), :] * (sy * (1.0 + y * (1.0 - sy)))
            dxh = dy * lg_ref[...]
            du1 = rs * (dxh - jnp.mean(dxh, axis=-1, keepdims=True) - xhat * jnp.mean(dxh * xhat, axis=-1, keepdims=True))
            dv_ref[0:1, :] += jnp.sum(du1, axis=0, keepdims=True)
            dv_ref[1:2, :] += jnp.sum(dy * xhat, axis=0, keepdims=True)
            dv_ref[2:3, :] += jnp.sum(dy, axis=0, keepdims=True)
            for k in range(CONV_WIDTH):
                dwacc[k] += _fold_rows(du1 * _phase_rows(extp, base + k, r0, rb))
            ext2[pl.ds(r0, rb), :] = du1
            return carry

        lax.fori_loop(0, tile // rb, block1, 0)
        _phase_copies(ext2, extp, tile + CONV_PAD)

        def block2(b, carry):
            r0 = pl.multiple_of(b * rb, rb)
            du0 = _conv_taps(w_ref, extp, 0, r0, rb, True)
            a = ag_ref[pl.ds(r0, rb), 0:c]
            sg = _sigmoid(ag_ref[pl.ds(r0, rb), c:2 * c])
            dag_ref[pl.ds(r0, rb), 0:c] = (du0 * sg).astype(BF16)
            dag_ref[pl.ds(r0, rb), c:2 * c] = (du0 * a * sg * (1.0 - sg)).astype(BF16)
            return carry

        lax.fori_loop(0, tile // rb, block2, 0)
        ext2[tile:tile + CONV_PAD, :] = ext2[0:CONV_PAD, :]

        @pl.when(i == n_t - 1)
        def _():
            for k in range(CONV_WIDTH):
                dw_ref[k:k + 1, :] = jnp.sum(dwacc[k], axis=0, keepdims=True)
            dw_ref[CONV_WIDTH:CONV_PAD, :] = jnp.zeros((CONV_PAD - CONV_WIDTH, c), F32)

    rev = lambda i: (n_t - 1 - i, 0)
    return pl.pallas_call(
        body, name=name, grid=(n_t,),
        in_specs=[pl.BlockSpec((tile, 2 * c), rev),
                  pl.BlockSpec((CONV_PAD, 2 * c), lambda i: (jnp.maximum((n_t - 1 - i) * per - 1, 0), 0)),
                  pl.BlockSpec((tile, c), rev), pl.BlockSpec((tile, c), rev), _full((CONV_PAD, c)), _full((1, c)), _full((1, c))],
        out_specs=[pl.BlockSpec((tile, 2 * c), rev), _full((CONV_PAD, c)), _full((8, c))],
        out_shape=[jax.ShapeDtypeStruct((s_len, 2 * c), BF16), jax.ShapeDtypeStruct((CONV_PAD, c), F32),
                   jax.ShapeDtypeStruct((8, c), F32)],
        scratch_shapes=[pltpu.VMEM((tile + CONV_PAD + SUBLANES, c), F32), pltpu.VMEM((tile + CONV_PAD + SUBLANES, c), F32),
                        pltpu.VMEM((SUBLANES, tile + CONV_PAD, c), F32), pltpu.VMEM((CONV_PAD, SUBLANES, c), F32)],
        compiler_params=_params("arbitrary"),
    )(ag, ag, u1, dcat, w, lg, lb)


Q_SCALE = FOX_HEAD_DIM ** -0.5


def _head_col(x, lane, h):
    return jnp.sum(jnp.where(lane == h, x, 0.0), axis=1, keepdims=True)


def _split3(x):
    hi = x.astype(BF16).astype(F32)
    r = x - hi
    mid = r.astype(BF16).astype(F32)
    lo = (r - mid).astype(BF16).astype(F32)
    return hi, mid, lo


def _in_lanes(lane, lo, n):
    return (lane >= lo) & (lane < lo + n)


def _put3(lane, lo, parts, rest):
    return jnp.where(lane == lo, parts[0], jnp.where(lane == lo + 1, parts[1], jnp.where(lane == lo + 2, parts[2], rest)))


def _spare_lane(h):
    return FOX_HEAD_DIM if h % 2 == 0 else 0


def _shift_div(i, num, den):
    return i * (num // den) if num >= den else lax.shift_right_logical(i, (den // num).bit_length() - 1)


def _foxa_prep(qkv, fl, b, tile, name):
    s_len = qkv.shape[0]

    def body(q_ref, k_ref, v_ref, fl_ref, b_ref, qa_ref, ka_ref, va_ref, carry):
        @pl.when(pl.program_id(0) == 0)
        def _():
            carry[...] = jnp.zeros_like(carry)

        xx = fl_ref[...] + b_ref[...]
        lf = jnp.minimum(xx, 0.0) - jnp.log1p(jnp.exp(-jnp.abs(xx)))
        tri = (lax.broadcasted_iota(jnp.int32, (tile, tile), 1) <= lax.broadcasted_iota(jnp.int32, (tile, tile), 0))
        cum_t = jnp.dot(tri.astype(F32), lf, precision=lax.Precision.HIGHEST, preferred_element_type=F32) + carry[...]
        carry[...] = cum_t[tile - 1:tile, :]
        lane = lax.broadcasted_iota(jnp.int32, (1, LANES), 1)
        for h in range(FOX_HEADS):
            e = _spare_lane(h)
            head = ~_in_lanes(lane, e, FOX_HEAD_DIM)
            blk = slice(LANES * (h // 2), LANES * (h // 2) + LANES)
            out = slice(LANES * h, LANES * h + LANES)
            c3 = _split3(_head_col(cum_t, lane, h))
            ex_q = _put3(lane, e, c3, jnp.where(_in_lanes(lane, e + 3, 3), 1.0, 0.0))
            qa_ref[:, out] = jnp.where(head, q_ref[:, blk].astype(F32) * Q_SCALE, ex_q).astype(BF16)
            ones = jnp.where(_in_lanes(lane, e, 3) | _in_lanes(lane, e + 6, 3), 1.0, 0.0)
            ex_k = _put3(lane, e + 3, [-c for c in c3], ones)
            ka_ref[:, out] = jnp.where(head, k_ref[:, blk].astype(F32), ex_k).astype(BF16)
            ex_v = jnp.where(_in_lanes(lane, e, 3), 1.0, 0.0)
            va_ref[:, out] = jnp.where(head, v_ref[:, blk].astype(F32), ex_v).astype(BF16)

    col = lambda c: pl.BlockSpec((tile, FOX_WIDTH), lambda i: (i, c))
    wide = pl.BlockSpec((tile, 2 * FOX_WIDTH), lambda i: (i, 0))
    return pl.pallas_call(
        body, name=name, grid=(s_len // tile,),
        in_specs=[col(0), col(1), col(2), pl.BlockSpec((tile, LANES), lambda i: (i, 0)), _full((1, LANES))],
        out_specs=[wide, wide, wide], out_shape=[jax.ShapeDtypeStruct((s_len, 2 * FOX_WIDTH), BF16)] * 3,
        scratch_shapes=[pltpu.VMEM((1, LANES), F32)],
        compiler_params=_params("arbitrary"),
    )(qkv, qkv, qkv, fl, b)


def _foxa_fwd(qa, ka, va, tq, tk, name, host=None):
    s_len = qa.shape[0]

    def body(q_ref, k_ref, v_ref, o_ref, lse_ref, ob_ref):
        i = pl.program_id(1)
        lane = lax.broadcasted_iota(jnp.int32, (1, LANES), 1)
        cols = [slice(LANES * hh, LANES * hh + LANES) for hh in range(2)]
        qh = [q_ref[:, c] for c in cols]

        def scores(j):
            off = pl.multiple_of(j * tk, tk)
            return [lax.dot_general(qh[hh], k_ref[pl.ds(off, tk), cols[hh]], NT_DIMS, preferred_element_type=F32)
                    for hh in range(2)]

        def update(j, s, m, acc, mask):
            off = pl.multiple_of(j * tk, tk)
            m_out, acc_out = [], []
            for hh in range(2):
                sh = s[hh] if mask is None else jnp.where(mask, s[hh], NEG_INF)
                m_new = jnp.maximum(m[hh], jnp.max(sh, axis=1, keepdims=True))
                pr = jnp.exp(sh - m_new).astype(BF16)
                acc_out.append(jnp.exp(m[hh] - m_new) * acc[hh]
                               + jnp.dot(pr, v_ref[pl.ds(off, tk), cols[hh]], preferred_element_type=F32))
                m_out.append(m_new)
            return m_out, acc_out

        def step(j, carry):
            s_next = scores(j + 1)
            m, acc = update(j, carry[0:2], carry[2:4], carry[4:6], None)
            return (*s_next, *m, *acc)

        n_full = _shift_div(i, tq, tk)
        n_part = max(tq // tk, 1)
        qi = lax.broadcasted_iota(jnp.int32, (tq, tk), 0) + i * tq
        ki = lax.broadcasted_iota(jnp.int32, (tq, tk), 1)
        carry = (*scores(0), *([jnp.full((tq, 1), NEG_INF, F32)] * 2), *([jnp.zeros((tq, LANES), F32)] * 2))
        carry = lax.fori_loop(0, n_full, step, carry)
        s, m, acc = list(carry[0:2]), list(carry[2:4]), list(carry[4:6])
        for jj in range(n_part):
            s_next = scores(n_full + jj + 1) if jj < n_part - 1 else None
            m, acc = update(n_full + jj, s, m, acc, ki + (n_full + jj) * tk <= qi)
            s = s_next
        res = []
        for hh in range(2):
            l = acc[hh][:, _spare_lane(hh):_spare_lane(hh) + 1]
            res.append((acc[hh] / l, m[hh] + jnp.log(l)))
        low = lane < FOX_HEAD_DIM
        o_pair = jnp.where(low, res[0][0], res[1][0])
        o_ref[...] = o_pair
        ob_ref[...] = o_pair.astype(BF16)
        lse_ref[...] = jnp.where(low, res[0][1], res[1][1])

    pair = pl.BlockSpec((s_len, 2 * LANES), lambda p, i: (0, p))
    out = pl.BlockSpec((tq, LANES), lambda p, i: (i, p))
    return _pcall(
        body, name=name, grid=(N_PAIRS, s_len // tq),
        in_specs=[pl.BlockSpec((tq, 2 * LANES), lambda p, i: (i, p)), pair, pair],
        out_specs=[out, out, out],
        out_shape=[jax.ShapeDtypeStruct((s_len, FOX_WIDTH), F32)] * 2 + [jax.ShapeDtypeStruct((s_len, FOX_WIDTH), BF16)],
        args=[qa, ka, va], sem=("parallel", "parallel"), host=host)


def _foxa_dq(qa, lse, o, dcat, ka, va, tq, tk, name, host=None):
    s_len = qa.shape[0]

    def body(qa_ref, lse_ref, o_ref, dcat_ref, k_ref, v_ref, dq_ref, rs_ref, q_ref, do_ref):
        i = pl.program_id(1)
        lane = lax.broadcasted_iota(jnp.int32, (1, LANES), 1)
        cols = [slice(LANES * hh, LANES * hh + LANES) for hh in range(2)]
        d_o = dcat_ref[...]
        prod = d_o * o_ref[...]
        for hh in range(2):
            e = _spare_lane(hh)
            head = ~_in_lanes(lane, e, FOX_HEAD_DIM)
            delta = jnp.sum(jnp.where(head, prod, 0.0), axis=1, keepdims=True)
            do_ref[:, cols[hh]] = _put3(lane, e, _split3(-delta), jnp.where(head, d_o, 0.0)).astype(BF16)
            l3 = _split3(-lse_ref[:, FOX_HEAD_DIM - e:FOX_HEAD_DIM - e + 1])
            q_ref[:, cols[hh]] = _put3(lane, e + 6, l3, qa_ref[:, cols[hh]].astype(F32)).astype(BF16)
        qh = [q_ref[:, c] for c in cols]
        doh = [do_ref[:, c] for c in cols]

        def update(j, acc, mask):
            off = pl.multiple_of(j * tk, tk)
            out = []
            for hh in range(2):
                kt = k_ref[pl.ds(off, tk), cols[hh]]
                pr = jnp.exp(lax.dot_general(qh[hh], kt, NT_DIMS, preferred_element_type=F32))
                if mask is not None:
                    pr = jnp.where(mask, pr, 0.0)
                ds = pr * lax.dot_general(doh[hh], v_ref[pl.ds(off, tk), cols[hh]], NT_DIMS, preferred_element_type=F32)
                out.append(acc[hh] + jnp.dot(ds.astype(BF16), kt, preferred_element_type=F32))
            return tuple(out)

        n_full = _shift_div(i, tq, tk)
        qi = lax.broadcasted_iota(jnp.int32, (tq, tk), 0) + i * tq
        ki = lax.broadcasted_iota(jnp.int32, (tq, tk), 1)
        acc = lax.fori_loop(0, n_full, lambda j, a: update(j, a, None), (jnp.zeros((tq, LANES), F32),) * 2)
        for jj in range(max(tq // tk, 1)):
            acc = update(n_full + jj, acc, ki + (n_full + jj) * tk <= qi)
        low = lane < FOX_HEAD_DIM
        dq_ref[...] = (jnp.where(low, acc[0], acc[1]) * Q_SCALE).astype(BF16)
        rs_ref[...] = jnp.where(low, acc[0][:, _spare_lane(0):_spare_lane(0) + 1], acc[1][:, _spare_lane(1):_spare_lane(1) + 1])

    pair = pl.BlockSpec((s_len, 2 * LANES), lambda p, i: (0, p))
    tile2 = pl.BlockSpec((tq, 2 * LANES), lambda p, i: (i, p))
    out = pl.BlockSpec((tq, LANES), lambda p, i: (i, p))
    wide = jax.ShapeDtypeStruct((s_len, 2 * FOX_WIDTH), BF16)
    return _pcall(
        body, name=name, grid=(N_PAIRS, s_len // tq),
        in_specs=[tile2, out, out, pl.BlockSpec((tq, LANES), lambda p, i: (i, N_PAIRS + p)), pair, pair],
        out_specs=[out, out, tile2, tile2],
        out_shape=[jax.ShapeDtypeStruct((s_len, FOX_WIDTH), BF16), jax.ShapeDtypeStruct((s_len, FOX_WIDTH), F32), wide, wide],
        args=[qa, lse, o, dcat, ka, va], sem=("parallel", "parallel"), host=host)


def _foxa_dkv(qb, ka, va, doa, tq, tk, name, host=None):
    s_len = qb.shape[0]
    n_q = s_len // tq

    def body(k_ref, v_ref, q_ref, do_ref, dk_ref, dv_ref, cs_ref):
        j = pl.program_id(1)
        lane = lax.broadcasted_iota(jnp.int32, (1, LANES), 1)
        cols = [slice(LANES * hh, LANES * hh + LANES) for hh in range(2)]
        kh = [k_ref[:, c] for c in cols]
        vh = [v_ref[:, c] for c in cols]

        def update(i, acc, mask):
            off = pl.multiple_of(i * tq, tq)
            out = []
            for hh in range(2):
                qt = q_ref[pl.ds(off, tq), cols[hh]]
                dot = do_ref[pl.ds(off, tq), cols[hh]]
                pt = jnp.exp(lax.dot_general(kh[hh], qt, NT_DIMS, preferred_element_type=F32))
                if mask is not None:
                    pt = jnp.where(mask, pt, 0.0)
                dv = acc[2 * hh + 1] + jnp.dot(pt.astype(BF16), dot, preferred_element_type=F32)
                dst = pt * lax.dot_general(vh[hh], dot, NT_DIMS, preferred_element_type=F32)
                out += [acc[2 * hh] + jnp.dot(dst.astype(BF16), qt, preferred_element_type=F32), dv]
            return tuple(out)

        i0 = _shift_div(j, tk, tq)
        n_part = max(tk // tq, 1)
        ki = lax.broadcasted_iota(jnp.int32, (tk, tq), 0) + j * tk
        qi = lax.broadcasted_iota(jnp.int32, (tk, tq), 1)
        acc = (jnp.zeros((tk, LANES), F32),) * 4
        for ii in range(n_part):
            acc = update(i0 + ii, acc, ki <= qi + (i0 + ii) * tq)
        acc = lax.fori_loop(i0 + n_part, n_q, lambda i, a: update(i, a, None), acc)
        low = lane < FOX_HEAD_DIM
        dk_ref[...] = jnp.where(low, acc[0], acc[2]).astype(BF16)
        dv_ref[...] = jnp.where(low, acc[1], acc[3]).astype(BF16)
        cs_ref[...] = jnp.where(low, acc[0][:, _spare_lane(0) + 3:_spare_lane(0) + 4],
                                acc[2][:, _spare_lane(1) + 3:_spare_lane(1) + 4])

    pair = pl.BlockSpec((s_len, 2 * LANES), lambda p, j: (0, p))
    tile2 = pl.BlockSpec((tk, 2 * LANES), lambda p, j: (j, p))
    out = pl.BlockSpec((tk, LANES), lambda p, j: (j, p))
    return _pcall(
        body, name=name, grid=(N_PAIRS, s_len // tk), in_specs=[tile2, tile2, pair, pair], out_specs=[out, out, out],
        out_shape=[jax.ShapeDtypeStruct((s_len, FOX_WIDTH), BF16), jax.ShapeDtypeStruct((s_len, FOX_WIDTH), BF16),
                   jax.ShapeDtypeStruct((s_len, FOX_WIDTH), F32)],
        args=[ka, va, qb, doa], sem=("parallel", "parallel"), host=host)


def _mem_scores_t(q, kv, h):
    lo = h * MEM_HEAD_DIM
    st = lax.dot_general(kv[:, lo:lo + MEM_HEAD_DIM], q[:, lo:lo + MEM_HEAD_DIM], NT_DIMS,
                         preferred_element_type=F32) * (MEM_HEAD_DIM ** -0.5)
    e = jnp.exp(st - jnp.max(st, axis=0, keepdims=True))
    return e / jnp.sum(e, axis=0, keepdims=True)


def _memattn_fwd(q, kv, tile, name):
    s_len = q.shape[0]
    n_mem = kv.shape[0]

    def body(q_ref, kv_ref, o_ref):
        q = q_ref[...]
        kv = kv_ref[...]
        for h in range(MEM_HEADS):
            lo = h * MEM_HEAD_DIM
            pt = _mem_scores_t(q, kv, h).astype(BF16)
            vh = kv[:, MEM_INNER + lo:MEM_INNER + lo + MEM_HEAD_DIM]
            o_ref[:, lo:lo + MEM_HEAD_DIM] = lax.dot_general(pt, vh, TN_DIMS, preferred_element_type=F32).astype(BF16)

    return pl.pallas_call(
        body, name=name, grid=(s_len // tile,),
        in_specs=[pl.BlockSpec((tile, MEM_INNER), lambda i: (i, 0)), _full((n_mem, 2 * MEM_INNER))],
        out_specs=pl.BlockSpec((tile, MEM_INNER), lambda i: (i, 0)),
        out_shape=jax.ShapeDtypeStruct((s_len, MEM_INNER), BF16),
        compiler_params=_params("parallel"),
    )(q, kv)


def _memattn_bwd(q, kv, do, tile, name):
    s_len = q.shape[0]
    n_mem = kv.shape[0]
    scale = MEM_HEAD_DIM ** -0.5

    def body(q_ref, kv_ref, do_ref, dq_ref, dkv_ref):
        @pl.when(pl.program_id(0) == 0)
        def _():
            dkv_ref[...] = jnp.zeros_like(dkv_ref)

        q = q_ref[...]
        kv = kv_ref[...]
        do = do_ref[...]
        for h in range(MEM_HEADS):
            lo = h * MEM_HEAD_DIM
            qh = q[:, lo:lo + MEM_HEAD_DIM]
            kh = kv[:, lo:lo + MEM_HEAD_DIM]
            vh = kv[:, MEM_INNER + lo:MEM_INNER + lo + MEM_HEAD_DIM]
            doh = do[:, lo:lo + MEM_HEAD_DIM]
            pt = _mem_scores_t(q, kv, h)
            dkv_ref[:, MEM_INNER + lo:MEM_INNER + lo + MEM_HEAD_DIM] += jnp.dot(
                pt.astype(BF16), doh, preferred_element_type=F32)
            dpt = lax.dot_general(vh, doh, NT_DIMS, preferred_element_type=F32)
            dst = (pt * (dpt - jnp.sum(pt * dpt, axis=0, keepdims=True)) * scale).astype(BF16)
            dkv_ref[:, lo:lo + MEM_HEAD_DIM] += jnp.dot(dst, qh, preferred_element_type=F32)
            dq_ref[:, lo:lo + MEM_HEAD_DIM] = lax.dot_general(dst, kh, TN_DIMS, preferred_element_type=F32).astype(BF16)

    return pl.pallas_call(
        body, name=name, grid=(s_len // tile,),
        in_specs=[pl.BlockSpec((tile, MEM_INNER), lambda i: (i, 0)), _full((n_mem, 2 * MEM_INNER)),
                  pl.BlockSpec((tile, MEM_INNER), lambda i: (i, 0))],
        out_specs=[pl.BlockSpec((tile, MEM_INNER), lambda i: (i, 0)), _full((n_mem, 2 * MEM_INNER))],
        out_shape=[jax.ShapeDtypeStruct((s_len, MEM_INNER), BF16), jax.ShapeDtypeStruct((n_mem, 2 * MEM_INNER), F32)],
        compiler_params=_params("arbitrary"),
    )(q, kv, do)


def _loss_head(y, target, tile, name):
    s_len, d = y.shape

    def body(y_ref, t_ref, dy_ref, l_ref):
        @pl.when(pl.program_id(0) == 0)
        def _():
            l_ref[...] = jnp.zeros_like(l_ref)

        err = y_ref[...] - t_ref[...]
        dy_ref[...] = err * (1.0 / d)
        l_ref[...] += jnp.sum(err * err, axis=0, keepdims=True) * (0.5 / d)

    row = lambda i: (i, 0)
    return pl.pallas_call(
        body, name=name, grid=(s_len // tile,),
        in_specs=[pl.BlockSpec((tile, d), row), pl.BlockSpec((tile, d), row)],
        out_specs=[pl.BlockSpec((tile, d), row), _full((1, d))],
        out_shape=[jax.ShapeDtypeStruct((s_len, d), F32), jax.ShapeDtypeStruct((1, d), F32)],
        compiler_params=_params("arbitrary"),
    )(y, target)


def _attn_tile(s_len):
    return min(256, s_len // 2)


REST = ("w_out", "w_mq", "w_mk", "w_mv", "w_mo", "w_up", "w_down")
REST_DQ = ("w_up", "w_down")
REST_DKV = ("w_out", "w_mq", "w_mk", "w_mv", "w_mo")
SHARD_AXIS = {"w_in": 1, "w_out": 0, "w_mq": 0, "w_mk": 0, "w_mv": 0, "w_mo": 1, "w_up": 1, "w_down": 0}


def _full_from_shards(sh, axis):
    n, r, c = sh.shape
    if axis == 0:
        return sh.reshape(n * r, c)
    return sh.transpose(1, 0, 2).reshape(r, n * c)


def _rest_weights(lands):
    w = {n: _full_from_shards(sh, SHARD_AXIS[n]) for n, sh in zip(REST, lands)}
    w["w_mkv"] = jnp.concatenate([w.pop("w_mk"), w.pop("w_mv")], axis=1)
    return w


def _w_in_cat(land):
    return jnp.pad(land.reshape(-1, land.shape[2]), ((0, IN_CAT - IN_COLS), (0, 0)))


def _layer_fwd(x0, mem, w, l, rest_src=None, next_src=None):
    s_len = x0.shape[0]
    tile = min(512, s_len)
    tile_ff = min(512, s_len)
    ta = _attn_tile(s_len)
    ident = lambda z: z
    sv = {"x0": x0}

    h1, ag, qkv, fl = _rms_matmul(
        x0, w["norm_mix_pre"], w["w_in_cat"],
        [(0, 2 * CONV_CH, [(F32, ident)]), (2 * CONV_CH, IN_MAIN, [(BF16, ident)]), (IN_MAIN, IN_CAT, [(F32, ident)])],
        tile, f"mix_in_{l}", w_t=True)
    u3, u1 = _conv_fwd(ag, w["conv_w"], w["conv_b"], w["conv_ln_g"], w["conv_ln_b"], tile, f"conv_fwd_{l}")
    qa, ka, va = _foxa_prep(qkv, fl, w["b_forget"], tile, f"fox_prep_{l}")
    next_land = None
    if rest_src is None:
        o, lse, o_bf = _foxa_fwd(qa, ka, va, ta, 2 * ta, f"fox_fwd_{l}")
    else:
        srcs = list(rest_src) + list(next_src or [])
        (o, lse, o_bf), lands = _foxa_fwd(qa, ka, va, ta, 2 * ta, f"fox_fwd_{l}", host=(srcs, [True] * len(srcs)))
        w = {**w, **_rest_weights(lands[:len(REST)])}
        next_land = lands[len(REST):]
    cat = jnp.concatenate([u3, o_bf], axis=1)
    y1, x1 = _matmul_resnorm(cat, w["w_out"], x0, w["norm_mix_post"], tile, f"mix_out_{l}")
    sv.update(h1=h1, ag=ag, u1=u1, fl=fl, qa=qa, ka=ka, va=va, o=o, lse=lse, cat=cat, y1=y1, x1=x1)

    h2, qm = _rms_matmul(x1, w["norm_mem_pre"], w["w_mq"], [(0, MEM_INNER, [(BF16, ident)])], tile, f"mem_q_{l}")
    mem_n, kv = _rms_matmul(mem, w["norm_memkv"], w["w_mkv"], [(0, 2 * MEM_INNER, [(BF16, ident)])],
                            mem.shape[0], f"mem_kv_{l}")
    om = _memattn_fwd(qm, kv, tile, f"mem_attn_fwd_{l}")
    y2, x2 = _matmul_resnorm(om, w["w_mo"], x1, w["norm_mem_post"], tile, f"mem_out_{l}")
    sv.update(h2=h2, qm=qm, mem_n=mem_n, kv=kv, om=om, y2=y2, x2=x2)

    relu2 = lambda z: jnp.square(jnp.maximum(z, 0.0))
    h3, pre, hid = _rms_matmul(x2, w["norm_mlp_pre"], w["w_up"], [(0, D_FF, [(BF16, ident), (BF16, relu2)])], tile_ff,
                               f"mlp_up_{l}")
    y3, x3 = _matmul_resnorm(hid, w["w_down"], x2, w["norm_mlp_post"], tile_ff, f"mlp_down_{l}")
    sv.update(h3=h3, pre=pre, hid=hid, y3=y3)
    return x3, sv, w, next_land


def _layer_bwd(dx3, mem, w, sv, l, scatter_rest=False, dkv_src=None, scatter_w_in=False):
    s_len = dx3.shape[0]
    tile = min(512, s_len)
    tile_ff = min(512, s_len)
    ta = _attn_tile(s_len)
    tk = min(2048, s_len)
    tk_in = min(1024, s_len)
    tk_all = s_len
    n_mem = mem.shape[0]
    g = {}

    dy3, dpre, g["norm_mlp_post"] = _resnorm_bwd_mm(dx3, sv["y3"], w["norm_mlp_post"], w["w_down"], tile_ff,
                                                    f"mlp_down_bwd_{l}", BF16, pre=sv["pre"])
    g["w_down"] = _matmul_tn_shards(sv["hid"], dy3, 0, tk_all, f"dw_down_{l}")
    dx2, g["norm_mlp_pre"] = _mm_prenorm_bwd(dpre, w["w_up"], sv["x2"], w["norm_mlp_pre"], dx3, tile_ff,
                                             f"mlp_up_bwd_{l}")
    g["w_up"] = _matmul_tn_shards(sv["h3"], dpre, 1, tk, f"dw_up_{l}")

    dy2, dom, g["norm_mem_post"] = _resnorm_bwd_mm(dx2, sv["y2"], w["norm_mem_post"], w["w_mo"], tile,
                                                   f"mem_out_bwd_{l}", BF16)
    g["w_mo"] = _matmul_tn_shards(sv["om"], dy2, 1, tk_all, f"dw_mo_{l}")
    dqm, dkv = _memattn_bwd(sv["qm"], sv["kv"], dom, tile, f"mem_attn_bwd_{l}")
    dkvb = dkv.astype(BF16)
    g["w_mq"] = _matmul_tn_shards(sv["h2"], dqm, 0, tk_all, f"dw_mq_{l}")
    dx1, g["norm_mem_pre"] = _mm_prenorm_bwd(dqm, w["w_mq"], sv["x1"], w["norm_mem_pre"], dx2, tile, f"mem_q_bwd_{l}")
    _, g["norm_memkv"] = _mm_prenorm_bwd(dkvb, w["w_mkv"], mem, w["norm_memkv"], None, n_mem, f"mem_kv_bwd_{l}")
    g["w_mk"] = _matmul_tn_shards(sv["mem_n"], dkvb[:, :MEM_INNER], 0, n_mem, f"dw_mk_{l}")
    g["w_mv"] = _matmul_tn_shards(sv["mem_n"], dkvb[:, MEM_INNER:], 0, n_mem, f"dw_mv_{l}")

    dy1, dcat, g["norm_mix_post"] = _resnorm_bwd_mm(dx1, sv["y1"], w["norm_mix_post"], w["w_out"], tile,
                                                    f"mix_out_bwd_{l}", F32)
    g["w_out"] = _matmul_tn_shards(sv["cat"], dy1, 0, tk_all, f"dw_out_{l}")
    dq_args = (sv["qa"], sv["lse"], sv["o"], dcat, sv["ka"], sv["va"], ta, 2 * ta, f"fox_dq_{l}")
    rest_land, dkv_land = None, None
    if scatter_rest:
        with_dq = [g[n] for n in REST_DQ]
        (dq, rs, qb, doa), land_dq = _foxa_dq(*dq_args, host=(with_dq, [False] * len(with_dq)))
        with_dkv = [g[n] for n in REST_DKV] + list(dkv_src or [])
        (dk, dv, cs), land_dkv = _foxa_dkv(qb, sv["ka"], sv["va"], doa, 2 * ta, ta, f"fox_dkv_{l}",
                                           host=(with_dkv, [False] * len(with_dkv)))
        by_name = dict(zip(REST_DQ + REST_DKV, land_dq + land_dkv))
        rest_land = [by_name[n] for n in REST]
        dkv_land = land_dkv[len(REST_DKV):]
    else:
        dq, rs, qb, doa = _foxa_dq(*dq_args)
        dk, dv, cs = _foxa_dkv(qb, sv["ka"], sv["va"], doa, 2 * ta, ta, f"fox_dkv_{l}")
    dfl, db = _cumsum_bwd(rs, cs, sv["fl"], w["b_forget"], tile, f"cumsum_bwd_{l}")
    g["b_forget"] = db[:, :FOX_HEADS]
    dag, dconv_w, dconv_v = _conv_bwd(sv["ag"], sv["u1"], dcat, w["conv_w"], w["conv_ln_g"], w["conv_ln_b"], tile,
                                      f"conv_bwd_{l}")
    g["conv_w"] = dconv_w[:CONV_WIDTH]
    g["conv_b"], g["conv_ln_g"], g["conv_ln_b"] = dconv_v[0:1], dconv_v[1:2], dconv_v[2:3]
    dz = [dag, dq, dk, dv, dfl]
    dw_in_t = _matmul_tn_rows(dz, sv["h1"], tk_in, f"dw_in_{l}")[:IN_COLS]
    g["w_in"] = dw_in_t.reshape(N_DEV, IN_COLS // N_DEV, D_MODEL).astype(BF16)
    res = _mm_prenorm_bwd(dz, w["w_in_cat"], sv["x0"], w["norm_mix_pre"], dx1, tile, f"mix_in_bwd_{l}",
                          host=([g["w_in"]], [False]) if scatter_w_in else None, w_t=True)
    (dx0, g["norm_mix_pre"]), w_in_land = res if scatter_w_in else (res, None)
    return dx0, g, rest_land, dkv_land, w_in_land


def _sum_blocks(a, name):
    n, rows, cols = a.shape

    def body(a_ref, o_ref):
        acc = a_ref[0]
        for j in range(1, n):
            acc = acc + a_ref[j]
        o_ref[...] = acc

    return pl.pallas_call(
        body, name=name, in_specs=[_full((n, rows, cols))], out_specs=_full((rows, cols)),
        out_shape=jax.ShapeDtypeStruct((rows, cols), F32), grid=(1,),
    )(a)


def _adamw(gparts, w, m, v, tile, name):
    n_l, rows, cols = w.shape
    n = gparts[0].shape[0]
    n_t = rows // tile
    c1 = 1.0 - ADAM_B1
    c2 = 1.0 - ADAM_B2
    bc1 = 1.0 - ADAM_B1 ** ADAM_STEP
    bc2 = 1.0 - ADAM_B2 ** ADAM_STEP

    def body(*refs):
        gp_refs, (w_ref, m_ref, v_ref, g_ref, d_ref, mo_ref, vo_ref) = refs[:n_l], refs[n_l:]
        layer = pl.program_id(0)
        g = None
        for l, gp_ref in enumerate(gp_refs):
            gl = gp_ref[0].astype(F32)
            for j in range(1, n):
                gl = gl + gp_ref[j].astype(F32)
            g = gl if g is None else jnp.where(layer == l, gl, g)
        g_ref[...] = g
        m_new = ADAM_B1 * m_ref[...] + c1 * g
        v_new = ADAM_B2 * v_ref[...] + c2 * (g * g)
        mo_ref[...] = m_new
        vo_ref[...] = v_new
        d_ref[...] = -ADAM_LR * ((m_new / bc1) / (jnp.sqrt(v_new / bc2) + ADAM_EPS) + ADAM_WD * w_ref[...])

    def gp_spec(l):
        return pl.BlockSpec((n, tile, cols), lambda L, i: (0, jnp.where(L < l, 0, jnp.where(L > l, n_t - 1, i)), 0))

    spec = pl.BlockSpec((None, tile, cols), lambda L, i: (L, i, 0))
    return pl.pallas_call(
        body, name=name, grid=(n_l, n_t),
        in_specs=[gp_spec(l) for l in range(n_l)] + [spec, spec, spec],
        out_specs=[spec] * 4, out_shape=[jax.ShapeDtypeStruct((n_l, rows, cols), F32)] * 4,
        compiler_params=_params("arbitrary", "arbitrary"),
    )(*gparts, w, m, v)


def _adamw_cols(gparts, w, m, v, name):
    n, rows, cols = gparts[0].shape
    n_l = len(gparts)
    c1 = 1.0 - ADAM_B1
    c2 = 1.0 - ADAM_B2
    bc1 = 1.0 - ADAM_B1 ** ADAM_STEP
    bc2 = 1.0 - ADAM_B2 ** ADAM_STEP

    def body(*refs):
        gp_refs, (w_ref, m_ref, v_ref, g_ref, d_ref, mo_ref, vo_ref) = refs[:n_l], refs[n_l:]
        layer = pl.program_id(0)
        g = None
        for l, gp_ref in enumerate(gp_refs):
            gl = gp_ref[0].astype(F32)
            for j in range(1, n):
                gl = gl + gp_ref[j].astype(F32)
            g = gl if g is None else jnp.where(layer == l, gl, g)
        g_ref[...] = g
        m_new = ADAM_B1 * m_ref[...] + c1 * g
        v_new = ADAM_B2 * v_ref[...] + c2 * (g * g)
        mo_ref[...] = m_new
        vo_ref[...] = v_new
        d_ref[...] = -ADAM_LR * ((m_new / bc1) / (jnp.sqrt(v_new / bc2) + ADAM_EPS) + ADAM_WD * w_ref[...])

    spec = pl.BlockSpec((rows, cols), lambda L: (0, L))
    return pl.pallas_call(
        body, name=name, grid=(n_l,),
        in_specs=[_full((n, rows, cols))] * n_l + [spec, spec, spec],
        out_specs=[spec] * 4, out_shape=[jax.ShapeDtypeStruct((rows, n_l * cols), F32)] * 4,
        compiler_params=_params("arbitrary"),
    )(*gparts, w, m, v)


def _pack_rows(parts, total_rows):
    flat = [p.reshape(-1, D_MODEL) for p in parts]
    used = sum(f.shape[0] for f in flat)
    if total_rows > used:
        flat.append(jnp.zeros((total_rows - used, D_MODEL), flat[0].dtype))
    return jnp.concatenate(flat, axis=0)


def kernel(x, mem, norm_mix_pre, norm_mix_post, w_in, b_forget, conv_w, conv_b, conv_ln_g, conv_ln_b, w_out, norm_mem_pre, norm_mem_post, norm_memkv, w_mq, w_mk, w_mv, w_mo, norm_mlp_pre, norm_mlp_post, w_up, w_down, loss_target, m_norm_mix_pre, m_norm_mix_post, m_w_in, m_b_forget, m_conv_w, m_conv_b, m_conv_ln_g, m_conv_ln_b, m_w_out, m_norm_mem_pre, m_norm_mem_post, m_norm_memkv, m_w_mq, m_w_mk, m_w_mv, m_w_mo, m_norm_mlp_pre, m_norm_mlp_post, m_w_up, m_w_down, v_norm_mix_pre, v_norm_mix_post, v_w_in, v_b_forget, v_conv_w, v_conv_b, v_conv_ln_g, v_conv_ln_b, v_w_out, v_norm_mem_pre, v_norm_mem_post, v_norm_memkv, v_w_mq, v_w_mk, v_w_mv, v_w_mo, v_norm_mlp_pre, v_norm_mlp_post, v_w_up, v_w_down):
    p = dict(locals())
    names = ("norm_mix_pre", "norm_mix_post", "w_in", "b_forget", "conv_w", "conv_b", "conv_ln_g", "conv_ln_b", "w_out",
             "norm_mem_pre", "norm_mem_post", "norm_memkv", "w_mq", "w_mk", "w_mv", "w_mo", "norm_mlp_pre",
             "norm_mlp_post", "w_up", "w_down")
    me = 4 * lax.axis_index("x") + 2 * lax.axis_index("y") + lax.axis_index("c")
    conv_cols = conv_w.shape[2]
    s_len = x.shape[1]
    bf = {n: p[n].astype(BF16) for n in REST}
    in_cols = w_in.shape[2]
    w_in_t = {pre: jnp.transpose(p[pre + "w_in"], (2, 0, 1)).reshape(in_cols, DEPTH * D_MODEL) for pre in ("", "m_", "v_")}
    bf_in = w_in_t[""].astype(BF16)
    bf["w_in"] = [bf_in[:, l * D_MODEL:(l + 1) * D_MODEL] for l in range(DEPTH)]

    conv_pack = _pack_rows([jnp.pad(conv_w, ((0, 0), (0, CONV_PAD - CONV_WIDTH), (0, 0)))], 8)
    win_land, conv_land = _exchange([bf["w_in"][0], conv_pack], [True, True], "gather_first")
    conv_rows = DEPTH * CONV_PAD * conv_cols // D_MODEL
    conv_full = conv_land[:, :conv_rows].reshape(N_DEV, DEPTH, CONV_PAD, conv_cols)
    conv_full = conv_full.transpose(1, 2, 0, 3).reshape(DEPTH, CONV_PAD, N_DEV * conv_cols)
    b_forget_pad = jnp.pad(b_forget, ((0, 0), (0, LANES - FOX_HEADS)))

    def first_weights(l, land):
        w = {"w_in_cat": _w_in_cat(land), "conv_w": conv_full[l], "b_forget": b_forget_pad[l:l + 1]}
        for n in VEC[:-1]:
            w[n] = p[n][l:l + 1]
        return w

    h, sv0, w0, win1_land = _layer_fwd(x[0], mem[0], first_weights(0, win_land), 0,
                                       rest_src=[bf[n][0] for n in REST], next_src=[bf["w_in"][1]])
    h, sv1, w1, _ = _layer_fwd(h, mem[0], first_weights(1, win1_land[0]), 1, rest_src=[bf[n][1] for n in REST])
    dh, loss_row = _loss_head(h, loss_target[0], min(512, s_len), "loss_head")
    dh, g1, rest_g1, _, _ = _layer_bwd(dh, mem[0], w1, sv1, 1, scatter_rest=True)
    grad_x, g0, rest_g0, win1_g, win0_g = _layer_bwd(dh, mem[0], w0, sv0, 0, scatter_rest=True, dkv_src=[g1["w_in"]],
                                                     scatter_w_in=True)

    def small_rows(get):
        rows = [jnp.concatenate([get(n) for n in VEC_1024], axis=0),
                jnp.concatenate([get(n) for n in VEC_512], axis=0).reshape(len(VEC_512), D_MODEL),
                jnp.pad(get("b_forget").reshape(1, -1), ((0, 0), (0, D_MODEL - DEPTH * FOX_HEADS)))]
        return jnp.concatenate(rows, axis=0)

    def tap_rows(conv):
        return jnp.pad(conv.reshape(1, -1), ((0, 0), (0, 4 * D_MODEL - conv.size))).reshape(4, D_MODEL)

    n_vec_rows = DEPTH * len(VEC_1024) + len(VEC_512) + 1
    part = small_rows(lambda n: jnp.concatenate([g0[n], g1[n]], axis=0))
    conv_part = jnp.stack([g0["conv_w"], g1["conv_w"]]).reshape(CONV_WIDTH, D_MODEL)
    n_part = 1 + n_vec_rows + CONV_WIDTH
    pad_rows = -n_part % 8
    small_land = _exchange([jnp.concatenate([loss_row, part, conv_part, jnp.zeros((pad_rows, D_MODEL), F32)], axis=0)],
                           [True], "gather_small")[0]
    total = _sum_blocks(small_land, "sum_small")
    loss = jnp.sum(total[0])
    conv_g = total[1 + n_vec_rows:n_part].reshape(DEPTH, CONV_WIDTH, CONV_CH)
    conv_g = lax.dynamic_slice_in_dim(conv_g, me * conv_cols, conv_cols, axis=2)
    fill = jnp.zeros((SMALL_ROWS - n_vec_rows - 4, D_MODEL), F32)

    def small_pack(vec_rows, conv):
        return jnp.concatenate([vec_rows, tap_rows(conv), fill], axis=0)

    small_out = _adamw([small_pack(total[1:1 + n_vec_rows], conv_g)[None]],
                       *[small_pack(small_rows(lambda n: p[pre + n]), p[pre + "conv_w"])[None] for pre in ("", "m_", "v_")],
                       SMALL_ROWS, "adamw_small")

    def unpack_small(buf):
        out = {}
        for k, n in enumerate(VEC_1024):
            out[n] = buf[DEPTH * k:DEPTH * (k + 1)]
        at = DEPTH * len(VEC_1024)
        for k, n in enumerate(VEC_512):
            out[n] = buf[at + k].reshape(DEPTH, CONV_CH)
        at += len(VEC_512)
        out["b_forget"] = buf[at, :DEPTH * FOX_HEADS].reshape(DEPTH, FOX_HEADS)
        out["conv_w"] = buf[at + 1:at + 5].reshape(-1)[:DEPTH * CONV_WIDTH * conv_cols].reshape(DEPTH, CONV_WIDTH, conv_cols)
        return out

    big_out = {n: _adamw([rest_g0[i], rest_g1[i]], p[n], p["m_" + n], p["v_" + n], ADAMW_TILE[n], f"adamw_{n}")
               for i, n in enumerate(REST)}
    in_out = _adamw_cols([win0_g[0], win1_g[0]], w_in_t[""], w_in_t["m_"], w_in_t["v_"], "adamw_w_in")
    big_out["w_in"] = [a.reshape(in_cols, DEPTH, D_MODEL).transpose(1, 2, 0) for a in in_out]

    result = [loss, grad_x[None]]
    for k in range(4):
        smalls = unpack_small(small_out[k][0])
        result += [big_out[n][k] if n in big_out else smalls[n] for n in names]
    return tuple(result)
```
